```python
import jax, jax.numpy as jnp
from jax import lax
import numpy as np

D_MODEL = 1024
BATCH = 8
SEQ = 16384
DEPTH = 4

HG_HEADS = 8
HG_DK = 128
HG_DV = 128
HG_WIDTH = HG_HEADS * HG_DK
HG_CHUNK = 64
ATT_Q_HEADS = 16
ATT_KV_HEADS = 4
ATT_GROUP = ATT_Q_HEADS // ATT_KV_HEADS
ATT_HEAD_DIM = 64
ATT_WIDTH = ATT_Q_HEADS * ATT_HEAD_DIM
ATT_KV_WIDTH = ATT_KV_HEADS * ATT_HEAD_DIM
WINDOW = 128
ATT_BLOCK = 128
ROPE_THETA = 500000.0
ROPE_DIM = ATT_HEAD_DIM // 4
FFN_HIDDEN = ((8 * D_MODEL // 3 + 255) // 256) * 256
EPS = 1e-6
MIN_F = 1e-30
IN_SIZES = (HG_WIDTH, HG_WIDTH, HG_WIDTH, HG_WIDTH, ATT_WIDTH, ATT_KV_WIDTH, ATT_KV_WIDTH, D_MODEL, D_MODEL)
IN_COLS = 4 * HG_WIDTH + ATT_WIDTH + 2 * ATT_KV_WIDTH + 2 * D_MODEL

kernel_name = "hgrn2_swa_sink_gated_hybrid"


def rmsnorm(x, g):
    xf = x.astype(jnp.float32)
    y = xf * lax.rsqrt(jnp.mean(xf * xf, axis=-1, keepdims=True) + EPS)
    return (y * g.astype(jnp.float32)).astype(x.dtype)


def split_cols(proj):
    points, acc = [], 0
    for s in IN_SIZES[:-1]:
        acc += s
        points.append(acc)
    return jnp.split(proj, points, axis=-1)


def rope_partial(x, pos):
    half = ROPE_DIM // 2
    inv = ROPE_THETA ** (-jnp.arange(half, dtype=jnp.float32) * 2.0 / ROPE_DIM)
    ang = pos.astype(jnp.float32)[:, None] * inv[None, :]
    cos = jnp.cos(ang)[None, :, None, :]
    sin = jnp.sin(ang)[None, :, None, :]
    xr = x[..., :ROPE_DIM].astype(jnp.float32)
    x1, x2 = xr[..., :half], xr[..., half:]
    rot = jnp.concatenate([x1 * cos - x2 * sin, x2 * cos + x1 * sin], axis=-1)
    return jnp.concatenate([rot.astype(x.dtype), x[..., ROPE_DIM:]], axis=-1)


def hgrn2_chunked(q, k, v, logf):
    B, S, H, DK = q.shape
    DV = v.shape[-1]
    C = HG_CHUNK
    n = S // C

    def to_chunks(t):
        return t.reshape(B, n, C, H, t.shape[-1]).transpose(1, 0, 3, 2, 4)

    qc, kc, vc, fc = to_chunks(q), to_chunks(k), to_chunks(v), to_chunks(logf)
    causal = jnp.tril(jnp.ones((C, C), dtype=bool))

    def step(state, inp):
        qi, ki, vi, fi = inp
        b = jnp.cumsum(fi, axis=2)
        o_inter = jnp.einsum('bhtk,bhkv->bhtv', qi * jnp.exp(b), state)
        diff = b[:, :, :, None, :] - b[:, :, None, :, :]
        decay = jnp.exp(jnp.where(causal[:, :, None], diff, -jnp.inf))
        scores = jnp.einsum('bhtk,bhsk,bhtsk->bhts', qi, ki, decay)
        o_intra = jnp.einsum('bhts,bhsv->bhtv', scores, vi)
        b_last = b[:, :, -1:, :]
        new_state = jnp.exp(b_last[:, :, 0, :])[..., None] * state + jnp.einsum(
            'bhsk,bhsv->bhkv', ki * jnp.exp(b_last - b), vi)
        return new_state, o_inter + o_intra

    init = jnp.zeros((B, H, DK, DV), jnp.float32)
    _, o = lax.scan(step, init, (qc, kc, vc, fc))
    return o.transpose(1, 0, 3, 2, 4).reshape(B, S, H, DV)


def swa_with_sinks(q, k, v, sinks):
    B, S, Hq, hd = q.shape
    W = ATT_BLOCK
    n = S // W
    qb = q.reshape(B, n, W, ATT_KV_HEADS, ATT_GROUP, hd)

    def with_prev(t):
        tb = t.reshape(B, n, W, ATT_KV_HEADS, hd)
        prev = jnp.pad(tb, ((0, 0), (1, 0), (0, 0), (0, 0), (0, 0)))[:, :-1]
        return jnp.concatenate([prev, tb], axis=2)

    kw, vw = with_prev(k), with_prev(v)
    s = jnp.einsum('bnqhgd,bnshd->bnhgqs', qb, kw).astype(jnp.float32) * (hd ** -0.5)
    blk = jnp.arange(n)[:, None, None] * W
    qpos = blk + jnp.arange(W)[None, :, None]
    kpos = blk - W + jnp.arange(2 * W)[None, None, :]
    delta = qpos - kpos
    mask = (delta >= 0) & (delta < WINDOW) & (kpos >= 0)
    s = jnp.where(mask[None, :, None, None], s, -jnp.inf)
    sink = sinks.astype(jnp.float32).reshape(1, 1, ATT_KV_HEADS, ATT_GROUP, 1, 1)
    m = jnp.maximum(jnp.max(s, axis=-1, keepdims=True), sink)
    p = jnp.exp(s - m)
    p = p / (jnp.sum(p, axis=-1, keepdims=True) + jnp.exp(sink - m))
    o = jnp.einsum('bnhgqs,bnshd->bnqhgd', p.astype(v.dtype), vw)
    return o.reshape(B, S, Hq * hd)


def _fwd_setup_inputs(seed: int = 0) -> dict:
    key = jax.random.key(seed)
    ks = jax.random.split(key, 16)
    f32 = jnp.float32
    nrm = lambda k, shape, scale: jax.random.normal(k, shape, f32) * scale
    return {
        "x": nrm(ks[0], (BATCH, SEQ, D_MODEL), 1.0),
        "norm1": 1.0 + nrm(ks[1], (DEPTH, D_MODEL), 0.01),
        "w_in": nrm(ks[2], (DEPTH, D_MODEL, IN_COLS), D_MODEL ** -0.5),
        "lb_logits": nrm(ks[3], (DEPTH, HG_WIDTH), 0.5),
        "hg_norm": 1.0 + nrm(ks[4], (DEPTH, HG_WIDTH), 0.01),
        "attn_sinks": nrm(ks[5], (DEPTH, ATT_Q_HEADS), 0.5),
        "w_pa": nrm(ks[6], (DEPTH, HG_WIDTH, D_MODEL), HG_WIDTH ** -0.5),
        "w_pb": nrm(ks[7], (DEPTH, ATT_WIDTH, D_MODEL), ATT_WIDTH ** -0.5),
        "w_o": nrm(ks[8], (DEPTH, D_MODEL, D_MODEL), D_MODEL ** -0.5),
        "norm2": 1.0 + nrm(ks[9], (DEPTH, D_MODEL), 0.01),
        "w_gate": nrm(ks[10], (DEPTH, D_MODEL, FFN_HIDDEN), D_MODEL ** -0.5),
        "w_up": nrm(ks[11], (DEPTH, D_MODEL, FFN_HIDDEN), D_MODEL ** -0.5),
        "w_down": nrm(ks[12], (DEPTH, FFN_HIDDEN, D_MODEL), FFN_HIDDEN ** -0.5),
        "final_norm": 1.0 + nrm(ks[13], (D_MODEL,), 0.01),
    }


def _fwd_reference(x, norm1, w_in, lb_logits, hg_norm, attn_sinks, w_pa, w_pb, w_o,
              norm2, w_gate, w_up, w_down, final_norm):
    B, S, _ = x.shape
    pos = jnp.arange(S)
    lb_p = jax.nn.softmax(lb_logits.astype(jnp.float32), axis=0)
    lb_all = jnp.cumsum(lb_p, axis=0) - lb_p[0:1]

    for l in range(DEPTH):
        h = rmsnorm(x, norm1[l])
        proj = h @ w_in[l]
        hq, hf, hi, hg, aq, ak, av, ga, gb = split_cols(proj)

        q = jax.nn.silu(hq).reshape(B, S, HG_HEADS, HG_DK).astype(jnp.float32)
        z = hf.reshape(B, S, HG_HEADS, HG_DK).astype(jnp.float32)
        lb = lb_all[l].reshape(HG_HEADS, HG_DK)
        f = lb + (1.0 - lb) * jax.nn.sigmoid(z)
        logf = jnp.log(jnp.maximum(f, MIN_F))
        kk = 1.0 - f
        vi = hi.reshape(B, S, HG_HEADS, HG_DV).astype(jnp.float32)
        o_hg = hgrn2_chunked(q, kk, vi, logf)
        o_hg = rmsnorm(o_hg, hg_norm[l].reshape(HG_HEADS, HG_DV)).astype(x.dtype)
        o_hg = o_hg.reshape(B, S, HG_WIDTH) * jax.nn.silu(hg)
        y_a = o_hg @ w_pa[l]

        qa = rope_partial(aq.reshape(B, S, ATT_Q_HEADS, ATT_HEAD_DIM), pos)
        ka = rope_partial(ak.reshape(B, S, ATT_KV_HEADS, ATT_HEAD_DIM), pos)
        va = av.reshape(B, S, ATT_KV_HEADS, ATT_HEAD_DIM)
        y_b = swa_with_sinks(qa, ka, va, attn_sinks[l]) @ w_pb[l]

        mix = jax.nn.sigmoid(ga) * y_a + jax.nn.sigmoid(gb) * y_b
        x = x + mix @ w_o[l]

        h2 = rmsnorm(x, norm2[l])
        x = x + (jax.nn.silu(h2 @ w_gate[l]) * (h2 @ w_up[l])) @ w_down[l]

    return rmsnorm(x, final_norm)


import jax as _jax
import jax.numpy as _jnp

TWIN_FORMAT = 'train_step'
FWD_PARAMS = ['x', 'norm1', 'w_in', 'lb_logits', 'hg_norm', 'attn_sinks', 'w_pa', 'w_pb', 'w_o', 'norm2', 'w_gate', 'w_up', 'w_down', 'final_norm']
TWIN_WEIGHTS = ['norm1', 'w_in', 'lb_logits', 'hg_norm', 'attn_sinks', 'w_pa', 'w_pb', 'w_o', 'norm2', 'w_gate', 'w_up', 'w_down', 'final_norm']
TWIN_DIFF_INPUT = 'x'
TWIN_INPUTS = ['x', 'norm1', 'w_in', 'lb_logits', 'hg_norm', 'attn_sinks', 'w_pa', 'w_pb', 'w_o', 'norm2', 'w_gate', 'w_up', 'w_down', 'final_norm', 'loss_target', 'm_norm1', 'm_w_in', 'm_lb_logits', 'm_hg_norm', 'm_attn_sinks', 'm_w_pa', 'm_w_pb', 'm_w_o', 'm_norm2', 'm_w_gate', 'm_w_up', 'm_w_down', 'm_final_norm', 'v_norm1', 'v_w_in', 'v_lb_logits', 'v_hg_norm', 'v_attn_sinks', 'v_w_pa', 'v_w_pb', 'v_w_o', 'v_norm2', 'v_w_gate', 'v_w_up', 'v_w_down', 'v_final_norm']
TWIN_OUTPUTS = ['loss', 'grad_x', 'grad_norm1', 'grad_w_in', 'grad_lb_logits', 'grad_hg_norm', 'grad_attn_sinks', 'grad_w_pa', 'grad_w_pb', 'grad_w_o', 'grad_norm2', 'grad_w_gate', 'grad_w_up', 'grad_w_down', 'grad_final_norm', 'delta_norm1', 'delta_w_in', 'delta_lb_logits', 'delta_hg_norm', 'delta_attn_sinks', 'delta_w_pa', 'delta_w_pb', 'delta_w_o', 'delta_norm2', 'delta_w_gate', 'delta_w_up', 'delta_w_down', 'delta_final_norm', 'new_m_norm1', 'new_m_w_in', 'new_m_lb_logits', 'new_m_hg_norm', 'new_m_attn_sinks', 'new_m_w_pa', 'new_m_w_pb', 'new_m_w_o', 'new_m_norm2', 'new_m_w_gate', 'new_m_w_up', 'new_m_w_down', 'new_m_final_norm', 'new_v_norm1', 'new_v_w_in', 'new_v_lb_logits', 'new_v_hg_norm', 'new_v_attn_sinks', 'new_v_w_pa', 'new_v_w_pb', 'new_v_w_o', 'new_v_norm2', 'new_v_w_gate', 'new_v_w_up', 'new_v_w_down', 'new_v_final_norm']
TWIN_LEAF_KINDS = {'loss': 'loss', 'grad_x': 'grad_x', 'grad_norm1': 'grad_w', 'grad_w_in': 'grad_w', 'grad_lb_logits': 'grad_w', 'grad_hg_norm': 'grad_w', 'grad_attn_sinks': 'grad_w', 'grad_w_pa': 'grad_w', 'grad_w_pb': 'grad_w', 'grad_w_o': 'grad_w', 'grad_norm2': 'grad_w', 'grad_w_gate': 'grad_w', 'grad_w_up': 'grad_w', 'grad_w_down': 'grad_w', 'grad_final_norm': 'grad_w', 'delta_norm1': 'delta_w', 'delta_w_in': 'delta_w', 'delta_lb_logits': 'delta_w', 'delta_hg_norm': 'delta_w', 'delta_attn_sinks': 'delta_w', 'delta_w_pa': 'delta_w', 'delta_w_pb': 'delta_w', 'delta_w_o': 'delta_w', 'delta_norm2': 'delta_w', 'delta_w_gate': 'delta_w', 'delta_w_up': 'delta_w', 'delta_w_down': 'delta_w', 'delta_final_norm': 'delta_w', 'new_m_norm1': 'new_m', 'new_m_w_in': 'new_m', 'new_m_lb_logits': 'new_m', 'new_m_hg_norm': 'new_m', 'new_m_attn_sinks': 'new_m', 'new_m_w_pa': 'new_m', 'new_m_w_pb': 'new_m', 'new_m_w_o': 'new_m', 'new_m_norm2': 'new_m', 'new_m_w_gate': 'new_m', 'new_m_w_up': 'new_m', 'new_m_w_down': 'new_m', 'new_m_final_norm': 'new_m', 'new_v_norm1': 'new_v', 'new_v_w_in': 'new_v', 'new_v_lb_logits': 'new_v', 'new_v_hg_norm': 'new_v', 'new_v_attn_sinks': 'new_v', 'new_v_w_pa': 'new_v', 'new_v_w_pb': 'new_v', 'new_v_w_o': 'new_v', 'new_v_norm2': 'new_v', 'new_v_w_gate': 'new_v', 'new_v_w_up': 'new_v', 'new_v_w_down': 'new_v', 'new_v_final_norm': 'new_v'}


def _forward(args):
    return _fwd_reference(*[args[k] for k in FWD_PARAMS])


def _output_shape():
    def fwd():
        inp = _fwd_setup_inputs(0)
        return _fwd_reference(*[inp[k] for k in FWD_PARAMS])
    out = _jax.eval_shape(fwd)
    return out.shape, out.dtype

N_MICROBATCH = 1
ADAM_LR = 0.001
ADAM_B1 = 0.9
ADAM_B2 = 0.999
ADAM_EPS = 1e-08
ADAM_WD = 0.01
ADAM_STEP = 10
PER_EXAMPLE_BATCH_AXIS = {'x': 0, 'loss_target': 0}
SHARED_INPUTS = []
_WEIGHT_DTYPES = {'norm1': _jnp.float32, 'w_in': _jnp.float32, 'lb_logits': _jnp.float32, 'hg_norm': _jnp.float32, 'attn_sinks': _jnp.float32, 'w_pa': _jnp.float32, 'w_pb': _jnp.float32, 'w_o': _jnp.float32, 'norm2': _jnp.float32, 'w_gate': _jnp.float32, 'w_up': _jnp.float32, 'w_down': _jnp.float32, 'final_norm': _jnp.float32}
MOMENT_SCALE = {'norm1': 1.936208e-01, 'w_in': 7.088908e-02, 'lb_logits': 9.406987e-03, 'hg_norm': 1.196536e-01, 'attn_sinks': 3.545682e-02, 'w_pa': 1.219423e-01, 'w_pb': 4.611003e-02, 'w_o': 1.291890e-01, 'norm2': 2.591660e-01, 'w_gate': 1.124224e-01, 'w_up': 1.089777e-01, 'w_down': 1.808605e-01, 'final_norm': 1.278566e+02}


def _to_microbatches(a, axis):
    t = _jnp.moveaxis(a, axis, 0)
    t = t.reshape((N_MICROBATCH, t.shape[0] // N_MICROBATCH) + t.shape[1:])
    return _jnp.moveaxis(t, 1, axis + 1)


def setup_inputs(seed: int = 0) -> dict:
    inp = _fwd_setup_inputs(seed)
    key = _jax.random.fold_in(_jax.random.key(seed), 7919)
    shape, _ = _output_shape()
    out = dict(inp)
    out["loss_target"] = _jax.random.normal(_jax.random.fold_in(key, 0), shape, _jnp.float32)
    for i, name in enumerate(TWIN_WEIGHTS):
        w = inp[name].astype(_jnp.float32)
        if MOMENT_SCALE is None:
            s = _jnp.sqrt(_jnp.mean(_jnp.square(w)) + 1e-30)
        else:
            s = MOMENT_SCALE[name]
        km, kv = _jax.random.split(_jax.random.fold_in(key, i + 1))
        out[name] = w
        out["m_" + name] = s * _jax.random.normal(km, w.shape, _jnp.float32)
        out["v_" + name] = (s * s) * _jax.random.uniform(kv, w.shape, _jnp.float32, 0.5, 1.5)
    if N_MICROBATCH > 1:
        for name, axis in PER_EXAMPLE_BATCH_AXIS.items():
            out[name] = _to_microbatches(out[name], axis)
    return {'x': out['x'], 'norm1': out['norm1'], 'w_in': out['w_in'], 'lb_logits': out['lb_logits'], 'hg_norm': out['hg_norm'], 'attn_sinks': out['attn_sinks'], 'w_pa': out['w_pa'], 'w_pb': out['w_pb'], 'w_o': out['w_o'], 'norm2': out['norm2'], 'w_gate': out['w_gate'], 'w_up': out['w_up'], 'w_down': out['w_down'], 'final_norm': out['final_norm'], 'loss_target': out['loss_target'], 'm_norm1': out['m_norm1'], 'm_w_in': out['m_w_in'], 'm_lb_logits': out['m_lb_logits'], 'm_hg_norm': out['m_hg_norm'], 'm_attn_sinks': out['m_attn_sinks'], 'm_w_pa': out['m_w_pa'], 'm_w_pb': out['m_w_pb'], 'm_w_o': out['m_w_o'], 'm_norm2': out['m_norm2'], 'm_w_gate': out['m_w_gate'], 'm_w_up': out['m_w_up'], 'm_w_down': out['m_w_down'], 'm_final_norm': out['m_final_norm'], 'v_norm1': out['v_norm1'], 'v_w_in': out['v_w_in'], 'v_lb_logits': out['v_lb_logits'], 'v_hg_norm': out['v_hg_norm'], 'v_attn_sinks': out['v_attn_sinks'], 'v_w_pa': out['v_w_pa'], 'v_w_pb': out['v_w_pb'], 'v_w_o': out['v_w_o'], 'v_norm2': out['v_norm2'], 'v_w_gate': out['v_w_gate'], 'v_w_up': out['v_w_up'], 'v_w_down': out['v_w_down'], 'v_final_norm': out['v_final_norm']}


def _loss(weights, diff, rest, loss_target):
    with _jax.named_scope("forward"):
        args = {**rest, TWIN_DIFF_INPUT: diff, **{k: w.astype(_WEIGHT_DTYPES[k]) for k, w in weights.items()}}
        y = _forward(args)
    with _jax.named_scope("loss_head"):
        err = _jnp.square(y.astype(_jnp.float32) - loss_target)
        return 0.5 * _jnp.sum(_jnp.mean(err, axis=-1)) if err.ndim else 0.5 * err


def _adamw(w, g, m, v):
    m = ADAM_B1 * m + (1.0 - ADAM_B1) * g
    v = ADAM_B2 * v + (1.0 - ADAM_B2) * _jnp.square(g)
    m_hat = m / (1.0 - ADAM_B1 ** ADAM_STEP)
    v_hat = v / (1.0 - ADAM_B2 ** ADAM_STEP)
    delta = -ADAM_LR * (m_hat / (_jnp.sqrt(v_hat) + ADAM_EPS) + ADAM_WD * w)
    return delta, m, v


def reference(x, norm1, w_in, lb_logits, hg_norm, attn_sinks, w_pa, w_pb, w_o, norm2, w_gate, w_up, w_down, final_norm, loss_target, m_norm1, m_w_in, m_lb_logits, m_hg_norm, m_attn_sinks, m_w_pa, m_w_pb, m_w_o, m_norm2, m_w_gate, m_w_up, m_w_down, m_final_norm, v_norm1, v_w_in, v_lb_logits, v_hg_norm, v_attn_sinks, v_w_pa, v_w_pb, v_w_o, v_norm2, v_w_gate, v_w_up, v_w_down, v_final_norm):
    given = dict(x=x, norm1=norm1, w_in=w_in, lb_logits=lb_logits, hg_norm=hg_norm, attn_sinks=attn_sinks, w_pa=w_pa, w_pb=w_pb, w_o=w_o, norm2=norm2, w_gate=w_gate, w_up=w_up, w_down=w_down, final_norm=final_norm, loss_target=loss_target, m_norm1=m_norm1, m_w_in=m_w_in, m_lb_logits=m_lb_logits, m_hg_norm=m_hg_norm, m_attn_sinks=m_attn_sinks, m_w_pa=m_w_pa, m_w_pb=m_w_pb, m_w_o=m_w_o, m_norm2=m_norm2, m_w_gate=m_w_gate, m_w_up=m_w_up, m_w_down=m_w_down, m_final_norm=m_final_norm, v_norm1=v_norm1, v_w_in=v_w_in, v_lb_logits=v_lb_logits, v_hg_norm=v_hg_norm, v_attn_sinks=v_attn_sinks, v_w_pa=v_w_pa, v_w_pb=v_w_pb, v_w_o=v_w_o, v_norm2=v_norm2, v_w_gate=v_w_gate, v_w_up=v_w_up, v_w_down=v_w_down, v_final_norm=v_final_norm)
    weights = {n: given[n] for n in TWIN_WEIGHTS}
    shared = {n: given[n] for n in SHARED_INPUTS}
    per_example = {n: given[n] for n in ['x']}
    grad_fn = _jax.value_and_grad(_loss, argnums=(0, 1))

    def one_microbatch(ex, loss_target):
        ex = dict(ex)
        diff = ex.pop(TWIN_DIFF_INPUT)
        return grad_fn(weights, diff, {**shared, **ex}, loss_target)

    if N_MICROBATCH == 1:
        loss, (grad_w, grad_x) = one_microbatch(per_example, given["loss_target"])
    else:
        def body(carry, xs):
            loss_sum, grad_sum = carry
            l_k, (gw_k, gx_k) = one_microbatch(xs[0], xs[1])
            with _jax.named_scope("update"):
                return (loss_sum + l_k, _jax.tree.map(_jnp.add, grad_sum, gw_k)), gx_k

        init = (_jnp.zeros((), _jnp.float32), _jax.tree.map(_jnp.zeros_like, weights))
        (loss, grad_w), grad_x = _jax.lax.scan(body, init, (per_example, given["loss_target"]))
    with _jax.named_scope("update"):
        delta_w, new_m, new_v = {}, {}, {}
        for n in TWIN_WEIGHTS:
            delta_w[n], new_m[n], new_v[n] = _adamw(weights[n], grad_w[n], given["m_" + n], given["v_" + n])
    return (loss, grad_x, *[grad_w[n] for n in TWIN_WEIGHTS], *[delta_w[n] for n in TWIN_WEIGHTS],
            *[new_m[n] for n in TWIN_WEIGHTS], *[new_v[n] for n in TWIN_WEIGHTS])
```

```python
import functools

import jax
import jax.numpy as jnp
import numpy as np
from jax import lax
from jax.experimental import pallas as pl
from jax.experimental.pallas import tpu as pltpu

F32 = jnp.float32
BF16 = jnp.bfloat16

D_MODEL = 1024
DEPTH = 4
HG_HEADS = 8
HG_DK = 128
HG_WIDTH = HG_HEADS * HG_DK
CHUNK = 64
ATT_Q_HEADS = 16
ATT_KV_HEADS = 4
ATT_GROUP = ATT_Q_HEADS // ATT_KV_HEADS
HEAD_DIM = 64
ATT_WIDTH = ATT_Q_HEADS * HEAD_DIM
ATT_KV_WIDTH = ATT_KV_HEADS * HEAD_DIM
WINDOW = 128
ROPE_THETA = 500000.0
ROPE_DIM = HEAD_DIM // 4
FFN_HIDDEN = 2816
IN_COLS = 4 * HG_WIDTH + ATT_WIDTH + 2 * ATT_KV_WIDTH + 2 * D_MODEL
EPS = 1e-6
MIN_F = 1e-30
COL_HQ, COL_HF, COL_HI, COL_HG = 0, 8, 16, 24
COL_AQ, COL_AK, COL_AV, COL_GA, COL_GB = 32, 40, 42, 44, 52

ADAM_LR = 0.001
ADAM_B1 = 0.9
ADAM_B2 = 0.999
ADAM_EPS = 1e-08
ADAM_WD = 0.01
ADAM_STEP = 10

VMEM_LIMIT_V7X = 56 * 1024 * 1024
MESH = pl.DeviceIdType.MESH


def _cp(sem, vmem=VMEM_LIMIT_V7X):
    return pltpu.CompilerParams(dimension_semantics=sem, vmem_limit_bytes=vmem)


def _sigmoid(x):
    return 1.0 / (1.0 + jnp.exp(-x))


def _dot(a, b):
    return jnp.dot(a, b, preferred_element_type=F32)


def _dot_nt(a, b):
    return lax.dot_general(a, b, (((1,), (1,)), ((), ())), preferred_element_type=F32)


def _dot_tn(a, b):
    return lax.dot_general(a, b, (((0,), (0,)), ((), ())), preferred_element_type=F32)


def _rowsum8(v):
    r, n = v.shape
    return jnp.sum(v.reshape(r // 8, 8, n), axis=0)


def ln_matmul(x, g, w, *, tm, tn, name, swiglu=False):
    T, Dm = x.shape
    N = w.shape[1]
    if swiglu:
        assert tn == N
    half = N // 2

    def body(x_ref, g_ref, w_ref, h_ref, o_ref, *rest):
        @pl.when(pl.program_id(1) == 0)
        def _():
            xf = x_ref[...]
            r = lax.rsqrt(jnp.mean(xf * xf, axis=-1, keepdims=True) + EPS)
            h_ref[...] = (xf * r * g_ref[...]).astype(BF16)

        acc = _dot(h_ref[...], w_ref[...])
        o_ref[...] = acc.astype(BF16)
        if swiglu:
            gt, up = acc[:, :half], acc[:, half:]
            rest[0][...] = (gt * _sigmoid(gt) * up).astype(BF16)

    out_shape = [jax.ShapeDtypeStruct((T, Dm), BF16), jax.ShapeDtypeStruct((T, N), BF16)]
    out_specs = [pl.BlockSpec((tm, Dm), lambda i, j: (i, 0)), pl.BlockSpec((tm, tn), lambda i, j: (i, j))]
    if swiglu:
        out_shape.append(jax.ShapeDtypeStruct((T, half), BF16))
        out_specs.append(pl.BlockSpec((tm, half), lambda i, j: (i, 0)))
    return pl.pallas_call(
        body, name=name, grid=(T // tm, N // tn),
        in_specs=[pl.BlockSpec((tm, Dm), lambda i, j: (i, 0)), pl.BlockSpec((1, Dm), lambda i, j: (0, 0)),
                  pl.BlockSpec((Dm, tn), lambda i, j: (0, j))],
        out_specs=out_specs, out_shape=out_shape, compiler_params=_cp(("parallel", "arbitrary")),
    )(x, g, w)


def matmul_residual(x, a, w, *, tm, name):
    T, N = x.shape
    K = a.shape[1]

    def body(x_ref, a_ref, w_ref, o_ref):
        o_ref[...] = x_ref[...] + _dot(a_ref[...], w_ref[...])

    return pl.pallas_call(
        body, name=name, grid=(T // tm,),
        in_specs=[pl.BlockSpec((tm, N), lambda i: (i, 0)), pl.BlockSpec((tm, K), lambda i: (i, 0)),
                  pl.BlockSpec((K, N), lambda i: (0, 0))],
        out_specs=pl.BlockSpec((tm, N), lambda i: (i, 0)), out_shape=jax.ShapeDtypeStruct((T, N), F32),
        compiler_params=_cp(("parallel",)),
    )(x, a, w)


def matmul_nt(a, w, *, tm, name):
    T, K = a.shape
    N = w.shape[0]

    def body(a_ref, w_ref, o_ref):
        o_ref[...] = _dot_nt(a_ref[...], w_ref[...]).astype(BF16)

    return pl.pallas_call(
        body, name=name, grid=(T // tm,),
        in_specs=[pl.BlockSpec((tm, K), lambda i: (i, 0)), pl.BlockSpec((N, K), lambda i: (0, 0))],
        out_specs=pl.BlockSpec((tm, N), lambda i: (i, 0)), out_shape=jax.ShapeDtypeStruct((T, N), BF16),
        compiler_params=_cp(("parallel",)),
    )(a, w)


def matmul_tn(a, g, *, tka, tn, tk, name):
    T, Ka = a.shape
    N = g.shape[1]
    nk = T // tk

    def body(a_ref, g_ref, o_ref, acc_ref):
        k = pl.program_id(2)

        @pl.when(k == 0)
        def _():
            acc_ref[...] = jnp.zeros_like(acc_ref)

        acc_ref[...] += _dot_tn(a_ref[...].astype(BF16), g_ref[...].astype(BF16))

        @pl.when(k == nk - 1)
        def _():
            o_ref[...] = acc_ref[...].astype(BF16)

    return pl.pallas_call(
        body, name=name, grid=(Ka // tka, N // tn, nk),
        in_specs=[pl.BlockSpec((tk, tka), lambda i, j, k: (k, i)), pl.BlockSpec((tk, tn), lambda i, j, k: (k, j))],
        out_specs=pl.BlockSpec((tka, tn), lambda i, j, k: (i, j)), out_shape=jax.ShapeDtypeStruct((Ka, N), BF16),
        scratch_shapes=[pltpu.VMEM((tka, tn), F32)],
        compiler_params=_cp(("parallel", "parallel", "arbitrary")),
    )(a, g)


def matmul_nt_rmsbwd(a, w, x, g, dres, *, tm, tk, name):
    T, K = a.shape
    Dm = w.shape[0]
    nk = K // tk

    def body(a_ref, w_ref, x_ref, g_ref, dres_ref, dx_ref, dg_ref, acc_ref):
        i, k = pl.program_id(0), pl.program_id(1)

        @pl.when(k == 0)
        def _():
            acc_ref[...] = jnp.zeros_like(acc_ref)

        @pl.when((i == 0) & (k == 0))
        def _():
            dg_ref[...] = jnp.zeros_like(dg_ref)

        acc_ref[...] += _dot_nt(a_ref[...], w_ref[...])

        @pl.when(k == nk - 1)
        def _():
            dh = acc_ref[...]
            xf = x_ref[...]
            r = lax.rsqrt(jnp.mean(xf * xf, axis=-1, keepdims=True) + EPS)
            xhat = xf * r
            dg_ref[...] += _rowsum8(dh * xhat)
            dxhat = dh * g_ref[...]
            dx_ref[...] = dres_ref[...] + r * (dxhat - xhat * jnp.mean(dxhat * xhat, axis=-1, keepdims=True))

    return pl.pallas_call(
        body, name=name, grid=(T // tm, nk),
        in_specs=[pl.BlockSpec((tm, tk), lambda i, k: (i, k)), pl.BlockSpec((Dm, tk), lambda i, k: (0, k)),
                  pl.BlockSpec((tm, Dm), lambda i, k: (i, 0)), pl.BlockSpec((1, Dm), lambda i, k: (0, 0)),
                  pl.BlockSpec((tm, Dm), lambda i, k: (i, 0))],
        out_specs=[pl.BlockSpec((tm, Dm), lambda i, k: (i, 0)), pl.BlockSpec((8, Dm), lambda i, k: (0, 0))],
        out_shape=[jax.ShapeDtypeStruct((T, Dm), F32), jax.ShapeDtypeStruct((8, Dm), F32)],
        scratch_shapes=[pltpu.VMEM((tm, Dm), F32)],
        compiler_params=_cp(("arbitrary", "arbitrary")),
    )(a, w, x, g, dres)


def ffn_down_bwd(dx, w_down, gu, *, tm, name):
    T, Dm = dx.shape
    H = w_down.shape[0]

    def body(dx_ref, w_ref, gu_ref, o_ref):
        dact = _dot_nt(dx_ref[...].astype(BF16), w_ref[...])
        gt = gu_ref[:, :H].astype(F32)
        up = gu_ref[:, H:].astype(F32)
        s = _sigmoid(gt)
        o_ref[:, :H] = (dact * up * (s * (1.0 + gt * (1.0 - s)))).astype(BF16)
        o_ref[:, H:] = (dact * gt * s).astype(BF16)

    return pl.pallas_call(
        body, name=name, grid=(T // tm,),
        in_specs=[pl.BlockSpec((tm, Dm), lambda i: (i, 0)), pl.BlockSpec((H, Dm), lambda i: (0, 0)),
                  pl.BlockSpec((tm, 2 * H), lambda i: (i, 0))],
        out_specs=pl.BlockSpec((tm, 2 * H), lambda i: (i, 0)), out_shape=jax.ShapeDtypeStruct((T, 2 * H), BF16),
        compiler_params=_cp(("parallel",)),
    )(dx, w_down, gu)


def _gate_specs(tm):
    half = D_MODEL // 2
    cols = (COL_GA // 4, COL_GA // 4 + 1, COL_GB // 4, COL_GB // 4 + 1)
    return [pl.BlockSpec((tm, half), lambda i, c=c: (i, c)) for c in cols]


def merge_fwd(og, att, proj, w_pa, w_pb, *, tm, name):
    T, Dm = og.shape

    half = Dm // 2

    def body(og_ref, att_ref, ga0, ga1, gb0, gb1, wa_ref, wb_ref, ya_ref, yb_ref, mix_ref):
        ya = _dot(og_ref[...], wa_ref[...])
        yb = _dot(att_ref[...], wb_ref[...])
        ya_ref[...] = ya.astype(BF16)
        yb_ref[...] = yb.astype(BF16)
        for c, (ga, gb) in enumerate(((ga0, gb0), (ga1, gb1))):
            cols = slice(c * half, (c + 1) * half)
            mix_ref[:, cols] = (_sigmoid(ga[...].astype(F32)) * ya[:, cols]
                                + _sigmoid(gb[...].astype(F32)) * yb[:, cols]).astype(BF16)

    row = pl.BlockSpec((tm, Dm), lambda i: (i, 0))
    wsp = pl.BlockSpec((Dm, Dm), lambda i: (0, 0))
    return pl.pallas_call(
        body, name=name, grid=(T // tm,),
        in_specs=[row, row] + _gate_specs(tm) + [wsp, wsp],
        out_specs=[row, row, row], out_shape=[jax.ShapeDtypeStruct((T, Dm), BF16)] * 3,
        compiler_params=_cp(("parallel",)),
    )(og, att, proj, proj, proj, proj, w_pa, w_pb)


def merge_bwd(dx, w_o, proj, ya, yb, *, tm, name):
    T, Dm = dx.shape

    half = Dm // 2

    def body(dx_ref, w_ref, ga0, ga1, gb0, gb1, ya_ref, yb_ref, dya_ref, dyb_ref, dgate_ref):
        dmix = _dot_nt(dx_ref[...].astype(BF16), w_ref[...])
        for c, (ga, gb) in enumerate(((ga0, gb0), (ga1, gb1))):
            cols = slice(c * half, (c + 1) * half)
            dm = dmix[:, cols]
            sa = _sigmoid(ga[...].astype(F32))
            sb = _sigmoid(gb[...].astype(F32))
            dya_ref[:, cols] = (dm * sa).astype(BF16)
            dyb_ref[:, cols] = (dm * sb).astype(BF16)
            dgate_ref[:, cols] = (dm * ya_ref[:, cols].astype(F32) * sa * (1.0 - sa)).astype(BF16)
            dgate_ref[:, slice(Dm + c * half, Dm + (c + 1) * half)] = (
                dm * yb_ref[:, cols].astype(F32) * sb * (1.0 - sb)).astype(BF16)

    row = pl.BlockSpec((tm, Dm), lambda i: (i, 0))
    return pl.pallas_call(
        body, name=name, grid=(T // tm,),
        in_specs=[row, pl.BlockSpec((Dm, Dm), lambda i: (0, 0))] + _gate_specs(tm) + [row, row],
        out_specs=[row, row, pl.BlockSpec((tm, 2 * Dm), lambda i: (i, 0))],
        out_shape=[jax.ShapeDtypeStruct((T, Dm), BF16)] * 2 + [jax.ShapeDtypeStruct((T, 2 * Dm), BF16)],
        compiler_params=_cp(("parallel",)),
    )(dx, w_o, proj, proj, proj, proj, ya, yb)


def loss_head(x, g, target, *, tm, name):
    T, Dm = x.shape

    def body(x_ref, g_ref, t_ref, dx_ref, l_ref, dg_ref):
        @pl.when(pl.program_id(0) == 0)
        def _():
            l_ref[...] = jnp.zeros_like(l_ref)
            dg_ref[...] = jnp.zeros_like(dg_ref)

        xf = x_ref[...]
        gv = g_ref[...]
        r = lax.rsqrt(jnp.mean(xf * xf, axis=-1, keepdims=True) + EPS)
        xhat = xf * r
        err = xhat * gv - t_ref[...]
        l_ref[...] += _rowsum8(err * err) * (0.5 / Dm)
        dy = err * (1.0 / Dm)
        dg_ref[...] += _rowsum8(dy * xhat)
        dxhat = dy * gv
        dx_ref[...] = r * (dxhat - xhat * jnp.mean(dxhat * xhat, axis=-1, keepdims=True))

    row = pl.BlockSpec((tm, Dm), lambda i: (i, 0))
    acc = pl.BlockSpec((8, Dm), lambda i: (0, 0))
    return pl.pallas_call(
        body, name=name, grid=(T // tm,),
        in_specs=[row, pl.BlockSpec((1, Dm), lambda i: (0, 0)), row],
        out_specs=[row, acc, acc],
        out_shape=[jax.ShapeDtypeStruct((T, Dm), F32), jax.ShapeDtypeStruct((8, Dm), F32), jax.ShapeDtypeStruct((8, Dm), F32)],
        compiler_params=_cp(("arbitrary",)),
    )(x, g, target)


_LEVELS = (32, 16, 8)
_DIAG = 8
_HI = lax.Precision.HIGHEST


def _chunk_consts():
    t = lax.broadcasted_iota(jnp.int32, (CHUNK, CHUNK), 0)
    s = lax.broadcasted_iota(jnp.int32, (CHUNK, CHUNK), 1)
    tri = (s <= t).astype(F32)
    tri_t = (s >= t).astype(F32)
    masks = []
    for m in _LEVELS:
        sh = int(np.log2(2 * m))
        masks.append(((t >> sh) == (s >> sh)) & ((t & (2 * m - 1)) >= m) & ((s & (2 * m - 1)) < m))
    return tri, tri_t, masks


def _level_ref(b, m):
    pieces = [jnp.broadcast_to(b[p * 2 * m + m - 1:p * 2 * m + m, :], (2 * m, b.shape[1])) for p in range(CHUNK // (2 * m))]
    return pieces[0] if len(pieces) == 1 else jnp.concatenate(pieces, axis=0)


def _intra_off(q, kk, b, masks):
    a = jnp.zeros((CHUNK, CHUNK), F32)
    keep = []
    for m, mask in zip(_LEVELS, masks):
        ref = _level_ref(b, m)
        eq = jnp.exp(jnp.minimum(b - ref, 0.0))
        ek = jnp.exp(jnp.minimum(ref - b, 0.0))
        qt = (q * eq).astype(BF16)
        kt = (kk * ek).astype(BF16)
        a = a + jnp.where(mask, _dot_nt(qt, kt), 0.0)
        keep.append((mask, eq, ek, qt, kt))
    return a, keep


def _diag_fwd(q, kk, v, b):
    rows = lax.broadcasted_iota(jnp.int32, (_DIAG, HG_DK), 0)
    outs = []
    for i in range(CHUNK // _DIAG):
        sl = slice(i * _DIAG, (i + 1) * _DIAG)
        qb, kb, vb, bb = q[sl], kk[sl], v[sl], b[sl]
        acc = jnp.zeros((_DIAG, HG_DK), F32)
        for j in range(_DIAG):
            dec = jnp.where(rows >= j, jnp.exp(jnp.minimum(bb - bb[j:j + 1], 0.0)), 0.0)
            a = jnp.sum(qb * dec * kb[j:j + 1], axis=-1, keepdims=True)
            acc = acc + a * vb[j:j + 1]
        outs.append(acc)
    return jnp.concatenate(outs, axis=0)


def _diag_bwd(q, kk, v, b, do):
    rows = lax.broadcasted_iota(jnp.int32, (_DIAG, HG_DK), 0)
    dqs, dks, dvs = [], [], []
    for i in range(CHUNK // _DIAG):
        sl = slice(i * _DIAG, (i + 1) * _DIAG)
        qb, kb, vb, bb, dob = q[sl], kk[sl], v[sl], b[sl], do[sl]
        dqb = jnp.zeros((_DIAG, HG_DK), F32)
        dkb = jnp.zeros((_DIAG, HG_DK), F32)
        dvb = jnp.zeros((_DIAG, HG_DK), F32)
        for j in range(_DIAG):
            dec = jnp.where(rows >= j, jnp.exp(jnp.minimum(bb - bb[j:j + 1], 0.0)), 0.0)
            qd = qb * dec
            a = jnp.sum(qd * kb[j:j + 1], axis=-1, keepdims=True)
            da = jnp.sum(dob * vb[j:j + 1], axis=-1, keepdims=True)
            dqb = dqb + da * dec * kb[j:j + 1]
            dkb = jnp.where(rows == j, jnp.sum(da * qd, axis=0, keepdims=True), dkb)
            dvb = jnp.where(rows == j, jnp.sum(a * dob, axis=0, keepdims=True), dvb)
        dqs.append(dqb)
        dks.append(dkb)
        dvs.append(dvb)
    return jnp.concatenate(dqs, axis=0), jnp.concatenate(dks, axis=0), jnp.concatenate(dvs, axis=0)


def _hgrn_in_specs(ct, order):
    def spec(col):
        return pl.BlockSpec((ct, HG_DK), lambda h, i, col=col: (order(i), col + h))
    vec = pl.BlockSpec((1, HG_DK), lambda h, i: (0, h))
    return [spec(COL_HQ), spec(COL_HF), spec(COL_HI), spec(COL_HG), vec, vec]


def hgrn_fwd(proj, lb, gn, *, ct, name):
    T = proj.shape[0]
    nc = ct // CHUNK

    def body(hq_ref, hf_ref, hi_ref, hg_ref, lb_ref, gn_ref, og_ref, o_ref, st_ref, s_ref):
        @pl.when(pl.program_id(1) == 0)
        def _():
            s_ref[...] = jnp.zeros_like(s_ref)

        tri, _, masks = _chunk_consts()
        lb_v = lb_ref[...]
        gn_v = gn_ref[...]

        def chunk(c, carry):
            sl = pl.ds(pl.multiple_of(c * CHUNK, CHUNK), CHUNK)
            hq = hq_ref[sl, :].astype(F32)
            v = hi_ref[sl, :].astype(F32)
            gate = hg_ref[sl, :].astype(F32)
            q = hq * _sigmoid(hq)
            f = lb_v + (1.0 - lb_v) * _sigmoid(hf_ref[sl, :].astype(F32))
            kk = 1.0 - f
            b = jnp.dot(tri, jnp.log(jnp.maximum(f, MIN_F)), precision=_HI, preferred_element_type=F32)
            st = s_ref[...]
            st_ref[0, c] = st
            a, _ = _intra_off(q, kk, b, masks)
            vb = v.astype(BF16)
            o = (_dot_nt((q * jnp.exp(b)).astype(BF16), st.astype(BF16)) + _dot(a.astype(BF16), vb)
                 + _diag_fwd(q, kk, v, b))
            bl = b[CHUNK - 1:CHUNK, :]
            s_ref[...] = st * jnp.exp(bl) + _dot_tn(vb, (kk * jnp.exp(bl - b)).astype(BF16))
            o_ref[sl, :] = o
            r = lax.rsqrt(jnp.mean(o * o, axis=-1, keepdims=True) + EPS)
            og_ref[sl, :] = (o * r * gn_v * (gate * _sigmoid(gate))).astype(BF16)
            return carry

        lax.fori_loop(0, nc, chunk, 0)

    blk = pl.BlockSpec((ct, HG_DK), lambda h, i: (i, h))
    return pl.pallas_call(
        body, name=name, grid=(HG_HEADS, T // ct),
        in_specs=_hgrn_in_specs(ct, lambda i: i),
        out_specs=[blk, blk, pl.BlockSpec((1, nc, HG_DK, HG_DK), lambda h, i: (h, i, 0, 0))],
        out_shape=[jax.ShapeDtypeStruct((T, HG_WIDTH), BF16), jax.ShapeDtypeStruct((T, HG_WIDTH), F32),
                   jax.ShapeDtypeStruct((HG_HEADS, T // CHUNK, HG_DK, HG_DK), F32)],
        scratch_shapes=[pltpu.VMEM((HG_DK, HG_DK), F32)],
        compiler_params=_cp(("parallel", "arbitrary")),
    )(proj, proj, proj, proj, lb, gn)


def hgrn_bwd(proj, lb, gn, o_raw, states, dog, *, ct, name):
    T = proj.shape[0]
    nc = ct // CHUNK
    nblk = T // ct

    def body(hq_ref, hf_ref, hi_ref, hg_ref, lb_ref, gn_ref, o_ref, st_ref, dog_ref,
             dhq_ref, dhf_ref, dhi_ref, dhg_ref, dlb_ref, dgn_ref, ds_ref):
        @pl.when(pl.program_id(1) == 0)
        def _():
            ds_ref[...] = jnp.zeros_like(ds_ref)
            dlb_ref[...] = jnp.zeros_like(dlb_ref)
            dgn_ref[...] = jnp.zeros_like(dgn_ref)

        tri, tri_t, masks = _chunk_consts()
        lb_v = lb_ref[...]
        gn_v = gn_ref[...]
        last_row = lax.broadcasted_iota(jnp.int32, (CHUNK, HG_DK), 0) == CHUNK - 1

        def chunk(ci, carry):
            c = nc - 1 - ci
            sl = pl.ds(pl.multiple_of(c * CHUNK, CHUNK), CHUNK)
            hq = hq_ref[sl, :].astype(F32)
            v = hi_ref[sl, :].astype(F32)
            gate = hg_ref[sl, :].astype(F32)
            sq = _sigmoid(hq)
            q = hq * sq
            sig = _sigmoid(hf_ref[sl, :].astype(F32))
            f = lb_v + (1.0 - lb_v) * sig
            kk = 1.0 - f
            b = jnp.dot(tri, jnp.log(jnp.maximum(f, MIN_F)), precision=_HI, preferred_element_type=F32)
            st = st_ref[0, c]
            dst = ds_ref[...]
            o = o_ref[sl, :]
            r = lax.rsqrt(jnp.mean(o * o, axis=-1, keepdims=True) + EPS)
            xhat = o * r
            sg = _sigmoid(gate)
            dog_v = dog_ref[sl, :].astype(F32)
            dy = dog_v * (gate * sg)
            dhg_ref[sl, :] = (dog_v * xhat * gn_v * (sg * (1.0 + gate * (1.0 - sg)))).astype(BF16)
            dgn_ref[...] += _rowsum8(dy * xhat)
            dxh = dy * gn_v
            do = r * (dxh - xhat * jnp.mean(dxh * xhat, axis=-1, keepdims=True))
            dob = do.astype(BF16)
            vb = v.astype(BF16)
            stb = st.astype(BF16)
            dstb = dst.astype(BF16)
            eb = jnp.exp(b)
            bl = b[CHUNK - 1:CHUNK, :]
            ebl = jnp.exp(bl)
            edec = jnp.exp(bl - b)
            qe = q * eb
            kdec = kk * edec
            a, keep = _intra_off(q, kk, b, masks)
            da = _dot_nt(dob, vb)
            dkdec = _dot(vb, dstb)
            dq = _dot(dob, stb) * eb
            dk = dkdec * edec
            dv = _dot_tn(a.astype(BF16), dob) + _dot_nt(kdec.astype(BF16), dstb)
            for mask, eq, ek, qt, kt in keep:
                dsm = jnp.where(mask, da, 0.0).astype(BF16)
                dq = dq + _dot(dsm, kt) * eq
                dk = dk + _dot_tn(dsm, qt) * ek
            ddq, ddk, ddv = _diag_bwd(q, kk, v, b, do)
            dq = dq + ddq
            dk = dk + ddk
            dhi_ref[sl, :] = (dv + ddv).astype(BF16)
            dbl = jnp.sum(dkdec * kdec, axis=0, keepdims=True) + ebl * jnp.sum(st * dst, axis=0, keepdims=True)
            db = q * dq - kk * dk
            db = jnp.where(last_row, db + dbl, db)
            dlogf = jnp.dot(tri_t, db, precision=_HI, preferred_element_type=F32)
            ds_ref[...] = _dot_tn(dob, qe.astype(BF16)) + dst * ebl
            df = jnp.where(f > MIN_F, dlogf / f, 0.0) - dk
            dhf_ref[sl, :] = (df * (1.0 - lb_v) * sig * (1.0 - sig)).astype(BF16)
            dlb_ref[...] += _rowsum8(df * (1.0 - sig))
            dhq_ref[sl, :] = (dq * (sq * (1.0 + hq * (1.0 - sq)))).astype(BF16)
            return carry

        lax.fori_loop(0, nc, chunk, 0)

    rev = lambda i: nblk - 1 - i
    blk = pl.BlockSpec((ct, HG_DK), lambda h, i: (rev(i), h))
    acc = pl.BlockSpec((8, HG_DK), lambda h, i: (0, h))
    return pl.pallas_call(
        body, name=name, grid=(HG_HEADS, nblk),
        in_specs=_hgrn_in_specs(ct, rev) + [blk, pl.BlockSpec((1, nc, HG_DK, HG_DK), lambda h, i: (h, rev(i), 0, 0)), blk],
        out_specs=[blk, blk, blk, blk, acc, acc],
        out_shape=[jax.ShapeDtypeStruct((T, HG_WIDTH), BF16)] * 4 + [jax.ShapeDtypeStruct((8, HG_WIDTH), F32)] * 2,
        scratch_shapes=[pltpu.VMEM((HG_DK, HG_DK), F32)],
        compiler_params=_cp(("parallel", "arbitrary")),
    )(proj, proj, proj, proj, lb, gn, o_raw, states, dog)


_GW = ATT_GROUP * HEAD_DIM
_NEG = -1e30


def rope_tables(T):
    half = ROPE_DIM // 2
    inv = ROPE_THETA ** (-jnp.arange(half, dtype=F32) * 2.0 / ROPE_DIM)
    ang = jnp.arange(T, dtype=F32)[:, None] * inv[None, :]
    pad1 = jnp.ones((T, HEAD_DIM - ROPE_DIM), F32)
    cos = jnp.concatenate([jnp.cos(ang), jnp.cos(ang), pad1], axis=1)
    sin = jnp.concatenate([jnp.sin(ang), jnp.sin(ang), 0.0 * pad1], axis=1)
    p = np.zeros((_GW, _GW), np.float32)
    for base in range(0, _GW, HEAD_DIM):
        for i in range(half):
            p[base + i + half, base + i] = -1.0
            p[base + i, base + i + half] = 1.0
    reps = _GW // HEAD_DIM
    return jnp.tile(cos, (1, reps)), jnp.tile(sin, (1, reps)), jnp.asarray(p, BF16)


def _rope(x, cos, sin, pm):
    return x.astype(F32) * cos + _dot(x, pm) * sin


def _unrope(dx, cos, sin, pm):
    return dx * cos + _dot_nt((dx * sin).astype(BF16), pm)


def _swa_mask(n):
    r = lax.broadcasted_iota(jnp.int32, (WINDOW, 2 * WINDOW), 0)
    c = lax.broadcasted_iota(jnp.int32, (WINDOW, 2 * WINDOW), 1)
    delta = r + WINDOW - c
    return (delta >= 0) & (delta < WINDOW) & ((c >= WINDOW) | (n > 0))


def _head(x, h):
    return x[:, h * HEAD_DIM:(h + 1) * HEAD_DIM]


def _swa_specs(cur):
    prev = lambda n: jnp.maximum(n - 1, 0)
    kv = lambda col, f: pl.BlockSpec((WINDOW, _GW), lambda n: (f(n), col))
    tab = lambda f: pl.BlockSpec((WINDOW, _GW), lambda n: (f(n), 0))
    return [pl.BlockSpec((WINDOW, ATT_WIDTH), lambda n: (cur(n), COL_AQ // 8)),
            kv(COL_AK // 2, prev), kv(COL_AK // 2, cur), kv(COL_AV // 2, prev), kv(COL_AV // 2, cur),
            tab(prev), tab(cur), tab(prev), tab(cur),
            pl.BlockSpec((_GW, _GW), lambda n: (0, 0)), pl.BlockSpec(memory_space=pltpu.SMEM)]


def swa_fwd(proj, cos, sin, pm, sinks, *, name):
    T = proj.shape[0]
    nb = T // WINDOW
    scale = HEAD_DIM ** -0.5

    def body(q_ref, kp_ref, kc_ref, vp_ref, vc_ref, cp_ref, cc_ref, sp_ref, sc_ref, pm_ref, sink_ref, o_ref):
        n = pl.program_id(0)
        pm_v = pm_ref[...]
        mask = _swa_mask(n)
        k_cat = jnp.concatenate([_rope(kp_ref[...], cp_ref[...], sp_ref[...], pm_v),
                                 _rope(kc_ref[...], cc_ref[...], sc_ref[...], pm_v)], axis=0).astype(BF16)
        v_cat = jnp.concatenate([vp_ref[...], vc_ref[...]], axis=0)
        for g in range(ATT_KV_HEADS):
            qg = _rope(q_ref[:, g * _GW:(g + 1) * _GW], cc_ref[...], sc_ref[...], pm_v).astype(BF16)
            kh, vh = _head(k_cat, g), _head(v_cat, g)
            for hh in range(ATT_GROUP):
                sk = sink_ref[g * ATT_GROUP + hh]
                s = jnp.where(mask, _dot_nt(_head(qg, hh), kh) * scale, _NEG)
                m = jnp.maximum(jnp.max(s, axis=-1, keepdims=True), sk)
                p = jnp.exp(s - m)
                l = jnp.sum(p, axis=-1, keepdims=True) + jnp.exp(sk - m)
                o = _dot(p.astype(BF16), vh) / l
                h = g * ATT_GROUP + hh
                o_ref[:, h * HEAD_DIM:(h + 1) * HEAD_DIM] = o.astype(BF16)

    return pl.pallas_call(
        body, name=name, grid=(nb,), in_specs=_swa_specs(lambda n: n),
        out_specs=pl.BlockSpec((WINDOW, ATT_WIDTH), lambda n: (n, 0)),
        out_shape=jax.ShapeDtypeStruct((T, ATT_WIDTH), BF16), compiler_params=_cp(("parallel",)),
    )(proj, proj, proj, proj, proj, cos, cos, sin, sin, pm, sinks)


def swa_bwd(proj, cos, sin, pm, sinks, att, datt, *, name):
    T = proj.shape[0]
    nb = T // WINDOW
    scale = HEAD_DIM ** -0.5
    clamp = lambda n: jnp.minimum(n, nb - 1)

    def body(q_ref, kp_ref, kc_ref, vp_ref, vc_ref, cp_ref, cc_ref, sp_ref, sc_ref, pm_ref, sink_ref, att_ref, datt_ref,
             dq_ref, dkv_ref, dsink_ref, ck_ref, cv_ref):
        n = pl.program_id(0)
        pm_v = pm_ref[...]

        @pl.when(n == 0)
        def _():
            dsink_ref[...] = jnp.zeros_like(dsink_ref)
            ck_ref[...] = jnp.zeros_like(ck_ref)
            cv_ref[...] = jnp.zeros_like(cv_ref)

        def finish_prev(dk_prev, dv_prev):
            dk = _unrope(ck_ref[...] + dk_prev, cp_ref[...], sp_ref[...], pm_v)
            dkv_ref[:, :_GW] = dk.astype(BF16)
            dkv_ref[:, _GW:] = (cv_ref[...] + dv_prev).astype(BF16)

        @pl.when(n < nb)
        def _():
            mask = _swa_mask(n)
            k_cat = jnp.concatenate([_rope(kp_ref[...], cp_ref[...], sp_ref[...], pm_v),
                                     _rope(kc_ref[...], cc_ref[...], sc_ref[...], pm_v)], axis=0).astype(BF16)
            v_cat = jnp.concatenate([vp_ref[...], vc_ref[...]], axis=0)
            dks, dvs = [], []
            for g in range(ATT_KV_HEADS):
                qg = _rope(q_ref[:, g * _GW:(g + 1) * _GW], cc_ref[...], sc_ref[...], pm_v).astype(BF16)
                kh, vh = _head(k_cat, g), _head(v_cat, g)
                dkg = jnp.zeros((2 * WINDOW, HEAD_DIM), F32)
                dvg = jnp.zeros((2 * WINDOW, HEAD_DIM), F32)
                dqs = []
                for hh in range(ATT_GROUP):
                    h = g * ATT_GROUP + hh
                    sk = sink_ref[h]
                    qh = _head(qg, hh)
                    s = jnp.where(mask, _dot_nt(qh, kh) * scale, _NEG)
                    m = jnp.maximum(jnp.max(s, axis=-1, keepdims=True), sk)
                    e = jnp.exp(s - m)
                    es = jnp.exp(sk - m)
                    inv_l = 1.0 / (jnp.sum(e, axis=-1, keepdims=True) + es)
                    p = e * inv_l
                    do = datt_ref[:, h * HEAD_DIM:(h + 1) * HEAD_DIM]
                    dsum = jnp.sum(do.astype(F32) * att_ref[:, h * HEAD_DIM:(h + 1) * HEAD_DIM].astype(F32),
                                   axis=-1, keepdims=True)
                    dsink_ref[h] += -(es * inv_l) * dsum
                    ds = (p * (_dot_nt(do, vh) - dsum) * scale).astype(BF16)
                    dqs.append(_dot(ds, kh))
                    dkg = dkg + _dot_tn(ds, qh)
                    dvg = dvg + _dot_tn(p.astype(BF16), do)
                dq_ref[:, g * _GW:(g + 1) * _GW] = _unrope(jnp.concatenate(dqs, axis=1), cc_ref[...], sc_ref[...], pm_v).astype(BF16)
                dks.append(dkg)
                dvs.append(dvg)
            dk_all = jnp.concatenate(dks, axis=1)
            dv_all = jnp.concatenate(dvs, axis=1)

            @pl.when(n > 0)
            def _():
                finish_prev(dk_all[:WINDOW], dv_all[:WINDOW])

            ck_ref[...] = dk_all[WINDOW:]
            cv_ref[...] = dv_all[WINDOW:]

        @pl.when(n == nb)
        def _():
            finish_prev(jnp.zeros((WINDOW, _GW), F32), jnp.zeros((WINDOW, _GW), F32))

    row = pl.BlockSpec((WINDOW, ATT_WIDTH), lambda n: (clamp(n), 0))
    return pl.pallas_call(
        body, name=name, grid=(nb + 1,), in_specs=_swa_specs(clamp) + [row, row],
        out_specs=[row, pl.BlockSpec((WINDOW, 2 * _GW), lambda n: (jnp.maximum(n - 1, 0), 0)),
                   pl.BlockSpec((ATT_Q_HEADS, WINDOW, 1), lambda n: (0, 0, 0))],
        out_shape=[jax.ShapeDtypeStruct((T, ATT_WIDTH), BF16), jax.ShapeDtypeStruct((T, 2 * _GW), BF16),
                   jax.ShapeDtypeStruct((ATT_Q_HEADS, WINDOW, 1), F32)],
        scratch_shapes=[pltpu.VMEM((WINDOW, _GW), F32), pltpu.VMEM((WINDOW, _GW), F32)],
        compiler_params=_cp(("arbitrary",)),
    )(proj, proj, proj, proj, proj, cos, cos, sin, sin, pm, sinks, att, datt)


_ANY = pl.BlockSpec(memory_space=pl.ANY)
N_CHIPS = 4


def _place():
    x, y, c = lax.axis_index("x"), lax.axis_index("y"), lax.axis_index("c")
    return x, y, c, [(1 - x, y), (x, 1 - y), (1 - x, 1 - y)]


def gather_from_chips(shards, *, name):
    n = len(shards)

    def body(*refs):
        ins, outs = refs[:n], refs[n:2 * n]
        send_sems, recv_sems, local_sems = refs[2 * n:]
        x, y, c, chips = _place()
        me = 2 * x + y

        def copy(a, j):
            px, py = chips[j]
            return pltpu.make_async_remote_copy(
                src_ref=ins[a], dst_ref=outs[a].at[me], send_sem=send_sems.at[a * 3 + j], recv_sem=recv_sems.at[a * 3 + j],
                device_id=(px, py, c), device_id_type=MESH)

        def arrival(a, j):
            px, py = chips[j]
            return pltpu.make_async_remote_copy(
                src_ref=ins[a], dst_ref=outs[a].at[2 * px + py], send_sem=send_sems.at[a * 3 + j],
                recv_sem=recv_sems.at[a * 3 + j], device_id=(px, py, c), device_id_type=MESH)

        local = [pltpu.make_async_copy(ins[a], outs[a].at[me], local_sems.at[a]) for a in range(n)]
        for a in range(n):
            local[a].start()
            for j in range(3):
                copy(a, j).start()
        for a in range(n):
            for j in range(3):
                arrival(a, j).wait()
            local[a].wait()

    return pl.pallas_call(
        body, name=name,
        out_shape=[jax.ShapeDtypeStruct((N_CHIPS,) + s.shape, s.dtype) for s in shards],
        in_specs=[_ANY] * n, out_specs=[_ANY] * n,
        scratch_shapes=[pltpu.SemaphoreType.DMA((3 * n,)), pltpu.SemaphoreType.DMA((3 * n,)), pltpu.SemaphoreType.DMA((n,))],
    )(*shards)


def scatter_to_chips(parts, *, name):
    n = len(parts)

    def body(*refs):
        ins, outs = refs[:n], refs[n:2 * n]
        send_sems, recv_sems, local_sems = refs[2 * n:]
        x, y, c, chips = _place()
        me = 2 * x + y

        def copy(a, j):
            px, py = chips[j]
            return pltpu.make_async_remote_copy(
                src_ref=ins[a].at[2 * px + py], dst_ref=outs[a].at[me], send_sem=send_sems.at[a * 3 + j],
                recv_sem=recv_sems.at[a * 3 + j], device_id=(px, py, c), device_id_type=MESH)

        def arrival(a, j):
            px, py = chips[j]
            return pltpu.make_async_remote_copy(
                src_ref=ins[a].at[2 * px + py], dst_ref=outs[a].at[2 * px + py], send_sem=send_sems.at[a * 3 + j],
                recv_sem=recv_sems.at[a * 3 + j], device_id=(px, py, c), device_id_type=MESH)

        local = [pltpu.make_async_copy(ins[a].at[me], outs[a].at[me], local_sems.at[a]) for a in range(n)]
        for a in range(n):
            local[a].start()
            for j in range(3):
                copy(a, j).start()
        for a in range(n):
            for j in range(3):
                arrival(a, j).wait()
            local[a].wait()

    return pl.pallas_call(
        body, name=name,
        out_shape=[jax.ShapeDtypeStruct(p.shape, p.dtype) for p in parts],
        in_specs=[_ANY] * n, out_specs=[_ANY] * n,
        scratch_shapes=[pltpu.SemaphoreType.DMA((3 * n,)), pltpu.SemaphoreType.DMA((3 * n,)), pltpu.SemaphoreType.DMA((n,))],
    )(*parts)


def swap_with_sibling(arrays, *, name):
    n = len(arrays)

    def body(*refs):
        ins, outs = refs[:n], refs[n:2 * n]
        send_sems, recv_sems = refs[2 * n:]
        x, y, c, _ = _place()
        copies = [pltpu.make_async_remote_copy(src_ref=ins[a], dst_ref=outs[a], send_sem=send_sems.at[a],
                                               recv_sem=recv_sems.at[a], device_id=(x, y, 1 - c), device_id_type=MESH)
                  for a in range(n)]
        for cp in copies:
            cp.start()
        for cp in copies:
            cp.wait()

    return pl.pallas_call(
        body, name=name, out_shape=[jax.ShapeDtypeStruct(a.shape, a.dtype) for a in arrays],
        in_specs=[_ANY] * n, out_specs=[_ANY] * n,
        scratch_shapes=[pltpu.SemaphoreType.DMA((n,)), pltpu.SemaphoreType.DMA((n,))],
    )(*arrays)


N_DEV = 8
SMALL_ROWS = 24


def allreduce_small(v, *, name):
    def body(v_ref, o_ref, recv_ref, send_sems, recv_sems):
        x, y, c, _ = _place()
        me = 4 * x + 2 * y + c
        recv_ref[me] = v_ref[...]

        def copy(k, slot):
            peer = (x ^ (k >> 2), y ^ ((k >> 1) & 1), c ^ (k & 1))
            return pltpu.make_async_remote_copy(src_ref=v_ref, dst_ref=recv_ref.at[slot], send_sem=send_sems.at[k - 1],
                                                recv_sem=recv_sems.at[k - 1], device_id=peer, device_id_type=MESH)

        for k in range(1, N_DEV):
            copy(k, me).start()
        for k in range(1, N_DEV):
            copy(k, me ^ k).wait()
        acc = recv_ref[0]
        for d in range(1, N_DEV):
            acc = acc + recv_ref[d]
        o_ref[...] = acc

    vm = pl.BlockSpec(memory_space=pltpu.VMEM)
    return pl.pallas_call(
        body, name=name, out_shape=jax.ShapeDtypeStruct(v.shape, v.dtype), in_specs=[vm], out_specs=vm,
        scratch_shapes=[pltpu.VMEM((N_DEV,) + v.shape, v.dtype), pltpu.SemaphoreType.DMA((N_DEV - 1,)),
                        pltpu.SemaphoreType.DMA((N_DEV - 1,))],
    )(v)


def sum_parts(parts, *, tr, name):
    _, R, C = parts.shape

    def body(p_ref, o_ref):
        acc = p_ref[0].astype(F32)
        for q in range(1, N_CHIPS):
            acc = acc + p_ref[q].astype(F32)
        o_ref[...] = acc

    return pl.pallas_call(
        body, name=name, grid=(R // tr,), in_specs=[pl.BlockSpec((N_CHIPS, tr, C), lambda i: (0, i, 0))],
        out_specs=pl.BlockSpec((tr, C), lambda i: (i, 0)), out_shape=jax.ShapeDtypeStruct((R, C), F32),
        compiler_params=_cp(("parallel",)),
    )(parts)


def adamw(g_a, g_b, w, m, v, *, tr, name):
    R, C = w.shape
    c1 = 1.0 - ADAM_B1 ** ADAM_STEP
    c2 = 1.0 - ADAM_B2 ** ADAM_STEP
    ng = 1 if g_b is None else 2

    def body(*refs):
        w_ref, m_ref, v_ref, g_ref, d_ref, nm_ref, nv_ref = refs[ng:]
        g = refs[0][...] if ng == 1 else refs[0][...] + refs[1][...]
        nm = ADAM_B1 * m_ref[...] + (1.0 - ADAM_B1) * g
        nv = ADAM_B2 * v_ref[...] + (1.0 - ADAM_B2) * (g * g)
        g_ref[...] = g
        nm_ref[...] = nm
        nv_ref[...] = nv
        d_ref[...] = -ADAM_LR * ((nm / c1) / (jnp.sqrt(nv / c2) + ADAM_EPS) + ADAM_WD * w_ref[...])

    blk = pl.BlockSpec((tr, C), lambda i: (i, 0))
    gs = [g_a] if g_b is None else [g_a, g_b]
    return pl.pallas_call(
        body, name=name, grid=(R // tr,), in_specs=[blk] * (ng + 3), out_specs=[blk] * 4,
        out_shape=[jax.ShapeDtypeStruct((R, C), F32)] * 4, compiler_params=_cp(("parallel",)),
    )(*gs, w, m, v)


def _lb_bounds(lb_logits):
    p = jax.nn.softmax(lb_logits.astype(F32), axis=0)
    return jnp.cumsum(p, axis=0) - p[0:1]


def _pack_small(n1, lb, hgn, n2, fin, sinks, extra):
    row = jnp.concatenate([sinks.reshape(-1), jnp.reshape(extra, (1,)),
                           jnp.zeros((D_MODEL - sinks.size - 1,), F32)])
    return jnp.concatenate([n1, lb, hgn, n2, fin[None], row[None], jnp.zeros((SMALL_ROWS - 18, D_MODEL), F32)], axis=0)


def _unpack_small(buf):
    return (buf[0:4], buf[4:8], buf[8:12], buf[12:16], buf[16], buf[17, :DEPTH * ATT_Q_HEADS].reshape(DEPTH, ATT_Q_HEADS),
            buf[17, DEPTH * ATT_Q_HEADS])


def _cols_full(g):
    q, l, r, c = g.shape
    return jnp.transpose(g, (1, 2, 0, 3)).reshape(l, r, q * c)


def _rows_full(g):
    q, l, r, c = g.shape
    return jnp.transpose(g, (1, 0, 2, 3)).reshape(l, q * r, c)


def _cols_split(w):
    l, r, qc = w.shape
    return jnp.transpose(w.reshape(l, r, N_CHIPS, qc // N_CHIPS), (2, 0, 1, 3))


def _rows_split(w):
    l, qr, c = w.shape
    return jnp.transpose(w.reshape(l, N_CHIPS, qr // N_CHIPS, c), (1, 0, 2, 3))


def kernel(x, norm1, w_in, lb_logits, hg_norm, attn_sinks, w_pa, w_pb, w_o, norm2, w_gate, w_up, w_down, final_norm, loss_target, m_norm1, m_w_in, m_lb_logits, m_hg_norm, m_attn_sinks, m_w_pa, m_w_pb, m_w_o, m_norm2, m_w_gate, m_w_up, m_w_down, m_final_norm, v_norm1, v_w_in, v_lb_logits, v_hg_norm, v_attn_sinks, v_w_pa, v_w_pb, v_w_o, v_norm2, v_w_gate, v_w_up, v_w_down, v_final_norm):
    T = x.shape[1]
    H = FFN_HIDDEN
    tm = min(512, T)
    big = dict(w_in=w_in, w_pa=w_pa, w_pb=w_pb, w_o=w_o, w_gate=w_gate, w_up=w_up, w_down=w_down)
    col_sharded = ("w_in", "w_gate", "w_up")

    gathered = gather_from_chips([w.astype(BF16) for w in big.values()], name="gather_weights")
    full = {k: (_cols_full(g) if k in col_sharded else _rows_full(g)) for k, g in zip(big, gathered)}
    wgu = jnp.concatenate([full["w_gate"], full["w_up"]], axis=-1)

    lb_all, lb_vjp = jax.vjp(_lb_bounds, lb_logits)
    cos, sin, pm = rope_tables(T)

    xs = x[0]
    saved = []
    for l in range(DEPTH):
        h, proj = ln_matmul(xs, norm1[l:l + 1], full["w_in"][l], tm=tm, tn=1536, name=f"in_proj_{l}")
        og, o_raw, states = hgrn_fwd(proj, lb_all[l:l + 1], hg_norm[l:l + 1], ct=tm, name=f"hgrn_fwd_{l}")
        att = swa_fwd(proj, cos, sin, pm, attn_sinks[l], name=f"swa_fwd_{l}")
        ya, yb, mix = merge_fwd(og, att, proj, full["w_pa"][l], full["w_pb"][l], tm=tm, name=f"merge_fwd_{l}")
        x1 = matmul_residual(xs, mix, full["w_o"][l], tm=tm, name=f"out_proj_{l}")
        h2, gu, act = ln_matmul(x1, norm2[l:l + 1], wgu[l], tm=min(256, T), tn=2 * H, name=f"ffn_up_{l}", swiglu=True)
        x2 = matmul_residual(x1, act, full["w_down"][l], tm=tm, name=f"ffn_down_{l}")
        saved.append((xs, h, proj, og, o_raw, states, att, ya, yb, mix, x1, h2, gu, act))
        xs = x2

    dx, loss_parts, dfin_parts = loss_head(xs, final_norm[None], loss_target[0], tm=tm, name="loss_head")

    dw = {k: [None] * DEPTH for k in big}
    dn1, dn2, dlb, dgn, dsk = ([None] * DEPTH for _ in range(5))
    for l in reversed(range(DEPTH)):
        xs, h, proj, og, o_raw, states, att, ya, yb, mix, x1, h2, gu, act = saved[l]
        dgu = ffn_down_bwd(dx, full["w_down"][l], gu, tm=min(256, T), name=f"ffn_down_bwd_{l}")
        dw["w_down"][l] = matmul_tn(act, dx, tka=H // 2, tn=D_MODEL, tk=tm, name=f"dw_down_{l}")
        dwgu = matmul_tn(h2, dgu, tka=D_MODEL, tn=H // 2, tk=tm, name=f"dw_gu_{l}")
        dw["w_gate"][l], dw["w_up"][l] = dwgu[:, :H], dwgu[:, H:]
        dx1, dn2_parts = matmul_nt_rmsbwd(dgu, wgu[l], x1, norm2[l:l + 1], dx, tm=tm, tk=H // 2, name=f"ffn_up_bwd_{l}")
        dya, dyb, dgate = merge_bwd(dx1, full["w_o"][l], proj, ya, yb, tm=tm, name=f"merge_bwd_{l}")
        dw["w_o"][l] = matmul_tn(mix, dx1, tka=D_MODEL, tn=D_MODEL, tk=tm, name=f"dw_o_{l}")
        dog = matmul_nt(dya, full["w_pa"][l], tm=tm, name=f"pa_bwd_{l}")
        datt = matmul_nt(dyb, full["w_pb"][l], tm=tm, name=f"pb_bwd_{l}")
        dw["w_pa"][l] = matmul_tn(og, dya, tka=D_MODEL, tn=D_MODEL, tk=tm, name=f"dw_pa_{l}")
        dw["w_pb"][l] = matmul_tn(att, dyb, tka=D_MODEL, tn=D_MODEL, tk=tm, name=f"dw_pb_{l}")
        daq, dkv, dsk_parts = swa_bwd(proj, cos, sin, pm, attn_sinks[l], att, datt, name=f"swa_bwd_{l}")
        dhq, dhf, dhi, dhg, dlb_parts, dgn_parts = hgrn_bwd(proj, lb_all[l:l + 1], hg_norm[l:l + 1], o_raw, states, dog,
                                                            ct=tm, name=f"hgrn_bwd_{l}")
        dproj = jnp.concatenate([dhq, dhf, dhi, dhg, daq, dkv, dgate], axis=1)
        dw["w_in"][l] = matmul_tn(h, dproj, tka=D_MODEL, tn=1536, tk=tm, name=f"dw_in_{l}")
        dx, dn1_parts = matmul_nt_rmsbwd(dproj, full["w_in"][l], xs, norm1[l:l + 1], dx1, tm=tm, tk=1536, name=f"in_proj_bwd_{l}")
        dn1[l], dn2[l], dlb[l], dgn[l] = dn1_parts.sum(0), dn2_parts.sum(0), dlb_parts.sum(0), dgn_parts.sum(0)
        dsk[l] = dsk_parts.sum((1, 2))
    grad_x = dx[None]

    mine = _pack_small(jnp.stack(dn1), jnp.stack(dlb), jnp.stack(dgn), jnp.stack(dn2), dfin_parts.sum(0), jnp.stack(dsk),
                       loss_parts.sum())
    total = allreduce_small(mine, name="allreduce_small")
    g_n1, g_lb_all, g_gn, g_n2, g_fin, g_sk, loss = _unpack_small(total)
    (g_lb,) = lb_vjp(g_lb_all)
    zero = jnp.zeros((), F32)
    small = adamw(_pack_small(g_n1, g_lb, g_gn, g_n2, g_fin, g_sk, zero), None,
                  _pack_small(norm1, lb_logits, hg_norm, norm2, final_norm, attn_sinks, zero),
                  _pack_small(m_norm1, m_lb_logits, m_hg_norm, m_norm2, m_final_norm, m_attn_sinks, zero),
                  _pack_small(v_norm1, v_lb_logits, v_hg_norm, v_norm2, v_final_norm, v_attn_sinks, zero),
                  tr=SMALL_ROWS, name="adamw_small")
    small = [_unpack_small(s)[:6] for s in small]

    parts = [(_cols_split if k in col_sharded else _rows_split)(jnp.stack(dw[k])) for k in big]
    arrived = scatter_to_chips(parts, name="scatter_grads")
    sums = []
    for k, a in zip(big, arrived):
        q, nl, r, c = a.shape
        sums.append(sum_parts(a.reshape(q, nl * r, c), tr=128, name=f"sum_{k}"))
    theirs = swap_with_sibling(sums, name="swap_sums")
    mom = dict(w_in=(m_w_in, v_w_in), w_pa=(m_w_pa, v_w_pa), w_pb=(m_w_pb, v_w_pb), w_o=(m_w_o, v_w_o),
               w_gate=(m_w_gate, v_w_gate), w_up=(m_w_up, v_w_up), w_down=(m_w_down, v_w_down))
    upd = {}
    for k, mine_k, theirs_k in zip(big, sums, theirs):
        shp = big[k].shape
        flat = lambda t: t.reshape(shp[0] * shp[1], shp[2])
        res = adamw(mine_k, theirs_k, flat(big[k]), flat(mom[k][0]), flat(mom[k][1]), tr=128, name=f"adamw_{k}")
        upd[k] = [t.reshape(shp) for t in res]

    order = ("norm1", "w_in", "lb_logits", "hg_norm", "attn_sinks", "w_pa", "w_pb", "w_o", "norm2", "w_gate", "w_up",
             "w_down", "final_norm")
    small_pos = dict(norm1=0, lb_logits=1, hg_norm=2, norm2=3, final_norm=4, attn_sinks=5)
    outs = [loss, grad_x]
    for kind in range(4):
        for name in order:
            outs.append(upd[name][kind] if name in upd else small[kind][small_pos[name]])
    return tuple(outs)
```

```python
import functools

import jax
import jax.numpy as jnp
import numpy as np
from jax import lax
from jax.experimental import pallas as pl
from jax.experimental.pallas import tpu as pltpu

F32 = jnp.float32
BF16 = jnp.bfloat16

D_MODEL = 1024
DEPTH = 4
HG_HEADS = 8
HG_DK = 128
HG_WIDTH = HG_HEADS * HG_DK
CHUNK = 64
ATT_Q_HEADS = 16
ATT_KV_HEADS = 4
ATT_GROUP = ATT_Q_HEADS // ATT_KV_HEADS
HEAD_DIM = 64
ATT_WIDTH = ATT_Q_HEADS * HEAD_DIM
ATT_KV_WIDTH = ATT_KV_HEADS * HEAD_DIM
WINDOW = 128
ROPE_THETA = 500000.0
ROPE_DIM = HEAD_DIM // 4
FFN_HIDDEN = 2816
IN_COLS = 4 * HG_WIDTH + ATT_WIDTH + 2 * ATT_KV_WIDTH + 2 * D_MODEL
EPS = 1e-6
MIN_F = 1e-30
COL_HQ, COL_HF, COL_HI, COL_HG = 0, 8, 16, 24
COL_GA, COL_GB, COL_AQ, COL_AK, COL_AV = 32, 40, 48, 56, 58


def _to_kernel_cols(w):
    return jnp.concatenate([w[..., :4096], w[..., 5632:7680], w[..., 4096:5120], w[..., 5120:5632]], axis=-1)


def _from_kernel_cols(w):
    return jnp.concatenate([w[..., :4096], w[..., 6144:7168], w[..., 7168:7680], w[..., 4096:6144]], axis=-1)

ADAM_LR = 0.001
ADAM_B1 = 0.9
ADAM_B2 = 0.999
ADAM_EPS = 1e-08
ADAM_WD = 0.01
ADAM_STEP = 10

VMEM_LIMIT_V7X = 56 * 1024 * 1024
MESH = pl.DeviceIdType.MESH


def _cp(sem, vmem=VMEM_LIMIT_V7X):
    return pltpu.CompilerParams(dimension_semantics=sem, vmem_limit_bytes=vmem)


def _sigmoid(x):
    return 1.0 / (1.0 + jnp.exp(-x))


def _dot(a, b):
    return jnp.dot(a, b, preferred_element_type=F32)


def _dot_nt(a, b):
    return lax.dot_general(a, b, (((1,), (1,)), ((), ())), preferred_element_type=F32)


def _dot_tn(a, b):
    return lax.dot_general(a, b, (((0,), (0,)), ((), ())), preferred_element_type=F32)


def _rowsum8(v):
    r, n = v.shape
    return jnp.sum(v.reshape(r // 8, 8, n), axis=0)


def ln_matmul(x, g, w, *, tm, tn, name, swiglu=False):
    T, Dm = x.shape
    N = w.shape[1]
    if swiglu:
        assert tn == N
    half = N // 2

    def body(x_ref, g_ref, w_ref, h_ref, o_ref, *rest):
        @pl.when(pl.program_id(1) == 0)
        def _():
            xf = x_ref[...]
            r = lax.rsqrt(jnp.mean(xf * xf, axis=-1, keepdims=True) + EPS)
            h_ref[...] = (xf * r * g_ref[...]).astype(BF16)

        acc = _dot(h_ref[...], w_ref[...])
        o_ref[...] = acc.astype(BF16)
        if swiglu:
            gt, up = acc[:, :half], acc[:, half:]
            rest[0][...] = (gt * _sigmoid(gt) * up).astype(BF16)

    out_shape = [jax.ShapeDtypeStruct((T, Dm), BF16), jax.ShapeDtypeStruct((T, N), BF16)]
    out_specs = [pl.BlockSpec((tm, Dm), lambda i, j: (i, 0)), pl.BlockSpec((tm, tn), lambda i, j: (i, j))]
    if swiglu:
        out_shape.append(jax.ShapeDtypeStruct((T, half), BF16))
        out_specs.append(pl.BlockSpec((tm, half), lambda i, j: (i, 0)))
    return pl.pallas_call(
        body, name=name, grid=(T // tm, N // tn),
        in_specs=[pl.BlockSpec((tm, Dm), lambda i, j: (i, 0)), pl.BlockSpec((1, Dm), lambda i, j: (0, 0)),
                  pl.BlockSpec((Dm, tn), lambda i, j: (0, j))],
        out_specs=out_specs, out_shape=out_shape, compiler_params=_cp(("parallel", "arbitrary")),
    )(x, g, w)


def matmul_residual(x, a, w, *, tm, name):
    T, N = x.shape
    K = a.shape[1]

    def body(x_ref, a_ref, w_ref, o_ref):
        o_ref[...] = x_ref[...] + _dot(a_ref[...], w_ref[...])

    return pl.pallas_call(
        body, name=name, grid=(T // tm,),
        in_specs=[pl.BlockSpec((tm, N), lambda i: (i, 0)), pl.BlockSpec((tm, K), lambda i: (i, 0)),
                  pl.BlockSpec((K, N), lambda i: (0, 0))],
        out_specs=pl.BlockSpec((tm, N), lambda i: (i, 0)), out_shape=jax.ShapeDtypeStruct((T, N), F32),
        compiler_params=_cp(("parallel",)),
    )(x, a, w)


def matmul_nt(a, w, *, tm, name):
    T, K = a.shape
    N = w.shape[0]

    def body(a_ref, w_ref, o_ref):
        o_ref[...] = _dot_nt(a_ref[...], w_ref[...]).astype(BF16)

    return pl.pallas_call(
        body, name=name, grid=(T // tm,),
        in_specs=[pl.BlockSpec((tm, K), lambda i: (i, 0)), pl.BlockSpec((N, K), lambda i: (0, 0))],
        out_specs=pl.BlockSpec((tm, N), lambda i: (i, 0)), out_shape=jax.ShapeDtypeStruct((T, N), BF16),
        compiler_params=_cp(("parallel",)),
    )(a, w)


def matmul_tn(a, g, *, tka, tn, tk, name):
    T, Ka = a.shape
    N = g.shape[1]
    nk = T // tk

    def body(a_ref, g_ref, o_ref, acc_ref):
        k = pl.program_id(2)

        @pl.when(k == 0)
        def _():
            acc_ref[...] = jnp.zeros_like(acc_ref)

        acc_ref[...] += _dot_tn(a_ref[...].astype(BF16), g_ref[...].astype(BF16))

        @pl.when(k == nk - 1)
        def _():
            o_ref[...] = acc_ref[...].astype(BF16)

    return pl.pallas_call(
        body, name=name, grid=(Ka // tka, N // tn, nk),
        in_specs=[pl.BlockSpec((tk, tka), lambda i, j, k: (k, i)), pl.BlockSpec((tk, tn), lambda i, j, k: (k, j))],
        out_specs=pl.BlockSpec((tka, tn), lambda i, j, k: (i, j)), out_shape=jax.ShapeDtypeStruct((Ka, N), BF16),
        scratch_shapes=[pltpu.VMEM((tka, tn), F32)],
        compiler_params=_cp(("parallel", "parallel", "arbitrary")),
    )(a, g)


def matmul_nt_rmsbwd(a, w, x, g, dres, *, tm, tk, name):
    T, K = a.shape
    Dm = w.shape[0]
    nk = K // tk

    def body(a_ref, w_ref, x_ref, g_ref, dres_ref, dx_ref, dg_ref, acc_ref):
        i, k = pl.program_id(0), pl.program_id(1)

        @pl.when(k == 0)
        def _():
            acc_ref[...] = jnp.zeros_like(acc_ref)

        @pl.when((i == 0) & (k == 0))
        def _():
            dg_ref[...] = jnp.zeros_like(dg_ref)

        acc_ref[...] += _dot_nt(a_ref[...], w_ref[...])

        @pl.when(k == nk - 1)
        def _():
            dh = acc_ref[...]
            xf = x_ref[...]
            r = lax.rsqrt(jnp.mean(xf * xf, axis=-1, keepdims=True) + EPS)
            xhat = xf * r
            dg_ref[...] += _rowsum8(dh * xhat)
            dxhat = dh * g_ref[...]
            dx_ref[...] = dres_ref[...] + r * (dxhat - xhat * jnp.mean(dxhat * xhat, axis=-1, keepdims=True))

    return pl.pallas_call(
        body, name=name, grid=(T // tm, nk),
        in_specs=[pl.BlockSpec((tm, tk), lambda i, k: (i, k)), pl.BlockSpec((Dm, tk), lambda i, k: (0, k)),
                  pl.BlockSpec((tm, Dm), lambda i, k: (i, 0)), pl.BlockSpec((1, Dm), lambda i, k: (0, 0)),
                  pl.BlockSpec((tm, Dm), lambda i, k: (i, 0))],
        out_specs=[pl.BlockSpec((tm, Dm), lambda i, k: (i, 0)), pl.BlockSpec((8, Dm), lambda i, k: (0, 0))],
        out_shape=[jax.ShapeDtypeStruct((T, Dm), F32), jax.ShapeDtypeStruct((8, Dm), F32)],
        scratch_shapes=[pltpu.VMEM((tm, Dm), F32)],
        compiler_params=_cp(("arbitrary", "arbitrary")),
    )(a, w, x, g, dres)


def ffn_down_bwd(dx, w_down, gu, *, tm, name):
    T, Dm = dx.shape
    H = w_down.shape[0]

    def body(dx_ref, w_ref, gu_ref, o_ref):
        dact = _dot_nt(dx_ref[...].astype(BF16), w_ref[...])
        gt = gu_ref[:, :H].astype(F32)
        up = gu_ref[:, H:].astype(F32)
        s = _sigmoid(gt)
        o_ref[:, :H] = (dact * up * (s * (1.0 + gt * (1.0 - s)))).astype(BF16)
        o_ref[:, H:] = (dact * gt * s).astype(BF16)

    return pl.pallas_call(
        body, name=name, grid=(T // tm,),
        in_specs=[pl.BlockSpec((tm, Dm), lambda i: (i, 0)), pl.BlockSpec((H, Dm), lambda i: (0, 0)),
                  pl.BlockSpec((tm, 2 * H), lambda i: (i, 0))],
        out_specs=pl.BlockSpec((tm, 2 * H), lambda i: (i, 0)), out_shape=jax.ShapeDtypeStruct((T, 2 * H), BF16),
        compiler_params=_cp(("parallel",)),
    )(dx, w_down, gu)


def _gate_specs(tm):
    return [pl.BlockSpec((tm, D_MODEL), lambda i, c=c: (i, c)) for c in (COL_GA // 8, COL_GB // 8)]


def merge_fwd(og, att, proj, w_pa, w_pb, *, tm, name):
    T, Dm = og.shape

    def body(og_ref, att_ref, ga_ref, gb_ref, wa_ref, wb_ref, ya_ref, yb_ref, mix_ref):
        ya = _dot(og_ref[...], wa_ref[...])
        yb = _dot(att_ref[...], wb_ref[...])
        ya_ref[...] = ya.astype(BF16)
        yb_ref[...] = yb.astype(BF16)
        mix_ref[...] = (_sigmoid(ga_ref[...].astype(F32)) * ya + _sigmoid(gb_ref[...].astype(F32)) * yb).astype(BF16)

    row = pl.BlockSpec((tm, Dm), lambda i: (i, 0))
    wsp = pl.BlockSpec((Dm, Dm), lambda i: (0, 0))
    return pl.pallas_call(
        body, name=name, grid=(T // tm,),
        in_specs=[row, row] + _gate_specs(tm) + [wsp, wsp],
        out_specs=[row, row, row], out_shape=[jax.ShapeDtypeStruct((T, Dm), BF16)] * 3,
        compiler_params=_cp(("parallel",)),
    )(og, att, proj, proj, w_pa, w_pb)


def merge_bwd(dx, w_o, proj, ya, yb, *, tm, name):
    T, Dm = dx.shape

    def body(dx_ref, w_ref, ga_ref, gb_ref, ya_ref, yb_ref, dya_ref, dyb_ref, dgate_ref):
        dmix = _dot_nt(dx_ref[...].astype(BF16), w_ref[...])
        sa = _sigmoid(ga_ref[...].astype(F32))
        sb = _sigmoid(gb_ref[...].astype(F32))
        dya_ref[...] = (dmix * sa).astype(BF16)
        dyb_ref[...] = (dmix * sb).astype(BF16)
        dgate_ref[:, :Dm] = (dmix * ya_ref[...].astype(F32) * sa * (1.0 - sa)).astype(BF16)
        dgate_ref[:, Dm:] = (dmix * yb_ref[...].astype(F32) * sb * (1.0 - sb)).astype(BF16)

    row = pl.BlockSpec((tm, Dm), lambda i: (i, 0))
    return pl.pallas_call(
        body, name=name, grid=(T // tm,),
        in_specs=[row, pl.BlockSpec((Dm, Dm), lambda i: (0, 0))] + _gate_specs(tm) + [row, row],
        out_specs=[row, row, pl.BlockSpec((tm, 2 * Dm), lambda i: (i, COL_GA // 16))],
        out_shape=[jax.ShapeDtypeStruct((T, Dm), BF16)] * 2 + [jax.ShapeDtypeStruct((T, IN_COLS), BF16)],
        compiler_params=_cp(("parallel",)),
    )(dx, w_o, proj, proj, ya, yb)


def loss_head(x, g, target, *, tm, name):
    T, Dm = x.shape

    def body(x_ref, g_ref, t_ref, dx_ref, l_ref, dg_ref):
        @pl.when(pl.program_id(0) == 0)
        def _():
            l_ref[...] = jnp.zeros_like(l_ref)
            dg_ref[...] = jnp.zeros_like(dg_ref)

        xf = x_ref[...]
        gv = g_ref[...]
        r = lax.rsqrt(jnp.mean(xf * xf, axis=-1, keepdims=True) + EPS)
        xhat = xf * r
        err = xhat * gv - t_ref[...]
        l_ref[...] += _rowsum8(err * err) * (0.5 / Dm)
        dy = err * (1.0 / Dm)
        dg_ref[...] += _rowsum8(dy * xhat)
        dxhat = dy * gv
        dx_ref[...] = r * (dxhat - xhat * jnp.mean(dxhat * xhat, axis=-1, keepdims=True))

    row = pl.BlockSpec((tm, Dm), lambda i: (i, 0))
    acc = pl.BlockSpec((8, Dm), lambda i: (0, 0))
    return pl.pallas_call(
        body, name=name, grid=(T // tm,),
        in_specs=[row, pl.BlockSpec((1, Dm), lambda i: (0, 0)), row],
        out_specs=[row, acc, acc],
        out_shape=[jax.ShapeDtypeStruct((T, Dm), F32), jax.ShapeDtypeStruct((8, Dm), F32), jax.ShapeDtypeStruct((8, Dm), F32)],
        compiler_params=_cp(("arbitrary",)),
    )(x, g, target)


_LEVELS = (32, 16, 8)
_DIAG = 8
_HI = lax.Precision.HIGHEST


def _chunk_consts():
    t = lax.broadcasted_iota(jnp.int32, (CHUNK, CHUNK), 0)
    s = lax.broadcasted_iota(jnp.int32, (CHUNK, CHUNK), 1)
    tri = (s <= t).astype(F32)
    tri_t = (s >= t).astype(F32)
    masks = []
    for m in _LEVELS:
        sh = int(np.log2(2 * m))
        masks.append(((t >> sh) == (s >> sh)) & ((t & (2 * m - 1)) >= m) & ((s & (2 * m - 1)) < m))
    return tri, tri_t, masks


def _level_ref(b, m):
    pieces = [jnp.broadcast_to(b[p * 2 * m + m - 1:p * 2 * m + m, :], (2 * m, b.shape[1])) for p in range(CHUNK // (2 * m))]
    return pieces[0] if len(pieces) == 1 else jnp.concatenate(pieces, axis=0)


def _intra_off(q, kk, b, masks):
    a = jnp.zeros((CHUNK, CHUNK), F32)
    keep = []
    for m, mask in zip(_LEVELS, masks):
        ref = _level_ref(b, m)
        eq = jnp.exp(jnp.minimum(b - ref, 0.0))
        ek = jnp.exp(jnp.minimum(ref - b, 0.0))
        qt = (q * eq).astype(BF16)
        kt = (kk * ek).astype(BF16)
        a = a + jnp.where(mask, _dot_nt(qt, kt), 0.0)
        keep.append((mask, eq, ek, qt, kt))
    return a, keep


def _diag_fwd(q, kk, v, b):
    rows = lax.broadcasted_iota(jnp.int32, (_DIAG, HG_DK), 0)
    outs = []
    for i in range(CHUNK // _DIAG):
        sl = slice(i * _DIAG, (i + 1) * _DIAG)
        qb, kb, vb, bb = q[sl], kk[sl], v[sl], b[sl]
        acc = jnp.zeros((_DIAG, HG_DK), F32)
        for j in range(_DIAG):
            dec = jnp.where(rows >= j, jnp.exp(jnp.minimum(bb - bb[j:j + 1], 0.0)), 0.0)
            a = jnp.sum(qb * dec * kb[j:j + 1], axis=-1, keepdims=True)
            acc = acc + a * vb[j:j + 1]
        outs.append(acc)
    return jnp.concatenate(outs, axis=0)


def _diag_bwd(q, kk, v, b, do):
    rows = lax.broadcasted_iota(jnp.int32, (_DIAG, HG_DK), 0)
    dqs, dks, dvs = [], [], []
    for i in range(CHUNK // _DIAG):
        sl = slice(i * _DIAG, (i + 1) * _DIAG)
        qb, kb, vb, bb, dob = q[sl], kk[sl], v[sl], b[sl], do[sl]
        dqb = jnp.zeros((_DIAG, HG_DK), F32)
        dkb = jnp.zeros((_DIAG, HG_DK), F32)
        dvb = jnp.zeros((_DIAG, HG_DK), F32)
        for j in range(_DIAG):
            dec = jnp.where(rows >= j, jnp.exp(jnp.minimum(bb - bb[j:j + 1], 0.0)), 0.0)
            qd = qb * dec
            a = jnp.sum(qd * kb[j:j + 1], axis=-1, keepdims=True)
            da = jnp.sum(dob * vb[j:j + 1], axis=-1, keepdims=True)
            dqb = dqb + da * dec * kb[j:j + 1]
            dkb = jnp.where(rows == j, jnp.sum(da * qd, axis=0, keepdims=True), dkb)
            dvb = jnp.where(rows == j, jnp.sum(a * dob, axis=0, keepdims=True), dvb)
        dqs.append(dqb)
        dks.append(dkb)
        dvs.append(dvb)
    return jnp.concatenate(dqs, axis=0), jnp.concatenate(dks, axis=0), jnp.concatenate(dvs, axis=0)


def _hgrn_in_specs(ct, hp, order):
    def spec(col):
        return pl.BlockSpec((ct, hp * HG_DK), lambda h, i, col=col: (order(i), col // hp + h))
    vec = pl.BlockSpec((1, hp * HG_DK), lambda h, i: (0, h))
    return [spec(COL_HQ), spec(COL_HF), spec(COL_HI), spec(COL_HG), vec, vec]


def hgrn_fwd(proj, lb, gn, *, ct, hp, name):
    T = proj.shape[0]
    nc = ct // CHUNK

    def body(hq_ref, hf_ref, hi_ref, hg_ref, lb_ref, gn_ref, og_ref, o_ref, st_ref, s_ref):
        @pl.when(pl.program_id(1) == 0)
        def _():
            s_ref[...] = jnp.zeros_like(s_ref)

        tri, _, masks = _chunk_consts()

        def chunk(c, carry):
            sl = pl.ds(pl.multiple_of(c * CHUNK, CHUNK), CHUNK)
            for p in range(hp):
                cols = slice(p * HG_DK, (p + 1) * HG_DK)
                lb_v = lb_ref[:, cols]
                hq = hq_ref[sl, cols].astype(F32)
                v = hi_ref[sl, cols].astype(F32)
                gate = hg_ref[sl, cols].astype(F32)
                q = hq * _sigmoid(hq)
                f = lb_v + (1.0 - lb_v) * _sigmoid(hf_ref[sl, cols].astype(F32))
                kk = 1.0 - f
                b = jnp.dot(tri, jnp.log(jnp.maximum(f, MIN_F)), precision=_HI, preferred_element_type=F32)
                st = s_ref[p]
                st_ref[p, c] = st
                a, _ = _intra_off(q, kk, b, masks)
                vb = v.astype(BF16)
                o = (_dot_nt((q * jnp.exp(b)).astype(BF16), st.astype(BF16)) + _dot(a.astype(BF16), vb)
                     + _diag_fwd(q, kk, v, b))
                bl = b[CHUNK - 1:CHUNK, :]
                s_ref[p] = st * jnp.exp(bl) + _dot_tn(vb, (kk * jnp.exp(bl - b)).astype(BF16))
                o_ref[sl, cols] = o
                r = lax.rsqrt(jnp.mean(o * o, axis=-1, keepdims=True) + EPS)
                og_ref[sl, cols] = (o * r * gn_ref[:, cols] * (gate * _sigmoid(gate))).astype(BF16)
            return carry

        lax.fori_loop(0, nc, chunk, 0)

    blk = pl.BlockSpec((ct, hp * HG_DK), lambda h, i: (i, h))
    return pl.pallas_call(
        body, name=name, grid=(HG_HEADS // hp, T // ct),
        in_specs=_hgrn_in_specs(ct, hp, lambda i: i),
        out_specs=[blk, blk, pl.BlockSpec((hp, nc, HG_DK, HG_DK), lambda h, i: (h, i, 0, 0))],
        out_shape=[jax.ShapeDtypeStruct((T, HG_WIDTH), BF16), jax.ShapeDtypeStruct((T, HG_WIDTH), F32),
                   jax.ShapeDtypeStruct((HG_HEADS, T // CHUNK, HG_DK, HG_DK), F32)],
        scratch_shapes=[pltpu.VMEM((hp, HG_DK, HG_DK), F32)],
        compiler_params=_cp(("parallel", "arbitrary")),
    )(proj, proj, proj, proj, lb, gn)


def hgrn_bwd(proj, lb, gn, o_raw, states, dog, dproj, *, ct, name):
    T = proj.shape[0]
    nc = ct // CHUNK
    nblk = T // ct
    hp = HG_HEADS

    def body(hq_ref, hf_ref, hi_ref, hg_ref, lb_ref, gn_ref, o_ref, st_ref, dog_ref, _dproj_in,
             dp_ref, dlb_ref, dgn_ref, ds_ref):
        dhq_ref, dhf_ref, dhi_ref, dhg_ref = (dp_ref.at[:, s * HG_WIDTH:(s + 1) * HG_WIDTH] for s in range(4))
        @pl.when(pl.program_id(1) == 0)
        def _():
            ds_ref[...] = jnp.zeros_like(ds_ref)
            dlb_ref[...] = jnp.zeros_like(dlb_ref)
            dgn_ref[...] = jnp.zeros_like(dgn_ref)

        tri, tri_t, masks = _chunk_consts()
        last_row = lax.broadcasted_iota(jnp.int32, (CHUNK, HG_DK), 0) == CHUNK - 1

        def chunk(ci, carry):
            c = nc - 1 - ci
            sl = pl.ds(pl.multiple_of(c * CHUNK, CHUNK), CHUNK)
            for p in range(hp):
                cols = slice(p * HG_DK, (p + 1) * HG_DK)
                lb_v = lb_ref[:, cols]
                gn_v = gn_ref[:, cols]
                hq = hq_ref[sl, cols].astype(F32)
                v = hi_ref[sl, cols].astype(F32)
                gate = hg_ref[sl, cols].astype(F32)
                sq = _sigmoid(hq)
                q = hq * sq
                sig = _sigmoid(hf_ref[sl, cols].astype(F32))
                f = lb_v + (1.0 - lb_v) * sig
                kk = 1.0 - f
                b = jnp.dot(tri, jnp.log(jnp.maximum(f, MIN_F)), precision=_HI, preferred_element_type=F32)
                st = st_ref[p, c]
                dst = ds_ref[p]
                o = o_ref[sl, cols]
                r = lax.rsqrt(jnp.mean(o * o, axis=-1, keepdims=True) + EPS)
                xhat = o * r
                sg = _sigmoid(gate)
                dog_v = dog_ref[sl, cols].astype(F32)
                dy = dog_v * (gate * sg)
                dhg_ref[sl, cols] = (dog_v * xhat * gn_v * (sg * (1.0 + gate * (1.0 - sg)))).astype(BF16)
                dgn_ref[:, cols] += _rowsum8(dy * xhat)
                dxh = dy * gn_v
                do = r * (dxh - xhat * jnp.mean(dxh * xhat, axis=-1, keepdims=True))
                dob = do.astype(BF16)
                vb = v.astype(BF16)
                stb = st.astype(BF16)
                dstb = dst.astype(BF16)
                eb = jnp.exp(b)
                bl = b[CHUNK - 1:CHUNK, :]
                ebl = jnp.exp(bl)
                edec = jnp.exp(bl - b)
                qe = q * eb
                kdec = kk * edec
                a, keep = _intra_off(q, kk, b, masks)
                da = _dot_nt(dob, vb)
                dkdec = _dot(vb, dstb)
                dq = _dot(dob, stb) * eb
                dk = dkdec * edec
                dv = _dot_tn(a.astype(BF16), dob) + _dot_nt(kdec.astype(BF16), dstb)
                for mask, eq, ek, qt, kt in keep:
                    dsm = jnp.where(mask, da, 0.0).astype(BF16)
                    dq = dq + _dot(dsm, kt) * eq
                    dk = dk + _dot_tn(dsm, qt) * ek
                ddq, ddk, ddv = _diag_bwd(q, kk, v, b, do)
                dq = dq + ddq
                dk = dk + ddk
                dhi_ref[sl, cols] = (dv + ddv).astype(BF16)
                dbl = jnp.sum(dkdec * kdec, axis=0, keepdims=True) + ebl * jnp.sum(st * dst, axis=0, keepdims=True)
                db = q * dq - kk * dk
                db = jnp.where(last_row, db + dbl, db)
                dlogf = jnp.dot(tri_t, db, precision=_HI, preferred_element_type=F32)
                ds_ref[p] = _dot_tn(dob, qe.astype(BF16)) + dst * ebl
                df = jnp.where(f > MIN_F, dlogf / f, 0.0) - dk
                dhf_ref[sl, cols] = (df * (1.0 - lb_v) * sig * (1.0 - sig)).astype(BF16)
                dlb_ref[:, cols] += _rowsum8(df * (1.0 - sig))
                dhq_ref[sl, cols] = (dq * (sq * (1.0 + hq * (1.0 - sq)))).astype(BF16)
            return carry

        lax.fori_loop(0, nc, chunk, 0)

    rev = lambda i: nblk - 1 - i
    blk = pl.BlockSpec((ct, hp * HG_DK), lambda h, i: (rev(i), h))
    acc = pl.BlockSpec((8, hp * HG_DK), lambda h, i: (0, h))
    return pl.pallas_call(
        body, name=name, grid=(HG_HEADS // hp, nblk),
        in_specs=_hgrn_in_specs(ct, hp, rev) + [blk, pl.BlockSpec((hp, nc, HG_DK, HG_DK), lambda h, i: (h, rev(i), 0, 0)), blk, _ANY],
        out_specs=[pl.BlockSpec((ct, 4 * HG_WIDTH), lambda h, i: (rev(i), 0)), acc, acc],
        out_shape=[jax.ShapeDtypeStruct(dproj.shape, dproj.dtype)] + [jax.ShapeDtypeStruct((8, HG_WIDTH), F32)] * 2,
        scratch_shapes=[pltpu.VMEM((hp, HG_DK, HG_DK), F32)],
        input_output_aliases={9: 0},
        compiler_params=_cp(("parallel", "arbitrary")),
    )(proj, proj, proj, proj, lb, gn, o_raw, states, dog, dproj)


_GW = ATT_GROUP * HEAD_DIM
_NEG = -1e30


def rope_tables(T):
    half = ROPE_DIM // 2
    inv = ROPE_THETA ** (-jnp.arange(half, dtype=F32) * 2.0 / ROPE_DIM)
    ang = jnp.arange(T, dtype=F32)[:, None] * inv[None, :]
    pad1 = jnp.ones((T, HEAD_DIM - ROPE_DIM), F32)
    cos = jnp.concatenate([jnp.cos(ang), jnp.cos(ang), pad1], axis=1)
    sin = jnp.concatenate([jnp.sin(ang), jnp.sin(ang), 0.0 * pad1], axis=1)
    p = np.zeros((_GW, _GW), np.float32)
    for base in range(0, _GW, HEAD_DIM):
        for i in range(half):
            p[base + i + half, base + i] = -1.0
            p[base + i, base + i + half] = 1.0
    reps = _GW // HEAD_DIM
    return jnp.tile(cos, (1, reps)), jnp.tile(sin, (1, reps)), jnp.asarray(p, BF16)


def _rope(x, cos, sin, pm):
    return x.astype(F32) * cos + _dot(x, pm) * sin


def _unrope(dx, cos, sin, pm):
    return dx * cos + _dot_nt((dx * sin).astype(BF16), pm)


_GQ = ATT_GROUP * WINDOW


def _swa_mask(n, queries_on_rows):
    shape = (_GQ, 2 * WINDOW) if queries_on_rows else (2 * WINDOW, _GQ)
    r = lax.broadcasted_iota(jnp.int32, shape, 0 if queries_on_rows else 1) & (WINDOW - 1)
    c = lax.broadcasted_iota(jnp.int32, shape, 1 if queries_on_rows else 0)
    delta = r + WINDOW - c
    return (delta >= 0) & (delta < WINDOW) & ((c >= WINDOW) | (n > 0))


def _sink_vector(sink_ref, g, queries_on_rows):
    shape = (_GQ, 1) if queries_on_rows else (1, _GQ)
    i = lax.broadcasted_iota(jnp.int32, shape, 0 if queries_on_rows else 1)
    out = jnp.full(shape, sink_ref[g * ATT_GROUP + ATT_GROUP - 1], F32)
    for hh in range(ATT_GROUP - 2, -1, -1):
        out = jnp.where(i < (hh + 1) * WINDOW, sink_ref[g * ATT_GROUP + hh], out)
    return out


def _head(x, h):
    return x[:, h * HEAD_DIM:(h + 1) * HEAD_DIM]


def _stack_heads(x):
    return jnp.concatenate([_head(x, hh) for hh in range(ATT_GROUP)], axis=0)


def _unstack_heads(x):
    return jnp.concatenate([x[hh * WINDOW:(hh + 1) * WINDOW] for hh in range(ATT_GROUP)], axis=1)


def _swa_specs(cur):
    prev = lambda n: jnp.maximum(n - 1, 0)
    kv = lambda col, f: pl.BlockSpec((WINDOW, _GW), lambda n: (f(n), col))
    tab = lambda f: pl.BlockSpec((WINDOW, _GW), lambda n: (f(n), 0))
    return [pl.BlockSpec((WINDOW, ATT_WIDTH), lambda n: (cur(n), COL_AQ // 8)),
            kv(COL_AK // 2, prev), kv(COL_AK // 2, cur), kv(COL_AV // 2, prev), kv(COL_AV // 2, cur),
            tab(prev), tab(cur), tab(prev), tab(cur),
            pl.BlockSpec((_GW, _GW), lambda n: (0, 0)), pl.BlockSpec(memory_space=pltpu.SMEM)]


def swa_fwd(proj, cos, sin, pm, sinks, *, name):
    T = proj.shape[0]
    nb = T // WINDOW
    scale = HEAD_DIM ** -0.5

    def body(q_ref, kp_ref, kc_ref, vp_ref, vc_ref, cp_ref, cc_ref, sp_ref, sc_ref, pm_ref, sink_ref, o_ref):
        n = pl.program_id(0)
        pm_v = pm_ref[...]
        mask = _swa_mask(n, True)
        k_cat = jnp.concatenate([_rope(kp_ref[...], cp_ref[...], sp_ref[...], pm_v),
                                 _rope(kc_ref[...], cc_ref[...], sc_ref[...], pm_v)], axis=0).astype(BF16)
        v_cat = jnp.concatenate([vp_ref[...], vc_ref[...]], axis=0)
        for g in range(ATT_KV_HEADS):
            qs = _stack_heads(_rope(q_ref[:, g * _GW:(g + 1) * _GW], cc_ref[...], sc_ref[...], pm_v).astype(BF16))
            sk = _sink_vector(sink_ref, g, True)
            s = jnp.where(mask, _dot_nt(qs, _head(k_cat, g)) * scale, _NEG)
            m = jnp.maximum(jnp.max(s, axis=-1, keepdims=True), sk)
            p = jnp.exp(s - m)
            l = jnp.sum(p, axis=-1, keepdims=True) + jnp.exp(sk - m)
            o = _dot(p.astype(BF16), _head(v_cat, g)) / l
            o_ref[:, g * _GW:(g + 1) * _GW] = _unstack_heads(o).astype(BF16)

    return pl.pallas_call(
        body, name=name, grid=(nb,), in_specs=_swa_specs(lambda n: n),
        out_specs=pl.BlockSpec((WINDOW, ATT_WIDTH), lambda n: (n, 0)),
        out_shape=jax.ShapeDtypeStruct((T, ATT_WIDTH), BF16), compiler_params=_cp(("parallel",)),
    )(proj, proj, proj, proj, proj, cos, cos, sin, sin, pm, sinks)


def swa_bwd(proj, cos, sin, pm, sinks, att, datt, dproj, *, name):
    T = proj.shape[0]
    nb = T // WINDOW
    scale = HEAD_DIM ** -0.5
    clamp = lambda n: jnp.minimum(n, nb - 1)

    def body(q_ref, kp_ref, kc_ref, vp_ref, vc_ref, cp_ref, cc_ref, sp_ref, sc_ref, pm_ref, sink_ref, att_ref, datt_ref,
             _dproj_in, dp_ref, dsink_ref, cq_ref, ck_ref, cv_ref):
        n = pl.program_id(0)
        pm_v = pm_ref[...]

        @pl.when(n == 0)
        def _():
            dsink_ref[...] = jnp.zeros_like(dsink_ref)
            cq_ref[...] = jnp.zeros_like(cq_ref)
            ck_ref[...] = jnp.zeros_like(ck_ref)
            cv_ref[...] = jnp.zeros_like(cv_ref)

        def write_prev(dk_prev, dv_prev):
            dp_ref[:, :ATT_WIDTH] = cq_ref[...]
            dk = _unrope(ck_ref[...] + dk_prev, cp_ref[...], sp_ref[...], pm_v)
            dp_ref[:, ATT_WIDTH:ATT_WIDTH + _GW] = dk.astype(BF16)
            dp_ref[:, ATT_WIDTH + _GW:] = (cv_ref[...] + dv_prev).astype(BF16)

        @pl.when(n < nb)
        def _():
            mask = _swa_mask(n, False)
            ones = jnp.ones((8, HEAD_DIM), F32)
            k_cat = jnp.concatenate([_rope(kp_ref[...], cp_ref[...], sp_ref[...], pm_v),
                                     _rope(kc_ref[...], cc_ref[...], sc_ref[...], pm_v)], axis=0).astype(BF16)
            v_cat = jnp.concatenate([vp_ref[...], vc_ref[...]], axis=0)
            dqs, dks, dvs = [], [], []
            for g in range(ATT_KV_HEADS):
                gc = slice(g * _GW, (g + 1) * _GW)
                qs = _stack_heads(_rope(q_ref[:, gc], cc_ref[...], sc_ref[...], pm_v).astype(BF16))
                dos = _stack_heads(datt_ref[:, gc])
                os_ = _stack_heads(att_ref[:, gc])
                kh, vh = _head(k_cat, g), _head(v_cat, g)
                sk = _sink_vector(sink_ref, g, False)
                s = jnp.where(mask, _dot_nt(kh, qs) * scale, _NEG)
                m = jnp.maximum(jnp.max(s, axis=0, keepdims=True), sk)
                e = jnp.exp(s - m)
                es = jnp.exp(sk - m)
                inv_l = 1.0 / (jnp.sum(e, axis=0, keepdims=True) + es)
                p = e * inv_l
                dsum = lax.dot_general(ones, dos.astype(F32) * os_.astype(F32), (((1,), (1,)), ((), ())),
                                       precision=_HI, preferred_element_type=F32)[0:1]
                dsink_ref[g, 0:1, :] += -(es * inv_l) * dsum
                ds = (p * (_dot_nt(vh, dos) - dsum) * scale).astype(BF16)
                dqs.append(_unrope(_unstack_heads(_dot_tn(ds, kh)), cc_ref[...], sc_ref[...], pm_v))
                dks.append(_dot(ds, qs))
                dvs.append(_dot(p.astype(BF16), dos))
            dk_all = jnp.concatenate(dks, axis=1)
            dv_all = jnp.concatenate(dvs, axis=1)

            @pl.when(n > 0)
            def _():
                write_prev(dk_all[:WINDOW], dv_all[:WINDOW])

            cq_ref[...] = jnp.concatenate(dqs, axis=1).astype(BF16)
            ck_ref[...] = dk_all[WINDOW:]
            cv_ref[...] = dv_all[WINDOW:]

        @pl.when(n == nb)
        def _():
            write_prev(jnp.zeros((WINDOW, _GW), F32), jnp.zeros((WINDOW, _GW), F32))

    row = pl.BlockSpec((WINDOW, ATT_WIDTH), lambda n: (clamp(n), 0))
    slab = ATT_WIDTH + 2 * _GW
    return pl.pallas_call(
        body, name=name, grid=(nb + 1,), in_specs=_swa_specs(clamp) + [row, row, _ANY],
        out_specs=[pl.BlockSpec((WINDOW, slab), lambda n: (jnp.maximum(n - 1, 0), COL_AQ * HG_DK // slab)),
                   pl.BlockSpec((ATT_KV_HEADS, 8, _GQ), lambda n: (0, 0, 0))],
        out_shape=[jax.ShapeDtypeStruct(dproj.shape, dproj.dtype), jax.ShapeDtypeStruct((ATT_KV_HEADS, 8, _GQ), F32)],
        scratch_shapes=[pltpu.VMEM((WINDOW, ATT_WIDTH), BF16), pltpu.VMEM((WINDOW, _GW), F32), pltpu.VMEM((WINDOW, _GW), F32)],
        input_output_aliases={13: 0},
        compiler_params=_cp(("arbitrary",)),
    )(proj, proj, proj, proj, proj, cos, cos, sin, sin, pm, sinks, att, datt, dproj)


_ANY = pl.BlockSpec(memory_space=pl.ANY)
N_CHIPS = 4


def _place():
    x, y, c = lax.axis_index("x"), lax.axis_index("y"), lax.axis_index("c")
    return x, y, c, [(1 - x, y), (x, 1 - y), (1 - x, 1 - y)]


def gather_from_chips(shards, *, name):
    n = len(shards)

    def body(*refs):
        ins, outs = refs[:n], refs[n:2 * n]
        send_sems, recv_sems, local_sems = refs[2 * n:]
        x, y, c, chips = _place()
        me = 2 * x + y

        def copy(a, j):
            px, py = chips[j]
            return pltpu.make_async_remote_copy(
                src_ref=ins[a], dst_ref=outs[a].at[me], send_sem=send_sems.at[a * 3 + j], recv_sem=recv_sems.at[a * 3 + j],
                device_id=(px, py, c), device_id_type=MESH)

        def arrival(a, j):
            px, py = chips[j]
            return pltpu.make_async_remote_copy(
                src_ref=ins[a], dst_ref=outs[a].at[2 * px + py], send_sem=send_sems.at[a * 3 + j],
                recv_sem=recv_sems.at[a * 3 + j], device_id=(px, py, c), device_id_type=MESH)

        local = [pltpu.make_async_copy(ins[a], outs[a].at[me], local_sems.at[a]) for a in range(n)]
        for a in range(n):
            local[a].start()
            for j in range(3):
                copy(a, j).start()
        for a in range(n):
            for j in range(3):
                arrival(a, j).wait()
            local[a].wait()

    return pl.pallas_call(
        body, name=name,
        out_shape=[jax.ShapeDtypeStruct((N_CHIPS,) + s.shape, s.dtype) for s in shards],
        in_specs=[_ANY] * n, out_specs=[_ANY] * n,
        scratch_shapes=[pltpu.SemaphoreType.DMA((3 * n,)), pltpu.SemaphoreType.DMA((3 * n,)), pltpu.SemaphoreType.DMA((n,))],
    )(*shards)


def scatter_to_chips(parts, *, name):
    n = len(parts)

    def body(*refs):
        ins, outs = refs[:n], refs[n:2 * n]
        send_sems, recv_sems, local_sems = refs[2 * n:]
        x, y, c, chips = _place()
        me = 2 * x + y

        def copy(a, j):
            px, py = chips[j]
            return pltpu.make_async_remote_copy(
                src_ref=ins[a].at[2 * px + py], dst_ref=outs[a].at[me], send_sem=send_sems.at[a * 3 + j],
                recv_sem=recv_sems.at[a * 3 + j], device_id=(px, py, c), device_id_type=MESH)

        def arrival(a, j):
            px, py = chips[j]
            return pltpu.make_async_remote_copy(
                src_ref=ins[a].at[2 * px + py], dst_ref=outs[a].at[2 * px + py], send_sem=send_sems.at[a * 3 + j],
                recv_sem=recv_sems.at[a * 3 + j], device_id=(px, py, c), device_id_type=MESH)

        local = [pltpu.make_async_copy(ins[a].at[me], outs[a].at[me], local_sems.at[a]) for a in range(n)]
        for a in range(n):
            local[a].start()
            for j in range(3):
                copy(a, j).start()
        for a in range(n):
            for j in range(3):
                arrival(a, j).wait()
            local[a].wait()

    return pl.pallas_call(
        body, name=name,
        out_shape=[jax.ShapeDtypeStruct(p.shape, p.dtype) for p in parts],
        in_specs=[_ANY] * n, out_specs=[_ANY] * n,
        scratch_shapes=[pltpu.SemaphoreType.DMA((3 * n,)), pltpu.SemaphoreType.DMA((3 * n,)), pltpu.SemaphoreType.DMA((n,))],
    )(*parts)


def swap_with_sibling(arrays, *, name):
    n = len(arrays)

    def body(*refs):
        ins, outs = refs[:n], refs[n:2 * n]
        send_sems, recv_sems = refs[2 * n:]
        x, y, c, _ = _place()
        copies = [pltpu.make_async_remote_copy(src_ref=ins[a], dst_ref=outs[a], send_sem=send_sems.at[a],
                                               recv_sem=recv_sems.at[a], device_id=(x, y, 1 - c), device_id_type=MESH)
                  for a in range(n)]
        for cp in copies:
            cp.start()
        for cp in copies:
            cp.wait()

    return pl.pallas_call(
        body, name=name, out_shape=[jax.ShapeDtypeStruct(a.shape, a.dtype) for a in arrays],
        in_specs=[_ANY] * n, out_specs=[_ANY] * n,
        scratch_shapes=[pltpu.SemaphoreType.DMA((n,)), pltpu.SemaphoreType.DMA((n,))],
    )(*arrays)


N_DEV = 8
SMALL_ROWS = 24


def allreduce_small(v, *, name):
    def body(v_ref, o_ref, recv_ref, send_sems, recv_sems):
        x, y, c, _ = _place()
        me = 4 * x + 2 * y + c
        recv_ref[me] = v_ref[...]

        def copy(k, slot):
            peer = (x ^ (k >> 2), y ^ ((k >> 1) & 1), c ^ (k & 1))
            return pltpu.make_async_remote_copy(src_ref=v_ref, dst_ref=recv_ref.at[slot], send_sem=send_sems.at[k - 1],
                                                recv_sem=recv_sems.at[k - 1], device_id=peer, device_id_type=MESH)

        for k in range(1, N_DEV):
            copy(k, me).start()
        for k in range(1, N_DEV):
            copy(k, me ^ k).wait()
        acc = recv_ref[0]
        for d in range(1, N_DEV):
            acc = acc + recv_ref[d]
        o_ref[...] = acc

    vm = pl.BlockSpec(memory_space=pltpu.VMEM)
    return pl.pallas_call(
        body, name=name, out_shape=jax.ShapeDtypeStruct(v.shape, v.dtype), in_specs=[vm], out_specs=vm,
        scratch_shapes=[pltpu.VMEM((N_DEV,) + v.shape, v.dtype), pltpu.SemaphoreType.DMA((N_DEV - 1,)),
                        pltpu.SemaphoreType.DMA((N_DEV - 1,))],
    )(v)


def sum_parts(parts, *, tr, name):
    _, R, C = parts.shape

    def body(p_ref, o_ref):
        acc = p_ref[0].astype(F32)
        for q in range(1, N_CHIPS):
            acc = acc + p_ref[q].astype(F32)
        o_ref[...] = acc

    return pl.pallas_call(
        body, name=name, grid=(R // tr,), in_specs=[pl.BlockSpec((N_CHIPS, tr, C), lambda i: (0, i, 0))],
        out_specs=pl.BlockSpec((tr, C), lambda i: (i, 0)), out_shape=jax.ShapeDtypeStruct((R, C), F32),
        compiler_params=_cp(("parallel",)),
    )(parts)


def adamw(g_a, g_b, w, m, v, *, tr, name):
    R, C = w.shape
    c1 = 1.0 - ADAM_B1 ** ADAM_STEP
    c2 = 1.0 - ADAM_B2 ** ADAM_STEP
    ng = 1 if g_b is None else 2

    def body(*refs):
        w_ref, m_ref, v_ref, g_ref, d_ref, nm_ref, nv_ref = refs[ng:]
        g = refs[0][...] if ng == 1 else refs[0][...] + refs[1][...]
        nm = ADAM_B1 * m_ref[...] + (1.0 - ADAM_B1) * g
        nv = ADAM_B2 * v_ref[...] + (1.0 - ADAM_B2) * (g * g)
        g_ref[...] = g
        nm_ref[...] = nm
        nv_ref[...] = nv
        d_ref[...] = -ADAM_LR * ((nm / c1) / (jnp.sqrt(nv / c2) + ADAM_EPS) + ADAM_WD * w_ref[...])

    blk = pl.BlockSpec((tr, C), lambda i: (i, 0))
    gs = [g_a] if g_b is None else [g_a, g_b]
    return pl.pallas_call(
        body, name=name, grid=(R // tr,), in_specs=[blk] * (ng + 3), out_specs=[blk] * 4,
        out_shape=[jax.ShapeDtypeStruct((R, C), F32)] * 4, compiler_params=_cp(("parallel",)),
    )(*gs, w, m, v)


def _lb_bounds(lb_logits):
    p = jax.nn.softmax(lb_logits.astype(F32), axis=0)
    return jnp.cumsum(p, axis=0) - p[0:1]


def _pack_small(n1, lb, hgn, n2, fin, sinks, extra):
    row = jnp.concatenate([sinks.reshape(-1), jnp.reshape(extra, (1,)),
                           jnp.zeros((D_MODEL - sinks.size - 1,), F32)])
    return jnp.concatenate([n1, lb, hgn, n2, fin[None], row[None], jnp.zeros((SMALL_ROWS - 18, D_MODEL), F32)], axis=0)


def _unpack_small(buf):
    return (buf[0:4], buf[4:8], buf[8:12], buf[12:16], buf[16], buf[17, :DEPTH * ATT_Q_HEADS].reshape(DEPTH, ATT_Q_HEADS),
            buf[17, DEPTH * ATT_Q_HEADS])


def _cols_full(g):
    q, l, r, c = g.shape
    return jnp.transpose(g, (1, 2, 0, 3)).reshape(l, r, q * c)


def _rows_full(g):
    q, l, r, c = g.shape
    return jnp.transpose(g, (1, 0, 2, 3)).reshape(l, q * r, c)


def _cols_split(w):
    l, r, qc = w.shape
    return jnp.transpose(w.reshape(l, r, N_CHIPS, qc // N_CHIPS), (2, 0, 1, 3))


def _rows_split(w):
    l, qr, c = w.shape
    return jnp.transpose(w.reshape(l, N_CHIPS, qr // N_CHIPS, c), (1, 0, 2, 3))


def kernel(x, norm1, w_in, lb_logits, hg_norm, attn_sinks, w_pa, w_pb, w_o, norm2, w_gate, w_up, w_down, final_norm, loss_target, m_norm1, m_w_in, m_lb_logits, m_hg_norm, m_attn_sinks, m_w_pa, m_w_pb, m_w_o, m_norm2, m_w_gate, m_w_up, m_w_down, m_final_norm, v_norm1, v_w_in, v_lb_logits, v_hg_norm, v_attn_sinks, v_w_pa, v_w_pb, v_w_o, v_norm2, v_w_gate, v_w_up, v_w_down, v_final_norm):
    T = x.shape[1]
    H = FFN_HIDDEN
    tm = min(512, T)
    big = dict(w_in=w_in, w_pa=w_pa, w_pb=w_pb, w_o=w_o, w_gate=w_gate, w_up=w_up, w_down=w_down)
    col_sharded = ("w_in", "w_gate", "w_up")

    gathered = gather_from_chips([w.astype(BF16) for w in big.values()], name="gather_weights")
    full = {k: (_cols_full(g) if k in col_sharded else _rows_full(g)) for k, g in zip(big, gathered)}
    full["w_in"] = _to_kernel_cols(full["w_in"])
    wgu = jnp.concatenate([full["w_gate"], full["w_up"]], axis=-1)

    lb_all, lb_vjp = jax.vjp(_lb_bounds, lb_logits)
    cos, sin, pm = rope_tables(T)

    xs = x[0]
    saved = []
    for l in range(DEPTH):
        h, proj = ln_matmul(xs, norm1[l:l + 1], full["w_in"][l], tm=tm, tn=1536, name=f"in_proj_{l}")
        og, o_raw, states = hgrn_fwd(proj, lb_all[l:l + 1], hg_norm[l:l + 1], ct=min(256, T), hp=HG_HEADS, name=f"hgrn_fwd_{l}")
        att = swa_fwd(proj, cos, sin, pm, attn_sinks[l], name=f"swa_fwd_{l}")
        ya, yb, mix = merge_fwd(og, att, proj, full["w_pa"][l], full["w_pb"][l], tm=tm, name=f"merge_fwd_{l}")
        x1 = matmul_residual(xs, mix, full["w_o"][l], tm=tm, name=f"out_proj_{l}")
        h2, gu, act = ln_matmul(x1, norm2[l:l + 1], wgu[l], tm=min(256, T), tn=2 * H, name=f"ffn_up_{l}", swiglu=True)
        x2 = matmul_residual(x1, act, full["w_down"][l], tm=tm, name=f"ffn_down_{l}")
        saved.append((xs, h, proj, og, o_raw, states, att, ya, yb, mix, x1, h2, gu, act))
        xs = x2

    dx, loss_parts, dfin_parts = loss_head(xs, final_norm[None], loss_target[0], tm=tm, name="loss_head")

    dw = {k: [None] * DEPTH for k in big}
    dn1, dn2, dlb, dgn, dsk = ([None] * DEPTH for _ in range(5))
    for l in reversed(range(DEPTH)):
        xs, h, proj, og, o_raw, states, att, ya, yb, mix, x1, h2, gu, act = saved[l]
        dgu = ffn_down_bwd(dx, full["w_down"][l], gu, tm=min(256, T), name=f"ffn_down_bwd_{l}")
        dw["w_down"][l] = matmul_tn(act, dx, tka=H // 2, tn=D_MODEL, tk=tm, name=f"dw_down_{l}")
        dwgu = matmul_tn(h2, dgu, tka=D_MODEL, tn=H // 2, tk=tm, name=f"dw_gu_{l}")
        dw["w_gate"][l], dw["w_up"][l] = dwgu[:, :H], dwgu[:, H:]
        dx1, dn2_parts = matmul_nt_rmsbwd(dgu, wgu[l], x1, norm2[l:l + 1], dx, tm=tm, tk=H // 2, name=f"ffn_up_bwd_{l}")
        dya, dyb, dproj = merge_bwd(dx1, full["w_o"][l], proj, ya, yb, tm=tm, name=f"merge_bwd_{l}")
        dw["w_o"][l] = matmul_tn(mix, dx1, tka=D_MODEL, tn=D_MODEL, tk=tm, name=f"dw_o_{l}")
        dog = matmul_nt(dya, full["w_pa"][l], tm=tm, name=f"pa_bwd_{l}")
        datt = matmul_nt(dyb, full["w_pb"][l], tm=tm, name=f"pb_bwd_{l}")
        dw["w_pa"][l] = matmul_tn(og, dya, tka=D_MODEL, tn=D_MODEL, tk=tm, name=f"dw_pa_{l}")
        dw["w_pb"][l] = matmul_tn(att, dyb, tka=D_MODEL, tn=D_MODEL, tk=tm, name=f"dw_pb_{l}")
        dproj, dsk_parts = swa_bwd(proj, cos, sin, pm, attn_sinks[l], att, datt, dproj, name=f"swa_bwd_{l}")
        dproj, dlb_parts, dgn_parts = hgrn_bwd(proj, lb_all[l:l + 1], hg_norm[l:l + 1], o_raw, states, dog, dproj,
                                               ct=min(256, T), name=f"hgrn_bwd_{l}")
        dw["w_in"][l] = matmul_tn(h, dproj, tka=D_MODEL, tn=1536, tk=tm, name=f"dw_in_{l}")
        dx, dn1_parts = matmul_nt_rmsbwd(dproj, full["w_in"][l], xs, norm1[l:l + 1], dx1, tm=tm, tk=1536, name=f"in_proj_bwd_{l}")
        dn1[l], dn2[l], dlb[l], dgn[l] = dn1_parts.sum(0), dn2_parts.sum(0), dlb_parts.sum(0), dgn_parts.sum(0)
        dsk[l] = dsk_parts[:, 0, :].reshape(ATT_Q_HEADS, WINDOW).sum(-1)
    grad_x = dx[None]

    mine = _pack_small(jnp.stack(dn1), jnp.stack(dlb), jnp.stack(dgn), jnp.stack(dn2), dfin_parts.sum(0), jnp.stack(dsk),
                       loss_parts.sum())
    total = allreduce_small(mine, name="allreduce_small")
    g_n1, g_lb_all, g_gn, g_n2, g_fin, g_sk, loss = _unpack_small(total)
    (g_lb,) = lb_vjp(g_lb_all)
    zero = jnp.zeros((), F32)
    small = adamw(_pack_small(g_n1, g_lb, g_gn, g_n2, g_fin, g_sk, zero), None,
                  _pack_small(norm1, lb_logits, hg_norm, norm2, final_norm, attn_sinks, zero),
                  _pack_small(m_norm1, m_lb_logits, m_hg_norm, m_norm2, m_final_norm, m_attn_sinks, zero),
                  _pack_small(v_norm1, v_lb_logits, v_hg_norm, v_norm2, v_final_norm, v_attn_sinks, zero),
                  tr=SMALL_ROWS, name="adamw_small")
    small = [_unpack_small(s)[:6] for s in small]

    stacked = {k: jnp.stack(dw[k]) for k in big}
    stacked["w_in"] = _from_kernel_cols(stacked["w_in"])
    parts = [(_cols_split if k in col_sharded else _rows_split)(stacked[k]) for k in big]
    arrived = scatter_to_chips(parts, name="scatter_grads")
    sums = []
    for k, a in zip(big, arrived):
        q, nl, r, c = a.shape
        sums.append(sum_parts(a.reshape(q, nl * r, c), tr=128, name=f"sum_{k}"))
    theirs = swap_with_sibling(sums, name="swap_sums")
    mom = dict(w_in=(m_w_in, v_w_in), w_pa=(m_w_pa, v_w_pa), w_pb=(m_w_pb, v_w_pb), w_o=(m_w_o, v_w_o),
               w_gate=(m_w_gate, v_w_gate), w_up=(m_w_up, v_w_up), w_down=(m_w_down, v_w_down))
    upd = {}
    for k, mine_k, theirs_k in zip(big, sums, theirs):
        shp = big[k].shape
        flat = lambda t: t.reshape(shp[0] * shp[1], shp[2])
        res = adamw(mine_k, theirs_k, flat(big[k]), flat(mom[k][0]), flat(mom[k][1]), tr=128, name=f"adamw_{k}")
        upd[k] = [t.reshape(shp) for t in res]

    order = ("norm1", "w_in", "lb_logits", "hg_norm", "attn_sinks", "w_pa", "w_pb", "w_o", "norm2", "w_gate", "w_up",
             "w_down", "final_norm")
    small_pos = dict(norm1=0, lb_logits=1, hg_norm=2, norm2=3, final_norm=4, attn_sinks=5)
    outs = [loss, grad_x]
    for kind in range(4):
        for name in order:
            outs.append(upd[name][kind] if name in upd else small[kind][small_pos[name]])
    return tuple(outs)
```

```python
import functools

import jax
import jax.numpy as jnp
import numpy as np
from jax import lax
from jax.experimental import pallas as pl
from jax.experimental.pallas import tpu as pltpu

F32 = jnp.float32
BF16 = jnp.bfloat16

D_MODEL = 1024
DEPTH = 4
HG_HEADS = 8
HG_DK = 128
HG_WIDTH = HG_HEADS * HG_DK
CHUNK = 64
ATT_Q_HEADS = 16
ATT_KV_HEADS = 4
ATT_GROUP = ATT_Q_HEADS // ATT_KV_HEADS
HEAD_DIM = 64
ATT_WIDTH = ATT_Q_HEADS * HEAD_DIM
ATT_KV_WIDTH = ATT_KV_HEADS * HEAD_DIM
WINDOW = 128
ROPE_THETA = 500000.0
ROPE_DIM = HEAD_DIM // 4
FFN_HIDDEN = 2816
IN_COLS = 4 * HG_WIDTH + ATT_WIDTH + 2 * ATT_KV_WIDTH + 2 * D_MODEL
EPS = 1e-6
MIN_F = 1e-30
COL_HQ, COL_HF, COL_HI, COL_HG = 0, 8, 16, 24
COL_GA, COL_GB, COL_AQ, COL_AK, COL_AV = 32, 40, 48, 56, 58


def _to_kernel_cols(w):
    return jnp.concatenate([w[..., :4096], w[..., 5632:7680], w[..., 4096:5120], w[..., 5120:5632]], axis=-1)


def _from_kernel_cols(w):
    return jnp.concatenate([w[..., :4096], w[..., 6144:7168], w[..., 7168:7680], w[..., 4096:6144]], axis=-1)

ADAM_LR = 0.001
ADAM_B1 = 0.9
ADAM_B2 = 0.999
ADAM_EPS = 1e-08
ADAM_WD = 0.01
ADAM_STEP = 10

VMEM_LIMIT_V7X = 56 * 1024 * 1024
MESH = pl.DeviceIdType.MESH


def _cp(sem, vmem=VMEM_LIMIT_V7X):
    return pltpu.CompilerParams(dimension_semantics=sem, vmem_limit_bytes=vmem)


def _sigmoid(x):
    return 1.0 / (1.0 + jnp.exp(-x))


def _dot(a, b):
    return jnp.dot(a, b, preferred_element_type=F32)


def _dot_nt(a, b):
    return lax.dot_general(a, b, (((1,), (1,)), ((), ())), preferred_element_type=F32)


def _dot_tn(a, b):
    return lax.dot_general(a, b, (((0,), (0,)), ((), ())), preferred_element_type=F32)


def _rowsum8(v):
    r, n = v.shape
    return jnp.sum(v.reshape(r // 8, 8, n), axis=0)


_ANY = pl.BlockSpec(memory_space=pl.ANY)
N_CHIPS = 4
PEER_CHIPS = N_CHIPS - 1


def _place():
    x, y, c = lax.axis_index("x"), lax.axis_index("y"), lax.axis_index("c")
    return x, y, c, [(1 - x, y), (x, 1 - y), (1 - x, 1 - y)]


def _exchange_scratch(n):
    return [pltpu.SemaphoreType.DMA((PEER_CHIPS * n,)), pltpu.SemaphoreType.DMA((PEER_CHIPS * n,)), pltpu.SemaphoreType.DMA((n,))]


def _exchange(ins, outs, sems, scatter):
    send_sems, recv_sems, local_sems = sems
    x, y, c, chips = _place()
    me = 2 * x + y
    n = len(ins)

    def remote(a, j, arriving):
        px, py = chips[j]
        them = 2 * px + py
        src = ins[a].at[them] if scatter else ins[a]
        return pltpu.make_async_remote_copy(
            src_ref=src, dst_ref=outs[a].at[them if arriving else me], send_sem=send_sems.at[a * PEER_CHIPS + j],
            recv_sem=recv_sems.at[a * PEER_CHIPS + j], device_id=(px, py, c), device_id_type=MESH)

    def local(a):
        return pltpu.make_async_copy(ins[a].at[me] if scatter else ins[a], outs[a].at[me], local_sems.at[a])

    def start():
        for a in range(n):
            local(a).start()
            for j in range(PEER_CHIPS):
                remote(a, j, False).start()

    def wait():
        for a in range(n):
            for j in range(PEER_CHIPS):
                remote(a, j, True).wait()
            local(a).wait()

    return start, wait


def _exchange_out_shapes(arrays, scatter):
    return [jax.ShapeDtypeStruct(a.shape if scatter else (N_CHIPS,) + a.shape, a.dtype) for a in arrays]


def exchange_between_chips(arrays, *, scatter, name):
    n = len(arrays)

    def body(*refs):
        start, wait = _exchange(refs[:n], refs[n:2 * n], refs[2 * n:], scatter)
        start()
        wait()

    return pl.pallas_call(
        body, name=name, out_shape=_exchange_out_shapes(arrays, scatter), in_specs=[_ANY] * n, out_specs=[_ANY] * n,
        scratch_shapes=_exchange_scratch(n),
    )(*arrays)


def ln_matmul(x, g, w, *, tm, tn, name, swiglu=False, gather=()):
    T, Dm = x.shape
    N = w.shape[1]
    if swiglu:
        assert tn == N
    half = N // 2
    ng = len(gather)
    n_out = 3 if swiglu else 2
    ni, nj = T // tm, N // tn

    def body(x_ref, g_ref, w_ref, *rest):
        gather_in, rest = rest[:ng], rest[ng:]
        h_ref, o_ref = rest[0], rest[1]
        i, j = pl.program_id(0), pl.program_id(1)
        if ng:
            start, wait = _exchange(gather_in, rest[n_out:n_out + ng], rest[n_out + ng:], False)
            pl.when((i == 0) & (j == 0))(start)

        @pl.when(j == 0)
        def _():
            xf = x_ref[...]
            r = lax.rsqrt(jnp.mean(xf * xf, axis=-1, keepdims=True) + EPS)
            h_ref[...] = (xf * r * g_ref[...]).astype(BF16)

        acc = _dot(h_ref[...], w_ref[...])
        o_ref[...] = acc.astype(BF16)
        if swiglu:
            gt, up = acc[:, :half], acc[:, half:]
            rest[2][...] = (gt * _sigmoid(gt) * up).astype(BF16)
        if ng:
            pl.when((i == ni - 1) & (j == nj - 1))(wait)

    out_shape = [jax.ShapeDtypeStruct((T, Dm), BF16), jax.ShapeDtypeStruct((T, N), BF16)]
    out_specs = [pl.BlockSpec((tm, Dm), lambda i, j: (i, 0)), pl.BlockSpec((tm, tn), lambda i, j: (i, j))]
    if swiglu:
        out_shape.append(jax.ShapeDtypeStruct((T, half), BF16))
        out_specs.append(pl.BlockSpec((tm, half), lambda i, j: (i, 0)))
    return pl.pallas_call(
        body, name=name, grid=(ni, nj),
        in_specs=[pl.BlockSpec((tm, Dm), lambda i, j: (i, 0)), pl.BlockSpec((1, Dm), lambda i, j: (0, 0)),
                  pl.BlockSpec((Dm, tn), lambda i, j: (0, j))] + [_ANY] * ng,
        out_specs=out_specs + [_ANY] * ng, out_shape=out_shape + _exchange_out_shapes(gather, False),
        scratch_shapes=_exchange_scratch(ng) if ng else [],
        compiler_params=_cp(("arbitrary", "arbitrary") if ng else ("parallel", "arbitrary")),
    )(x, g, w, *gather)


def matmul_residual(x, a, w, *, tm, name):
    T, N = x.shape
    K = a.shape[1]

    def body(x_ref, a_ref, w_ref, o_ref):
        o_ref[...] = x_ref[...] + _dot(a_ref[...], w_ref[...])

    return pl.pallas_call(
        body, name=name, grid=(T // tm,),
        in_specs=[pl.BlockSpec((tm, N), lambda i: (i, 0)), pl.BlockSpec((tm, K), lambda i: (i, 0)),
                  pl.BlockSpec((K, N), lambda i: (0, 0))],
        out_specs=pl.BlockSpec((tm, N), lambda i: (i, 0)), out_shape=jax.ShapeDtypeStruct((T, N), F32),
        compiler_params=_cp(("parallel",)),
    )(x, a, w)


def matmul_nt(a, w, *, tm, name):
    T, K = a.shape
    N = w.shape[0]

    def body(a_ref, w_ref, o_ref):
        o_ref[...] = _dot_nt(a_ref[...], w_ref[...]).astype(BF16)

    return pl.pallas_call(
        body, name=name, grid=(T // tm,),
        in_specs=[pl.BlockSpec((tm, K), lambda i: (i, 0)), pl.BlockSpec((N, K), lambda i: (0, 0))],
        out_specs=pl.BlockSpec((tm, N), lambda i: (i, 0)), out_shape=jax.ShapeDtypeStruct((T, N), BF16),
        compiler_params=_cp(("parallel",)),
    )(a, w)


def matmul_tn(a, g, *, tka, tn, tk, name):
    T, Ka = a.shape
    N = g.shape[1]
    nk = T // tk

    def body(a_ref, g_ref, o_ref, acc_ref):
        k = pl.program_id(2)

        @pl.when(k == 0)
        def _():
            acc_ref[...] = jnp.zeros_like(acc_ref)

        acc_ref[...] += _dot_tn(a_ref[...].astype(BF16), g_ref[...].astype(BF16))

        @pl.when(k == nk - 1)
        def _():
            o_ref[...] = acc_ref[...].astype(BF16)

    return pl.pallas_call(
        body, name=name, grid=(Ka // tka, N // tn, nk),
        in_specs=[pl.BlockSpec((tk, tka), lambda i, j, k: (k, i)), pl.BlockSpec((tk, tn), lambda i, j, k: (k, j))],
        out_specs=pl.BlockSpec((tka, tn), lambda i, j, k: (i, j)), out_shape=jax.ShapeDtypeStruct((Ka, N), BF16),
        scratch_shapes=[pltpu.VMEM((tka, tn), F32)],
        compiler_params=_cp(("parallel", "parallel", "arbitrary")),
    )(a, g)


def matmul_nt_rmsbwd(a, w, x, g, dres, *, tm, tk, name, scatter=()):
    T, K = a.shape
    Dm = w.shape[0]
    nk = K // tk
    ni = T // tm
    ns = len(scatter)

    def body(a_ref, w_ref, x_ref, g_ref, dres_ref, *rest):
        scatter_in, rest = rest[:ns], rest[ns:]
        dx_ref, dg_ref = rest[0], rest[1]
        acc_ref = rest[2 + ns]
        i, k = pl.program_id(0), pl.program_id(1)
        if ns:
            start, wait = _exchange(scatter_in, rest[2:2 + ns], rest[3 + ns:], True)
            pl.when((i == 0) & (k == 0))(start)

        @pl.when(k == 0)
        def _():
            acc_ref[...] = jnp.zeros_like(acc_ref)

        @pl.when((i == 0) & (k == 0))
        def _():
            dg_ref[...] = jnp.zeros_like(dg_ref)

        acc_ref[...] += _dot_nt(a_ref[...], w_ref[...])

        @pl.when(k == nk - 1)
        def _():
            dh = acc_ref[...]
            xf = x_ref[...]
            r = lax.rsqrt(jnp.mean(xf * xf, axis=-1, keepdims=True) + EPS)
            xhat = xf * r
            dg_ref[...] += _rowsum8(dh * xhat)
            dxhat = dh * g_ref[...]
            dx_ref[...] = dres_ref[...] + r * (dxhat - xhat * jnp.mean(dxhat * xhat, axis=-1, keepdims=True))

        if ns:
            pl.when((i == ni - 1) & (k == nk - 1))(wait)

    return pl.pallas_call(
        body, name=name, grid=(ni, nk),
        in_specs=[pl.BlockSpec((tm, tk), lambda i, k: (i, k)), pl.BlockSpec((Dm, tk), lambda i, k: (0, k)),
                  pl.BlockSpec((tm, Dm), lambda i, k: (i, 0)), pl.BlockSpec((1, Dm), lambda i, k: (0, 0)),
                  pl.BlockSpec((tm, Dm), lambda i, k: (i, 0))] + [_ANY] * ns,
        out_specs=[pl.BlockSpec((tm, Dm), lambda i, k: (i, 0)), pl.BlockSpec((8, Dm), lambda i, k: (0, 0))] + [_ANY] * ns,
        out_shape=[jax.ShapeDtypeStruct((T, Dm), F32), jax.ShapeDtypeStruct((8, Dm), F32)] + _exchange_out_shapes(scatter, True),
        scratch_shapes=[pltpu.VMEM((tm, Dm), F32)] + (_exchange_scratch(ns) if ns else []),
        compiler_params=_cp(("arbitrary", "arbitrary")),
    )(a, w, x, g, dres, *scatter)


def ffn_down_bwd(dx, w_down, gu, *, tm, name):
    T, Dm = dx.shape
    H = w_down.shape[0]

    def body(dx_ref, w_ref, gu_ref, o_ref):
        dact = _dot_nt(dx_ref[...].astype(BF16), w_ref[...])
        gt = gu_ref[:, :H].astype(F32)
        up = gu_ref[:, H:].astype(F32)
        s = _sigmoid(gt)
        o_ref[:, :H] = (dact * up * (s * (1.0 + gt * (1.0 - s)))).astype(BF16)
        o_ref[:, H:] = (dact * gt * s).astype(BF16)

    return pl.pallas_call(
        body, name=name, grid=(T // tm,),
        in_specs=[pl.BlockSpec((tm, Dm), lambda i: (i, 0)), pl.BlockSpec((H, Dm), lambda i: (0, 0)),
                  pl.BlockSpec((tm, 2 * H), lambda i: (i, 0))],
        out_specs=pl.BlockSpec((tm, 2 * H), lambda i: (i, 0)), out_shape=jax.ShapeDtypeStruct((T, 2 * H), BF16),
        compiler_params=_cp(("parallel",)),
    )(dx, w_down, gu)


def _gate_specs(tm):
    return [pl.BlockSpec((tm, D_MODEL), lambda i, c=c: (i, c)) for c in (COL_GA // 8, COL_GB // 8)]


def merge_fwd(og, att, proj, w_pa, w_pb, *, tm, name):
    T, Dm = og.shape

    def body(og_ref, att_ref, ga_ref, gb_ref, wa_ref, wb_ref, ya_ref, yb_ref, mix_ref):
        ya = _dot(og_ref[...], wa_ref[...])
        yb = _dot(att_ref[...], wb_ref[...])
        ya_ref[...] = ya.astype(BF16)
        yb_ref[...] = yb.astype(BF16)
        mix_ref[...] = (_sigmoid(ga_ref[...].astype(F32)) * ya + _sigmoid(gb_ref[...].astype(F32)) * yb).astype(BF16)

    row = pl.BlockSpec((tm, Dm), lambda i: (i, 0))
    wsp = pl.BlockSpec((Dm, Dm), lambda i: (0, 0))
    return pl.pallas_call(
        body, name=name, grid=(T // tm,),
        in_specs=[row, row] + _gate_specs(tm) + [wsp, wsp],
        out_specs=[row, row, row], out_shape=[jax.ShapeDtypeStruct((T, Dm), BF16)] * 3,
        compiler_params=_cp(("parallel",)),
    )(og, att, proj, proj, w_pa, w_pb)


def merge_bwd(dx, w_o, proj, ya, yb, *, tm, name):
    T, Dm = dx.shape

    def body(dx_ref, w_ref, ga_ref, gb_ref, ya_ref, yb_ref, dya_ref, dyb_ref, dgate_ref):
        dmix = _dot_nt(dx_ref[...].astype(BF16), w_ref[...])
        sa = _sigmoid(ga_ref[...].astype(F32))
        sb = _sigmoid(gb_ref[...].astype(F32))
        dya_ref[...] = (dmix * sa).astype(BF16)
        dyb_ref[...] = (dmix * sb).astype(BF16)
        dgate_ref[:, :Dm] = (dmix * ya_ref[...].astype(F32) * sa * (1.0 - sa)).astype(BF16)
        dgate_ref[:, Dm:] = (dmix * yb_ref[...].astype(F32) * sb * (1.0 - sb)).astype(BF16)

    row = pl.BlockSpec((tm, Dm), lambda i: (i, 0))
    return pl.pallas_call(
        body, name=name, grid=(T // tm,),
        in_specs=[row, pl.BlockSpec((Dm, Dm), lambda i: (0, 0))] + _gate_specs(tm) + [row, row],
        out_specs=[row, row, pl.BlockSpec((tm, 2 * Dm), lambda i: (i, COL_GA // 16))],
        out_shape=[jax.ShapeDtypeStruct((T, Dm), BF16)] * 2 + [jax.ShapeDtypeStruct((T, IN_COLS), BF16)],
        compiler_params=_cp(("parallel",)),
    )(dx, w_o, proj, proj, ya, yb)


def loss_head(x, g, target, *, tm, name):
    T, Dm = x.shape

    def body(x_ref, g_ref, t_ref, dx_ref, l_ref, dg_ref):
        @pl.when(pl.program_id(0) == 0)
        def _():
            l_ref[...] = jnp.zeros_like(l_ref)
            dg_ref[...] = jnp.zeros_like(dg_ref)

        xf = x_ref[...]
        gv = g_ref[...]
        r = lax.rsqrt(jnp.mean(xf * xf, axis=-1, keepdims=True) + EPS)
        xhat = xf * r
        err = xhat * gv - t_ref[...]
        l_ref[...] += _rowsum8(err * err) * (0.5 / Dm)
        dy = err * (1.0 / Dm)
        dg_ref[...] += _rowsum8(dy * xhat)
        dxhat = dy * gv
        dx_ref[...] = r * (dxhat - xhat * jnp.mean(dxhat * xhat, axis=-1, keepdims=True))

    row = pl.BlockSpec((tm, Dm), lambda i: (i, 0))
    acc = pl.BlockSpec((8, Dm), lambda i: (0, 0))
    return pl.pallas_call(
        body, name=name, grid=(T // tm,),
        in_specs=[row, pl.BlockSpec((1, Dm), lambda i: (0, 0)), row],
        out_specs=[row, acc, acc],
        out_shape=[jax.ShapeDtypeStruct((T, Dm), F32), jax.ShapeDtypeStruct((8, Dm), F32), jax.ShapeDtypeStruct((8, Dm), F32)],
        compiler_params=_cp(("arbitrary",)),
    )(x, g, target)


_LEVELS = (32, 16, 8)
_DIAG = 8
_SUBLANES = 8
_HI = lax.Precision.HIGHEST


def _chunk_consts():
    t = lax.broadcasted_iota(jnp.int32, (CHUNK, CHUNK), 0)
    s = lax.broadcasted_iota(jnp.int32, (CHUNK, CHUNK), 1)
    tri = (s <= t).astype(F32)
    tri_t = (s >= t).astype(F32)
    masks = []
    for m in _LEVELS:
        sh = int(np.log2(2 * m))
        masks.append(((t >> sh) == (s >> sh)) & ((t & (2 * m - 1)) >= m) & ((s & (2 * m - 1)) < m))
    return tri, tri_t, masks


def _level_ref(b, m):
    if 2 * m >= _SUBLANES:
        pieces = [jnp.broadcast_to(b[p * 2 * m + m - 1:p * 2 * m + m, :], (2 * m, b.shape[1])) for p in range(CHUNK // (2 * m))]
        return pieces[0] if len(pieces) == 1 else jnp.concatenate(pieces, axis=0)
    groups = CHUNK // _SUBLANES
    b3 = b.reshape(groups, _SUBLANES, b.shape[1])
    row = lax.broadcasted_iota(jnp.int32, b3.shape, 1)
    ref = None
    for p in reversed(range(_SUBLANES // (2 * m))):
        src = jnp.broadcast_to(b3[:, p * 2 * m + m - 1:p * 2 * m + m, :], b3.shape)
        ref = src if ref is None else jnp.where(row < (p + 1) * 2 * m, src, ref)
    return ref.reshape(b.shape)


def _intra_off(q, kk, b, masks):
    a = jnp.zeros((CHUNK, CHUNK), F32)
    keep = []
    for m, mask in zip(_LEVELS, masks):
        ref = _level_ref(b, m)
        eq = jnp.exp(jnp.minimum(b - ref, 0.0))
        ek = jnp.exp(jnp.minimum(ref - b, 0.0))
        qt = (q * eq).astype(BF16)
        kt = (kk * ek).astype(BF16)
        a = a + jnp.where(mask, _dot_nt(qt, kt), 0.0)
        keep.append((mask, eq, ek, qt, kt))
    return a, keep


def _roll_rows(x, d):
    g = x.shape[0] // _DIAG
    return pltpu.roll(x.reshape(g, _DIAG, x.shape[1]), d % _DIAG, 1).reshape(x.shape)


def _diag_fwd(q, kk, v, b):
    lower = lax.broadcasted_iota(jnp.int32, q.shape, 0) & (_DIAG - 1)
    out = jnp.sum(q * kk, axis=-1, keepdims=True) * v
    for d in range(1, _DIAG):
        dec = jnp.where(lower >= d, jnp.exp(jnp.minimum(b - _roll_rows(b, d), 0.0)), 0.0)
        out = out + jnp.sum(q * dec * _roll_rows(kk, d), axis=-1, keepdims=True) * _roll_rows(v, d)
    return out


def _diag_bwd(q, kk, v, b, do):
    lower = lax.broadcasted_iota(jnp.int32, q.shape, 0) & (_DIAG - 1)
    a = jnp.sum(q * kk, axis=-1, keepdims=True)
    da = jnp.sum(do * v, axis=-1, keepdims=True)
    dq, dk, dv = da * kk, da * q, a * do
    for d in range(1, _DIAG):
        kr = _roll_rows(kk, d)
        dec = jnp.where(lower >= d, jnp.exp(jnp.minimum(b - _roll_rows(b, d), 0.0)), 0.0)
        qd = q * dec
        a = jnp.sum(qd * kr, axis=-1, keepdims=True)
        da = jnp.sum(do * _roll_rows(v, d), axis=-1, keepdims=True)
        dq = dq + da * dec * kr
        dk = dk + _roll_rows(da * qd, -d)
        dv = dv + _roll_rows(a * do, -d)
    return dq, dk, dv


def _hgrn_in_specs(ct, hp, order):
    def spec(col):
        return pl.BlockSpec((ct, hp * HG_DK), lambda h, i, col=col: (order(i), col // hp + h))
    vec = pl.BlockSpec((1, hp * HG_DK), lambda h, i: (0, h))
    return [spec(COL_HQ), spec(COL_HF), spec(COL_HI), spec(COL_HG), vec, vec]


def hgrn_fwd(proj, lb, gn, *, ct, hp, name):
    T = proj.shape[0]
    nc = ct // CHUNK

    def body(hq_ref, hf_ref, hi_ref, hg_ref, lb_ref, gn_ref, og_ref, o_ref, st_ref, s_ref):
        @pl.when(pl.program_id(1) == 0)
        def _():
            s_ref[...] = jnp.zeros_like(s_ref)

        tri, _, masks = _chunk_consts()

        def chunk(c, carry):
            sl = pl.ds(pl.multiple_of(c * CHUNK, CHUNK), CHUNK)
            for p in range(hp):
                cols = slice(p * HG_DK, (p + 1) * HG_DK)
                lb_v = lb_ref[:, cols]
                hq = hq_ref[sl, cols].astype(F32)
                v = hi_ref[sl, cols].astype(F32)
                gate = hg_ref[sl, cols].astype(F32)
                q = hq * _sigmoid(hq)
                f = lb_v + (1.0 - lb_v) * _sigmoid(hf_ref[sl, cols].astype(F32))
                kk = 1.0 - f
                b = jnp.dot(tri, jnp.log(jnp.maximum(f, MIN_F)), precision=_HI, preferred_element_type=F32)
                st = s_ref[p]
                st_ref[p, c] = st
                a, _ = _intra_off(q, kk, b, masks)
                vb = v.astype(BF16)
                o = (_dot_nt((q * jnp.exp(b)).astype(BF16), st.astype(BF16)) + _dot(a.astype(BF16), vb)
                     + _diag_fwd(q, kk, v, b))
                bl = b[CHUNK - 1:CHUNK, :]
                s_ref[p] = st * jnp.exp(bl) + _dot_tn(vb, (kk * jnp.exp(bl - b)).astype(BF16))
                o_ref[sl, cols] = o
                r = lax.rsqrt(jnp.mean(o * o, axis=-1, keepdims=True) + EPS)
                og_ref[sl, cols] = (o * r * gn_ref[:, cols] * (gate * _sigmoid(gate))).astype(BF16)
            return carry

        lax.fori_loop(0, nc, chunk, 0)

    blk = pl.BlockSpec((ct, hp * HG_DK), lambda h, i: (i, h))
    return pl.pallas_call(
        body, name=name, grid=(HG_HEADS // hp, T // ct),
        in_specs=_hgrn_in_specs(ct, hp, lambda i: i),
        out_specs=[blk, blk, pl.BlockSpec((hp, nc, HG_DK, HG_DK), lambda h, i: (h, i, 0, 0))],
        out_shape=[jax.ShapeDtypeStruct((T, HG_WIDTH), BF16), jax.ShapeDtypeStruct((T, HG_WIDTH), F32),
                   jax.ShapeDtypeStruct((HG_HEADS, T // CHUNK, HG_DK, HG_DK), F32)],
        scratch_shapes=[pltpu.VMEM((hp, HG_DK, HG_DK), F32)],
        compiler_params=_cp(("parallel", "arbitrary")),
    )(proj, proj, proj, proj, lb, gn)


def hgrn_bwd(proj, lb, gn, o_raw, states, dog, dproj, *, ct, name):
    T = proj.shape[0]
    nc = ct // CHUNK
    nblk = T // ct
    hp = HG_HEADS

    def body(hq_ref, hf_ref, hi_ref, hg_ref, lb_ref, gn_ref, o_ref, st_ref, dog_ref, _dproj_in,
             dp_ref, dlb_ref, dgn_ref, ds_ref):
        dhq_ref, dhf_ref, dhi_ref, dhg_ref = (dp_ref.at[:, s * HG_WIDTH:(s + 1) * HG_WIDTH] for s in range(4))
        @pl.when(pl.program_id(1) == 0)
        def _():
            ds_ref[...] = jnp.zeros_like(ds_ref)
            dlb_ref[...] = jnp.zeros_like(dlb_ref)
            dgn_ref[...] = jnp.zeros_like(dgn_ref)

        tri, tri_t, masks = _chunk_consts()
        last_row = lax.broadcasted_iota(jnp.int32, (CHUNK, HG_DK), 0) == CHUNK - 1

        def chunk(ci, carry):
            c = nc - 1 - ci
            sl = pl.ds(pl.multiple_of(c * CHUNK, CHUNK), CHUNK)
            for p in range(hp):
                cols = slice(p * HG_DK, (p + 1) * HG_DK)
                lb_v = lb_ref[:, cols]
                gn_v = gn_ref[:, cols]
                hq = hq_ref[sl, cols].astype(F32)
                v = hi_ref[sl, cols].astype(F32)
                gate = hg_ref[sl, cols].astype(F32)
                sq = _sigmoid(hq)
                q = hq * sq
                sig = _sigmoid(hf_ref[sl, cols].astype(F32))
                f = lb_v + (1.0 - lb_v) * sig
                kk = 1.0 - f
                b = jnp.dot(tri, jnp.log(jnp.maximum(f, MIN_F)), precision=_HI, preferred_element_type=F32)
                st = st_ref[p, c]
                dst = ds_ref[p]
                o = o_ref[sl, cols]
                r = lax.rsqrt(jnp.mean(o * o, axis=-1, keepdims=True) + EPS)
                xhat = o * r
                sg = _sigmoid(gate)
                dog_v = dog_ref[sl, cols].astype(F32)
                dy = dog_v * (gate * sg)
                dhg_ref[sl, cols] = (dog_v * xhat * gn_v * (sg * (1.0 + gate * (1.0 - sg)))).astype(BF16)
                dgn_ref[:, cols] += _rowsum8(dy * xhat)
                dxh = dy * gn_v
                do = r * (dxh - xhat * jnp.mean(dxh * xhat, axis=-1, keepdims=True))
                dob = do.astype(BF16)
                vb = v.astype(BF16)
                stb = st.astype(BF16)
                dstb = dst.astype(BF16)
                eb = jnp.exp(b)
                bl = b[CHUNK - 1:CHUNK, :]
                ebl = jnp.exp(bl)
                edec = jnp.exp(bl - b)
                qe = q * eb
                kdec = kk * edec
                a, keep = _intra_off(q, kk, b, masks)
                da = _dot_nt(dob, vb)
                dkdec = _dot(vb, dstb)
                dq = _dot(dob, stb) * eb
                dk = dkdec * edec
                dv = _dot_tn(a.astype(BF16), dob) + _dot_nt(kdec.astype(BF16), dstb)
                for mask, eq, ek, qt, kt in keep:
                    dsm = jnp.where(mask, da, 0.0).astype(BF16)
                    dq = dq + _dot(dsm, kt) * eq
                    dk = dk + _dot_tn(dsm, qt) * ek
                ddq, ddk, ddv = _diag_bwd(q, kk, v, b, do)
                dq = dq + ddq
                dk = dk + ddk
                dhi_ref[sl, cols] = (dv + ddv).astype(BF16)
                dbl = jnp.sum(dkdec * kdec, axis=0, keepdims=True) + ebl * jnp.sum(st * dst, axis=0, keepdims=True)
                db = q * dq - kk * dk
                db = jnp.where(last_row, db + dbl, db)
                dlogf = jnp.dot(tri_t, db, precision=_HI, preferred_element_type=F32)
                ds_ref[p] = _dot_tn(dob, qe.astype(BF16)) + dst * ebl
                df = jnp.where(f > MIN_F, dlogf / f, 0.0) - dk
                dhf_ref[sl, cols] = (df * (1.0 - lb_v) * sig * (1.0 - sig)).astype(BF16)
                dlb_ref[:, cols] += _rowsum8(df * (1.0 - sig))
                dhq_ref[sl, cols] = (dq * (sq * (1.0 + hq * (1.0 - sq)))).astype(BF16)
            return carry

        lax.fori_loop(0, nc, chunk, 0)

    rev = lambda i: nblk - 1 - i
    blk = pl.BlockSpec((ct, hp * HG_DK), lambda h, i: (rev(i), h))
    acc = pl.BlockSpec((8, hp * HG_DK), lambda h, i: (0, h))
    return pl.pallas_call(
        body, name=name, grid=(HG_HEADS // hp, nblk),
        in_specs=_hgrn_in_specs(ct, hp, rev) + [blk, pl.BlockSpec((hp, nc, HG_DK, HG_DK), lambda h, i: (h, rev(i), 0, 0)), blk, _ANY],
        out_specs=[pl.BlockSpec((ct, 4 * HG_WIDTH), lambda h, i: (rev(i), 0)), acc, acc],
        out_shape=[jax.ShapeDtypeStruct(dproj.shape, dproj.dtype)] + [jax.ShapeDtypeStruct((8, HG_WIDTH), F32)] * 2,
        scratch_shapes=[pltpu.VMEM((hp, HG_DK, HG_DK), F32)],
        input_output_aliases={9: 0},
        compiler_params=_cp(("parallel", "arbitrary")),
    )(proj, proj, proj, proj, lb, gn, o_raw, states, dog, dproj)


_GW = ATT_GROUP * HEAD_DIM
_NEG = -1e30


def rope_tables(T):
    half = ROPE_DIM // 2
    inv = ROPE_THETA ** (-jnp.arange(half, dtype=F32) * 2.0 / ROPE_DIM)
    ang = jnp.arange(T, dtype=F32)[:, None] * inv[None, :]
    pad1 = jnp.ones((T, HEAD_DIM - ROPE_DIM), F32)
    cos = jnp.concatenate([jnp.cos(ang), jnp.cos(ang), pad1], axis=1)
    sin = jnp.concatenate([jnp.sin(ang), jnp.sin(ang), 0.0 * pad1], axis=1)
    p = np.zeros((_GW, _GW), np.float32)
    for base in range(0, _GW, HEAD_DIM):
        for i in range(half):
            p[base + i + half, base + i] = -1.0
            p[base + i, base + i + half] = 1.0
    reps = _GW // HEAD_DIM
    return jnp.tile(cos, (1, reps)), jnp.tile(sin, (1, reps)), jnp.asarray(p, BF16)


def _rope(x, cos, sin, pm):
    return x.astype(F32) * cos + _dot(x, pm) * sin


def _unrope(dx, cos, sin, pm):
    return dx * cos + _dot_nt((dx * sin).astype(BF16), pm)


_GQ = ATT_GROUP * WINDOW


def _swa_mask(n, queries_on_rows):
    shape = (_GQ, 2 * WINDOW) if queries_on_rows else (2 * WINDOW, _GQ)
    r = lax.broadcasted_iota(jnp.int32, shape, 0 if queries_on_rows else 1) & (WINDOW - 1)
    c = lax.broadcasted_iota(jnp.int32, shape, 1 if queries_on_rows else 0)
    delta = r + WINDOW - c
    return (delta >= 0) & (delta < WINDOW) & ((c >= WINDOW) | (n > 0))


def _sink_vector(sink_ref, g, queries_on_rows):
    shape = (_GQ, 1) if queries_on_rows else (1, _GQ)
    i = lax.broadcasted_iota(jnp.int32, shape, 0 if queries_on_rows else 1)
    out = jnp.full(shape, sink_ref[g * ATT_GROUP + ATT_GROUP - 1], F32)
    for hh in range(ATT_GROUP - 2, -1, -1):
        out = jnp.where(i < (hh + 1) * WINDOW, sink_ref[g * ATT_GROUP + hh], out)
    return out


def _head(x, h):
    return x[:, h * HEAD_DIM:(h + 1) * HEAD_DIM]


def _stack_heads(x):
    return jnp.concatenate([_head(x, hh) for hh in range(ATT_GROUP)], axis=0)


def _unstack_heads(x):
    return jnp.concatenate([x[hh * WINDOW:(hh + 1) * WINDOW] for hh in range(ATT_GROUP)], axis=1)


def _swa_specs(cur):
    prev = lambda n: jnp.maximum(n - 1, 0)
    kv = lambda col, f: pl.BlockSpec((WINDOW, _GW), lambda n: (f(n), col))
    tab = lambda f: pl.BlockSpec((WINDOW, _GW), lambda n: (f(n), 0))
    return [pl.BlockSpec((WINDOW, ATT_WIDTH), lambda n: (cur(n), COL_AQ // 8)),
            kv(COL_AK // 2, prev), kv(COL_AK // 2, cur), kv(COL_AV // 2, prev), kv(COL_AV // 2, cur),
            tab(prev), tab(cur), tab(prev), tab(cur),
            pl.BlockSpec((_GW, _GW), lambda n: (0, 0)), pl.BlockSpec(memory_space=pltpu.SMEM)]


def swa_fwd(proj, cos, sin, pm, sinks, *, name):
    T = proj.shape[0]
    nb = T // WINDOW
    scale = HEAD_DIM ** -0.5

    def body(q_ref, kp_ref, kc_ref, vp_ref, vc_ref, cp_ref, cc_ref, sp_ref, sc_ref, pm_ref, sink_ref, o_ref):
        n = pl.program_id(0)
        pm_v = pm_ref[...]
        mask = _swa_mask(n, True)
        k_cat = jnp.concatenate([_rope(kp_ref[...], cp_ref[...], sp_ref[...], pm_v),
                                 _rope(kc_ref[...], cc_ref[...], sc_ref[...], pm_v)], axis=0).astype(BF16)
        v_cat = jnp.concatenate([vp_ref[...], vc_ref[...]], axis=0)
        for g in range(ATT_KV_HEADS):
            qs = _stack_heads(_rope(q_ref[:, g * _GW:(g + 1) * _GW], cc_ref[...], sc_ref[...], pm_v).astype(BF16))
            sk = _sink_vector(sink_ref, g, True)
            s = jnp.where(mask, _dot_nt(qs, _head(k_cat, g)) * scale, _NEG)
            m = jnp.maximum(jnp.max(s, axis=-1, keepdims=True), sk)
            p = jnp.exp(s - m)
            l = jnp.sum(p, axis=-1, keepdims=True) + jnp.exp(sk - m)
            o = _dot(p.astype(BF16), _head(v_cat, g)) / l
            o_ref[:, g * _GW:(g + 1) * _GW] = _unstack_heads(o).astype(BF16)

    return pl.pallas_call(
        body, name=name, grid=(nb,), in_specs=_swa_specs(lambda n: n),
        out_specs=pl.BlockSpec((WINDOW, ATT_WIDTH), lambda n: (n, 0)),
        out_shape=jax.ShapeDtypeStruct((T, ATT_WIDTH), BF16), compiler_params=_cp(("parallel",)),
    )(proj, proj, proj, proj, proj, cos, cos, sin, sin, pm, sinks)


def swa_bwd(proj, cos, sin, pm, sinks, att, datt, dproj, *, name):
    T = proj.shape[0]
    nb = T // WINDOW
    scale = HEAD_DIM ** -0.5
    clamp = lambda n: jnp.minimum(n, nb - 1)

    def body(q_ref, kp_ref, kc_ref, vp_ref, vc_ref, cp_ref, cc_ref, sp_ref, sc_ref, pm_ref, sink_ref, att_ref, datt_ref,
             _dproj_in, dp_ref, dsink_ref, cq_ref, ck_ref, cv_ref):
        n = pl.program_id(0)
        pm_v = pm_ref[...]

        @pl.when(n == 0)
        def _():
            dsink_ref[...] = jnp.zeros_like(dsink_ref)
            cq_ref[...] = jnp.zeros_like(cq_ref)
            ck_ref[...] = jnp.zeros_like(ck_ref)
            cv_ref[...] = jnp.zeros_like(cv_ref)

        def write_prev(dk_prev, dv_prev):
            dp_ref[:, :ATT_WIDTH] = cq_ref[...]
            dk = _unrope(ck_ref[...] + dk_prev, cp_ref[...], sp_ref[...], pm_v)
            dp_ref[:, ATT_WIDTH:ATT_WIDTH + _GW] = dk.astype(BF16)
            dp_ref[:, ATT_WIDTH + _GW:] = (cv_ref[...] + dv_prev).astype(BF16)

        @pl.when(n < nb)
        def _():
            mask = _swa_mask(n, False)
            ones = jnp.ones((8, HEAD_DIM), F32)
            k_cat = jnp.concatenate([_rope(kp_ref[...], cp_ref[...], sp_ref[...], pm_v),
                                     _rope(kc_ref[...], cc_ref[...], sc_ref[...], pm_v)], axis=0).astype(BF16)
            v_cat = jnp.concatenate([vp_ref[...], vc_ref[...]], axis=0)
            dqs, dks, dvs = [], [], []
            for g in range(ATT_KV_HEADS):
                gc = slice(g * _GW, (g + 1) * _GW)
                qs = _stack_heads(_rope(q_ref[:, gc], cc_ref[...], sc_ref[...], pm_v).astype(BF16))
                dos = _stack_heads(datt_ref[:, gc])
                os_ = _stack_heads(att_ref[:, gc])
                kh, vh = _head(k_cat, g), _head(v_cat, g)
                sk = _sink_vector(sink_ref, g, False)
                s = jnp.where(mask, _dot_nt(kh, qs) * scale, _NEG)
                m = jnp.maximum(jnp.max(s, axis=0, keepdims=True), sk)
                e = jnp.exp(s - m)
                es = jnp.exp(sk - m)
                inv_l = 1.0 / (jnp.sum(e, axis=0, keepdims=True) + es)
                p = e * inv_l
                dsum = lax.dot_general(ones, dos.astype(F32) * os_.astype(F32), (((1,), (1,)), ((), ())),
                                       precision=_HI, preferred_element_type=F32)[0:1]
                dsink_ref[g, 0:1, :] += -(es * inv_l) * dsum
                ds = (p * (_dot_nt(vh, dos) - dsum) * scale).astype(BF16)
                dqs.append(_unrope(_unstack_heads(_dot_tn(ds, kh)), cc_ref[...], sc_ref[...], pm_v))
                dks.append(_dot(ds, qs))
                dvs.append(_dot(p.astype(BF16), dos))
            dk_all = jnp.concatenate(dks, axis=1)
            dv_all = jnp.concatenate(dvs, axis=1)

            @pl.when(n > 0)
            def _():
                write_prev(dk_all[:WINDOW], dv_all[:WINDOW])

            cq_ref[...] = jnp.concatenate(dqs, axis=1).astype(BF16)
            ck_ref[...] = dk_all[WINDOW:]
            cv_ref[...] = dv_all[WINDOW:]

        @pl.when(n == nb)
        def _():
            write_prev(jnp.zeros((WINDOW, _GW), F32), jnp.zeros((WINDOW, _GW), F32))

    row = pl.BlockSpec((WINDOW, ATT_WIDTH), lambda n: (clamp(n), 0))
    slab = ATT_WIDTH + 2 * _GW
    return pl.pallas_call(
        body, name=name, grid=(nb + 1,), in_specs=_swa_specs(clamp) + [row, row, _ANY],
        out_specs=[pl.BlockSpec((WINDOW, slab), lambda n: (jnp.maximum(n - 1, 0), COL_AQ * HG_DK // slab)),
                   pl.BlockSpec((ATT_KV_HEADS, 8, _GQ), lambda n: (0, 0, 0))],
        out_shape=[jax.ShapeDtypeStruct(dproj.shape, dproj.dtype), jax.ShapeDtypeStruct((ATT_KV_HEADS, 8, _GQ), F32)],
        scratch_shapes=[pltpu.VMEM((WINDOW, ATT_WIDTH), BF16), pltpu.VMEM((WINDOW, _GW), F32), pltpu.VMEM((WINDOW, _GW), F32)],
        input_output_aliases={13: 0},
        compiler_params=_cp(("arbitrary",)),
    )(proj, proj, proj, proj, proj, cos, cos, sin, sin, pm, sinks, att, datt, dproj)


def swap_with_sibling(arrays, *, name):
    n = len(arrays)

    def body(*refs):
        ins, outs = refs[:n], refs[n:2 * n]
        send_sems, recv_sems = refs[2 * n:]
        x, y, c, _ = _place()
        copies = [pltpu.make_async_remote_copy(src_ref=ins[a], dst_ref=outs[a], send_sem=send_sems.at[a],
                                               recv_sem=recv_sems.at[a], device_id=(x, y, 1 - c), device_id_type=MESH)
                  for a in range(n)]
        for cp in copies:
            cp.start()
        for cp in copies:
            cp.wait()

    return pl.pallas_call(
        body, name=name, out_shape=[jax.ShapeDtypeStruct(a.shape, a.dtype) for a in arrays],
        in_specs=[_ANY] * n, out_specs=[_ANY] * n,
        scratch_shapes=[pltpu.SemaphoreType.DMA((n,)), pltpu.SemaphoreType.DMA((n,))],
    )(*arrays)


N_DEV = 8
SMALL_ROWS = 24


def allreduce_small(v, *, name):
    def body(v_ref, o_ref, recv_ref, send_sems, recv_sems):
        x, y, c, _ = _place()
        me = 4 * x + 2 * y + c
        recv_ref[me] = v_ref[...]

        def copy(k, slot):
            peer = (x ^ (k >> 2), y ^ ((k >> 1) & 1), c ^ (k & 1))
            return pltpu.make_async_remote_copy(src_ref=v_ref, dst_ref=recv_ref.at[slot], send_sem=send_sems.at[k - 1],
                                                recv_sem=recv_sems.at[k - 1], device_id=peer, device_id_type=MESH)

        for k in range(1, N_DEV):
            copy(k, me).start()
        for k in range(1, N_DEV):
            copy(k, me ^ k).wait()
        acc = recv_ref[0]
        for d in range(1, N_DEV):
            acc = acc + recv_ref[d]
        o_ref[...] = acc

    vm = pl.BlockSpec(memory_space=pltpu.VMEM)
    return pl.pallas_call(
        body, name=name, out_shape=jax.ShapeDtypeStruct(v.shape, v.dtype), in_specs=[vm], out_specs=vm,
        scratch_shapes=[pltpu.VMEM((N_DEV,) + v.shape, v.dtype), pltpu.SemaphoreType.DMA((N_DEV - 1,)),
                        pltpu.SemaphoreType.DMA((N_DEV - 1,))],
    )(v)


def sum_parts(parts, *, tr, name):
    _, R, C = parts.shape

    def body(p_ref, o_ref):
        acc = p_ref[0].astype(F32)
        for q in range(1, N_CHIPS):
            acc = acc + p_ref[q].astype(F32)
        o_ref[...] = acc

    return pl.pallas_call(
        body, name=name, grid=(R // tr,), in_specs=[pl.BlockSpec((N_CHIPS, tr, C), lambda i: (0, i, 0))],
        out_specs=pl.BlockSpec((tr, C), lambda i: (i, 0)), out_shape=jax.ShapeDtypeStruct((R, C), F32),
        compiler_params=_cp(("parallel",)),
    )(parts)


def adamw(g_a, g_b, w, m, v, *, tr, name):
    R, C = w.shape
    c1 = 1.0 - ADAM_B1 ** ADAM_STEP
    c2 = 1.0 - ADAM_B2 ** ADAM_STEP
    ng = 1 if g_b is None else 2

    def body(*refs):
        w_ref, m_ref, v_ref, g_ref, d_ref, nm_ref, nv_ref = refs[ng:]
        g = refs[0][...] if ng == 1 else refs[0][...] + refs[1][...]
        nm = ADAM_B1 * m_ref[...] + (1.0 - ADAM_B1) * g
        nv = ADAM_B2 * v_ref[...] + (1.0 - ADAM_B2) * (g * g)
        g_ref[...] = g
        nm_ref[...] = nm
        nv_ref[...] = nv
        d_ref[...] = -ADAM_LR * ((nm / c1) / (jnp.sqrt(nv / c2) + ADAM_EPS) + ADAM_WD * w_ref[...])

    blk = pl.BlockSpec((tr, C), lambda i: (i, 0))
    gs = [g_a] if g_b is None else [g_a, g_b]
    return pl.pallas_call(
        body, name=name, grid=(R // tr,), in_specs=[blk] * (ng + 3), out_specs=[blk] * 4,
        out_shape=[jax.ShapeDtypeStruct((R, C), F32)] * 4, compiler_params=_cp(("parallel",)),
    )(*gs, w, m, v)


def _lb_bounds(lb_logits):
    p = jax.nn.softmax(lb_logits.astype(F32), axis=0)
    return jnp.cumsum(p, axis=0) - p[0:1]


def _pack_small(n1, lb, hgn, n2, fin, sinks, extra):
    row = jnp.concatenate([sinks.reshape(-1), jnp.reshape(extra, (1,)),
                           jnp.zeros((D_MODEL - sinks.size - 1,), F32)])
    return jnp.concatenate([n1, lb, hgn, n2, fin[None], row[None], jnp.zeros((SMALL_ROWS - 18, D_MODEL), F32)], axis=0)


def _unpack_small(buf):
    return (buf[0:4], buf[4:8], buf[8:12], buf[12:16], buf[16], buf[17, :DEPTH * ATT_Q_HEADS].reshape(DEPTH, ATT_Q_HEADS),
            buf[17, DEPTH * ATT_Q_HEADS])


def _cols_full(g):
    q, r, c = g.shape
    return jnp.transpose(g, (1, 0, 2)).reshape(r, q * c)


def _cols_split(w):
    r, qc = w.shape
    return jnp.transpose(w.reshape(r, N_CHIPS, qc // N_CHIPS), (1, 0, 2))


def kernel(x, norm1, w_in, lb_logits, hg_norm, attn_sinks, w_pa, w_pb, w_o, norm2, w_gate, w_up, w_down, final_norm, loss_target, m_norm1, m_w_in, m_lb_logits, m_hg_norm, m_attn_sinks, m_w_pa, m_w_pb, m_w_o, m_norm2, m_w_gate, m_w_up, m_w_down, m_final_norm, v_norm1, v_w_in, v_lb_logits, v_hg_norm, v_attn_sinks, v_w_pa, v_w_pb, v_w_o, v_norm2, v_w_gate, v_w_up, v_w_down, v_final_norm):
    T = x.shape[1]
    H = FFN_HIDDEN
    tm = min(512, T)
    big = dict(w_in=w_in, w_pa=w_pa, w_pb=w_pb, w_o=w_o, w_gate=w_gate, w_up=w_up, w_down=w_down)
    col_sharded = ("w_in", "w_gate", "w_up")

    shards = [w.astype(BF16) for w in big.values()]

    def layer_weights(gathered):
        full = {k: (_cols_full(g) if k in col_sharded else g.reshape(-1, g.shape[-1])) for k, g in zip(big, gathered)}
        full["w_in"] = _to_kernel_cols(full["w_in"])
        full["w_gu"] = jnp.concatenate([full.pop("w_gate"), full.pop("w_up")], axis=-1)
        return full

    lb_all, lb_vjp = jax.vjp(_lb_bounds, lb_logits)
    cos, sin, pm = rope_tables(T)

    xs = x[0]
    saved = []
    weights = [layer_weights(exchange_between_chips([s[0] for s in shards], scatter=False, name="gather_weights_0"))]
    for l in range(DEPTH):
        full = weights[l]
        nxt = [s[l + 1] for s in shards] if l + 1 < DEPTH else []
        h, proj, *gathered = ln_matmul(xs, norm1[l:l + 1], full["w_in"], tm=tm, tn=1536, name=f"in_proj_{l}", gather=nxt)
        if nxt:
            weights.append(layer_weights(gathered))
        og, o_raw, states = hgrn_fwd(proj, lb_all[l:l + 1], hg_norm[l:l + 1], ct=min(256, T), hp=HG_HEADS, name=f"hgrn_fwd_{l}")
        att = swa_fwd(proj, cos, sin, pm, attn_sinks[l], name=f"swa_fwd_{l}")
        ya, yb, mix = merge_fwd(og, att, proj, full["w_pa"], full["w_pb"], tm=tm, name=f"merge_fwd_{l}")
        x1 = matmul_residual(xs, mix, full["w_o"], tm=tm, name=f"out_proj_{l}")
        h2, gu, act = ln_matmul(x1, norm2[l:l + 1], full["w_gu"], tm=min(256, T), tn=2 * H, name=f"ffn_up_{l}", swiglu=True)
        x2 = matmul_residual(x1, act, full["w_down"], tm=tm, name=f"ffn_down_{l}")
        saved.append((xs, h, proj, og, o_raw, states, att, ya, yb, mix, x1, h2, gu, act))
        xs = x2

    dx, loss_parts, dfin_parts = loss_head(xs, final_norm[None], loss_target[0], tm=tm, name="loss_head")

    arrived = [None] * DEPTH
    dn1, dn2, dlb, dgn, dsk = ([None] * DEPTH for _ in range(5))
    for l in reversed(range(DEPTH)):
        xs, h, proj, og, o_raw, states, att, ya, yb, mix, x1, h2, gu, act = saved[l]
        full = weights[l]
        dw = {}
        dgu = ffn_down_bwd(dx, full["w_down"], gu, tm=min(256, T), name=f"ffn_down_bwd_{l}")
        dw["w_down"] = matmul_tn(act, dx, tka=H // 2, tn=D_MODEL, tk=tm, name=f"dw_down_{l}")
        dwgu = matmul_tn(h2, dgu, tka=D_MODEL, tn=H // 2, tk=tm, name=f"dw_gu_{l}")
        dw["w_gate"], dw["w_up"] = dwgu[:, :H], dwgu[:, H:]
        dx1, dn2_parts = matmul_nt_rmsbwd(dgu, full["w_gu"], x1, norm2[l:l + 1], dx, tm=tm, tk=H // 2, name=f"ffn_up_bwd_{l}")
        dya, dyb, dproj = merge_bwd(dx1, full["w_o"], proj, ya, yb, tm=tm, name=f"merge_bwd_{l}")
        dw["w_o"] = matmul_tn(mix, dx1, tka=D_MODEL, tn=D_MODEL, tk=tm, name=f"dw_o_{l}")
        dog = matmul_nt(dya, full["w_pa"], tm=tm, name=f"pa_bwd_{l}")
        datt = matmul_nt(dyb, full["w_pb"], tm=tm, name=f"pb_bwd_{l}")
        dw["w_pa"] = matmul_tn(og, dya, tka=D_MODEL, tn=D_MODEL, tk=tm, name=f"dw_pa_{l}")
        dw["w_pb"] = matmul_tn(att, dyb, tka=D_MODEL, tn=D_MODEL, tk=tm, name=f"dw_pb_{l}")
        dproj, dsk_parts = swa_bwd(proj, cos, sin, pm, attn_sinks[l], att, datt, dproj, name=f"swa_bwd_{l}")
        dproj, dlb_parts, dgn_parts = hgrn_bwd(proj, lb_all[l:l + 1], hg_norm[l:l + 1], o_raw, states, dog, dproj,
                                               ct=min(256, T), name=f"hgrn_bwd_{l}")
        dw["w_in"] = _from_kernel_cols(matmul_tn(h, dproj, tka=D_MODEL, tn=1536, tk=tm, name=f"dw_in_{l}"))
        parts = [(_cols_split(dw[k]) if k in col_sharded else dw[k].reshape(N_CHIPS, -1, dw[k].shape[-1])) for k in big]
        dx, dn1_parts, *arrived[l] = matmul_nt_rmsbwd(dproj, full["w_in"], xs, norm1[l:l + 1], dx1, tm=tm, tk=1536,
                                                      name=f"in_proj_bwd_{l}", scatter=parts)
        dn1[l], dn2[l], dlb[l], dgn[l] = dn1_parts.sum(0), dn2_parts.sum(0), dlb_parts.sum(0), dgn_parts.sum(0)
        dsk[l] = dsk_parts[:, 0, :].reshape(ATT_Q_HEADS, WINDOW).sum(-1)
    grad_x = dx[None]

    mine = _pack_small(jnp.stack(dn1), jnp.stack(dlb), jnp.stack(dgn), jnp.stack(dn2), dfin_parts.sum(0), jnp.stack(dsk),
                       loss_parts.sum())
    total = allreduce_small(mine, name="allreduce_small")
    g_n1, g_lb_all, g_gn, g_n2, g_fin, g_sk, loss = _unpack_small(total)
    (g_lb,) = lb_vjp(g_lb_all)
    zero = jnp.zeros((), F32)
    small = adamw(_pack_small(g_n1, g_lb, g_gn, g_n2, g_fin, g_sk, zero), None,
                  _pack_small(norm1, lb_logits, hg_norm, norm2, final_norm, attn_sinks, zero),
                  _pack_small(m_norm1, m_lb_logits, m_hg_norm, m_norm2, m_final_norm, m_attn_sinks, zero),
                  _pack_small(v_norm1, v_lb_logits, v_hg_norm, v_norm2, v_final_norm, v_attn_sinks, zero),
                  tr=SMALL_ROWS, name="adamw_small")
    small = [_unpack_small(s)[:6] for s in small]

    sums = []
    for i, k in enumerate(big):
        a = jnp.stack([arrived[l][i] for l in range(DEPTH)], axis=1)
        q, nl, r, c = a.shape
        sums.append(sum_parts(a.reshape(q, nl * r, c), tr=128, name=f"sum_{k}"))
    theirs = swap_with_sibling(sums, name="swap_sums")
    mom = dict(w_in=(m_w_in, v_w_in), w_pa=(m_w_pa, v_w_pa), w_pb=(m_w_pb, v_w_pb), w_o=(m_w_o, v_w_o),
               w_gate=(m_w_gate, v_w_gate), w_up=(m_w_up, v_w_up), w_down=(m_w_down, v_w_down))
    upd = {}
    for k, mine_k, theirs_k in zip(big, sums, theirs):
        shp = big[k].shape
        flat = lambda t: t.reshape(shp[0] * shp[1], shp[2])
        res = adamw(mine_k, theirs_k, flat(big[k]), flat(mom[k][0]), flat(mom[k][1]), tr=128, name=f"adamw_{k}")
        upd[k] = [t.reshape(shp) for t in res]

    order = ("norm1", "w_in", "lb_logits", "hg_norm", "attn_sinks", "w_pa", "w_pb", "w_o", "norm2", "w_gate", "w_up",
             "w_down", "final_norm")
    small_pos = dict(norm1=0, lb_logits=1, hg_norm=2, norm2=3, final_norm=4, attn_sinks=5)
    outs = [loss, grad_x]
    for kind in range(4):
        for name in order:
            outs.append(upd[name][kind] if name in upd else small[kind][small_pos[name]])
    return tuple(outs)
```

```python
import functools

import jax
import jax.numpy as jnp
import numpy as np
from jax import lax
from jax.experimental import pallas as pl
from jax.experimental.pallas import tpu as pltpu

F32 = jnp.float32
BF16 = jnp.bfloat16

D_MODEL = 1024
DEPTH = 4
HG_HEADS = 8
HG_DK = 128
HG_WIDTH = HG_HEADS * HG_DK
CHUNK = 64
ATT_Q_HEADS = 16
ATT_KV_HEADS = 4
ATT_GROUP = ATT_Q_HEADS // ATT_KV_HEADS
HEAD_DIM = 64
ATT_WIDTH = ATT_Q_HEADS * HEAD_DIM
ATT_KV_WIDTH = ATT_KV_HEADS * HEAD_DIM
WINDOW = 128
ROPE_THETA = 500000.0
ROPE_DIM = HEAD_DIM // 4
FFN_HIDDEN = 2816
IN_COLS = 4 * HG_WIDTH + ATT_WIDTH + 2 * ATT_KV_WIDTH + 2 * D_MODEL
EPS = 1e-6
MIN_F = 1e-30
COL_HQ, COL_HF, COL_HI, COL_HG = 0, 8, 16, 24
COL_GA, COL_GB, COL_AQ, COL_AK, COL_AV = 32, 40, 48, 56, 58


def _to_kernel_cols(w):
    return jnp.concatenate([w[..., :4096], w[..., 5632:7680], w[..., 4096:5120], w[..., 5120:5632]], axis=-1)


def _from_kernel_cols(w):
    return jnp.concatenate([w[..., :4096], w[..., 6144:7168], w[..., 7168:7680], w[..., 4096:6144]], axis=-1)

ADAM_LR = 0.001
ADAM_B1 = 0.9
ADAM_B2 = 0.999
ADAM_EPS = 1e-08
ADAM_WD = 0.01
ADAM_STEP = 10

VMEM_LIMIT_V7X = 56 * 1024 * 1024
MESH = pl.DeviceIdType.MESH


def _cp(sem, vmem=VMEM_LIMIT_V7X):
    return pltpu.CompilerParams(dimension_semantics=sem, vmem_limit_bytes=vmem)


def _sigmoid(x):
    return 1.0 / (1.0 + jnp.exp(-x))


def _dot(a, b):
    return jnp.dot(a, b, preferred_element_type=F32)


def _dot_nt(a, b):
    return lax.dot_general(a, b, (((1,), (1,)), ((), ())), preferred_element_type=F32)


def _dot_tn(a, b):
    return lax.dot_general(a, b, (((0,), (0,)), ((), ())), preferred_element_type=F32)


def _rowsum8(v):
    r, n = v.shape
    return jnp.sum(v.reshape(r // 8, 8, n), axis=0)


_ANY = pl.BlockSpec(memory_space=pl.ANY)
N_CHIPS = 4
PEER_CHIPS = N_CHIPS - 1


def _place():
    x, y, c = lax.axis_index("x"), lax.axis_index("y"), lax.axis_index("c")
    return x, y, c, [(1 - x, y), (x, 1 - y), (1 - x, 1 - y)]


def _exchange_scratch(n):
    return [pltpu.SemaphoreType.DMA((PEER_CHIPS * n,)), pltpu.SemaphoreType.DMA((PEER_CHIPS * n,)), pltpu.SemaphoreType.DMA((n,))]


def _exchange(ins, outs, sems, scatter):
    send_sems, recv_sems, local_sems = sems
    x, y, c, chips = _place()
    me = 2 * x + y
    n = len(ins)

    def remote(a, j, arriving):
        px, py = chips[j]
        them = 2 * px + py
        src = ins[a].at[them] if scatter else ins[a]
        return pltpu.make_async_remote_copy(
            src_ref=src, dst_ref=outs[a].at[them if arriving else me], send_sem=send_sems.at[a * PEER_CHIPS + j],
            recv_sem=recv_sems.at[a * PEER_CHIPS + j], device_id=(px, py, c), device_id_type=MESH)

    def local(a):
        return pltpu.make_async_copy(ins[a].at[me] if scatter else ins[a], outs[a].at[me], local_sems.at[a])

    def start():
        for a in range(n):
            local(a).start()
            for j in range(PEER_CHIPS):
                remote(a, j, False).start()

    def wait():
        for a in range(n):
            for j in range(PEER_CHIPS):
                remote(a, j, True).wait()
            local(a).wait()

    return start, wait


def _exchange_out_shapes(arrays, scatter):
    return [jax.ShapeDtypeStruct(a.shape if scatter else (N_CHIPS,) + a.shape, a.dtype) for a in arrays]


def exchange_between_chips(arrays, *, scatter, name):
    n = len(arrays)

    def body(*refs):
        start, wait = _exchange(refs[:n], refs[n:2 * n], refs[2 * n:], scatter)
        start()
        wait()

    return pl.pallas_call(
        body, name=name, out_shape=_exchange_out_shapes(arrays, scatter), in_specs=[_ANY] * n, out_specs=[_ANY] * n,
        scratch_shapes=_exchange_scratch(n),
    )(*arrays)


def ln_matmul(x, g, w, *, tm, tn, name, swiglu=False, gather=()):
    T, Dm = x.shape
    N = w.shape[1]
    if swiglu:
        assert tn == N
    half = N // 2
    ng = len(gather)
    n_out = 3 if swiglu else 2
    ni, nj = T // tm, N // tn

    def body(x_ref, g_ref, w_ref, *rest):
        gather_in, rest = rest[:ng], rest[ng:]
        h_ref, o_ref = rest[0], rest[1]
        i, j = pl.program_id(0), pl.program_id(1)
        if ng:
            start, wait = _exchange(gather_in, rest[n_out:n_out + ng], rest[n_out + ng:], False)
            pl.when((i == 0) & (j == 0))(start)

        @pl.when(j == 0)
        def _():
            xf = x_ref[...]
            r = lax.rsqrt(jnp.mean(xf * xf, axis=-1, keepdims=True) + EPS)
            h_ref[...] = (xf * r * g_ref[...]).astype(BF16)

        acc = _dot(h_ref[...], w_ref[...])
        o_ref[...] = acc.astype(BF16)
        if swiglu:
            gt, up = acc[:, :half], acc[:, half:]
            rest[2][...] = (gt * _sigmoid(gt) * up).astype(BF16)
        if ng:
            pl.when((i == ni - 1) & (j == nj - 1))(wait)

    out_shape = [jax.ShapeDtypeStruct((T, Dm), BF16), jax.ShapeDtypeStruct((T, N), BF16)]
    out_specs = [pl.BlockSpec((tm, Dm), lambda i, j: (i, 0)), pl.BlockSpec((tm, tn), lambda i, j: (i, j))]
    if swiglu:
        out_shape.append(jax.ShapeDtypeStruct((T, half), BF16))
        out_specs.append(pl.BlockSpec((tm, half), lambda i, j: (i, 0)))
    return pl.pallas_call(
        body, name=name, grid=(ni, nj),
        in_specs=[pl.BlockSpec((tm, Dm), lambda i, j: (i, 0)), pl.BlockSpec((1, Dm), lambda i, j: (0, 0)),
                  pl.BlockSpec((Dm, tn), lambda i, j: (0, j))] + [_ANY] * ng,
        out_specs=out_specs + [_ANY] * ng, out_shape=out_shape + _exchange_out_shapes(gather, False),
        scratch_shapes=_exchange_scratch(ng) if ng else [],
        compiler_params=_cp(("arbitrary", "arbitrary") if ng else ("parallel", "arbitrary")),
    )(x, g, w, *gather)


def matmul_residual(x, a, w, *, tm, name):
    T, N = x.shape
    K = a.shape[1]

    def body(x_ref, a_ref, w_ref, o_ref):
        o_ref[...] = x_ref[...] + _dot(a_ref[...], w_ref[...])

    return pl.pallas_call(
        body, name=name, grid=(T // tm,),
        in_specs=[pl.BlockSpec((tm, N), lambda i: (i, 0)), pl.BlockSpec((tm, K), lambda i: (i, 0)),
                  pl.BlockSpec((K, N), lambda i: (0, 0))],
        out_specs=pl.BlockSpec((tm, N), lambda i: (i, 0)), out_shape=jax.ShapeDtypeStruct((T, N), F32),
        compiler_params=_cp(("parallel",)),
    )(x, a, w)


def matmul_nt(a, w, *, tm, name):
    T, K = a.shape
    N = w.shape[0]

    def body(a_ref, w_ref, o_ref):
        o_ref[...] = _dot_nt(a_ref[...], w_ref[...]).astype(BF16)

    return pl.pallas_call(
        body, name=name, grid=(T // tm,),
        in_specs=[pl.BlockSpec((tm, K), lambda i: (i, 0)), pl.BlockSpec((N, K), lambda i: (0, 0))],
        out_specs=pl.BlockSpec((tm, N), lambda i: (i, 0)), out_shape=jax.ShapeDtypeStruct((T, N), BF16),
        compiler_params=_cp(("parallel",)),
    )(a, w)


def matmul_tn(a, g, *, tka, tn, tk, name):
    T, Ka = a.shape
    N = g.shape[1]
    nk = T // tk

    def body(a_ref, g_ref, o_ref, acc_ref):
        k = pl.program_id(2)

        @pl.when(k == 0)
        def _():
            acc_ref[...] = jnp.zeros_like(acc_ref)

        acc_ref[...] += _dot_tn(a_ref[...].astype(BF16), g_ref[...].astype(BF16))

        @pl.when(k == nk - 1)
        def _():
            o_ref[...] = acc_ref[...].astype(BF16)

    return pl.pallas_call(
        body, name=name, grid=(Ka // tka, N // tn, nk),
        in_specs=[pl.BlockSpec((tk, tka), lambda i, j, k: (k, i)), pl.BlockSpec((tk, tn), lambda i, j, k: (k, j))],
        out_specs=pl.BlockSpec((tka, tn), lambda i, j, k: (i, j)), out_shape=jax.ShapeDtypeStruct((Ka, N), BF16),
        scratch_shapes=[pltpu.VMEM((tka, tn), F32)],
        compiler_params=_cp(("parallel", "parallel", "arbitrary")),
    )(a, g)


def matmul_nt_rmsbwd(a, w, x, g, dres, *, tm, tk, name, scatter=()):
    T, K = a.shape
    Dm = w.shape[0]
    nk = K // tk
    ni = T // tm
    ns = len(scatter)

    def body(a_ref, w_ref, x_ref, g_ref, dres_ref, *rest):
        scatter_in, rest = rest[:ns], rest[ns:]
        dx_ref, dg_ref = rest[0], rest[1]
        acc_ref = rest[2 + ns]
        i, k = pl.program_id(0), pl.program_id(1)
        if ns:
            start, wait = _exchange(scatter_in, rest[2:2 + ns], rest[3 + ns:], True)
            pl.when((i == 0) & (k == 0))(start)

        @pl.when(k == 0)
        def _():
            acc_ref[...] = jnp.zeros_like(acc_ref)

        @pl.when((i == 0) & (k == 0))
        def _():
            dg_ref[...] = jnp.zeros_like(dg_ref)

        acc_ref[...] += _dot_nt(a_ref[...], w_ref[...])

        @pl.when(k == nk - 1)
        def _():
            dh = acc_ref[...]
            xf = x_ref[...]
            r = lax.rsqrt(jnp.mean(xf * xf, axis=-1, keepdims=True) + EPS)
            xhat = xf * r
            dg_ref[...] += _rowsum8(dh * xhat)
            dxhat = dh * g_ref[...]
            dx_ref[...] = dres_ref[...] + r * (dxhat - xhat * jnp.mean(dxhat * xhat, axis=-1, keepdims=True))

        if ns:
            pl.when((i == ni - 1) & (k == nk - 1))(wait)

    return pl.pallas_call(
        body, name=name, grid=(ni, nk),
        in_specs=[pl.BlockSpec((tm, tk), lambda i, k: (i, k)), pl.BlockSpec((Dm, tk), lambda i, k: (0, k)),
                  pl.BlockSpec((tm, Dm), lambda i, k: (i, 0)), pl.BlockSpec((1, Dm), lambda i, k: (0, 0)),
                  pl.BlockSpec((tm, Dm), lambda i, k: (i, 0))] + [_ANY] * ns,
        out_specs=[pl.BlockSpec((tm, Dm), lambda i, k: (i, 0)), pl.BlockSpec((8, Dm), lambda i, k: (0, 0))] + [_ANY] * ns,
        out_shape=[jax.ShapeDtypeStruct((T, Dm), F32), jax.ShapeDtypeStruct((8, Dm), F32)] + _exchange_out_shapes(scatter, True),
        scratch_shapes=[pltpu.VMEM((tm, Dm), F32)] + (_exchange_scratch(ns) if ns else []),
        compiler_params=_cp(("arbitrary", "arbitrary")),
    )(a, w, x, g, dres, *scatter)


def ffn_down_bwd(dx, w_down, gu, *, tm, name):
    T, Dm = dx.shape
    H = w_down.shape[0]

    def body(dx_ref, w_ref, gu_ref, o_ref):
        dact = _dot_nt(dx_ref[...].astype(BF16), w_ref[...])
        gt = gu_ref[:, :H].astype(F32)
        up = gu_ref[:, H:].astype(F32)
        s = _sigmoid(gt)
        o_ref[:, :H] = (dact * up * (s * (1.0 + gt * (1.0 - s)))).astype(BF16)
        o_ref[:, H:] = (dact * gt * s).astype(BF16)

    return pl.pallas_call(
        body, name=name, grid=(T // tm,),
        in_specs=[pl.BlockSpec((tm, Dm), lambda i: (i, 0)), pl.BlockSpec((H, Dm), lambda i: (0, 0)),
                  pl.BlockSpec((tm, 2 * H), lambda i: (i, 0))],
        out_specs=pl.BlockSpec((tm, 2 * H), lambda i: (i, 0)), out_shape=jax.ShapeDtypeStruct((T, 2 * H), BF16),
        compiler_params=_cp(("parallel",)),
    )(dx, w_down, gu)


def _gate_specs(tm):
    return [pl.BlockSpec((tm, D_MODEL), lambda i, c=c: (i, c)) for c in (COL_GA // 8, COL_GB // 8)]


def merge_fwd(og, att, proj, w_pa, w_pb, *, tm, name):
    T, Dm = og.shape

    def body(og_ref, att_ref, ga_ref, gb_ref, wa_ref, wb_ref, ya_ref, yb_ref, mix_ref):
        ya = _dot(og_ref[...], wa_ref[...])
        yb = _dot(att_ref[...], wb_ref[...])
        ya_ref[...] = ya.astype(BF16)
        yb_ref[...] = yb.astype(BF16)
        mix_ref[...] = (_sigmoid(ga_ref[...].astype(F32)) * ya + _sigmoid(gb_ref[...].astype(F32)) * yb).astype(BF16)

    row = pl.BlockSpec((tm, Dm), lambda i: (i, 0))
    wsp = pl.BlockSpec((Dm, Dm), lambda i: (0, 0))
    return pl.pallas_call(
        body, name=name, grid=(T // tm,),
        in_specs=[row, row] + _gate_specs(tm) + [wsp, wsp],
        out_specs=[row, row, row], out_shape=[jax.ShapeDtypeStruct((T, Dm), BF16)] * 3,
        compiler_params=_cp(("parallel",)),
    )(og, att, proj, proj, w_pa, w_pb)


def merge_bwd(dx, w_o, proj, ya, yb, *, tm, name):
    T, Dm = dx.shape

    def body(dx_ref, w_ref, ga_ref, gb_ref, ya_ref, yb_ref, dya_ref, dyb_ref, dgate_ref):
        dmix = _dot_nt(dx_ref[...].astype(BF16), w_ref[...])
        sa = _sigmoid(ga_ref[...].astype(F32))
        sb = _sigmoid(gb_ref[...].astype(F32))
        dya_ref[...] = (dmix * sa).astype(BF16)
        dyb_ref[...] = (dmix * sb).astype(BF16)
        dgate_ref[:, :Dm] = (dmix * ya_ref[...].astype(F32) * sa * (1.0 - sa)).astype(BF16)
        dgate_ref[:, Dm:] = (dmix * yb_ref[...].astype(F32) * sb * (1.0 - sb)).astype(BF16)

    row = pl.BlockSpec((tm, Dm), lambda i: (i, 0))
    return pl.pallas_call(
        body, name=name, grid=(T // tm,),
        in_specs=[row, pl.BlockSpec((Dm, Dm), lambda i: (0, 0))] + _gate_specs(tm) + [row, row],
        out_specs=[row, row, pl.BlockSpec((tm, 2 * Dm), lambda i: (i, COL_GA // 16))],
        out_shape=[jax.ShapeDtypeStruct((T, Dm), BF16)] * 2 + [jax.ShapeDtypeStruct((T, IN_COLS), BF16)],
        compiler_params=_cp(("parallel",)),
    )(dx, w_o, proj, proj, ya, yb)


def loss_head(x, g, target, *, tm, name):
    T, Dm = x.shape

    def body(x_ref, g_ref, t_ref, dx_ref, l_ref, dg_ref):
        @pl.when(pl.program_id(0) == 0)
        def _():
            l_ref[...] = jnp.zeros_like(l_ref)
            dg_ref[...] = jnp.zeros_like(dg_ref)

        xf = x_ref[...]
        gv = g_ref[...]
        r = lax.rsqrt(jnp.mean(xf * xf, axis=-1, keepdims=True) + EPS)
        xhat = xf * r
        err = xhat * gv - t_ref[...]
        l_ref[...] += _rowsum8(err * err) * (0.5 / Dm)
        dy = err * (1.0 / Dm)
        dg_ref[...] += _rowsum8(dy * xhat)
        dxhat = dy * gv
        dx_ref[...] = r * (dxhat - xhat * jnp.mean(dxhat * xhat, axis=-1, keepdims=True))

    row = pl.BlockSpec((tm, Dm), lambda i: (i, 0))
    acc = pl.BlockSpec((8, Dm), lambda i: (0, 0))
    return pl.pallas_call(
        body, name=name, grid=(T // tm,),
        in_specs=[row, pl.BlockSpec((1, Dm), lambda i: (0, 0)), row],
        out_specs=[row, acc, acc],
        out_shape=[jax.ShapeDtypeStruct((T, Dm), F32), jax.ShapeDtypeStruct((8, Dm), F32), jax.ShapeDtypeStruct((8, Dm), F32)],
        compiler_params=_cp(("arbitrary",)),
    )(x, g, target)


_LEVELS = (32, 16, 8)
_DIAG = 8
_SUBLANES = 8
_HI = lax.Precision.HIGHEST


def _chunk_consts():
    t = lax.broadcasted_iota(jnp.int32, (CHUNK, CHUNK), 0)
    s = lax.broadcasted_iota(jnp.int32, (CHUNK, CHUNK), 1)
    tri = (s <= t).astype(F32)
    tri_t = (s >= t).astype(F32)
    masks = []
    for m in _LEVELS:
        sh = int(np.log2(2 * m))
        masks.append(((t >> sh) == (s >> sh)) & ((t & (2 * m - 1)) >= m) & ((s & (2 * m - 1)) < m))
    return tri, tri_t, masks


def _level_ref(b, m):
    if 2 * m >= _SUBLANES:
        pieces = [jnp.broadcast_to(b[p * 2 * m + m - 1:p * 2 * m + m, :], (2 * m, b.shape[1])) for p in range(CHUNK // (2 * m))]
        return pieces[0] if len(pieces) == 1 else jnp.concatenate(pieces, axis=0)
    groups = CHUNK // _SUBLANES
    b3 = b.reshape(groups, _SUBLANES, b.shape[1])
    row = lax.broadcasted_iota(jnp.int32, b3.shape, 1)
    ref = None
    for p in reversed(range(_SUBLANES // (2 * m))):
        src = jnp.broadcast_to(b3[:, p * 2 * m + m - 1:p * 2 * m + m, :], b3.shape)
        ref = src if ref is None else jnp.where(row < (p + 1) * 2 * m, src, ref)
    return ref.reshape(b.shape)


def _intra_off(q, kk, b, masks):
    a = jnp.zeros((CHUNK, CHUNK), F32)
    keep = []
    for m, mask in zip(_LEVELS, masks):
        e = jnp.exp2(-jnp.abs(b - _level_ref(b, m)))
        qt = (q * e).astype(BF16)
        kt = (kk * e).astype(BF16)
        a = a + jnp.where(mask, _dot_nt(qt, kt), 0.0)
        keep.append((mask, e, qt, kt))
    return a, keep


def _roll_rows(x, d):
    g = x.shape[0] // _DIAG
    return pltpu.roll(x.reshape(g, _DIAG, x.shape[1]), d % _DIAG, 1).reshape(x.shape)


def _diag_fwd(q, kk, v, b):
    lower = lax.broadcasted_iota(jnp.int32, q.shape, 0) & (_DIAG - 1)
    out = jnp.sum(q * kk, axis=-1, keepdims=True) * v
    for d in range(1, _DIAG):
        dec = jnp.exp2(jnp.where(lower >= d, b - _roll_rows(b, d), _NEG))
        out = out + jnp.sum(q * dec * _roll_rows(kk, d), axis=-1, keepdims=True) * _roll_rows(v, d)
    return out


def _diag_bwd(q, kk, v, b, do):
    lower = lax.broadcasted_iota(jnp.int32, q.shape, 0) & (_DIAG - 1)
    a = jnp.sum(q * kk, axis=-1, keepdims=True)
    da = jnp.sum(do * v, axis=-1, keepdims=True)
    dq, dk, dv = da * kk, da * q, a * do
    for d in range(1, _DIAG):
        kr = _roll_rows(kk, d)
        dec = jnp.exp2(jnp.where(lower >= d, b - _roll_rows(b, d), _NEG))
        qd = q * dec
        a = jnp.sum(qd * kr, axis=-1, keepdims=True)
        da = jnp.sum(do * _roll_rows(v, d), axis=-1, keepdims=True)
        dq = dq + da * dec * kr
        dk = dk + _roll_rows(da * qd, -d)
        dv = dv + _roll_rows(a * do, -d)
    return dq, dk, dv


def _hgrn_in_specs(ct, hp, order):
    def spec(col):
        return pl.BlockSpec((ct, hp * HG_DK), lambda h, i, col=col: (order(i), col // hp + h))
    vec = pl.BlockSpec((1, hp * HG_DK), lambda h, i: (0, h))
    return [spec(COL_HQ), spec(COL_HF), spec(COL_HI), spec(COL_HG), vec, vec]


def hgrn_fwd(proj, lb, gn, *, ct, hp, name):
    T = proj.shape[0]
    nc = ct // CHUNK

    def body(hq_ref, hf_ref, hi_ref, hg_ref, lb_ref, gn_ref, og_ref, o_ref, st_ref, s_ref):
        @pl.when(pl.program_id(1) == 0)
        def _():
            s_ref[...] = jnp.zeros_like(s_ref)

        tri, _, masks = _chunk_consts()

        def chunk(c, carry):
            sl = pl.ds(pl.multiple_of(c * CHUNK, CHUNK), CHUNK)
            for p in range(hp):
                cols = slice(p * HG_DK, (p + 1) * HG_DK)
                lb_v = lb_ref[:, cols]
                hq = hq_ref[sl, cols].astype(F32)
                v = hi_ref[sl, cols].astype(F32)
                gate = hg_ref[sl, cols].astype(F32)
                q = hq * _sigmoid(hq)
                f = lb_v + (1.0 - lb_v) * _sigmoid(hf_ref[sl, cols].astype(F32))
                kk = 1.0 - f
                b = jnp.dot(tri, jnp.log2(jnp.maximum(f, MIN_F)), precision=_HI, preferred_element_type=F32)
                st = s_ref[p]
                st_ref[p, c] = st
                a, _ = _intra_off(q, kk, b, masks)
                vb = v.astype(BF16)
                o = (_dot_nt((q * jnp.exp2(b)).astype(BF16), st.astype(BF16)) + _dot(a.astype(BF16), vb)
                     + _diag_fwd(q, kk, v, b))
                bl = b[CHUNK - 1:CHUNK, :]
                s_ref[p] = st * jnp.exp2(bl) + _dot_tn(vb, (kk * jnp.exp2(bl - b)).astype(BF16))
                o_ref[sl, cols] = o
                r = lax.rsqrt(jnp.mean(o * o, axis=-1, keepdims=True) + EPS)
                og_ref[sl, cols] = (o * r * gn_ref[:, cols] * (gate * _sigmoid(gate))).astype(BF16)
            return carry

        lax.fori_loop(0, nc, chunk, 0)

    blk = pl.BlockSpec((ct, hp * HG_DK), lambda h, i: (i, h))
    return pl.pallas_call(
        body, name=name, grid=(HG_HEADS // hp, T // ct),
        in_specs=_hgrn_in_specs(ct, hp, lambda i: i),
        out_specs=[blk, blk, pl.BlockSpec((hp, nc, HG_DK, HG_DK), lambda h, i: (h, i, 0, 0))],
        out_shape=[jax.ShapeDtypeStruct((T, HG_WIDTH), BF16), jax.ShapeDtypeStruct((T, HG_WIDTH), F32),
                   jax.ShapeDtypeStruct((HG_HEADS, T // CHUNK, HG_DK, HG_DK), F32)],
        scratch_shapes=[pltpu.VMEM((hp, HG_DK, HG_DK), F32)],
        compiler_params=_cp(("parallel", "arbitrary")),
    )(proj, proj, proj, proj, lb, gn)


def hgrn_bwd(proj, lb, gn, o_raw, states, dog, dproj, *, ct, name):
    T = proj.shape[0]
    nc = ct // CHUNK
    nblk = T // ct
    hp = HG_HEADS

    def body(hq_ref, hf_ref, hi_ref, hg_ref, lb_ref, gn_ref, o_ref, st_ref, dog_ref, _dproj_in,
             dp_ref, dlb_ref, dgn_ref, ds_ref):
        dhq_ref, dhf_ref, dhi_ref, dhg_ref = (dp_ref.at[:, s * HG_WIDTH:(s + 1) * HG_WIDTH] for s in range(4))
        @pl.when(pl.program_id(1) == 0)
        def _():
            ds_ref[...] = jnp.zeros_like(ds_ref)
            dlb_ref[...] = jnp.zeros_like(dlb_ref)
            dgn_ref[...] = jnp.zeros_like(dgn_ref)

        tri, tri_t, masks = _chunk_consts()
        last_row = lax.broadcasted_iota(jnp.int32, (CHUNK, HG_DK), 0) == CHUNK - 1

        def chunk(ci, carry):
            c = nc - 1 - ci
            sl = pl.ds(pl.multiple_of(c * CHUNK, CHUNK), CHUNK)
            for p in range(hp):
                cols = slice(p * HG_DK, (p + 1) * HG_DK)
                lb_v = lb_ref[:, cols]
                gn_v = gn_ref[:, cols]
                hq = hq_ref[sl, cols].astype(F32)
                v = hi_ref[sl, cols].astype(F32)
                gate = hg_ref[sl, cols].astype(F32)
                sq = _sigmoid(hq)
                q = hq * sq
                sig = _sigmoid(hf_ref[sl, cols].astype(F32))
                f = lb_v + (1.0 - lb_v) * sig
                kk = 1.0 - f
                b = jnp.dot(tri, jnp.log2(jnp.maximum(f, MIN_F)), precision=_HI, preferred_element_type=F32)
                st = st_ref[p, c]
                dst = ds_ref[p]
                o = o_ref[sl, cols]
                r = lax.rsqrt(jnp.mean(o * o, axis=-1, keepdims=True) + EPS)
                xhat = o * r
                sg = _sigmoid(gate)
                dog_v = dog_ref[sl, cols].astype(F32)
                dy = dog_v * (gate * sg)
                dhg_ref[sl, cols] = (dog_v * xhat * gn_v * (sg * (1.0 + gate * (1.0 - sg)))).astype(BF16)
                dgn_ref[:, cols] += _rowsum8(dy * xhat)
                dxh = dy * gn_v
                do = r * (dxh - xhat * jnp.mean(dxh * xhat, axis=-1, keepdims=True))
                dob = do.astype(BF16)
                vb = v.astype(BF16)
                stb = st.astype(BF16)
                dstb = dst.astype(BF16)
                eb = jnp.exp2(b)
                bl = b[CHUNK - 1:CHUNK, :]
                ebl = jnp.exp2(bl)
                edec = jnp.exp2(bl - b)
                qe = q * eb
                kdec = kk * edec
                a, keep = _intra_off(q, kk, b, masks)
                da = _dot_nt(dob, vb)
                dkdec = _dot(vb, dstb)
                dq = _dot(dob, stb) * eb
                dk = dkdec * edec
                dv = _dot_tn(a.astype(BF16), dob) + _dot_nt(kdec.astype(BF16), dstb)
                for mask, e, qt, kt in keep:
                    dsm = jnp.where(mask, da, 0.0).astype(BF16)
                    dq = dq + _dot(dsm, kt) * e
                    dk = dk + _dot_tn(dsm, qt) * e
                ddq, ddk, ddv = _diag_bwd(q, kk, v, b, do)
                dq = dq + ddq
                dk = dk + ddk
                dhi_ref[sl, cols] = (dv + ddv).astype(BF16)
                dbl = jnp.sum(dkdec * kdec, axis=0, keepdims=True) + ebl * jnp.sum(st * dst, axis=0, keepdims=True)
                db = q * dq - kk * dk
                db = jnp.where(last_row, db + dbl, db)
                dlogf = jnp.dot(tri_t, db, precision=_HI, preferred_element_type=F32)
                ds_ref[p] = _dot_tn(dob, qe.astype(BF16)) + dst * ebl
                df = jnp.where(f > MIN_F, dlogf / f, 0.0) - dk
                dhf_ref[sl, cols] = (df * (1.0 - lb_v) * sig * (1.0 - sig)).astype(BF16)
                dlb_ref[:, cols] += _rowsum8(df * (1.0 - sig))
                dhq_ref[sl, cols] = (dq * (sq * (1.0 + hq * (1.0 - sq)))).astype(BF16)
            return carry

        lax.fori_loop(0, nc, chunk, 0)

    rev = lambda i: nblk - 1 - i
    blk = pl.BlockSpec((ct, hp * HG_DK), lambda h, i: (rev(i), h))
    acc = pl.BlockSpec((8, hp * HG_DK), lambda h, i: (0, h))
    return pl.pallas_call(
        body, name=name, grid=(HG_HEADS // hp, nblk),
        in_specs=_hgrn_in_specs(ct, hp, rev) + [blk, pl.BlockSpec((hp, nc, HG_DK, HG_DK), lambda h, i: (h, rev(i), 0, 0)), blk, _ANY],
        out_specs=[pl.BlockSpec((ct, 4 * HG_WIDTH), lambda h, i: (rev(i), 0)), acc, acc],
        out_shape=[jax.ShapeDtypeStruct(dproj.shape, dproj.dtype)] + [jax.ShapeDtypeStruct((8, HG_WIDTH), F32)] * 2,
        scratch_shapes=[pltpu.VMEM((hp, HG_DK, HG_DK), F32)],
        input_output_aliases={9: 0},
        compiler_params=_cp(("parallel", "arbitrary")),
    )(proj, proj, proj, proj, lb, gn, o_raw, states, dog, dproj)


_GW = ATT_GROUP * HEAD_DIM
_NEG = -1e30


def rope_tables(T):
    half = ROPE_DIM // 2
    inv = ROPE_THETA ** (-jnp.arange(half, dtype=F32) * 2.0 / ROPE_DIM)
    ang = jnp.arange(T, dtype=F32)[:, None] * inv[None, :]
    pad1 = jnp.ones((T, HEAD_DIM - ROPE_DIM), F32)
    cos = jnp.concatenate([jnp.cos(ang), jnp.cos(ang), pad1], axis=1)
    sin = jnp.concatenate([jnp.sin(ang), jnp.sin(ang), 0.0 * pad1], axis=1)
    p = np.zeros((_GW, _GW), np.float32)
    for base in range(0, _GW, HEAD_DIM):
        for i in range(half):
            p[base + i + half, base + i] = -1.0
            p[base + i, base + i + half] = 1.0
    reps = _GW // HEAD_DIM
    return jnp.tile(cos, (1, reps)), jnp.tile(sin, (1, reps)), jnp.asarray(p, BF16)


def _rope(x, cos, sin, pm):
    return x.astype(F32) * cos + _dot(x, pm) * sin


def _unrope(dx, cos, sin, pm):
    return dx * cos + _dot_nt((dx * sin).astype(BF16), pm)


_GQ = ATT_GROUP * WINDOW


def _swa_mask(n, queries_on_rows):
    shape = (_GQ, 2 * WINDOW) if queries_on_rows else (2 * WINDOW, _GQ)
    r = lax.broadcasted_iota(jnp.int32, shape, 0 if queries_on_rows else 1) & (WINDOW - 1)
    c = lax.broadcasted_iota(jnp.int32, shape, 1 if queries_on_rows else 0)
    delta = r + WINDOW - c
    return (delta >= 0) & (delta < WINDOW) & ((c >= WINDOW) | (n > 0))


def _sink_vector(sink_ref, g, queries_on_rows):
    shape = (_GQ, 1) if queries_on_rows else (1, _GQ)
    i = lax.broadcasted_iota(jnp.int32, shape, 0 if queries_on_rows else 1)
    out = jnp.full(shape, sink_ref[g * ATT_GROUP + ATT_GROUP - 1], F32)
    for hh in range(ATT_GROUP - 2, -1, -1):
        out = jnp.where(i < (hh + 1) * WINDOW, sink_ref[g * ATT_GROUP + hh], out)
    return out


def _head(x, h):
    return x[:, h * HEAD_DIM:(h + 1) * HEAD_DIM]


def _stack_heads(x):
    return jnp.concatenate([_head(x, hh) for hh in range(ATT_GROUP)], axis=0)


def _unstack_heads(x):
    return jnp.concatenate([x[hh * WINDOW:(hh + 1) * WINDOW] for hh in range(ATT_GROUP)], axis=1)


def _swa_specs(cur):
    prev = lambda n: jnp.maximum(n - 1, 0)
    kv = lambda col, f: pl.BlockSpec((WINDOW, _GW), lambda n: (f(n), col))
    tab = lambda f: pl.BlockSpec((WINDOW, _GW), lambda n: (f(n), 0))
    return [pl.BlockSpec((WINDOW, ATT_WIDTH), lambda n: (cur(n), COL_AQ // 8)),
            kv(COL_AK // 2, prev), kv(COL_AK // 2, cur), kv(COL_AV // 2, prev), kv(COL_AV // 2, cur),
            tab(prev), tab(cur), tab(prev), tab(cur),
            pl.BlockSpec((_GW, _GW), lambda n: (0, 0)), pl.BlockSpec(memory_space=pltpu.SMEM)]


def swa_fwd(proj, cos, sin, pm, sinks, *, name):
    T = proj.shape[0]
    nb = T // WINDOW
    scale = HEAD_DIM ** -0.5

    def body(q_ref, kp_ref, kc_ref, vp_ref, vc_ref, cp_ref, cc_ref, sp_ref, sc_ref, pm_ref, sink_ref, o_ref):
        n = pl.program_id(0)
        pm_v = pm_ref[...]
        mask = _swa_mask(n, True)
        k_cat = jnp.concatenate([_rope(kp_ref[...], cp_ref[...], sp_ref[...], pm_v),
                                 _rope(kc_ref[...], cc_ref[...], sc_ref[...], pm_v)], axis=0).astype(BF16)
        v_cat = jnp.concatenate([vp_ref[...], vc_ref[...]], axis=0)
        for g in range(ATT_KV_HEADS):
            qs = _stack_heads(_rope(q_ref[:, g * _GW:(g + 1) * _GW], cc_ref[...], sc_ref[...], pm_v).astype(BF16))
            sk = _sink_vector(sink_ref, g, True)
            s = jnp.where(mask, _dot_nt(qs, _head(k_cat, g)) * scale, _NEG)
            m = jnp.maximum(jnp.max(s, axis=-1, keepdims=True), sk)
            p = jnp.exp(s - m)
            l = jnp.sum(p, axis=-1, keepdims=True) + jnp.exp(sk - m)
            o = _dot(p.astype(BF16), _head(v_cat, g)) / l
            o_ref[:, g * _GW:(g + 1) * _GW] = _unstack_heads(o).astype(BF16)

    return pl.pallas_call(
        body, name=name, grid=(nb,), in_specs=_swa_specs(lambda n: n),
        out_specs=pl.BlockSpec((WINDOW, ATT_WIDTH), lambda n: (n, 0)),
        out_shape=jax.ShapeDtypeStruct((T, ATT_WIDTH), BF16), compiler_params=_cp(("parallel",)),
    )(proj, proj, proj, proj, proj, cos, cos, sin, sin, pm, sinks)


def swa_bwd(proj, cos, sin, pm, sinks, att, datt, dproj, *, name):
    T = proj.shape[0]
    nb = T // WINDOW
    scale = HEAD_DIM ** -0.5
    clamp = lambda n: jnp.minimum(n, nb - 1)

    def body(q_ref, kp_ref, kc_ref, vp_ref, vc_ref, cp_ref, cc_ref, sp_ref, sc_ref, pm_ref, sink_ref, att_ref, datt_ref,
             _dproj_in, dp_ref, dsink_ref, cq_ref, ck_ref, cv_ref):
        n = pl.program_id(0)
        pm_v = pm_ref[...]

        @pl.when(n == 0)
        def _():
            dsink_ref[...] = jnp.zeros_like(dsink_ref)
            cq_ref[...] = jnp.zeros_like(cq_ref)
            ck_ref[...] = jnp.zeros_like(ck_ref)
            cv_ref[...] = jnp.zeros_like(cv_ref)

        def write_prev(dk_prev, dv_prev):
            dp_ref[:, :ATT_WIDTH] = cq_ref[...]
            dk = _unrope(ck_ref[...] + dk_prev, cp_ref[...], sp_ref[...], pm_v)
            dp_ref[:, ATT_WIDTH:ATT_WIDTH + _GW] = dk.astype(BF16)
            dp_ref[:, ATT_WIDTH + _GW:] = (cv_ref[...] + dv_prev).astype(BF16)

        @pl.when(n < nb)
        def _():
            mask = _swa_mask(n, False)
            ones = jnp.ones((8, HEAD_DIM), F32)
            k_cat = jnp.concatenate([_rope(kp_ref[...], cp_ref[...], sp_ref[...], pm_v),
                                     _rope(kc_ref[...], cc_ref[...], sc_ref[...], pm_v)], axis=0).astype(BF16)
            v_cat = jnp.concatenate([vp_ref[...], vc_ref[...]], axis=0)
            dqs, dks, dvs = [], [], []
            for g in range(ATT_KV_HEADS):
                gc = slice(g * _GW, (g + 1) * _GW)
                qs = _stack_heads(_rope(q_ref[:, gc], cc_ref[...], sc_ref[...], pm_v).astype(BF16))
                dos = _stack_heads(datt_ref[:, gc])
                os_ = _stack_heads(att_ref[:, gc])
                kh, vh = _head(k_cat, g), _head(v_cat, g)
                sk = _sink_vector(sink_ref, g, False)
                s = jnp.where(mask, _dot_nt(kh, qs) * scale, _NEG)
                m = jnp.maximum(jnp.max(s, axis=0, keepdims=True), sk)
                e = jnp.exp(s - m)
                es = jnp.exp(sk - m)
                inv_l = 1.0 / (jnp.sum(e, axis=0, keepdims=True) + es)
                p = e * inv_l
                dsum = lax.dot_general(ones, dos.astype(F32) * os_.astype(F32), (((1,), (1,)), ((), ())),
                                       precision=_HI, preferred_element_type=F32)[0:1]
                dsink_ref[g, 0:1, :] += -(es * inv_l) * dsum
                ds = (p * (_dot_nt(vh, dos) - dsum) * scale).astype(BF16)
                dqs.append(_unrope(_unstack_heads(_dot_tn(ds, kh)), cc_ref[...], sc_ref[...], pm_v))
                dks.append(_dot(ds, qs))
                dvs.append(_dot(p.astype(BF16), dos))
            dk_all = jnp.concatenate(dks, axis=1)
            dv_all = jnp.concatenate(dvs, axis=1)

            @pl.when(n > 0)
            def _():
                write_prev(dk_all[:WINDOW], dv_all[:WINDOW])

            cq_ref[...] = jnp.concatenate(dqs, axis=1).astype(BF16)
            ck_ref[...] = dk_all[WINDOW:]
            cv_ref[...] = dv_all[WINDOW:]

        @pl.when(n == nb)
        def _():
            write_prev(jnp.zeros((WINDOW, _GW), F32), jnp.zeros((WINDOW, _GW), F32))

    row = pl.BlockSpec((WINDOW, ATT_WIDTH), lambda n: (clamp(n), 0))
    slab = ATT_WIDTH + 2 * _GW
    return pl.pallas_call(
        body, name=name, grid=(nb + 1,), in_specs=_swa_specs(clamp) + [row, row, _ANY],
        out_specs=[pl.BlockSpec((WINDOW, slab), lambda n: (jnp.maximum(n - 1, 0), COL_AQ * HG_DK // slab)),
                   pl.BlockSpec((ATT_KV_HEADS, 8, _GQ), lambda n: (0, 0, 0))],
        out_shape=[jax.ShapeDtypeStruct(dproj.shape, dproj.dtype), jax.ShapeDtypeStruct((ATT_KV_HEADS, 8, _GQ), F32)],
        scratch_shapes=[pltpu.VMEM((WINDOW, ATT_WIDTH), BF16), pltpu.VMEM((WINDOW, _GW), F32), pltpu.VMEM((WINDOW, _GW), F32)],
        input_output_aliases={13: 0},
        compiler_params=_cp(("arbitrary",)),
    )(proj, proj, proj, proj, proj, cos, cos, sin, sin, pm, sinks, att, datt, dproj)


def swap_with_sibling(arrays, *, name):
    n = len(arrays)

    def body(*refs):
        ins, outs = refs[:n], refs[n:2 * n]
        send_sems, recv_sems = refs[2 * n:]
        x, y, c, _ = _place()
        copies = [pltpu.make_async_remote_copy(src_ref=ins[a], dst_ref=outs[a], send_sem=send_sems.at[a],
                                               recv_sem=recv_sems.at[a], device_id=(x, y, 1 - c), device_id_type=MESH)
                  for a in range(n)]
        for cp in copies:
            cp.start()
        for cp in copies:
            cp.wait()

    return pl.pallas_call(
        body, name=name, out_shape=[jax.ShapeDtypeStruct(a.shape, a.dtype) for a in arrays],
        in_specs=[_ANY] * n, out_specs=[_ANY] * n,
        scratch_shapes=[pltpu.SemaphoreType.DMA((n,)), pltpu.SemaphoreType.DMA((n,))],
    )(*arrays)


N_DEV = 8
SMALL_ROWS = 24


def allreduce_small(v, *, name):
    def body(v_ref, o_ref, recv_ref, send_sems, recv_sems):
        x, y, c, _ = _place()
        me = 4 * x + 2 * y + c
        recv_ref[me] = v_ref[...]

        def copy(k, slot):
            peer = (x ^ (k >> 2), y ^ ((k >> 1) & 1), c ^ (k & 1))
            return pltpu.make_async_remote_copy(src_ref=v_ref, dst_ref=recv_ref.at[slot], send_sem=send_sems.at[k - 1],
                                                recv_sem=recv_sems.at[k - 1], device_id=peer, device_id_type=MESH)

        for k in range(1, N_DEV):
            copy(k, me).start()
        for k in range(1, N_DEV):
            copy(k, me ^ k).wait()
        acc = recv_ref[0]
        for d in range(1, N_DEV):
            acc = acc + recv_ref[d]
        o_ref[...] = acc

    vm = pl.BlockSpec(memory_space=pltpu.VMEM)
    return pl.pallas_call(
        body, name=name, out_shape=jax.ShapeDtypeStruct(v.shape, v.dtype), in_specs=[vm], out_specs=vm,
        scratch_shapes=[pltpu.VMEM((N_DEV,) + v.shape, v.dtype), pltpu.SemaphoreType.DMA((N_DEV - 1,)),
                        pltpu.SemaphoreType.DMA((N_DEV - 1,))],
    )(v)


def sum_parts(parts, *, tr, name):
    _, R, C = parts.shape

    def body(p_ref, o_ref):
        acc = p_ref[0].astype(F32)
        for q in range(1, N_CHIPS):
            acc = acc + p_ref[q].astype(F32)
        o_ref[...] = acc

    return pl.pallas_call(
        body, name=name, grid=(R // tr,), in_specs=[pl.BlockSpec((N_CHIPS, tr, C), lambda i: (0, i, 0))],
        out_specs=pl.BlockSpec((tr, C), lambda i: (i, 0)), out_shape=jax.ShapeDtypeStruct((R, C), F32),
        compiler_params=_cp(("parallel",)),
    )(parts)


def adamw(g_a, g_b, w, m, v, *, tr, name):
    R, C = w.shape
    c1 = 1.0 - ADAM_B1 ** ADAM_STEP
    c2 = 1.0 - ADAM_B2 ** ADAM_STEP
    ng = 1 if g_b is None else 2

    def body(*refs):
        w_ref, m_ref, v_ref, g_ref, d_ref, nm_ref, nv_ref = refs[ng:]
        g = refs[0][...] if ng == 1 else refs[0][...] + refs[1][...]
        nm = ADAM_B1 * m_ref[...] + (1.0 - ADAM_B1) * g
        nv = ADAM_B2 * v_ref[...] + (1.0 - ADAM_B2) * (g * g)
        g_ref[...] = g
        nm_ref[...] = nm
        nv_ref[...] = nv
        d_ref[...] = -ADAM_LR * ((nm / c1) / (jnp.sqrt(nv / c2) + ADAM_EPS) + ADAM_WD * w_ref[...])

    blk = pl.BlockSpec((tr, C), lambda i: (i, 0))
    gs = [g_a] if g_b is None else [g_a, g_b]
    return pl.pallas_call(
        body, name=name, grid=(R // tr,), in_specs=[blk] * (ng + 3), out_specs=[blk] * 4,
        out_shape=[jax.ShapeDtypeStruct((R, C), F32)] * 4, compiler_params=_cp(("parallel",)),
    )(*gs, w, m, v)


def _lb_bounds(lb_logits):
    p = jax.nn.softmax(lb_logits.astype(F32), axis=0)
    return jnp.cumsum(p, axis=0) - p[0:1]


def _pack_small(n1, lb, hgn, n2, fin, sinks, extra):
    row = jnp.concatenate([sinks.reshape(-1), jnp.reshape(extra, (1,)),
                           jnp.zeros((D_MODEL - sinks.size - 1,), F32)])
    return jnp.concatenate([n1, lb, hgn, n2, fin[None], row[None], jnp.zeros((SMALL_ROWS - 18, D_MODEL), F32)], axis=0)


def _unpack_small(buf):
    return (buf[0:4], buf[4:8], buf[8:12], buf[12:16], buf[16], buf[17, :DEPTH * ATT_Q_HEADS].reshape(DEPTH, ATT_Q_HEADS),
            buf[17, DEPTH * ATT_Q_HEADS])


def _cols_full(g):
    q, r, c = g.shape
    return jnp.transpose(g, (1, 0, 2)).reshape(r, q * c)


def _cols_split(w):
    r, qc = w.shape
    return jnp.transpose(w.reshape(r, N_CHIPS, qc // N_CHIPS), (1, 0, 2))


def kernel(x, norm1, w_in, lb_logits, hg_norm, attn_sinks, w_pa, w_pb, w_o, norm2, w_gate, w_up, w_down, final_norm, loss_target, m_norm1, m_w_in, m_lb_logits, m_hg_norm, m_attn_sinks, m_w_pa, m_w_pb, m_w_o, m_norm2, m_w_gate, m_w_up, m_w_down, m_final_norm, v_norm1, v_w_in, v_lb_logits, v_hg_norm, v_attn_sinks, v_w_pa, v_w_pb, v_w_o, v_norm2, v_w_gate, v_w_up, v_w_down, v_final_norm):
    T = x.shape[1]
    H = FFN_HIDDEN
    tm = min(512, T)
    tmx = min(1024, T)
    tkx = min(2048, T)
    big = dict(w_in=w_in, w_pa=w_pa, w_pb=w_pb, w_o=w_o, w_gate=w_gate, w_up=w_up, w_down=w_down)
    col_sharded = ("w_in", "w_gate", "w_up")

    shards = [w.astype(BF16) for w in big.values()]

    def layer_weights(gathered):
        full = {k: (_cols_full(g) if k in col_sharded else g.reshape(-1, g.shape[-1])) for k, g in zip(big, gathered)}
        full["w_in"] = _to_kernel_cols(full["w_in"])
        full["w_gu"] = jnp.concatenate([full.pop("w_gate"), full.pop("w_up")], axis=-1)
        return full

    lb_all, lb_vjp = jax.vjp(_lb_bounds, lb_logits)
    cos, sin, pm = rope_tables(T)

    xs = x[0]
    saved = []
    weights = [layer_weights(exchange_between_chips([s[0] for s in shards], scatter=False, name="gather_weights_0"))]
    for l in range(DEPTH):
        full = weights[l]
        nxt = [s[l + 1] for s in shards] if l + 1 < DEPTH else []
        h, proj, *gathered = ln_matmul(xs, norm1[l:l + 1], full["w_in"], tm=tmx, tn=1536, name=f"in_proj_{l}", gather=nxt)
        if nxt:
            weights.append(layer_weights(gathered))
        og, o_raw, states = hgrn_fwd(proj, lb_all[l:l + 1], hg_norm[l:l + 1], ct=min(256, T), hp=HG_HEADS, name=f"hgrn_fwd_{l}")
        att = swa_fwd(proj, cos, sin, pm, attn_sinks[l], name=f"swa_fwd_{l}")
        ya, yb, mix = merge_fwd(og, att, proj, full["w_pa"], full["w_pb"], tm=tm, name=f"merge_fwd_{l}")
        x1 = matmul_residual(xs, mix, full["w_o"], tm=tmx, name=f"out_proj_{l}")
        h2, gu, act = ln_matmul(x1, norm2[l:l + 1], full["w_gu"], tm=min(256, T), tn=2 * H, name=f"ffn_up_{l}", swiglu=True)
        x2 = matmul_residual(x1, act, full["w_down"], tm=tmx, name=f"ffn_down_{l}")
        saved.append((xs, h, proj, og, o_raw, states, att, ya, yb, mix, x1, h2, gu, act))
        xs = x2

    dx, loss_parts, dfin_parts = loss_head(xs, final_norm[None], loss_target[0], tm=tm, name="loss_head")

    arrived = [None] * DEPTH
    dn1, dn2, dlb, dgn, dsk = ([None] * DEPTH for _ in range(5))
    for l in reversed(range(DEPTH)):
        xs, h, proj, og, o_raw, states, att, ya, yb, mix, x1, h2, gu, act = saved[l]
        full = weights[l]
        dw = {}
        dgu = ffn_down_bwd(dx, full["w_down"], gu, tm=min(256, T), name=f"ffn_down_bwd_{l}")
        dw["w_down"] = matmul_tn(act, dx, tka=H // 2, tn=D_MODEL, tk=tkx, name=f"dw_down_{l}")
        dwgu = matmul_tn(h2, dgu, tka=D_MODEL, tn=H // 2, tk=tkx, name=f"dw_gu_{l}")
        dw["w_gate"], dw["w_up"] = dwgu[:, :H], dwgu[:, H:]
        dx1, dn2_parts = matmul_nt_rmsbwd(dgu, full["w_gu"], x1, norm2[l:l + 1], dx, tm=tmx, tk=H // 2, name=f"ffn_up_bwd_{l}")
        dya, dyb, dproj = merge_bwd(dx1, full["w_o"], proj, ya, yb, tm=tm, name=f"merge_bwd_{l}")
        dw["w_o"] = matmul_tn(mix, dx1, tka=D_MODEL, tn=D_MODEL, tk=tkx, name=f"dw_o_{l}")
        dog = matmul_nt(dya, full["w_pa"], tm=tmx, name=f"pa_bwd_{l}")
        datt = matmul_nt(dyb, full["w_pb"], tm=tmx, name=f"pb_bwd_{l}")
        dw["w_pa"] = matmul_tn(og, dya, tka=D_MODEL, tn=D_MODEL, tk=tkx, name=f"dw_pa_{l}")
        dw["w_pb"] = matmul_tn(att, dyb, tka=D_MODEL, tn=D_MODEL, tk=tkx, name=f"dw_pb_{l}")
        dproj, dsk_parts = swa_bwd(proj, cos, sin, pm, attn_sinks[l], att, datt, dproj, name=f"swa_bwd_{l}")
        dproj, dlb_parts, dgn_parts = hgrn_bwd(proj, lb_all[l:l + 1], hg_norm[l:l + 1], o_raw, states, dog, dproj,
                                               ct=min(256, T), name=f"hgrn_bwd_{l}")
        dw["w_in"] = _from_kernel_cols(matmul_tn(h, dproj, tka=D_MODEL, tn=1536, tk=tkx, name=f"dw_in_{l}"))
        parts = [(_cols_split(dw[k]) if k in col_sharded else dw[k].reshape(N_CHIPS, -1, dw[k].shape[-1])) for k in big]
        dx, dn1_parts, *arrived[l] = matmul_nt_rmsbwd(dproj, full["w_in"], xs, norm1[l:l + 1], dx1, tm=tmx, tk=1536,
                                                      name=f"in_proj_bwd_{l}", scatter=parts)
        dn1[l], dn2[l], dlb[l], dgn[l] = dn1_parts.sum(0), dn2_parts.sum(0), dlb_parts.sum(0), dgn_parts.sum(0)
        dsk[l] = dsk_parts[:, 0, :].reshape(ATT_Q_HEADS, WINDOW).sum(-1)
    grad_x = dx[None]

    mine = _pack_small(jnp.stack(dn1), jnp.stack(dlb), jnp.stack(dgn), jnp.stack(dn2), dfin_parts.sum(0), jnp.stack(dsk),
                       loss_parts.sum())
    total = allreduce_small(mine, name="allreduce_small")
    g_n1, g_lb_all, g_gn, g_n2, g_fin, g_sk, loss = _unpack_small(total)
    (g_lb,) = lb_vjp(g_lb_all)
    zero = jnp.zeros((), F32)
    small = adamw(_pack_small(g_n1, g_lb, g_gn, g_n2, g_fin, g_sk, zero), None,
                  _pack_small(norm1, lb_logits, hg_norm, norm2, final_norm, attn_sinks, zero),
                  _pack_small(m_norm1, m_lb_logits, m_hg_norm, m_norm2, m_final_norm, m_attn_sinks, zero),
                  _pack_small(v_norm1, v_lb_logits, v_hg_norm, v_norm2, v_final_norm, v_attn_sinks, zero),
                  tr=SMALL_ROWS, name="adamw_small")
    small = [_unpack_small(s)[:6] for s in small]

    sums = []
    for i, k in enumerate(big):
        a = jnp.stack([arrived[l][i] for l in range(DEPTH)], axis=1)
        q, nl, r, c = a.shape
        sums.append(sum_parts(a.reshape(q, nl * r, c), tr=128, name=f"sum_{k}"))
    theirs = swap_with_sibling(sums, name="swap_sums")
    mom = dict(w_in=(m_w_in, v_w_in), w_pa=(m_w_pa, v_w_pa), w_pb=(m_w_pb, v_w_pb), w_o=(m_w_o, v_w_o),
               w_gate=(m_w_gate, v_w_gate), w_up=(m_w_up, v_w_up), w_down=(m_w_down, v_w_down))
    upd = {}
    for k, mine_k, theirs_k in zip(big, sums, theirs):
        shp = big[k].shape
        flat = lambda t: t.reshape(shp[0] * shp[1], shp[2])
        res = adamw(mine_k, theirs_k, flat(big[k]), flat(mom[k][0]), flat(mom[k][1]), tr=128, name=f"adamw_{k}")
        upd[k] = [t.reshape(shp) for t in res]

    order = ("norm1", "w_in", "lb_logits", "hg_norm", "attn_sinks", "w_pa", "w_pb", "w_o", "norm2", "w_gate", "w_up",
             "w_down", "final_norm")
    small_pos = dict(norm1=0, lb_logits=1, hg_norm=2, norm2=3, final_norm=4, attn_sinks=5)
    outs = [loss, grad_x]
    for kind in range(4):
        for name in order:
            outs.append(upd[name][kind] if name in upd else small[kind][small_pos[name]])
    return tuple(outs)
```

```python
import functools

import jax
import jax.numpy as jnp
import numpy as np
from jax import lax
from jax.experimental import pallas as pl
from jax.experimental.pallas import tpu as pltpu

F32 = jnp.float32
BF16 = jnp.bfloat16

D_MODEL = 1024
DEPTH = 4
HG_HEADS = 8
HG_DK = 128
HG_WIDTH = HG_HEADS * HG_DK
CHUNK = 64
ATT_Q_HEADS = 16
ATT_KV_HEADS = 4
ATT_GROUP = ATT_Q_HEADS // ATT_KV_HEADS
HEAD_DIM = 64
ATT_WIDTH = ATT_Q_HEADS * HEAD_DIM
ATT_KV_WIDTH = ATT_KV_HEADS * HEAD_DIM
WINDOW = 128
ROPE_THETA = 500000.0
ROPE_DIM = HEAD_DIM // 4
FFN_HIDDEN = 2816
IN_COLS = 4 * HG_WIDTH + ATT_WIDTH + 2 * ATT_KV_WIDTH + 2 * D_MODEL
EPS = 1e-6
MIN_F = 1e-30
COL_HQ, COL_HF, COL_HI, COL_HG = 0, 8, 16, 24
COL_GA, COL_GB, COL_AQ, COL_AK, COL_AV = 32, 40, 48, 56, 58


def _to_kernel_cols(w):
    return jnp.concatenate([w[..., :4096], w[..., 5632:7680], w[..., 4096:5120], w[..., 5120:5632]], axis=-1)


def _from_kernel_cols(w):
    return jnp.concatenate([w[..., :4096], w[..., 6144:7168], w[..., 7168:7680], w[..., 4096:6144]], axis=-1)

ADAM_LR = 0.001
ADAM_B1 = 0.9
ADAM_B2 = 0.999
ADAM_EPS = 1e-08
ADAM_WD = 0.01
ADAM_STEP = 10

VMEM_LIMIT_V7X = 56 * 1024 * 1024
MESH = pl.DeviceIdType.MESH


def _cp(sem, vmem=VMEM_LIMIT_V7X):
    return pltpu.CompilerParams(dimension_semantics=sem, vmem_limit_bytes=vmem)


def _sigmoid(x):
    return 1.0 / (1.0 + jnp.exp(-x))


def _dot(a, b):
    return jnp.dot(a, b, preferred_element_type=F32)


def _dot_nt(a, b):
    return lax.dot_general(a, b, (((1,), (1,)), ((), ())), preferred_element_type=F32)


def _dot_tn(a, b):
    return lax.dot_general(a, b, (((0,), (0,)), ((), ())), preferred_element_type=F32)


def _rowsum8(v):
    r, n = v.shape
    return jnp.sum(v.reshape(r // 8, 8, n), axis=0)


_ANY = pl.BlockSpec(memory_space=pl.ANY)
N_CHIPS = 4
PEER_CHIPS = N_CHIPS - 1


def _place():
    x, y, c = lax.axis_index("x"), lax.axis_index("y"), lax.axis_index("c")
    return x, y, c, [(1 - x, y), (x, 1 - y), (1 - x, 1 - y)]


def _exchange_scratch(n):
    return [pltpu.SemaphoreType.DMA((PEER_CHIPS * n,)), pltpu.SemaphoreType.DMA((PEER_CHIPS * n,)), pltpu.SemaphoreType.DMA((n,))]


def _exchange(ins, outs, sems, scatter):
    send_sems, recv_sems, local_sems = sems
    x, y, c, chips = _place()
    me = 2 * x + y
    n = len(ins)

    def remote(a, j, arriving):
        px, py = chips[j]
        them = 2 * px + py
        src = ins[a].at[them] if scatter else ins[a]
        return pltpu.make_async_remote_copy(
            src_ref=src, dst_ref=outs[a].at[them if arriving else me], send_sem=send_sems.at[a * PEER_CHIPS + j],
            recv_sem=recv_sems.at[a * PEER_CHIPS + j], device_id=(px, py, c), device_id_type=MESH)

    def local(a):
        return pltpu.make_async_copy(ins[a].at[me] if scatter else ins[a], outs[a].at[me], local_sems.at[a])

    def start():
        for a in range(n):
            local(a).start()
            for j in range(PEER_CHIPS):
                remote(a, j, False).start()

    def wait():
        for a in range(n):
            for j in range(PEER_CHIPS):
                remote(a, j, True).wait()
            local(a).wait()

    return start, wait


def _exchange_out_shapes(arrays, scatter):
    return [jax.ShapeDtypeStruct(a.shape if scatter else (N_CHIPS,) + a.shape, a.dtype) for a in arrays]


def exchange_between_chips(arrays, *, scatter, name):
    n = len(arrays)

    def body(*refs):
        start, wait = _exchange(refs[:n], refs[n:2 * n], refs[2 * n:], scatter)
        start()
        wait()

    return pl.pallas_call(
        body, name=name, out_shape=_exchange_out_shapes(arrays, scatter), in_specs=[_ANY] * n, out_specs=[_ANY] * n,
        scratch_shapes=_exchange_scratch(n),
    )(*arrays)


def ln_matmul(x, g, w, *, tm, tn, name, swiglu=False, gather=()):
    T, Dm = x.shape
    N = w.shape[1]
    if swiglu:
        assert tn == N
    half = N // 2
    ng = len(gather)
    n_out = 3 if swiglu else 2
    ni, nj = T // tm, N // tn

    def body(x_ref, g_ref, w_ref, *rest):
        gather_in, rest = rest[:ng], rest[ng:]
        h_ref, o_ref = rest[0], rest[1]
        i, j = pl.program_id(0), pl.program_id(1)
        if ng:
            start, wait = _exchange(gather_in, rest[n_out:n_out + ng], rest[n_out + ng:], False)
            pl.when((i == 0) & (j == 0))(start)

        @pl.when(j == 0)
        def _():
            xf = x_ref[...]
            r = lax.rsqrt(jnp.mean(xf * xf, axis=-1, keepdims=True) + EPS)
            h_ref[...] = (xf * r * g_ref[...]).astype(BF16)

        acc = _dot(h_ref[...], w_ref[...])
        o_ref[...] = acc.astype(BF16)
        if swiglu:
            gt, up = acc[:, :half], acc[:, half:]
            rest[2][...] = (gt * _sigmoid(gt) * up).astype(BF16)
        if ng:
            pl.when((i == ni - 1) & (j == nj - 1))(wait)

    out_shape = [jax.ShapeDtypeStruct((T, Dm), BF16), jax.ShapeDtypeStruct((T, N), BF16)]
    out_specs = [pl.BlockSpec((tm, Dm), lambda i, j: (i, 0)), pl.BlockSpec((tm, tn), lambda i, j: (i, j))]
    if swiglu:
        out_shape.append(jax.ShapeDtypeStruct((T, half), BF16))
        out_specs.append(pl.BlockSpec((tm, half), lambda i, j: (i, 0)))
    return pl.pallas_call(
        body, name=name, grid=(ni, nj),
        in_specs=[pl.BlockSpec((tm, Dm), lambda i, j: (i, 0)), pl.BlockSpec((1, Dm), lambda i, j: (0, 0)),
                  pl.BlockSpec((Dm, tn), lambda i, j: (0, j))] + [_ANY] * ng,
        out_specs=out_specs + [_ANY] * ng, out_shape=out_shape + _exchange_out_shapes(gather, False),
        scratch_shapes=_exchange_scratch(ng) if ng else [],
        compiler_params=_cp(("arbitrary", "arbitrary") if ng else ("parallel", "arbitrary")),
    )(x, g, w, *gather)


def matmul_residual(x, a, w, *, tm, name):
    T, N = x.shape
    K = a.shape[1]

    def body(x_ref, a_ref, w_ref, o_ref):
        o_ref[...] = x_ref[...] + _dot(a_ref[...], w_ref[...])

    return pl.pallas_call(
        body, name=name, grid=(T // tm,),
        in_specs=[pl.BlockSpec((tm, N), lambda i: (i, 0)), pl.BlockSpec((tm, K), lambda i: (i, 0)),
                  pl.BlockSpec((K, N), lambda i: (0, 0))],
        out_specs=pl.BlockSpec((tm, N), lambda i: (i, 0)), out_shape=jax.ShapeDtypeStruct((T, N), F32),
        compiler_params=_cp(("parallel",)),
    )(x, a, w)


def matmul_nt(a, w, *, tm, name):
    T, K = a.shape
    N = w.shape[0]

    def body(a_ref, w_ref, o_ref):
        o_ref[...] = _dot_nt(a_ref[...], w_ref[...]).astype(BF16)

    return pl.pallas_call(
        body, name=name, grid=(T // tm,),
        in_specs=[pl.BlockSpec((tm, K), lambda i: (i, 0)), pl.BlockSpec((N, K), lambda i: (0, 0))],
        out_specs=pl.BlockSpec((tm, N), lambda i: (i, 0)), out_shape=jax.ShapeDtypeStruct((T, N), BF16),
        compiler_params=_cp(("parallel",)),
    )(a, w)


def matmul_tn(a, g, *, tka, tn, tk, name, scatter=()):
    T, Ka = a.shape
    N = g.shape[1]
    ni, nj, nk = Ka // tka, N // tn, T // tk
    ns = len(scatter)

    def body(a_ref, g_ref, *rest):
        scatter_in, rest = rest[:ns], rest[ns:]
        o_ref, acc_ref = rest[0], rest[1 + ns]
        i, j, k = pl.program_id(0), pl.program_id(1), pl.program_id(2)
        if ns:
            start, wait = _exchange(scatter_in, rest[1:1 + ns], rest[2 + ns:], True)
            pl.when((i == 0) & (j == 0) & (k == 0))(start)

        @pl.when(k == 0)
        def _():
            acc_ref[...] = jnp.zeros_like(acc_ref)

        acc_ref[...] += _dot_tn(a_ref[...].astype(BF16), g_ref[...].astype(BF16))

        @pl.when(k == nk - 1)
        def _():
            o_ref[...] = acc_ref[...].astype(BF16)

        if ns:
            pl.when((i == ni - 1) & (j == nj - 1) & (k == nk - 1))(wait)

    res = pl.pallas_call(
        body, name=name, grid=(ni, nj, nk),
        in_specs=[pl.BlockSpec((tk, tka), lambda i, j, k: (k, i)), pl.BlockSpec((tk, tn), lambda i, j, k: (k, j))] + [_ANY] * ns,
        out_specs=[pl.BlockSpec((tka, tn), lambda i, j, k: (i, j))] + [_ANY] * ns,
        out_shape=[jax.ShapeDtypeStruct((Ka, N), BF16)] + _exchange_out_shapes(scatter, True),
        scratch_shapes=[pltpu.VMEM((tka, tn), F32)] + (_exchange_scratch(ns) if ns else []),
        compiler_params=_cp(("arbitrary",) * 3 if ns else ("parallel", "parallel", "arbitrary")),
    )(a, g, *scatter)
    return res if ns else res[0]


def matmul_nt_rmsbwd(a, w, x, g, dres, *, tm, tk, name, scatter=()):
    T, K = a.shape
    Dm = w.shape[0]
    nk = K // tk
    ni = T // tm
    ns = len(scatter)

    def body(a_ref, w_ref, x_ref, g_ref, dres_ref, *rest):
        scatter_in, rest = rest[:ns], rest[ns:]
        dx_ref, dg_ref = rest[0], rest[1]
        acc_ref = rest[2 + ns]
        i, k = pl.program_id(0), pl.program_id(1)
        if ns:
            start, wait = _exchange(scatter_in, rest[2:2 + ns], rest[3 + ns:], True)
            pl.when((i == 0) & (k == 0))(start)

        @pl.when(k == 0)
        def _():
            acc_ref[...] = jnp.zeros_like(acc_ref)

        @pl.when((i == 0) & (k == 0))
        def _():
            dg_ref[...] = jnp.zeros_like(dg_ref)

        acc_ref[...] += _dot_nt(a_ref[...], w_ref[...])

        @pl.when(k == nk - 1)
        def _():
            dh = acc_ref[...]
            xf = x_ref[...]
            r = lax.rsqrt(jnp.mean(xf * xf, axis=-1, keepdims=True) + EPS)
            xhat = xf * r
            dg_ref[...] += _rowsum8(dh * xhat)
            dxhat = dh * g_ref[...]
            dx_ref[...] = dres_ref[...] + r * (dxhat - xhat * jnp.mean(dxhat * xhat, axis=-1, keepdims=True))

        if ns:
            pl.when((i == ni - 1) & (k == nk - 1))(wait)

    return pl.pallas_call(
        body, name=name, grid=(ni, nk),
        in_specs=[pl.BlockSpec((tm, tk), lambda i, k: (i, k)), pl.BlockSpec((Dm, tk), lambda i, k: (0, k)),
                  pl.BlockSpec((tm, Dm), lambda i, k: (i, 0)), pl.BlockSpec((1, Dm), lambda i, k: (0, 0)),
                  pl.BlockSpec((tm, Dm), lambda i, k: (i, 0))] + [_ANY] * ns,
        out_specs=[pl.BlockSpec((tm, Dm), lambda i, k: (i, 0)), pl.BlockSpec((8, Dm), lambda i, k: (0, 0))] + [_ANY] * ns,
        out_shape=[jax.ShapeDtypeStruct((T, Dm), F32), jax.ShapeDtypeStruct((8, Dm), F32)] + _exchange_out_shapes(scatter, True),
        scratch_shapes=[pltpu.VMEM((tm, Dm), F32)] + (_exchange_scratch(ns) if ns else []),
        compiler_params=_cp(("arbitrary", "arbitrary")),
    )(a, w, x, g, dres, *scatter)


def ffn_down_bwd(dx, w_down, gu, *, tm, name):
    T, Dm = dx.shape
    H = w_down.shape[0]

    def body(dx_ref, w_ref, gu_ref, o_ref):
        dact = _dot_nt(dx_ref[...].astype(BF16), w_ref[...])
        gt = gu_ref[:, :H].astype(F32)
        up = gu_ref[:, H:].astype(F32)
        s = _sigmoid(gt)
        o_ref[:, :H] = (dact * up * (s * (1.0 + gt * (1.0 - s)))).astype(BF16)
        o_ref[:, H:] = (dact * gt * s).astype(BF16)

    return pl.pallas_call(
        body, name=name, grid=(T // tm,),
        in_specs=[pl.BlockSpec((tm, Dm), lambda i: (i, 0)), pl.BlockSpec((H, Dm), lambda i: (0, 0)),
                  pl.BlockSpec((tm, 2 * H), lambda i: (i, 0))],
        out_specs=pl.BlockSpec((tm, 2 * H), lambda i: (i, 0)), out_shape=jax.ShapeDtypeStruct((T, 2 * H), BF16),
        compiler_params=_cp(("parallel",)),
    )(dx, w_down, gu)


def _gate_specs(tm):
    return [pl.BlockSpec((tm, D_MODEL), lambda i, c=c: (i, c)) for c in (COL_GA // 8, COL_GB // 8)]


def merge_fwd(og, att, proj, w_pa, w_pb, *, tm, name):
    T, Dm = og.shape

    def body(og_ref, att_ref, ga_ref, gb_ref, wa_ref, wb_ref, ya_ref, yb_ref, mix_ref):
        ya = _dot(og_ref[...], wa_ref[...])
        yb = _dot(att_ref[...], wb_ref[...])
        ya_ref[...] = ya.astype(BF16)
        yb_ref[...] = yb.astype(BF16)
        mix_ref[...] = (_sigmoid(ga_ref[...].astype(F32)) * ya + _sigmoid(gb_ref[...].astype(F32)) * yb).astype(BF16)

    row = pl.BlockSpec((tm, Dm), lambda i: (i, 0))
    wsp = pl.BlockSpec((Dm, Dm), lambda i: (0, 0))
    return pl.pallas_call(
        body, name=name, grid=(T // tm,),
        in_specs=[row, row] + _gate_specs(tm) + [wsp, wsp],
        out_specs=[row, row, row], out_shape=[jax.ShapeDtypeStruct((T, Dm), BF16)] * 3,
        compiler_params=_cp(("parallel",)),
    )(og, att, proj, proj, w_pa, w_pb)


def merge_bwd(dx, w_o, proj, ya, yb, *, tm, name):
    T, Dm = dx.shape

    def body(dx_ref, w_ref, ga_ref, gb_ref, ya_ref, yb_ref, dya_ref, dyb_ref, dgate_ref):
        dmix = _dot_nt(dx_ref[...].astype(BF16), w_ref[...])
        sa = _sigmoid(ga_ref[...].astype(F32))
        sb = _sigmoid(gb_ref[...].astype(F32))
        dya_ref[...] = (dmix * sa).astype(BF16)
        dyb_ref[...] = (dmix * sb).astype(BF16)
        dgate_ref[:, :Dm] = (dmix * ya_ref[...].astype(F32) * sa * (1.0 - sa)).astype(BF16)
        dgate_ref[:, Dm:] = (dmix * yb_ref[...].astype(F32) * sb * (1.0 - sb)).astype(BF16)

    row = pl.BlockSpec((tm, Dm), lambda i: (i, 0))
    return pl.pallas_call(
        body, name=name, grid=(T // tm,),
        in_specs=[row, pl.BlockSpec((Dm, Dm), lambda i: (0, 0))] + _gate_specs(tm) + [row, row],
        out_specs=[row, row, pl.BlockSpec((tm, 2 * Dm), lambda i: (i, COL_GA // 16))],
        out_shape=[jax.ShapeDtypeStruct((T, Dm), BF16)] * 2 + [jax.ShapeDtypeStruct((T, IN_COLS), BF16)],
        compiler_params=_cp(("parallel",)),
    )(dx, w_o, proj, proj, ya, yb)


def loss_head(x, g, target, *, tm, name):
    T, Dm = x.shape

    def body(x_ref, g_ref, t_ref, dx_ref, l_ref, dg_ref):
        @pl.when(pl.program_id(0) == 0)
        def _():
            l_ref[...] = jnp.zeros_like(l_ref)
            dg_ref[...] = jnp.zeros_like(dg_ref)

        xf = x_ref[...]
        gv = g_ref[...]
        r = lax.rsqrt(jnp.mean(xf * xf, axis=-1, keepdims=True) + EPS)
        xhat = xf * r
        err = xhat * gv - t_ref[...]
        l_ref[...] += _rowsum8(err * err) * (0.5 / Dm)
        dy = err * (1.0 / Dm)
        dg_ref[...] += _rowsum8(dy * xhat)
        dxhat = dy * gv
        dx_ref[...] = r * (dxhat - xhat * jnp.mean(dxhat * xhat, axis=-1, keepdims=True))

    row = pl.BlockSpec((tm, Dm), lambda i: (i, 0))
    acc = pl.BlockSpec((8, Dm), lambda i: (0, 0))
    return pl.pallas_call(
        body, name=name, grid=(T // tm,),
        in_specs=[row, pl.BlockSpec((1, Dm), lambda i: (0, 0)), row],
        out_specs=[row, acc, acc],
        out_shape=[jax.ShapeDtypeStruct((T, Dm), F32), jax.ShapeDtypeStruct((8, Dm), F32), jax.ShapeDtypeStruct((8, Dm), F32)],
        compiler_params=_cp(("arbitrary",)),
    )(x, g, target)


_LEVELS = (32, 16, 8)
_DIAG = 8
_SUBLANES = 8
_HI = lax.Precision.HIGHEST


def _chunk_consts():
    t = lax.broadcasted_iota(jnp.int32, (CHUNK, CHUNK), 0)
    s = lax.broadcasted_iota(jnp.int32, (CHUNK, CHUNK), 1)
    tri = (s <= t).astype(F32)
    tri_t = (s >= t).astype(F32)
    masks = []
    for m in _LEVELS:
        sh = int(np.log2(2 * m))
        masks.append(((t >> sh) == (s >> sh)) & ((t & (2 * m - 1)) >= m) & ((s & (2 * m - 1)) < m))
    return tri, tri_t, masks


def _level_ref(b, m):
    if 2 * m >= _SUBLANES:
        pieces = [jnp.broadcast_to(b[p * 2 * m + m - 1:p * 2 * m + m, :], (2 * m, b.shape[1])) for p in range(CHUNK // (2 * m))]
        return pieces[0] if len(pieces) == 1 else jnp.concatenate(pieces, axis=0)
    groups = CHUNK // _SUBLANES
    b3 = b.reshape(groups, _SUBLANES, b.shape[1])
    row = lax.broadcasted_iota(jnp.int32, b3.shape, 1)
    ref = None
    for p in reversed(range(_SUBLANES // (2 * m))):
        src = jnp.broadcast_to(b3[:, p * 2 * m + m - 1:p * 2 * m + m, :], b3.shape)
        ref = src if ref is None else jnp.where(row < (p + 1) * 2 * m, src, ref)
    return ref.reshape(b.shape)


def _intra_off(q, kk, b, masks):
    a = jnp.zeros((CHUNK, CHUNK), F32)
    keep = []
    for m, mask in zip(_LEVELS, masks):
        e = jnp.exp2(-jnp.abs(b - _level_ref(b, m)))
        qt = (q * e).astype(BF16)
        kt = (kk * e).astype(BF16)
        a = a + jnp.where(mask, _dot_nt(qt, kt), 0.0)
        keep.append((mask, e, qt, kt))
    return a, keep


def _roll_rows(x, d):
    g = x.shape[0] // _DIAG
    return pltpu.roll(x.reshape(g, _DIAG, x.shape[1]), d % _DIAG, 1).reshape(x.shape)


def _diag_fwd(q, kk, v, b):
    lower = lax.broadcasted_iota(jnp.int32, q.shape, 0) & (_DIAG - 1)
    out = jnp.sum(q * kk, axis=-1, keepdims=True) * v
    for d in range(1, _DIAG):
        dec = jnp.exp2(jnp.where(lower >= d, b - _roll_rows(b, d), _NEG))
        out = out + jnp.sum(q * dec * _roll_rows(kk, d), axis=-1, keepdims=True) * _roll_rows(v, d)
    return out


def _diag_bwd(q, kk, v, b, do):
    lower = lax.broadcasted_iota(jnp.int32, q.shape, 0) & (_DIAG - 1)
    a = jnp.sum(q * kk, axis=-1, keepdims=True)
    da = jnp.sum(do * v, axis=-1, keepdims=True)
    dq, dk, dv = da * kk, da * q, a * do
    for d in range(1, _DIAG):
        kr = _roll_rows(kk, d)
        dec = jnp.exp2(jnp.where(lower >= d, b - _roll_rows(b, d), _NEG))
        qd = q * dec
        a = jnp.sum(qd * kr, axis=-1, keepdims=True)
        da = jnp.sum(do * _roll_rows(v, d), axis=-1, keepdims=True)
        dq = dq + da * dec * kr
        dk = dk + _roll_rows(da * qd, -d)
        dv = dv + _roll_rows(a * do, -d)
    return dq, dk, dv


def _hgrn_in_specs(ct, hp, order):
    def spec(col):
        return pl.BlockSpec((ct, hp * HG_DK), lambda h, i, col=col: (order(i), col // hp + h))
    vec = pl.BlockSpec((1, hp * HG_DK), lambda h, i: (0, h))
    return [spec(COL_HQ), spec(COL_HF), spec(COL_HI), spec(COL_HG), vec, vec]


def hgrn_fwd(proj, lb, gn, *, ct, hp, name):
    T = proj.shape[0]
    nc = ct // CHUNK

    def body(hq_ref, hf_ref, hi_ref, hg_ref, lb_ref, gn_ref, og_ref, o_ref, st_ref, s_ref):
        @pl.when(pl.program_id(1) == 0)
        def _():
            s_ref[...] = jnp.zeros_like(s_ref)

        tri, _, masks = _chunk_consts()

        def chunk(c, carry):
            sl = pl.ds(pl.multiple_of(c * CHUNK, CHUNK), CHUNK)
            for p in range(hp):
                cols = slice(p * HG_DK, (p + 1) * HG_DK)
                lb_v = lb_ref[:, cols]
                hq = hq_ref[sl, cols].astype(F32)
                v = hi_ref[sl, cols].astype(F32)
                gate = hg_ref[sl, cols].astype(F32)
                q = hq * _sigmoid(hq)
                f = lb_v + (1.0 - lb_v) * _sigmoid(hf_ref[sl, cols].astype(F32))
                kk = 1.0 - f
                b = jnp.dot(tri, jnp.log2(jnp.maximum(f, MIN_F)), precision=_HI, preferred_element_type=F32)
                st = s_ref[p]
                st_ref[p, c] = st
                a, _ = _intra_off(q, kk, b, masks)
                vb = v.astype(BF16)
                o = (_dot_nt((q * jnp.exp2(b)).astype(BF16), st.astype(BF16)) + _dot(a.astype(BF16), vb)
                     + _diag_fwd(q, kk, v, b))
                bl = b[CHUNK - 1:CHUNK, :]
                s_ref[p] = st * jnp.exp2(bl) + _dot_tn(vb, (kk * jnp.exp2(bl - b)).astype(BF16))
                o_ref[sl, cols] = o
                r = lax.rsqrt(jnp.mean(o * o, axis=-1, keepdims=True) + EPS)
                og_ref[sl, cols] = (o * r * gn_ref[:, cols] * (gate * _sigmoid(gate))).astype(BF16)
            return carry

        lax.fori_loop(0, nc, chunk, 0)

    blk = pl.BlockSpec((ct, hp * HG_DK), lambda h, i: (i, h))
    return pl.pallas_call(
        body, name=name, grid=(HG_HEADS // hp, T // ct),
        in_specs=_hgrn_in_specs(ct, hp, lambda i: i),
        out_specs=[blk, blk, pl.BlockSpec((hp, nc, HG_DK, HG_DK), lambda h, i: (h, i, 0, 0))],
        out_shape=[jax.ShapeDtypeStruct((T, HG_WIDTH), BF16), jax.ShapeDtypeStruct((T, HG_WIDTH), F32),
                   jax.ShapeDtypeStruct((HG_HEADS, T // CHUNK, HG_DK, HG_DK), F32)],
        scratch_shapes=[pltpu.VMEM((hp, HG_DK, HG_DK), F32)],
        compiler_params=_cp(("parallel", "arbitrary")),
    )(proj, proj, proj, proj, lb, gn)


def hgrn_bwd(proj, lb, gn, o_raw, states, dog, dproj, *, ct, name):
    T = proj.shape[0]
    nc = ct // CHUNK
    nblk = T // ct
    hp = HG_HEADS

    def body(hq_ref, hf_ref, hi_ref, hg_ref, lb_ref, gn_ref, o_ref, st_ref, dog_ref, _dproj_in,
             dp_ref, dlb_ref, dgn_ref, ds_ref):
        dhq_ref, dhf_ref, dhi_ref, dhg_ref = (dp_ref.at[:, s * HG_WIDTH:(s + 1) * HG_WIDTH] for s in range(4))
        @pl.when(pl.program_id(1) == 0)
        def _():
            ds_ref[...] = jnp.zeros_like(ds_ref)
            dlb_ref[...] = jnp.zeros_like(dlb_ref)
            dgn_ref[...] = jnp.zeros_like(dgn_ref)

        tri, tri_t, masks = _chunk_consts()
        last_row = lax.broadcasted_iota(jnp.int32, (CHUNK, HG_DK), 0) == CHUNK - 1

        def chunk(ci, carry):
            c = nc - 1 - ci
            sl = pl.ds(pl.multiple_of(c * CHUNK, CHUNK), CHUNK)
            for p in range(hp):
                cols = slice(p * HG_DK, (p + 1) * HG_DK)
                lb_v = lb_ref[:, cols]
                gn_v = gn_ref[:, cols]
                hq = hq_ref[sl, cols].astype(F32)
                v = hi_ref[sl, cols].astype(F32)
                gate = hg_ref[sl, cols].astype(F32)
                sq = _sigmoid(hq)
                q = hq * sq
                sig = _sigmoid(hf_ref[sl, cols].astype(F32))
                f = lb_v + (1.0 - lb_v) * sig
                kk = 1.0 - f
                b = jnp.dot(tri, jnp.log2(jnp.maximum(f, MIN_F)), precision=_HI, preferred_element_type=F32)
                st = st_ref[p, c]
                dst = ds_ref[p]
                o = o_ref[sl, cols]
                r = lax.rsqrt(jnp.mean(o * o, axis=-1, keepdims=True) + EPS)
                xhat = o * r
                sg = _sigmoid(gate)
                dog_v = dog_ref[sl, cols].astype(F32)
                dy = dog_v * (gate * sg)
                dhg_ref[sl, cols] = (dog_v * xhat * gn_v * (sg * (1.0 + gate * (1.0 - sg)))).astype(BF16)
                dgn_ref[:, cols] += _rowsum8(dy * xhat)
                dxh = dy * gn_v
                do = r * (dxh - xhat * jnp.mean(dxh * xhat, axis=-1, keepdims=True))
                dob = do.astype(BF16)
                vb = v.astype(BF16)
                stb = st.astype(BF16)
                dstb = dst.astype(BF16)
                eb = jnp.exp2(b)
                bl = b[CHUNK - 1:CHUNK, :]
                ebl = jnp.exp2(bl)
                edec = jnp.exp2(bl - b)
                qe = q * eb
                kdec = kk * edec
                a, keep = _intra_off(q, kk, b, masks)
                da = _dot_nt(dob, vb)
                dkdec = _dot(vb, dstb)
                dq = _dot(dob, stb) * eb
                dk = dkdec * edec
                dv = _dot_tn(a.astype(BF16), dob) + _dot_nt(kdec.astype(BF16), dstb)
                for mask, e, qt, kt in keep:
                    dsm = jnp.where(mask, da, 0.0).astype(BF16)
                    dq = dq + _dot(dsm, kt) * e
                    dk = dk + _dot_tn(dsm, qt) * e
                ddq, ddk, ddv = _diag_bwd(q, kk, v, b, do)
                dq = dq + ddq
                dk = dk + ddk
                dhi_ref[sl, cols] = (dv + ddv).astype(BF16)
                dbl = jnp.sum(dkdec * kdec, axis=0, keepdims=True) + ebl * jnp.sum(st * dst, axis=0, keepdims=True)
                db = q * dq - kk * dk
                db = jnp.where(last_row, db + dbl, db)
                dlogf = jnp.dot(tri_t, db, precision=_HI, preferred_element_type=F32)
                ds_ref[p] = _dot_tn(dob, qe.astype(BF16)) + dst * ebl
                df = jnp.where(f > MIN_F, dlogf / f, 0.0) - dk
                dhf_ref[sl, cols] = (df * (1.0 - lb_v) * sig * (1.0 - sig)).astype(BF16)
                dlb_ref[:, cols] += _rowsum8(df * (1.0 - sig))
                dhq_ref[sl, cols] = (dq * (sq * (1.0 + hq * (1.0 - sq)))).astype(BF16)
            return carry

        lax.fori_loop(0, nc, chunk, 0)

    rev = lambda i: nblk - 1 - i
    blk = pl.BlockSpec((ct, hp * HG_DK), lambda h, i: (rev(i), h))
    acc = pl.BlockSpec((8, hp * HG_DK), lambda h, i: (0, h))
    return pl.pallas_call(
        body, name=name, grid=(HG_HEADS // hp, nblk),
        in_specs=_hgrn_in_specs(ct, hp, rev) + [blk, pl.BlockSpec((hp, nc, HG_DK, HG_DK), lambda h, i: (h, rev(i), 0, 0)), blk, _ANY],
        out_specs=[pl.BlockSpec((ct, 4 * HG_WIDTH), lambda h, i: (rev(i), 0)), acc, acc],
        out_shape=[jax.ShapeDtypeStruct(dproj.shape, dproj.dtype)] + [jax.ShapeDtypeStruct((8, HG_WIDTH), F32)] * 2,
        scratch_shapes=[pltpu.VMEM((hp, HG_DK, HG_DK), F32)],
        input_output_aliases={9: 0},
        compiler_params=_cp(("parallel", "arbitrary")),
    )(proj, proj, proj, proj, lb, gn, o_raw, states, dog, dproj)


_GW = ATT_GROUP * HEAD_DIM
_NEG = -1e30


def rope_tables(T):
    half = ROPE_DIM // 2
    inv = ROPE_THETA ** (-jnp.arange(half, dtype=F32) * 2.0 / ROPE_DIM)
    ang = jnp.arange(T, dtype=F32)[:, None] * inv[None, :]
    pad1 = jnp.ones((T, HEAD_DIM - ROPE_DIM), F32)
    cos = jnp.concatenate([jnp.cos(ang), jnp.cos(ang), pad1], axis=1)
    sin = jnp.concatenate([jnp.sin(ang), jnp.sin(ang), 0.0 * pad1], axis=1)
    p = np.zeros((_GW, _GW), np.float32)
    for base in range(0, _GW, HEAD_DIM):
        for i in range(half):
            p[base + i + half, base + i] = -1.0
            p[base + i, base + i + half] = 1.0
    reps = _GW // HEAD_DIM
    return jnp.tile(cos, (1, reps)), jnp.tile(sin, (1, reps)), jnp.asarray(p, BF16)


def _rope(x, cos, sin, pm):
    return x.astype(F32) * cos + _dot(x, pm) * sin


def _unrope(dx, cos, sin, pm):
    return dx * cos + _dot_nt((dx * sin).astype(BF16), pm)


_GQ = ATT_GROUP * WINDOW


def _swa_mask(n, queries_on_rows):
    shape = (_GQ, 2 * WINDOW) if queries_on_rows else (2 * WINDOW, _GQ)
    r = lax.broadcasted_iota(jnp.int32, shape, 0 if queries_on_rows else 1) & (WINDOW - 1)
    c = lax.broadcasted_iota(jnp.int32, shape, 1 if queries_on_rows else 0)
    delta = r + WINDOW - c
    return (delta >= 0) & (delta < WINDOW) & ((c >= WINDOW) | (n > 0))


def _sink_vector(sink_ref, g, queries_on_rows):
    shape = (_GQ, 1) if queries_on_rows else (1, _GQ)
    i = lax.broadcasted_iota(jnp.int32, shape, 0 if queries_on_rows else 1)
    out = jnp.full(shape, sink_ref[g * ATT_GROUP + ATT_GROUP - 1], F32)
    for hh in range(ATT_GROUP - 2, -1, -1):
        out = jnp.where(i < (hh + 1) * WINDOW, sink_ref[g * ATT_GROUP + hh], out)
    return out


def _head(x, h):
    return x[:, h * HEAD_DIM:(h + 1) * HEAD_DIM]


def _stack_heads(x):
    return jnp.concatenate([_head(x, hh) for hh in range(ATT_GROUP)], axis=0)


def _unstack_heads(x):
    return jnp.concatenate([x[hh * WINDOW:(hh + 1) * WINDOW] for hh in range(ATT_GROUP)], axis=1)


def _swa_specs(cur):
    prev = lambda n: jnp.maximum(n - 1, 0)
    kv = lambda col, f: pl.BlockSpec((WINDOW, _GW), lambda n: (f(n), col))
    tab = lambda f: pl.BlockSpec((WINDOW, _GW), lambda n: (f(n), 0))
    return [pl.BlockSpec((WINDOW, ATT_WIDTH), lambda n: (cur(n), COL_AQ // 8)),
            kv(COL_AK // 2, prev), kv(COL_AK // 2, cur), kv(COL_AV // 2, prev), kv(COL_AV // 2, cur),
            tab(prev), tab(cur), tab(prev), tab(cur),
            pl.BlockSpec((_GW, _GW), lambda n: (0, 0)), pl.BlockSpec(memory_space=pltpu.SMEM)]


def swa_fwd(proj, cos, sin, pm, sinks, *, name):
    T = proj.shape[0]
    nb = T // WINDOW
    scale = HEAD_DIM ** -0.5

    def body(q_ref, kp_ref, kc_ref, vp_ref, vc_ref, cp_ref, cc_ref, sp_ref, sc_ref, pm_ref, sink_ref, o_ref):
        n = pl.program_id(0)
        pm_v = pm_ref[...]
        mask = _swa_mask(n, True)
        k_cat = jnp.concatenate([_rope(kp_ref[...], cp_ref[...], sp_ref[...], pm_v),
                                 _rope(kc_ref[...], cc_ref[...], sc_ref[...], pm_v)], axis=0).astype(BF16)
        v_cat = jnp.concatenate([vp_ref[...], vc_ref[...]], axis=0)
        for g in range(ATT_KV_HEADS):
            qs = _stack_heads(_rope(q_ref[:, g * _GW:(g + 1) * _GW], cc_ref[...], sc_ref[...], pm_v).astype(BF16))
            sk = _sink_vector(sink_ref, g, True)
            s = jnp.where(mask, _dot_nt(qs, _head(k_cat, g)) * scale, _NEG)
            m = jnp.maximum(jnp.max(s, axis=-1, keepdims=True), sk)
            p = jnp.exp(s - m)
            l = jnp.sum(p, axis=-1, keepdims=True) + jnp.exp(sk - m)
            o = _dot(p.astype(BF16), _head(v_cat, g)) / l
            o_ref[:, g * _GW:(g + 1) * _GW] = _unstack_heads(o).astype(BF16)

    return pl.pallas_call(
        body, name=name, grid=(nb,), in_specs=_swa_specs(lambda n: n),
        out_specs=pl.BlockSpec((WINDOW, ATT_WIDTH), lambda n: (n, 0)),
        out_shape=jax.ShapeDtypeStruct((T, ATT_WIDTH), BF16), compiler_params=_cp(("parallel",)),
    )(proj, proj, proj, proj, proj, cos, cos, sin, sin, pm, sinks)


def swa_bwd(proj, cos, sin, pm, sinks, att, datt, dproj, *, name):
    T = proj.shape[0]
    nb = T // WINDOW
    scale = HEAD_DIM ** -0.5
    clamp = lambda n: jnp.minimum(n, nb - 1)

    def body(q_ref, kp_ref, kc_ref, vp_ref, vc_ref, cp_ref, cc_ref, sp_ref, sc_ref, pm_ref, sink_ref, att_ref, datt_ref,
             _dproj_in, dp_ref, dsink_ref, cq_ref, ck_ref, cv_ref):
        n = pl.program_id(0)
        pm_v = pm_ref[...]

        @pl.when(n == 0)
        def _():
            dsink_ref[...] = jnp.zeros_like(dsink_ref)
            cq_ref[...] = jnp.zeros_like(cq_ref)
            ck_ref[...] = jnp.zeros_like(ck_ref)
            cv_ref[...] = jnp.zeros_like(cv_ref)

        def write_prev(dk_prev, dv_prev):
            dp_ref[:, :ATT_WIDTH] = cq_ref[...]
            dk = _unrope(ck_ref[...] + dk_prev, cp_ref[...], sp_ref[...], pm_v)
            dp_ref[:, ATT_WIDTH:ATT_WIDTH + _GW] = dk.astype(BF16)
            dp_ref[:, ATT_WIDTH + _GW:] = (cv_ref[...] + dv_prev).astype(BF16)

        @pl.when(n < nb)
        def _():
            mask = _swa_mask(n, False)
            ones = jnp.ones((8, HEAD_DIM), F32)
            k_cat = jnp.concatenate([_rope(kp_ref[...], cp_ref[...], sp_ref[...], pm_v),
                                     _rope(kc_ref[...], cc_ref[...], sc_ref[...], pm_v)], axis=0).astype(BF16)
            v_cat = jnp.concatenate([vp_ref[...], vc_ref[...]], axis=0)
            dqs, dks, dvs = [], [], []
            for g in range(ATT_KV_HEADS):
                gc = slice(g * _GW, (g + 1) * _GW)
                qs = _stack_heads(_rope(q_ref[:, gc], cc_ref[...], sc_ref[...], pm_v).astype(BF16))
                dos = _stack_heads(datt_ref[:, gc])
                os_ = _stack_heads(att_ref[:, gc])
                kh, vh = _head(k_cat, g), _head(v_cat, g)
                sk = _sink_vector(sink_ref, g, False)
                s = jnp.where(mask, _dot_nt(kh, qs) * scale, _NEG)
                m = jnp.maximum(jnp.max(s, axis=0, keepdims=True), sk)
                e = jnp.exp(s - m)
                es = jnp.exp(sk - m)
                inv_l = 1.0 / (jnp.sum(e, axis=0, keepdims=True) + es)
                p = e * inv_l
                dsum = lax.dot_general(ones, dos.astype(F32) * os_.astype(F32), (((1,), (1,)), ((), ())),
                                       precision=_HI, preferred_element_type=F32)[0:1]
                dsink_ref[g, 0:1, :] += -(es * inv_l) * dsum
                ds = (p * (_dot_nt(vh, dos) - dsum) * scale).astype(BF16)
                dqs.append(_unrope(_unstack_heads(_dot_tn(ds, kh)), cc_ref[...], sc_ref[...], pm_v))
                dks.append(_dot(ds, qs))
                dvs.append(_dot(p.astype(BF16), dos))
            dk_all = jnp.concatenate(dks, axis=1)
            dv_all = jnp.concatenate(dvs, axis=1)

            @pl.when(n > 0)
            def _():
                write_prev(dk_all[:WINDOW], dv_all[:WINDOW])

            cq_ref[...] = jnp.concatenate(dqs, axis=1).astype(BF16)
            ck_ref[...] = dk_all[WINDOW:]
            cv_ref[...] = dv_all[WINDOW:]

        @pl.when(n == nb)
        def _():
            write_prev(jnp.zeros((WINDOW, _GW), F32), jnp.zeros((WINDOW, _GW), F32))

    row = pl.BlockSpec((WINDOW, ATT_WIDTH), lambda n: (clamp(n), 0))
    slab = ATT_WIDTH + 2 * _GW
    return pl.pallas_call(
        body, name=name, grid=(nb + 1,), in_specs=_swa_specs(clamp) + [row, row, _ANY],
        out_specs=[pl.BlockSpec((WINDOW, slab), lambda n: (jnp.maximum(n - 1, 0), COL_AQ * HG_DK // slab)),
                   pl.BlockSpec((ATT_KV_HEADS, 8, _GQ), lambda n: (0, 0, 0))],
        out_shape=[jax.ShapeDtypeStruct(dproj.shape, dproj.dtype), jax.ShapeDtypeStruct((ATT_KV_HEADS, 8, _GQ), F32)],
        scratch_shapes=[pltpu.VMEM((WINDOW, ATT_WIDTH), BF16), pltpu.VMEM((WINDOW, _GW), F32), pltpu.VMEM((WINDOW, _GW), F32)],
        input_output_aliases={13: 0},
        compiler_params=_cp(("arbitrary",)),
    )(proj, proj, proj, proj, proj, cos, cos, sin, sin, pm, sinks, att, datt, dproj)


def swap_with_sibling(arrays, *, name):
    n = len(arrays)

    def body(*refs):
        ins, outs = refs[:n], refs[n:2 * n]
        send_sems, recv_sems = refs[2 * n:]
        x, y, c, _ = _place()
        copies = [pltpu.make_async_remote_copy(src_ref=ins[a], dst_ref=outs[a], send_sem=send_sems.at[a],
                                               recv_sem=recv_sems.at[a], device_id=(x, y, 1 - c), device_id_type=MESH)
                  for a in range(n)]
        for cp in copies:
            cp.start()
        for cp in copies:
            cp.wait()

    return pl.pallas_call(
        body, name=name, out_shape=[jax.ShapeDtypeStruct(a.shape, a.dtype) for a in arrays],
        in_specs=[_ANY] * n, out_specs=[_ANY] * n,
        scratch_shapes=[pltpu.SemaphoreType.DMA((n,)), pltpu.SemaphoreType.DMA((n,))],
    )(*arrays)


N_DEV = 8
SMALL_ROWS = 24


def allreduce_small(v, *, name):
    def body(v_ref, o_ref, recv_ref, send_sems, recv_sems):
        x, y, c, _ = _place()
        me = 4 * x + 2 * y + c
        recv_ref[me] = v_ref[...]

        def copy(k, slot):
            peer = (x ^ (k >> 2), y ^ ((k >> 1) & 1), c ^ (k & 1))
            return pltpu.make_async_remote_copy(src_ref=v_ref, dst_ref=recv_ref.at[slot], send_sem=send_sems.at[k - 1],
                                                recv_sem=recv_sems.at[k - 1], device_id=peer, device_id_type=MESH)

        for k in range(1, N_DEV):
            copy(k, me).start()
        for k in range(1, N_DEV):
            copy(k, me ^ k).wait()
        acc = recv_ref[0]
        for d in range(1, N_DEV):
            acc = acc + recv_ref[d]
        o_ref[...] = acc

    vm = pl.BlockSpec(memory_space=pltpu.VMEM)
    return pl.pallas_call(
        body, name=name, out_shape=jax.ShapeDtypeStruct(v.shape, v.dtype), in_specs=[vm], out_specs=vm,
        scratch_shapes=[pltpu.VMEM((N_DEV,) + v.shape, v.dtype), pltpu.SemaphoreType.DMA((N_DEV - 1,)),
                        pltpu.SemaphoreType.DMA((N_DEV - 1,))],
    )(v)


def sum_parts(parts, *, tr, name):
    _, R, C = parts.shape

    def body(p_ref, o_ref):
        acc = p_ref[0].astype(F32)
        for q in range(1, N_CHIPS):
            acc = acc + p_ref[q].astype(F32)
        o_ref[...] = acc

    return pl.pallas_call(
        body, name=name, grid=(R // tr,), in_specs=[pl.BlockSpec((N_CHIPS, tr, C), lambda i: (0, i, 0))],
        out_specs=pl.BlockSpec((tr, C), lambda i: (i, 0)), out_shape=jax.ShapeDtypeStruct((R, C), F32),
        compiler_params=_cp(("parallel",)),
    )(parts)


def adamw(g_a, g_b, w, m, v, *, tr, name):
    R, C = w.shape
    c1 = 1.0 - ADAM_B1 ** ADAM_STEP
    c2 = 1.0 - ADAM_B2 ** ADAM_STEP
    ng = 1 if g_b is None else 2

    def body(*refs):
        w_ref, m_ref, v_ref, g_ref, d_ref, nm_ref, nv_ref = refs[ng:]
        g = refs[0][...] if ng == 1 else refs[0][...] + refs[1][...]
        nm = ADAM_B1 * m_ref[...] + (1.0 - ADAM_B1) * g
        nv = ADAM_B2 * v_ref[...] + (1.0 - ADAM_B2) * (g * g)
        g_ref[...] = g
        nm_ref[...] = nm
        nv_ref[...] = nv
        d_ref[...] = -ADAM_LR * ((nm / c1) / (jnp.sqrt(nv / c2) + ADAM_EPS) + ADAM_WD * w_ref[...])

    blk = pl.BlockSpec((tr, C), lambda i: (i, 0))
    gs = [g_a] if g_b is None else [g_a, g_b]
    return pl.pallas_call(
        body, name=name, grid=(R // tr,), in_specs=[blk] * (ng + 3), out_specs=[blk] * 4,
        out_shape=[jax.ShapeDtypeStruct((R, C), F32)] * 4, compiler_params=_cp(("parallel",)),
    )(*gs, w, m, v)


def _lb_bounds(lb_logits):
    p = jax.nn.softmax(lb_logits.astype(F32), axis=0)
    return jnp.cumsum(p, axis=0) - p[0:1]


def _pack_small(n1, lb, hgn, n2, fin, sinks, extra):
    row = jnp.concatenate([sinks.reshape(-1), jnp.reshape(extra, (1,)),
                           jnp.zeros((D_MODEL - sinks.size - 1,), F32)])
    return jnp.concatenate([n1, lb, hgn, n2, fin[None], row[None], jnp.zeros((SMALL_ROWS - 18, D_MODEL), F32)], axis=0)


def _unpack_small(buf):
    return (buf[0:4], buf[4:8], buf[8:12], buf[12:16], buf[16], buf[17, :DEPTH * ATT_Q_HEADS].reshape(DEPTH, ATT_Q_HEADS),
            buf[17, DEPTH * ATT_Q_HEADS])


def _cols_full(g):
    q, r, c = g.shape
    return jnp.transpose(g, (1, 0, 2)).reshape(r, q * c)


def _cols_split(w):
    r, qc = w.shape
    return jnp.transpose(w.reshape(r, N_CHIPS, qc // N_CHIPS), (1, 0, 2))


def kernel(x, norm1, w_in, lb_logits, hg_norm, attn_sinks, w_pa, w_pb, w_o, norm2, w_gate, w_up, w_down, final_norm, loss_target, m_norm1, m_w_in, m_lb_logits, m_hg_norm, m_attn_sinks, m_w_pa, m_w_pb, m_w_o, m_norm2, m_w_gate, m_w_up, m_w_down, m_final_norm, v_norm1, v_w_in, v_lb_logits, v_hg_norm, v_attn_sinks, v_w_pa, v_w_pb, v_w_o, v_norm2, v_w_gate, v_w_up, v_w_down, v_final_norm):
    T = x.shape[1]
    H = FFN_HIDDEN
    tm = min(512, T)
    tmx = min(1024, T)
    tkx = min(2048, T)
    big = dict(w_in=w_in, w_pa=w_pa, w_pb=w_pb, w_o=w_o, w_gate=w_gate, w_up=w_up, w_down=w_down)
    col_sharded = ("w_in", "w_gate", "w_up")

    shards = {k: w.astype(BF16) for k, w in big.items()}
    others = [k for k in big if k != "w_in"]

    def whole(k, g):
        return _cols_full(g) if k in col_sharded else g.reshape(-1, g.shape[-1])

    def layer_weights(w_in_blocks, other_blocks):
        full = {k: whole(k, g) for k, g in zip(others, other_blocks)}
        full["w_in"] = _to_kernel_cols(whole("w_in", w_in_blocks))
        full["w_gu"] = jnp.concatenate([full.pop("w_gate"), full.pop("w_up")], axis=-1)
        return full

    lb_all, lb_vjp = jax.vjp(_lb_bounds, lb_logits)
    cos, sin, pm = rope_tables(T)

    xs = x[0]
    saved, weights = [], []
    (w_in_blocks,) = exchange_between_chips([shards["w_in"][0]], scatter=False, name="gather_w_in_0")
    w_in_now = _to_kernel_cols(whole("w_in", w_in_blocks))
    other_blocks = None
    for l in range(DEPTH):
        carried = [shards["w_in"][l + 1]] if l + 1 < DEPTH else []
        if l == 0:
            carried = carried + [shards[k][0] for k in others]
        h, proj, *got = ln_matmul(xs, norm1[l:l + 1], w_in_now, tm=tmx, tn=1536, name=f"in_proj_{l}", gather=carried)
        if l == 0:
            other_blocks = got[1:]
        full = layer_weights(w_in_blocks, other_blocks)
        weights.append(full)
        w_in_blocks = got[0] if got else None
        og, o_raw, states = hgrn_fwd(proj, lb_all[l:l + 1], hg_norm[l:l + 1], ct=min(256, T), hp=HG_HEADS, name=f"hgrn_fwd_{l}")
        att = swa_fwd(proj, cos, sin, pm, attn_sinks[l], name=f"swa_fwd_{l}")
        ya, yb, mix = merge_fwd(og, att, proj, full["w_pa"], full["w_pb"], tm=tm, name=f"merge_fwd_{l}")
        x1 = matmul_residual(xs, mix, full["w_o"], tm=tmx, name=f"out_proj_{l}")
        carried = [shards[k][l + 1] for k in others] if l + 1 < DEPTH else []
        h2, gu, act, *other_blocks = ln_matmul(x1, norm2[l:l + 1], full["w_gu"], tm=min(256, T), tn=2 * H, name=f"ffn_up_{l}",
                                               swiglu=True, gather=carried)
        if l + 1 < DEPTH:
            w_in_now = _to_kernel_cols(whole("w_in", w_in_blocks))
        x2 = matmul_residual(x1, act, full["w_down"], tm=tmx, name=f"ffn_down_{l}")
        saved.append((xs, h, proj, og, o_raw, states, att, ya, yb, mix, x1, h2, gu, act))
        xs = x2

    dx, loss_parts, dfin_parts = loss_head(xs, final_norm[None], loss_target[0], tm=tm, name="loss_head")

    arrived = [None] * DEPTH
    dn1, dn2, dlb, dgn, dsk = ([None] * DEPTH for _ in range(5))
    for l in reversed(range(DEPTH)):
        xs, h, proj, og, o_raw, states, att, ya, yb, mix, x1, h2, gu, act = saved[l]
        full = weights[l]
        dw = {}
        dgu = ffn_down_bwd(dx, full["w_down"], gu, tm=min(256, T), name=f"ffn_down_bwd_{l}")
        dw["w_down"] = matmul_tn(act, dx, tka=H // 2, tn=D_MODEL, tk=tkx, name=f"dw_down_{l}")
        dwgu = matmul_tn(h2, dgu, tka=D_MODEL, tn=H // 2, tk=tkx, name=f"dw_gu_{l}")
        dw["w_gate"], dw["w_up"] = dwgu[:, :H], dwgu[:, H:]
        dx1, dn2_parts = matmul_nt_rmsbwd(dgu, full["w_gu"], x1, norm2[l:l + 1], dx, tm=tmx, tk=H // 2, name=f"ffn_up_bwd_{l}")
        dya, dyb, dproj = merge_bwd(dx1, full["w_o"], proj, ya, yb, tm=tm, name=f"merge_bwd_{l}")
        dw["w_o"] = matmul_tn(mix, dx1, tka=D_MODEL, tn=D_MODEL, tk=tkx, name=f"dw_o_{l}")
        dog = matmul_nt(dya, full["w_pa"], tm=tmx, name=f"pa_bwd_{l}")
        datt = matmul_nt(dyb, full["w_pb"], tm=tmx, name=f"pb_bwd_{l}")
        dw["w_pa"] = matmul_tn(og, dya, tka=D_MODEL, tn=D_MODEL, tk=tkx, name=f"dw_pa_{l}")
        dw["w_pb"] = matmul_tn(att, dyb, tka=D_MODEL, tn=D_MODEL, tk=tkx, name=f"dw_pb_{l}")
        dproj, dsk_parts = swa_bwd(proj, cos, sin, pm, attn_sinks[l], att, datt, dproj, name=f"swa_bwd_{l}")
        dproj, dlb_parts, dgn_parts = hgrn_bwd(proj, lb_all[l:l + 1], hg_norm[l:l + 1], o_raw, states, dog, dproj,
                                               ct=min(256, T), name=f"hgrn_bwd_{l}")
        split = lambda k, g: _cols_split(g) if k in col_sharded else g.reshape(N_CHIPS, -1, g.shape[-1])
        dw_in, *arrived_others = matmul_tn(h, dproj, tka=D_MODEL, tn=1536, tk=tkx, name=f"dw_in_{l}",
                                           scatter=[split(k, dw[k]) for k in others])
        dx, dn1_parts, arrived_w_in = matmul_nt_rmsbwd(dproj, full["w_in"], xs, norm1[l:l + 1], dx1, tm=tmx, tk=1536,
                                                       name=f"in_proj_bwd_{l}", scatter=[split("w_in", _from_kernel_cols(dw_in))])
        arrived[l] = [arrived_w_in] + arrived_others
        dn1[l], dn2[l], dlb[l], dgn[l] = dn1_parts.sum(0), dn2_parts.sum(0), dlb_parts.sum(0), dgn_parts.sum(0)
        dsk[l] = dsk_parts[:, 0, :].reshape(ATT_Q_HEADS, WINDOW).sum(-1)
    grad_x = dx[None]

    mine = _pack_small(jnp.stack(dn1), jnp.stack(dlb), jnp.stack(dgn), jnp.stack(dn2), dfin_parts.sum(0), jnp.stack(dsk),
                       loss_parts.sum())
    total = allreduce_small(mine, name="allreduce_small")
    g_n1, g_lb_all, g_gn, g_n2, g_fin, g_sk, loss = _unpack_small(total)
    (g_lb,) = lb_vjp(g_lb_all)
    zero = jnp.zeros((), F32)
    small = adamw(_pack_small(g_n1, g_lb, g_gn, g_n2, g_fin, g_sk, zero), None,
                  _pack_small(norm1, lb_logits, hg_norm, norm2, final_norm, attn_sinks, zero),
                  _pack_small(m_norm1, m_lb_logits, m_hg_norm, m_norm2, m_final_norm, m_attn_sinks, zero),
                  _pack_small(v_norm1, v_lb_logits, v_hg_norm, v_norm2, v_final_norm, v_attn_sinks, zero),
                  tr=SMALL_ROWS, name="adamw_small")
    small = [_unpack_small(s)[:6] for s in small]

    sums = []
    for i, k in enumerate(big):
        a = jnp.stack([arrived[l][i] for l in range(DEPTH)], axis=1)
        q, nl, r, c = a.shape
        sums.append(sum_parts(a.reshape(q, nl * r, c), tr=128, name=f"sum_{k}"))
    theirs = swap_with_sibling(sums, name="swap_sums")
    mom = dict(w_in=(m_w_in, v_w_in), w_pa=(m_w_pa, v_w_pa), w_pb=(m_w_pb, v_w_pb), w_o=(m_w_o, v_w_o),
               w_gate=(m_w_gate, v_w_gate), w_up=(m_w_up, v_w_up), w_down=(m_w_down, v_w_down))
    upd = {}
    for k, mine_k, theirs_k in zip(big, sums, theirs):
        shp = big[k].shape
        flat = lambda t: t.reshape(shp[0] * shp[1], shp[2])
        res = adamw(mine_k, theirs_k, flat(big[k]), flat(mom[k][0]), flat(mom[k][1]), tr=128, name=f"adamw_{k}")
        upd[k] = [t.reshape(shp) for t in res]

    order = ("norm1", "w_in", "lb_logits", "hg_norm", "attn_sinks", "w_pa", "w_pb", "w_o", "norm2", "w_gate", "w_up",
             "w_down", "final_norm")
    small_pos = dict(norm1=0, lb_logits=1, hg_norm=2, norm2=3, final_norm=4, attn_sinks=5)
    outs = [loss, grad_x]
    for kind in range(4):
        for name in order:
            outs.append(upd[name][kind] if name in upd else small[kind][small_pos[name]])
    return tuple(outs)
```

```python
import functools

import jax
import jax.numpy as jnp
import numpy as np
from jax import lax
from jax.experimental import pallas as pl
from jax.experimental.pallas import tpu as pltpu

F32 = jnp.float32
BF16 = jnp.bfloat16

D_MODEL = 1024
DEPTH = 4
HG_HEADS = 8
HG_DK = 128
HG_WIDTH = HG_HEADS * HG_DK
CHUNK = 64
ATT_Q_HEADS = 16
ATT_KV_HEADS = 4
ATT_GROUP = ATT_Q_HEADS // ATT_KV_HEADS
HEAD_DIM = 64
ATT_WIDTH = ATT_Q_HEADS * HEAD_DIM
ATT_KV_WIDTH = ATT_KV_HEADS * HEAD_DIM
WINDOW = 128
ROPE_THETA = 500000.0
ROPE_DIM = HEAD_DIM // 4
FFN_HIDDEN = 2816
IN_COLS = 4 * HG_WIDTH + ATT_WIDTH + 2 * ATT_KV_WIDTH + 2 * D_MODEL
EPS = 1e-6
MIN_F = 1e-30
COL_HQ, COL_HF, COL_HI, COL_HG = 0, 8, 16, 24
COL_GA, COL_GB, COL_AQ, COL_AK, COL_AV = 32, 40, 48, 56, 58


def _to_kernel_cols(w):
    return jnp.concatenate([w[..., :4096], w[..., 5632:7680], w[..., 4096:5120], w[..., 5120:5632]], axis=-1)


def _from_kernel_cols(w):
    return jnp.concatenate([w[..., :4096], w[..., 6144:7168], w[..., 7168:7680], w[..., 4096:6144]], axis=-1)

ADAM_LR = 0.001
ADAM_B1 = 0.9
ADAM_B2 = 0.999
ADAM_EPS = 1e-08
ADAM_WD = 0.01
ADAM_STEP = 10

VMEM_LIMIT_V7X = 56 * 1024 * 1024
MESH = pl.DeviceIdType.MESH


def _cp(sem, vmem=VMEM_LIMIT_V7X):
    return pltpu.CompilerParams(dimension_semantics=sem, vmem_limit_bytes=vmem)


def _sigmoid(x):
    return 1.0 / (1.0 + jnp.exp(-x))


def _dot(a, b):
    return jnp.dot(a, b, preferred_element_type=F32)


def _dot_nt(a, b):
    return lax.dot_general(a, b, (((1,), (1,)), ((), ())), preferred_element_type=F32)


def _dot_tn(a, b):
    return lax.dot_general(a, b, (((0,), (0,)), ((), ())), preferred_element_type=F32)


def _rowsum8(v):
    r, n = v.shape
    return jnp.sum(v.reshape(r // 8, 8, n), axis=0)


_ANY = pl.BlockSpec(memory_space=pl.ANY)
N_CHIPS = 4
PEER_CHIPS = N_CHIPS - 1


def _place():
    x, y, c = lax.axis_index("x"), lax.axis_index("y"), lax.axis_index("c")
    return x, y, c, [(1 - x, y), (x, 1 - y), (1 - x, 1 - y)]


def _exchange_scratch(n):
    return [pltpu.SemaphoreType.DMA((PEER_CHIPS * n,)), pltpu.SemaphoreType.DMA((PEER_CHIPS * n,)), pltpu.SemaphoreType.DMA((n,))]


def _exchange(ins, outs, sems, scatter):
    send_sems, recv_sems, local_sems = sems
    x, y, c, chips = _place()
    me = 2 * x + y
    n = len(ins)

    def remote(a, j, arriving):
        px, py = chips[j]
        them = 2 * px + py
        src = ins[a].at[them] if scatter else ins[a]
        return pltpu.make_async_remote_copy(
            src_ref=src, dst_ref=outs[a].at[them if arriving else me], send_sem=send_sems.at[a * PEER_CHIPS + j],
            recv_sem=recv_sems.at[a * PEER_CHIPS + j], device_id=(px, py, c), device_id_type=MESH)

    def local(a):
        return pltpu.make_async_copy(ins[a].at[me] if scatter else ins[a], outs[a].at[me], local_sems.at[a])

    def start():
        for a in range(n):
            local(a).start()
            for j in range(PEER_CHIPS):
                remote(a, j, False).start()

    def wait():
        for a in range(n):
            for j in range(PEER_CHIPS):
                remote(a, j, True).wait()
            local(a).wait()

    return start, wait


def _exchange_out_shapes(arrays, scatter):
    return [jax.ShapeDtypeStruct(a.shape if scatter else (N_CHIPS,) + a.shape, a.dtype) for a in arrays]


def exchange_between_chips(arrays, *, scatter, name):
    n = len(arrays)

    def body(*refs):
        start, wait = _exchange(refs[:n], refs[n:2 * n], refs[2 * n:], scatter)
        start()
        wait()

    return pl.pallas_call(
        body, name=name, out_shape=_exchange_out_shapes(arrays, scatter), in_specs=[_ANY] * n, out_specs=[_ANY] * n,
        scratch_shapes=_exchange_scratch(n),
    )(*arrays)


def ln_matmul(x, g, w, *, tm, tn, name, swiglu=False, gather=()):
    T, Dm = x.shape
    N = w.shape[1]
    if swiglu:
        assert tn == N
    half = N // 2
    ng = len(gather)
    n_out = 3 if swiglu else 2
    ni, nj = T // tm, N // tn

    def body(x_ref, g_ref, w_ref, *rest):
        gather_in, rest = rest[:ng], rest[ng:]
        h_ref, o_ref = rest[0], rest[1]
        i, j = pl.program_id(0), pl.program_id(1)
        if ng:
            start, wait = _exchange(gather_in, rest[n_out:n_out + ng], rest[n_out + ng:], False)
            pl.when((i == 0) & (j == 0))(start)

        @pl.when(j == 0)
        def _():
            xf = x_ref[...]
            r = lax.rsqrt(jnp.mean(xf * xf, axis=-1, keepdims=True) + EPS)
            h_ref[...] = (xf * r * g_ref[...]).astype(BF16)

        acc = _dot(h_ref[...], w_ref[...])
        o_ref[...] = acc.astype(BF16)
        if swiglu:
            gt, up = acc[:, :half], acc[:, half:]
            rest[2][...] = (gt * _sigmoid(gt) * up).astype(BF16)
        if ng:
            pl.when((i == ni - 1) & (j == nj - 1))(wait)

    out_shape = [jax.ShapeDtypeStruct((T, Dm), BF16), jax.ShapeDtypeStruct((T, N), BF16)]
    out_specs = [pl.BlockSpec((tm, Dm), lambda i, j: (i, 0)), pl.BlockSpec((tm, tn), lambda i, j: (i, j))]
    if swiglu:
        out_shape.append(jax.ShapeDtypeStruct((T, half), BF16))
        out_specs.append(pl.BlockSpec((tm, half), lambda i, j: (i, 0)))
    return pl.pallas_call(
        body, name=name, grid=(ni, nj),
        in_specs=[pl.BlockSpec((tm, Dm), lambda i, j: (i, 0)), pl.BlockSpec((1, Dm), lambda i, j: (0, 0)),
                  pl.BlockSpec((Dm, tn), lambda i, j: (0, j))] + [_ANY] * ng,
        out_specs=out_specs + [_ANY] * ng, out_shape=out_shape + _exchange_out_shapes(gather, False),
        scratch_shapes=_exchange_scratch(ng) if ng else [],
        compiler_params=_cp(("arbitrary", "arbitrary") if ng else ("parallel", "arbitrary")),
    )(x, g, w, *gather)


def matmul_residual(x, a, w, *, tm, name):
    T, N = x.shape
    K = a.shape[1]

    def body(x_ref, a_ref, w_ref, o_ref):
        o_ref[...] = x_ref[...] + _dot(a_ref[...], w_ref[...])

    return pl.pallas_call(
        body, name=name, grid=(T // tm,),
        in_specs=[pl.BlockSpec((tm, N), lambda i: (i, 0)), pl.BlockSpec((tm, K), lambda i: (i, 0)),
                  pl.BlockSpec((K, N), lambda i: (0, 0))],
        out_specs=pl.BlockSpec((tm, N), lambda i: (i, 0)), out_shape=jax.ShapeDtypeStruct((T, N), F32),
        compiler_params=_cp(("parallel",)),
    )(x, a, w)


def matmul_nt(a, w, *, tm, name):
    T, K = a.shape
    N = w.shape[0]

    def body(a_ref, w_ref, o_ref):
        o_ref[...] = _dot_nt(a_ref[...], w_ref[...]).astype(BF16)

    return pl.pallas_call(
        body, name=name, grid=(T // tm,),
        in_specs=[pl.BlockSpec((tm, K), lambda i: (i, 0)), pl.BlockSpec((N, K), lambda i: (0, 0))],
        out_specs=pl.BlockSpec((tm, N), lambda i: (i, 0)), out_shape=jax.ShapeDtypeStruct((T, N), BF16),
        compiler_params=_cp(("parallel",)),
    )(a, w)


def matmul_tn(a, g, *, tka, tn, tk, name, scatter=()):
    T, Ka = a.shape
    N = g.shape[1]
    ni, nj, nk = Ka // tka, N // tn, T // tk
    ns = len(scatter)

    def body(a_ref, g_ref, *rest):
        scatter_in, rest = rest[:ns], rest[ns:]
        o_ref, acc_ref = rest[0], rest[1 + ns]
        i, j, k = pl.program_id(0), pl.program_id(1), pl.program_id(2)
        if ns:
            start, wait = _exchange(scatter_in, rest[1:1 + ns], rest[2 + ns:], True)
            pl.when((i == 0) & (j == 0) & (k == 0))(start)

        @pl.when(k == 0)
        def _():
            acc_ref[...] = jnp.zeros_like(acc_ref)

        acc_ref[...] += _dot_tn(a_ref[...].astype(BF16), g_ref[...].astype(BF16))

        @pl.when(k == nk - 1)
        def _():
            o_ref[...] = acc_ref[...].astype(BF16)

        if ns:
            pl.when((i == ni - 1) & (j == nj - 1) & (k == nk - 1))(wait)

    res = pl.pallas_call(
        body, name=name, grid=(ni, nj, nk),
        in_specs=[pl.BlockSpec((tk, tka), lambda i, j, k: (k, i)), pl.BlockSpec((tk, tn), lambda i, j, k: (k, j))] + [_ANY] * ns,
        out_specs=[pl.BlockSpec((tka, tn), lambda i, j, k: (i, j))] + [_ANY] * ns,
        out_shape=[jax.ShapeDtypeStruct((Ka, N), BF16)] + _exchange_out_shapes(scatter, True),
        scratch_shapes=[pltpu.VMEM((tka, tn), F32)] + (_exchange_scratch(ns) if ns else []),
        compiler_params=_cp(("arbitrary",) * 3 if ns else ("parallel", "parallel", "arbitrary")),
    )(a, g, *scatter)
    return res if ns else res[0]


def matmul_nt_rmsbwd(a, w, x, g, dres, *, tm, tk, name, scatter=()):
    T, K = a.shape
    Dm = w.shape[0]
    nk = K // tk
    ni = T // tm
    ns = len(scatter)

    def body(a_ref, w_ref, x_ref, g_ref, dres_ref, *rest):
        scatter_in, rest = rest[:ns], rest[ns:]
        dx_ref, dg_ref = rest[0], rest[1]
        acc_ref = rest[2 + ns]
        i, k = pl.program_id(0), pl.program_id(1)
        if ns:
            start, wait = _exchange(scatter_in, rest[2:2 + ns], rest[3 + ns:], True)
            pl.when((i == 0) & (k == 0))(start)

        @pl.when(k == 0)
        def _():
            acc_ref[...] = jnp.zeros_like(acc_ref)

        @pl.when((i == 0) & (k == 0))
        def _():
            dg_ref[...] = jnp.zeros_like(dg_ref)

        acc_ref[...] += _dot_nt(a_ref[...], w_ref[...])

        @pl.when(k == nk - 1)
        def _():
            dh = acc_ref[...]
            xf = x_ref[...]
            r = lax.rsqrt(jnp.mean(xf * xf, axis=-1, keepdims=True) + EPS)
            xhat = xf * r
            dg_ref[...] += _rowsum8(dh * xhat)
            dxhat = dh * g_ref[...]
            dx_ref[...] = dres_ref[...] + r * (dxhat - xhat * jnp.mean(dxhat * xhat, axis=-1, keepdims=True))

        if ns:
            pl.when((i == ni - 1) & (k == nk - 1))(wait)

    return pl.pallas_call(
        body, name=name, grid=(ni, nk),
        in_specs=[pl.BlockSpec((tm, tk), lambda i, k: (i, k)), pl.BlockSpec((Dm, tk), lambda i, k: (0, k)),
                  pl.BlockSpec((tm, Dm), lambda i, k: (i, 0)), pl.BlockSpec((1, Dm), lambda i, k: (0, 0)),
                  pl.BlockSpec((tm, Dm), lambda i, k: (i, 0))] + [_ANY] * ns,
        out_specs=[pl.BlockSpec((tm, Dm), lambda i, k: (i, 0)), pl.BlockSpec((8, Dm), lambda i, k: (0, 0))] + [_ANY] * ns,
        out_shape=[jax.ShapeDtypeStruct((T, Dm), F32), jax.ShapeDtypeStruct((8, Dm), F32)] + _exchange_out_shapes(scatter, True),
        scratch_shapes=[pltpu.VMEM((tm, Dm), F32)] + (_exchange_scratch(ns) if ns else []),
        compiler_params=_cp(("arbitrary", "arbitrary")),
    )(a, w, x, g, dres, *scatter)


def ffn_down_bwd(dx, w_down, gu, *, tm, name):
    T, Dm = dx.shape
    H = w_down.shape[0]

    def body(dx_ref, w_ref, gu_ref, o_ref):
        dact = _dot_nt(dx_ref[...].astype(BF16), w_ref[...])
        gt = gu_ref[:, :H].astype(F32)
        up = gu_ref[:, H:].astype(F32)
        s = _sigmoid(gt)
        o_ref[:, :H] = (dact * up * (s * (1.0 + gt * (1.0 - s)))).astype(BF16)
        o_ref[:, H:] = (dact * gt * s).astype(BF16)

    return pl.pallas_call(
        body, name=name, grid=(T // tm,),
        in_specs=[pl.BlockSpec((tm, Dm), lambda i: (i, 0)), pl.BlockSpec((H, Dm), lambda i: (0, 0)),
                  pl.BlockSpec((tm, 2 * H), lambda i: (i, 0))],
        out_specs=pl.BlockSpec((tm, 2 * H), lambda i: (i, 0)), out_shape=jax.ShapeDtypeStruct((T, 2 * H), BF16),
        compiler_params=_cp(("parallel",)),
    )(dx, w_down, gu)


def _gate_specs(tm):
    return [pl.BlockSpec((tm, D_MODEL), lambda i, c=c: (i, c)) for c in (COL_GA // 8, COL_GB // 8)]


def merge_fwd(og, att, proj, w_pa, w_pb, *, tm, name):
    T, Dm = og.shape

    def body(og_ref, att_ref, ga_ref, gb_ref, wa_ref, wb_ref, ya_ref, yb_ref, mix_ref):
        ya = _dot(og_ref[...], wa_ref[...])
        yb = _dot(att_ref[...], wb_ref[...])
        ya_ref[...] = ya.astype(BF16)
        yb_ref[...] = yb.astype(BF16)
        mix_ref[...] = (_sigmoid(ga_ref[...].astype(F32)) * ya + _sigmoid(gb_ref[...].astype(F32)) * yb).astype(BF16)

    row = pl.BlockSpec((tm, Dm), lambda i: (i, 0))
    wsp = pl.BlockSpec((Dm, Dm), lambda i: (0, 0))
    return pl.pallas_call(
        body, name=name, grid=(T // tm,),
        in_specs=[row, row] + _gate_specs(tm) + [wsp, wsp],
        out_specs=[row, row, row], out_shape=[jax.ShapeDtypeStruct((T, Dm), BF16)] * 3,
        compiler_params=_cp(("parallel",)),
    )(og, att, proj, proj, w_pa, w_pb)


def merge_bwd(dx, w_o, proj, ya, yb, *, tm, name):
    T, Dm = dx.shape

    def body(dx_ref, w_ref, ga_ref, gb_ref, ya_ref, yb_ref, dya_ref, dyb_ref, dgate_ref):
        dmix = _dot_nt(dx_ref[...].astype(BF16), w_ref[...])
        sa = _sigmoid(ga_ref[...].astype(F32))
        sb = _sigmoid(gb_ref[...].astype(F32))
        dya_ref[...] = (dmix * sa).astype(BF16)
        dyb_ref[...] = (dmix * sb).astype(BF16)
        dgate_ref[:, :Dm] = (dmix * ya_ref[...].astype(F32) * sa * (1.0 - sa)).astype(BF16)
        dgate_ref[:, Dm:] = (dmix * yb_ref[...].astype(F32) * sb * (1.0 - sb)).astype(BF16)

    row = pl.BlockSpec((tm, Dm), lambda i: (i, 0))
    return pl.pallas_call(
        body, name=name, grid=(T // tm,),
        in_specs=[row, pl.BlockSpec((Dm, Dm), lambda i: (0, 0))] + _gate_specs(tm) + [row, row],
        out_specs=[row, row, pl.BlockSpec((tm, 2 * Dm), lambda i: (i, COL_GA // 16))],
        out_shape=[jax.ShapeDtypeStruct((T, Dm), BF16)] * 2 + [jax.ShapeDtypeStruct((T, IN_COLS), BF16)],
        compiler_params=_cp(("parallel",)),
    )(dx, w_o, proj, proj, ya, yb)


def loss_head(x, g, target, *, tm, name):
    T, Dm = x.shape

    def body(x_ref, g_ref, t_ref, dx_ref, l_ref, dg_ref):
        @pl.when(pl.program_id(0) == 0)
        def _():
            l_ref[...] = jnp.zeros_like(l_ref)
            dg_ref[...] = jnp.zeros_like(dg_ref)

        xf = x_ref[...]
        gv = g_ref[...]
        r = lax.rsqrt(jnp.mean(xf * xf, axis=-1, keepdims=True) + EPS)
        xhat = xf * r
        err = xhat * gv - t_ref[...]
        l_ref[...] += _rowsum8(err * err) * (0.5 / Dm)
        dy = err * (1.0 / Dm)
        dg_ref[...] += _rowsum8(dy * xhat)
        dxhat = dy * gv
        dx_ref[...] = r * (dxhat - xhat * jnp.mean(dxhat * xhat, axis=-1, keepdims=True))

    row = pl.BlockSpec((tm, Dm), lambda i: (i, 0))
    acc = pl.BlockSpec((8, Dm), lambda i: (0, 0))
    return pl.pallas_call(
        body, name=name, grid=(T // tm,),
        in_specs=[row, pl.BlockSpec((1, Dm), lambda i: (0, 0)), row],
        out_specs=[row, acc, acc],
        out_shape=[jax.ShapeDtypeStruct((T, Dm), F32), jax.ShapeDtypeStruct((8, Dm), F32), jax.ShapeDtypeStruct((8, Dm), F32)],
        compiler_params=_cp(("arbitrary",)),
    )(x, g, target)


_LEVELS = (32, 16, 8)
_DIAG = 8
_SUBLANES = 8
_HI = lax.Precision.HIGHEST


def _chunk_consts():
    t = lax.broadcasted_iota(jnp.int32, (CHUNK, CHUNK), 0)
    s = lax.broadcasted_iota(jnp.int32, (CHUNK, CHUNK), 1)
    tri = (s <= t).astype(F32)
    tri_t = (s >= t).astype(F32)
    masks = []
    for m in _LEVELS:
        sh = int(np.log2(2 * m))
        masks.append(((t >> sh) == (s >> sh)) & ((t & (2 * m - 1)) >= m) & ((s & (2 * m - 1)) < m))
    return tri, tri_t, masks


def _exact_dot(sel, x):
    hi = x.astype(BF16)
    r1 = x - hi.astype(F32)
    mid = r1.astype(BF16)
    lo = (r1 - mid.astype(F32)).astype(BF16)
    return _dot(jnp.concatenate([sel, sel, sel], axis=1).astype(BF16), jnp.concatenate([hi, mid, lo], axis=0))


def _level_ref(b, m):
    if 2 * m >= _SUBLANES:
        pieces = [jnp.broadcast_to(b[p * 2 * m + m - 1:p * 2 * m + m, :], (2 * m, b.shape[1])) for p in range(CHUNK // (2 * m))]
        return pieces[0] if len(pieces) == 1 else jnp.concatenate(pieces, axis=0)
    groups = CHUNK // _SUBLANES
    b3 = b.reshape(groups, _SUBLANES, b.shape[1])
    row = lax.broadcasted_iota(jnp.int32, b3.shape, 1)
    ref = None
    for p in reversed(range(_SUBLANES // (2 * m))):
        src = jnp.broadcast_to(b3[:, p * 2 * m + m - 1:p * 2 * m + m, :], b3.shape)
        ref = src if ref is None else jnp.where(row < (p + 1) * 2 * m, src, ref)
    return ref.reshape(b.shape)


def _intra_off(q, kk, b, masks):
    a = jnp.zeros((CHUNK, CHUNK), F32)
    keep = []
    for m, mask in zip(_LEVELS, masks):
        e = jnp.exp2(-jnp.abs(b - _level_ref(b, m)))
        qt = (q * e).astype(BF16)
        kt = (kk * e).astype(BF16)
        a = a + jnp.where(mask, _dot_nt(qt, kt), 0.0)
        keep.append((mask, e, qt, kt))
    return a, keep


def _roll_rows(x, d):
    g = x.shape[0] // _DIAG
    return pltpu.roll(x.reshape(g, _DIAG, x.shape[1]), d % _DIAG, 1).reshape(x.shape)


def _diag_fwd(q, kk, v, b):
    lower = lax.broadcasted_iota(jnp.int32, q.shape, 0) & (_DIAG - 1)
    out = jnp.sum(q * kk, axis=-1, keepdims=True) * v
    for d in range(1, _DIAG):
        dec = jnp.exp2(jnp.where(lower >= d, b - _roll_rows(b, d), _NEG))
        out = out + jnp.sum(q * dec * _roll_rows(kk, d), axis=-1, keepdims=True) * _roll_rows(v, d)
    return out


def _diag_bwd(q, kk, v, b, do):
    lower = lax.broadcasted_iota(jnp.int32, q.shape, 0) & (_DIAG - 1)
    a = jnp.sum(q * kk, axis=-1, keepdims=True)
    da = jnp.sum(do * v, axis=-1, keepdims=True)
    dq, dk, dv = da * kk, da * q, a * do
    for d in range(1, _DIAG):
        kr = _roll_rows(kk, d)
        dec = jnp.exp2(jnp.where(lower >= d, b - _roll_rows(b, d), _NEG))
        qd = q * dec
        a = jnp.sum(qd * kr, axis=-1, keepdims=True)
        da = jnp.sum(do * _roll_rows(v, d), axis=-1, keepdims=True)
        dq = dq + da * dec * kr
        dk = dk + _roll_rows(da * qd, -d)
        dv = dv + _roll_rows(a * do, -d)
    return dq, dk, dv


def _hgrn_in_specs(ct, hp, order):
    def spec(col):
        return pl.BlockSpec((ct, hp * HG_DK), lambda h, i, col=col: (order(i), col // hp + h))
    vec = pl.BlockSpec((1, hp * HG_DK), lambda h, i: (0, h))
    return [spec(COL_HQ), spec(COL_HF), spec(COL_HI), spec(COL_HG), vec, vec]


def hgrn_fwd(proj, lb, gn, *, ct, hp, name):
    T = proj.shape[0]
    nc = ct // CHUNK

    def body(hq_ref, hf_ref, hi_ref, hg_ref, lb_ref, gn_ref, og_ref, o_ref, st_ref, s_ref):
        @pl.when(pl.program_id(1) == 0)
        def _():
            s_ref[...] = jnp.zeros_like(s_ref)

        tri, _, masks = _chunk_consts()

        def chunk(c, carry):
            sl = pl.ds(pl.multiple_of(c * CHUNK, CHUNK), CHUNK)
            lb_all = lb_ref[...]
            f_all = lb_all + (1.0 - lb_all) * _sigmoid(hf_ref[sl, :].astype(F32))
            b_all = _exact_dot(tri, jnp.log2(jnp.maximum(f_all, MIN_F)))
            for p in range(hp):
                cols = slice(p * HG_DK, (p + 1) * HG_DK)
                hq = hq_ref[sl, cols].astype(F32)
                v = hi_ref[sl, cols].astype(F32)
                gate = hg_ref[sl, cols].astype(F32)
                q = hq * _sigmoid(hq)
                f = f_all[:, cols]
                kk = 1.0 - f
                b = b_all[:, cols]
                st = s_ref[p]
                st_ref[p, c] = st
                a, _ = _intra_off(q, kk, b, masks)
                vb = v.astype(BF16)
                o = (_dot_nt((q * jnp.exp2(b)).astype(BF16), st.astype(BF16)) + _dot(a.astype(BF16), vb)
                     + _diag_fwd(q, kk, v, b))
                bl = b[CHUNK - 1:CHUNK, :]
                s_ref[p] = st * jnp.exp2(bl) + _dot_tn(vb, (kk * jnp.exp2(bl - b)).astype(BF16))
                o_ref[sl, cols] = o
                r = lax.rsqrt(jnp.mean(o * o, axis=-1, keepdims=True) + EPS)
                og_ref[sl, cols] = (o * r * gn_ref[:, cols] * (gate * _sigmoid(gate))).astype(BF16)
            return carry

        lax.fori_loop(0, nc, chunk, 0)

    blk = pl.BlockSpec((ct, hp * HG_DK), lambda h, i: (i, h))
    return pl.pallas_call(
        body, name=name, grid=(HG_HEADS // hp, T // ct),
        in_specs=_hgrn_in_specs(ct, hp, lambda i: i),
        out_specs=[blk, blk, pl.BlockSpec((hp, nc, HG_DK, HG_DK), lambda h, i: (h, i, 0, 0))],
        out_shape=[jax.ShapeDtypeStruct((T, HG_WIDTH), BF16), jax.ShapeDtypeStruct((T, HG_WIDTH), F32),
                   jax.ShapeDtypeStruct((HG_HEADS, T // CHUNK, HG_DK, HG_DK), F32)],
        scratch_shapes=[pltpu.VMEM((hp, HG_DK, HG_DK), F32)],
        compiler_params=_cp(("parallel", "arbitrary")),
    )(proj, proj, proj, proj, lb, gn)


def hgrn_bwd(proj, lb, gn, o_raw, states, dog, dproj, *, ct, name):
    T = proj.shape[0]
    nc = ct // CHUNK
    nblk = T // ct
    hp = HG_HEADS

    def body(hq_ref, hf_ref, hi_ref, hg_ref, lb_ref, gn_ref, o_ref, st_ref, dog_ref, _dproj_in,
             dp_ref, dlb_ref, dgn_ref, ds_ref):
        dhq_ref, dhf_ref, dhi_ref, dhg_ref = (dp_ref.at[:, s * HG_WIDTH:(s + 1) * HG_WIDTH] for s in range(4))
        @pl.when(pl.program_id(1) == 0)
        def _():
            ds_ref[...] = jnp.zeros_like(ds_ref)
            dlb_ref[...] = jnp.zeros_like(dlb_ref)
            dgn_ref[...] = jnp.zeros_like(dgn_ref)

        tri, tri_t, masks = _chunk_consts()
        last_row = lax.broadcasted_iota(jnp.int32, (CHUNK, HG_DK), 0) == CHUNK - 1

        def chunk(ci, carry):
            c = nc - 1 - ci
            sl = pl.ds(pl.multiple_of(c * CHUNK, CHUNK), CHUNK)
            lb_all = lb_ref[...]
            sig_all = _sigmoid(hf_ref[sl, :].astype(F32))
            f_all = lb_all + (1.0 - lb_all) * sig_all
            b_all = _exact_dot(tri, jnp.log2(jnp.maximum(f_all, MIN_F)))
            dbs, dks = [], []
            for p in range(hp):
                cols = slice(p * HG_DK, (p + 1) * HG_DK)
                gn_v = gn_ref[:, cols]
                hq = hq_ref[sl, cols].astype(F32)
                v = hi_ref[sl, cols].astype(F32)
                gate = hg_ref[sl, cols].astype(F32)
                sq = _sigmoid(hq)
                q = hq * sq
                f = f_all[:, cols]
                kk = 1.0 - f
                b = b_all[:, cols]
                st = st_ref[p, c]
                dst = ds_ref[p]
                o = o_ref[sl, cols]
                r = lax.rsqrt(jnp.mean(o * o, axis=-1, keepdims=True) + EPS)
                xhat = o * r
                sg = _sigmoid(gate)
                dog_v = dog_ref[sl, cols].astype(F32)
                dy = dog_v * (gate * sg)
                dhg_ref[sl, cols] = (dog_v * xhat * gn_v * (sg * (1.0 + gate * (1.0 - sg)))).astype(BF16)
                dgn_ref[:, cols] += _rowsum8(dy * xhat)
                dxh = dy * gn_v
                do = r * (dxh - xhat * jnp.mean(dxh * xhat, axis=-1, keepdims=True))
                dob = do.astype(BF16)
                vb = v.astype(BF16)
                stb = st.astype(BF16)
                dstb = dst.astype(BF16)
                eb = jnp.exp2(b)
                bl = b[CHUNK - 1:CHUNK, :]
                ebl = jnp.exp2(bl)
                edec = jnp.exp2(bl - b)
                qe = q * eb
                kdec = kk * edec
                a, keep = _intra_off(q, kk, b, masks)
                da = _dot_nt(dob, vb)
                dkdec = _dot(vb, dstb)
                dq = _dot(dob, stb) * eb
                dk = dkdec * edec
                dv = _dot_tn(a.astype(BF16), dob) + _dot_nt(kdec.astype(BF16), dstb)
                for mask, e, qt, kt in keep:
                    dsm = jnp.where(mask, da, 0.0).astype(BF16)
                    dq = dq + _dot(dsm, kt) * e
                    dk = dk + _dot_tn(dsm, qt) * e
                ddq, ddk, ddv = _diag_bwd(q, kk, v, b, do)
                dq = dq + ddq
                dk = dk + ddk
                dhi_ref[sl, cols] = (dv + ddv).astype(BF16)
                dbl = jnp.sum(dkdec * kdec, axis=0, keepdims=True) + ebl * jnp.sum(st * dst, axis=0, keepdims=True)
                db = q * dq - kk * dk
                dbs.append(jnp.where(last_row, db + dbl, db))
                dks.append(dk)
                ds_ref[p] = _dot_tn(dob, qe.astype(BF16)) + dst * ebl
                dhq_ref[sl, cols] = (dq * (sq * (1.0 + hq * (1.0 - sq)))).astype(BF16)
            dlogf = _exact_dot(tri_t, jnp.concatenate(dbs, axis=1))
            df = jnp.where(f_all > MIN_F, dlogf / f_all, 0.0) - jnp.concatenate(dks, axis=1)
            dhf_ref[sl, :] = (df * (1.0 - lb_all) * sig_all * (1.0 - sig_all)).astype(BF16)
            dlb_ref[...] += _rowsum8(df * (1.0 - sig_all))
            return carry

        lax.fori_loop(0, nc, chunk, 0)

    rev = lambda i: nblk - 1 - i
    blk = pl.BlockSpec((ct, hp * HG_DK), lambda h, i: (rev(i), h))
    acc = pl.BlockSpec((8, hp * HG_DK), lambda h, i: (0, h))
    return pl.pallas_call(
        body, name=name, grid=(HG_HEADS // hp, nblk),
        in_specs=_hgrn_in_specs(ct, hp, rev) + [blk, pl.BlockSpec((hp, nc, HG_DK, HG_DK), lambda h, i: (h, rev(i), 0, 0)), blk, _ANY],
        out_specs=[pl.BlockSpec((ct, 4 * HG_WIDTH), lambda h, i: (rev(i), 0)), acc, acc],
        out_shape=[jax.ShapeDtypeStruct(dproj.shape, dproj.dtype)] + [jax.ShapeDtypeStruct((8, HG_WIDTH), F32)] * 2,
        scratch_shapes=[pltpu.VMEM((hp, HG_DK, HG_DK), F32)],
        input_output_aliases={9: 0},
        compiler_params=_cp(("parallel", "arbitrary")),
    )(proj, proj, proj, proj, lb, gn, o_raw, states, dog, dproj)


_GW = ATT_GROUP * HEAD_DIM
_NEG = -1e30


def rope_tables(T):
    half = ROPE_DIM // 2
    inv = ROPE_THETA ** (-jnp.arange(half, dtype=F32) * 2.0 / ROPE_DIM)
    ang = jnp.arange(T, dtype=F32)[:, None] * inv[None, :]
    pad1 = jnp.ones((T, HEAD_DIM - ROPE_DIM), F32)
    cos = jnp.concatenate([jnp.cos(ang), jnp.cos(ang), pad1], axis=1)
    sin = jnp.concatenate([jnp.sin(ang), jnp.sin(ang), 0.0 * pad1], axis=1)
    p = np.zeros((_GW, _GW), np.float32)
    for base in range(0, _GW, HEAD_DIM):
        for i in range(half):
            p[base + i + half, base + i] = -1.0
            p[base + i, base + i + half] = 1.0
    reps = _GW // HEAD_DIM
    return jnp.tile(cos, (1, reps)), jnp.tile(sin, (1, reps)), jnp.asarray(p, BF16)


def _rope(x, cos, sin, pm):
    return x.astype(F32) * cos + _dot(x, pm) * sin


def _unrope(dx, cos, sin, pm):
    return dx * cos + _dot_nt((dx * sin).astype(BF16), pm)


_GQ = ATT_GROUP * WINDOW


def _swa_mask(n, queries_on_rows):
    shape = (_GQ, 2 * WINDOW) if queries_on_rows else (2 * WINDOW, _GQ)
    r = lax.broadcasted_iota(jnp.int32, shape, 0 if queries_on_rows else 1) & (WINDOW - 1)
    c = lax.broadcasted_iota(jnp.int32, shape, 1 if queries_on_rows else 0)
    delta = r + WINDOW - c
    return (delta >= 0) & (delta < WINDOW) & ((c >= WINDOW) | (n > 0))


def _sink_vector(sink_ref, g, queries_on_rows):
    shape = (_GQ, 1) if queries_on_rows else (1, _GQ)
    i = lax.broadcasted_iota(jnp.int32, shape, 0 if queries_on_rows else 1)
    out = jnp.full(shape, sink_ref[g * ATT_GROUP + ATT_GROUP - 1], F32)
    for hh in range(ATT_GROUP - 2, -1, -1):
        out = jnp.where(i < (hh + 1) * WINDOW, sink_ref[g * ATT_GROUP + hh], out)
    return out


def _head(x, h):
    return x[:, h * HEAD_DIM:(h + 1) * HEAD_DIM]


def _stack_heads(x):
    return jnp.concatenate([_head(x, hh) for hh in range(ATT_GROUP)], axis=0)


def _unstack_heads(x):
    return jnp.concatenate([x[hh * WINDOW:(hh + 1) * WINDOW] for hh in range(ATT_GROUP)], axis=1)


def _swa_specs(cur):
    prev = lambda n: jnp.maximum(n - 1, 0)
    kv = lambda col, f: pl.BlockSpec((WINDOW, _GW), lambda n: (f(n), col))
    tab = lambda f: pl.BlockSpec((WINDOW, _GW), lambda n: (f(n), 0))
    return [pl.BlockSpec((WINDOW, ATT_WIDTH), lambda n: (cur(n), COL_AQ // 8)),
            kv(COL_AK // 2, prev), kv(COL_AK // 2, cur), kv(COL_AV // 2, prev), kv(COL_AV // 2, cur),
            tab(prev), tab(cur), tab(prev), tab(cur),
            pl.BlockSpec((_GW, _GW), lambda n: (0, 0)), pl.BlockSpec(memory_space=pltpu.SMEM)]


def swa_fwd(proj, cos, sin, pm, sinks, *, name):
    T = proj.shape[0]
    nb = T // WINDOW
    scale = HEAD_DIM ** -0.5

    def body(q_ref, kp_ref, kc_ref, vp_ref, vc_ref, cp_ref, cc_ref, sp_ref, sc_ref, pm_ref, sink_ref, o_ref):
        n = pl.program_id(0)
        pm_v = pm_ref[...]
        mask = _swa_mask(n, True)
        k_cat = jnp.concatenate([_rope(kp_ref[...], cp_ref[...], sp_ref[...], pm_v),
                                 _rope(kc_ref[...], cc_ref[...], sc_ref[...], pm_v)], axis=0).astype(BF16)
        v_cat = jnp.concatenate([vp_ref[...], vc_ref[...]], axis=0)
        for g in range(ATT_KV_HEADS):
            qs = _stack_heads(_rope(q_ref[:, g * _GW:(g + 1) * _GW], cc_ref[...], sc_ref[...], pm_v).astype(BF16))
            sk = _sink_vector(sink_ref, g, True)
            s = jnp.where(mask, _dot_nt(qs, _head(k_cat, g)) * scale, _NEG)
            m = jnp.maximum(jnp.max(s, axis=-1, keepdims=True), sk)
            p = jnp.exp(s - m)
            l = jnp.sum(p, axis=-1, keepdims=True) + jnp.exp(sk - m)
            o = _dot(p.astype(BF16), _head(v_cat, g)) / l
            o_ref[:, g * _GW:(g + 1) * _GW] = _unstack_heads(o).astype(BF16)

    return pl.pallas_call(
        body, name=name, grid=(nb,), in_specs=_swa_specs(lambda n: n),
        out_specs=pl.BlockSpec((WINDOW, ATT_WIDTH), lambda n: (n, 0)),
        out_shape=jax.ShapeDtypeStruct((T, ATT_WIDTH), BF16), compiler_params=_cp(("parallel",)),
    )(proj, proj, proj, proj, proj, cos, cos, sin, sin, pm, sinks)


def swa_bwd(proj, cos, sin, pm, sinks, att, datt, dproj, *, name):
    T = proj.shape[0]
    nb = T // WINDOW
    scale = HEAD_DIM ** -0.5
    clamp = lambda n: jnp.minimum(n, nb - 1)

    def body(q_ref, kp_ref, kc_ref, vp_ref, vc_ref, cp_ref, cc_ref, sp_ref, sc_ref, pm_ref, sink_ref, att_ref, datt_ref,
             _dproj_in, dp_ref, dsink_ref, cq_ref, ck_ref, cv_ref):
        n = pl.program_id(0)
        pm_v = pm_ref[...]

        @pl.when(n == 0)
        def _():
            dsink_ref[...] = jnp.zeros_like(dsink_ref)
            cq_ref[...] = jnp.zeros_like(cq_ref)
            ck_ref[...] = jnp.zeros_like(ck_ref)
            cv_ref[...] = jnp.zeros_like(cv_ref)

        def write_prev(dk_prev, dv_prev):
            dp_ref[:, :ATT_WIDTH] = cq_ref[...]
            dk = _unrope(ck_ref[...] + dk_prev, cp_ref[...], sp_ref[...], pm_v)
            dp_ref[:, ATT_WIDTH:ATT_WIDTH + _GW] = dk.astype(BF16)
            dp_ref[:, ATT_WIDTH + _GW:] = (cv_ref[...] + dv_prev).astype(BF16)

        @pl.when(n < nb)
        def _():
            mask = _swa_mask(n, False)
            ones = jnp.ones((8, HEAD_DIM), F32)
            k_cat = jnp.concatenate([_rope(kp_ref[...], cp_ref[...], sp_ref[...], pm_v),
                                     _rope(kc_ref[...], cc_ref[...], sc_ref[...], pm_v)], axis=0).astype(BF16)
            v_cat = jnp.concatenate([vp_ref[...], vc_ref[...]], axis=0)
            dqs, dks, dvs = [], [], []
            for g in range(ATT_KV_HEADS):
                gc = slice(g * _GW, (g + 1) * _GW)
                qs = _stack_heads(_rope(q_ref[:, gc], cc_ref[...], sc_ref[...], pm_v).astype(BF16))
                dos = _stack_heads(datt_ref[:, gc])
                os_ = _stack_heads(att_ref[:, gc])
                kh, vh = _head(k_cat, g), _head(v_cat, g)
                sk = _sink_vector(sink_ref, g, False)
                s = jnp.where(mask, _dot_nt(kh, qs) * scale, _NEG)
                m = jnp.maximum(jnp.max(s, axis=0, keepdims=True), sk)
                e = jnp.exp(s - m)
                es = jnp.exp(sk - m)
                inv_l = 1.0 / (jnp.sum(e, axis=0, keepdims=True) + es)
                p = e * inv_l
                dsum = lax.dot_general(ones, dos.astype(F32) * os_.astype(F32), (((1,), (1,)), ((), ())),
                                       precision=_HI, preferred_element_type=F32)[0:1]
                dsink_ref[g, 0:1, :] += -(es * inv_l) * dsum
                ds = (p * (_dot_nt(vh, dos) - dsum) * scale).astype(BF16)
                dqs.append(_unrope(_unstack_heads(_dot_tn(ds, kh)), cc_ref[...], sc_ref[...], pm_v))
                dks.append(_dot(ds, qs))
                dvs.append(_dot(p.astype(BF16), dos))
            dk_all = jnp.concatenate(dks, axis=1)
            dv_all = jnp.concatenate(dvs, axis=1)

            @pl.when(n > 0)
            def _():
                write_prev(dk_all[:WINDOW], dv_all[:WINDOW])

            cq_ref[...] = jnp.concatenate(dqs, axis=1).astype(BF16)
            ck_ref[...] = dk_all[WINDOW:]
            cv_ref[...] = dv_all[WINDOW:]

        @pl.when(n == nb)
        def _():
            write_prev(jnp.zeros((WINDOW, _GW), F32), jnp.zeros((WINDOW, _GW), F32))

    row = pl.BlockSpec((WINDOW, ATT_WIDTH), lambda n: (clamp(n), 0))
    slab = ATT_WIDTH + 2 * _GW
    return pl.pallas_call(
        body, name=name, grid=(nb + 1,), in_specs=_swa_specs(clamp) + [row, row, _ANY],
        out_specs=[pl.BlockSpec((WINDOW, slab), lambda n: (jnp.maximum(n - 1, 0), COL_AQ * HG_DK // slab)),
                   pl.BlockSpec((ATT_KV_HEADS, 8, _GQ), lambda n: (0, 0, 0))],
        out_shape=[jax.ShapeDtypeStruct(dproj.shape, dproj.dtype), jax.ShapeDtypeStruct((ATT_KV_HEADS, 8, _GQ), F32)],
        scratch_shapes=[pltpu.VMEM((WINDOW, ATT_WIDTH), BF16), pltpu.VMEM((WINDOW, _GW), F32), pltpu.VMEM((WINDOW, _GW), F32)],
        input_output_aliases={13: 0},
        compiler_params=_cp(("arbitrary",)),
    )(proj, proj, proj, proj, proj, cos, cos, sin, sin, pm, sinks, att, datt, dproj)


def swap_with_sibling(arrays, *, name):
    n = len(arrays)

    def body(*refs):
        ins, outs = refs[:n], refs[n:2 * n]
        send_sems, recv_sems = refs[2 * n:]
        x, y, c, _ = _place()
        copies = [pltpu.make_async_remote_copy(src_ref=ins[a], dst_ref=outs[a], send_sem=send_sems.at[a],
                                               recv_sem=recv_sems.at[a], device_id=(x, y, 1 - c), device_id_type=MESH)
                  for a in range(n)]
        for cp in copies:
            cp.start()
        for cp in copies:
            cp.wait()

    return pl.pallas_call(
        body, name=name, out_shape=[jax.ShapeDtypeStruct(a.shape, a.dtype) for a in arrays],
        in_specs=[_ANY] * n, out_specs=[_ANY] * n,
        scratch_shapes=[pltpu.SemaphoreType.DMA((n,)), pltpu.SemaphoreType.DMA((n,))],
    )(*arrays)


N_DEV = 8
SMALL_ROWS = 24


def allreduce_small(v, *, name):
    def body(v_ref, o_ref, recv_ref, send_sems, recv_sems):
        x, y, c, _ = _place()
        me = 4 * x + 2 * y + c
        recv_ref[me] = v_ref[...]

        def copy(k, slot):
            peer = (x ^ (k >> 2), y ^ ((k >> 1) & 1), c ^ (k & 1))
            return pltpu.make_async_remote_copy(src_ref=v_ref, dst_ref=recv_ref.at[slot], send_sem=send_sems.at[k - 1],
                                                recv_sem=recv_sems.at[k - 1], device_id=peer, device_id_type=MESH)

        for k in range(1, N_DEV):
            copy(k, me).start()
        for k in range(1, N_DEV):
            copy(k, me ^ k).wait()
        acc = recv_ref[0]
        for d in range(1, N_DEV):
            acc = acc + recv_ref[d]
        o_ref[...] = acc

    vm = pl.BlockSpec(memory_space=pltpu.VMEM)
    return pl.pallas_call(
        body, name=name, out_shape=jax.ShapeDtypeStruct(v.shape, v.dtype), in_specs=[vm], out_specs=vm,
        scratch_shapes=[pltpu.VMEM((N_DEV,) + v.shape, v.dtype), pltpu.SemaphoreType.DMA((N_DEV - 1,)),
                        pltpu.SemaphoreType.DMA((N_DEV - 1,))],
    )(v)


def sum_parts(parts, *, tr, name):
    _, R, C = parts.shape

    def body(p_ref, o_ref):
        acc = p_ref[0].astype(F32)
        for q in range(1, N_CHIPS):
            acc = acc + p_ref[q].astype(F32)
        o_ref[...] = acc

    return pl.pallas_call(
        body, name=name, grid=(R // tr,), in_specs=[pl.BlockSpec((N_CHIPS, tr, C), lambda i: (0, i, 0))],
        out_specs=pl.BlockSpec((tr, C), lambda i: (i, 0)), out_shape=jax.ShapeDtypeStruct((R, C), F32),
        compiler_params=_cp(("parallel",)),
    )(parts)


def adamw(g_a, g_b, w, m, v, *, tr, name):
    R, C = w.shape
    c1 = 1.0 - ADAM_B1 ** ADAM_STEP
    c2 = 1.0 - ADAM_B2 ** ADAM_STEP
    ng = 1 if g_b is None else 2

    def body(*refs):
        w_ref, m_ref, v_ref, g_ref, d_ref, nm_ref, nv_ref = refs[ng:]
        g = refs[0][...] if ng == 1 else refs[0][...] + refs[1][...]
        nm = ADAM_B1 * m_ref[...] + (1.0 - ADAM_B1) * g
        nv = ADAM_B2 * v_ref[...] + (1.0 - ADAM_B2) * (g * g)
        g_ref[...] = g
        nm_ref[...] = nm
        nv_ref[...] = nv
        d_ref[...] = -ADAM_LR * ((nm / c1) / (jnp.sqrt(nv / c2) + ADAM_EPS) + ADAM_WD * w_ref[...])

    blk = pl.BlockSpec((tr, C), lambda i: (i, 0))
    gs = [g_a] if g_b is None else [g_a, g_b]
    return pl.pallas_call(
        body, name=name, grid=(R // tr,), in_specs=[blk] * (ng + 3), out_specs=[blk] * 4,
        out_shape=[jax.ShapeDtypeStruct((R, C), F32)] * 4, compiler_params=_cp(("parallel",)),
    )(*gs, w, m, v)


def _lb_bounds(lb_logits):
    p = jax.nn.softmax(lb_logits.astype(F32), axis=0)
    return jnp.cumsum(p, axis=0) - p[0:1]


def _pack_small(n1, lb, hgn, n2, fin, sinks, extra):
    row = jnp.concatenate([sinks.reshape(-1), jnp.reshape(extra, (1,)),
                           jnp.zeros((D_MODEL - sinks.size - 1,), F32)])
    return jnp.concatenate([n1, lb, hgn, n2, fin[None], row[None], jnp.zeros((SMALL_ROWS - 18, D_MODEL), F32)], axis=0)


def _unpack_small(buf):
    return (buf[0:4], buf[4:8], buf[8:12], buf[12:16], buf[16], buf[17, :DEPTH * ATT_Q_HEADS].reshape(DEPTH, ATT_Q_HEADS),
            buf[17, DEPTH * ATT_Q_HEADS])


def _cols_full(g):
    q, r, c = g.shape
    return jnp.transpose(g, (1, 0, 2)).reshape(r, q * c)


def _cols_split(w):
    r, qc = w.shape
    return jnp.transpose(w.reshape(r, N_CHIPS, qc // N_CHIPS), (1, 0, 2))


def kernel(x, norm1, w_in, lb_logits, hg_norm, attn_sinks, w_pa, w_pb, w_o, norm2, w_gate, w_up, w_down, final_norm, loss_target, m_norm1, m_w_in, m_lb_logits, m_hg_norm, m_attn_sinks, m_w_pa, m_w_pb, m_w_o, m_norm2, m_w_gate, m_w_up, m_w_down, m_final_norm, v_norm1, v_w_in, v_lb_logits, v_hg_norm, v_attn_sinks, v_w_pa, v_w_pb, v_w_o, v_norm2, v_w_gate, v_w_up, v_w_down, v_final_norm):
    T = x.shape[1]
    H = FFN_HIDDEN
    tm = min(512, T)
    tmx = min(1024, T)
    tkx = min(2048, T)
    big = dict(w_in=w_in, w_pa=w_pa, w_pb=w_pb, w_o=w_o, w_gate=w_gate, w_up=w_up, w_down=w_down)
    col_sharded = ("w_in", "w_gate", "w_up")

    shards = {k: w.astype(BF16) for k, w in big.items()}
    others = [k for k in big if k != "w_in"]

    def whole(k, g):
        return _cols_full(g) if k in col_sharded else g.reshape(-1, g.shape[-1])

    def layer_weights(w_in_blocks, other_blocks):
        full = {k: whole(k, g) for k, g in zip(others, other_blocks)}
        full["w_in"] = _to_kernel_cols(whole("w_in", w_in_blocks))
        full["w_gu"] = jnp.concatenate([full.pop("w_gate"), full.pop("w_up")], axis=-1)
        return full

    lb_all, lb_vjp = jax.vjp(_lb_bounds, lb_logits)
    cos, sin, pm = rope_tables(T)

    xs = x[0]
    saved, weights = [], []
    (w_in_blocks,) = exchange_between_chips([shards["w_in"][0]], scatter=False, name="gather_w_in_0")
    w_in_now = _to_kernel_cols(whole("w_in", w_in_blocks))
    other_blocks = None
    for l in range(DEPTH):
        carried = [shards["w_in"][l + 1]] if l + 1 < DEPTH else []
        if l == 0:
            carried = carried + [shards[k][0] for k in others]
        h, proj, *got = ln_matmul(xs, norm1[l:l + 1], w_in_now, tm=tmx, tn=1536, name=f"in_proj_{l}", gather=carried)
        if l == 0:
            other_blocks = got[1:]
        full = layer_weights(w_in_blocks, other_blocks)
        weights.append(full)
        w_in_blocks = got[0] if got else None
        og, o_raw, states = hgrn_fwd(proj, lb_all[l:l + 1], hg_norm[l:l + 1], ct=min(256, T), hp=HG_HEADS, name=f"hgrn_fwd_{l}")
        att = swa_fwd(proj, cos, sin, pm, attn_sinks[l], name=f"swa_fwd_{l}")
        ya, yb, mix = merge_fwd(og, att, proj, full["w_pa"], full["w_pb"], tm=tm, name=f"merge_fwd_{l}")
        x1 = matmul_residual(xs, mix, full["w_o"], tm=tmx, name=f"out_proj_{l}")
        carried = [shards[k][l + 1] for k in others] if l + 1 < DEPTH else []
        h2, gu, act, *other_blocks = ln_matmul(x1, norm2[l:l + 1], full["w_gu"], tm=min(256, T), tn=2 * H, name=f"ffn_up_{l}",
                                               swiglu=True, gather=carried)
        if l + 1 < DEPTH:
            w_in_now = _to_kernel_cols(whole("w_in", w_in_blocks))
        x2 = matmul_residual(x1, act, full["w_down"], tm=tmx, name=f"ffn_down_{l}")
        saved.append((xs, h, proj, og, o_raw, states, att, ya, yb, mix, x1, h2, gu, act))
        xs = x2

    dx, loss_parts, dfin_parts = loss_head(xs, final_norm[None], loss_target[0], tm=tm, name="loss_head")

    arrived = [None] * DEPTH
    dn1, dn2, dlb, dgn, dsk = ([None] * DEPTH for _ in range(5))
    for l in reversed(range(DEPTH)):
        xs, h, proj, og, o_raw, states, att, ya, yb, mix, x1, h2, gu, act = saved[l]
        full = weights[l]
        dw = {}
        dgu = ffn_down_bwd(dx, full["w_down"], gu, tm=min(256, T), name=f"ffn_down_bwd_{l}")
        dw["w_down"] = matmul_tn(act, dx, tka=H // 2, tn=D_MODEL, tk=tkx, name=f"dw_down_{l}")
        dwgu = matmul_tn(h2, dgu, tka=D_MODEL, tn=H // 2, tk=tkx, name=f"dw_gu_{l}")
        dw["w_gate"], dw["w_up"] = dwgu[:, :H], dwgu[:, H:]
        dx1, dn2_parts = matmul_nt_rmsbwd(dgu, full["w_gu"], x1, norm2[l:l + 1], dx, tm=tmx, tk=H // 2, name=f"ffn_up_bwd_{l}")
        dya, dyb, dproj = merge_bwd(dx1, full["w_o"], proj, ya, yb, tm=tm, name=f"merge_bwd_{l}")
        dw["w_o"] = matmul_tn(mix, dx1, tka=D_MODEL, tn=D_MODEL, tk=tkx, name=f"dw_o_{l}")
        dog = matmul_nt(dya, full["w_pa"], tm=tmx, name=f"pa_bwd_{l}")
        datt = matmul_nt(dyb, full["w_pb"], tm=tmx, name=f"pb_bwd_{l}")
        dw["w_pa"] = matmul_tn(og, dya, tka=D_MODEL, tn=D_MODEL, tk=tkx, name=f"dw_pa_{l}")
        dw["w_pb"] = matmul_tn(att, dyb, tka=D_MODEL, tn=D_MODEL, tk=tkx, name=f"dw_pb_{l}")
        dproj, dsk_parts = swa_bwd(proj, cos, sin, pm, attn_sinks[l], att, datt, dproj, name=f"swa_bwd_{l}")
        dproj, dlb_parts, dgn_parts = hgrn_bwd(proj, lb_all[l:l + 1], hg_norm[l:l + 1], o_raw, states, dog, dproj,
                                               ct=min(256, T), name=f"hgrn_bwd_{l}")
        split = lambda k, g: _cols_split(g) if k in col_sharded else g.reshape(N_CHIPS, -1, g.shape[-1])
        dw_in, *arrived_others = matmul_tn(h, dproj, tka=D_MODEL, tn=1536, tk=tkx, name=f"dw_in_{l}",
                                           scatter=[split(k, dw[k]) for k in others])
        dx, dn1_parts, arrived_w_in = matmul_nt_rmsbwd(dproj, full["w_in"], xs, norm1[l:l + 1], dx1, tm=tmx, tk=1536,
                                                       name=f"in_proj_bwd_{l}", scatter=[split("w_in", _from_kernel_cols(dw_in))])
        arrived[l] = [arrived_w_in] + arrived_others
        dn1[l], dn2[l], dlb[l], dgn[l] = dn1_parts.sum(0), dn2_parts.sum(0), dlb_parts.sum(0), dgn_parts.sum(0)
        dsk[l] = dsk_parts[:, 0, :].reshape(ATT_Q_HEADS, WINDOW).sum(-1)
    grad_x = dx[None]

    mine = _pack_small(jnp.stack(dn1), jnp.stack(dlb), jnp.stack(dgn), jnp.stack(dn2), dfin_parts.sum(0), jnp.stack(dsk),
                       loss_parts.sum())
    total = allreduce_small(mine, name="allreduce_small")
    g_n1, g_lb_all, g_gn, g_n2, g_fin, g_sk, loss = _unpack_small(total)
    (g_lb,) = lb_vjp(g_lb_all)
    zero = jnp.zeros((), F32)
    small = adamw(_pack_small(g_n1, g_lb, g_gn, g_n2, g_fin, g_sk, zero), None,
                  _pack_small(norm1, lb_logits, hg_norm, norm2, final_norm, attn_sinks, zero),
                  _pack_small(m_norm1, m_lb_logits, m_hg_norm, m_norm2, m_final_norm, m_attn_sinks, zero),
                  _pack_small(v_norm1, v_lb_logits, v_hg_norm, v_norm2, v_final_norm, v_attn_sinks, zero),
                  tr=SMALL_ROWS, name="adamw_small")
    small = [_unpack_small(s)[:6] for s in small]

    sums = []
    for i, k in enumerate(big):
        a = jnp.stack([arrived[l][i] for l in range(DEPTH)], axis=1)
        q, nl, r, c = a.shape
        sums.append(sum_parts(a.reshape(q, nl * r, c), tr=128, name=f"sum_{k}"))
    theirs = swap_with_sibling(sums, name="swap_sums")
    mom = dict(w_in=(m_w_in, v_w_in), w_pa=(m_w_pa, v_w_pa), w_pb=(m_w_pb, v_w_pb), w_o=(m_w_o, v_w_o),
               w_gate=(m_w_gate, v_w_gate), w_up=(m_w_up, v_w_up), w_down=(m_w_down, v_w_down))
    upd = {}
    for k, mine_k, theirs_k in zip(big, sums, theirs):
        shp = big[k].shape
        flat = lambda t: t.reshape(shp[0] * shp[1], shp[2])
        res = adamw(mine_k, theirs_k, flat(big[k]), flat(mom[k][0]), flat(mom[k][1]), tr=128, name=f"adamw_{k}")
        upd[k] = [t.reshape(shp) for t in res]

    order = ("norm1", "w_in", "lb_logits", "hg_norm", "attn_sinks", "w_pa", "w_pb", "w_o", "norm2", "w_gate", "w_up",
             "w_down", "final_norm")
    small_pos = dict(norm1=0, lb_logits=1, hg_norm=2, norm2=3, final_norm=4, attn_sinks=5)
    outs = [loss, grad_x]
    for kind in range(4):
        for name in order:
            outs.append(upd[name][kind] if name in upd else small[kind][small_pos[name]])
    return tuple(outs)
```

```python
import functools

import jax
import jax.numpy as jnp
import numpy as np
from jax import lax
from jax.experimental import pallas as pl
from jax.experimental.pallas import tpu as pltpu

F32 = jnp.float32
BF16 = jnp.bfloat16

D_MODEL = 1024
DEPTH = 4
HG_HEADS = 8
HG_DK = 128
HG_WIDTH = HG_HEADS * HG_DK
CHUNK = 64
ATT_Q_HEADS = 16
ATT_KV_HEADS = 4
ATT_GROUP = ATT_Q_HEADS // ATT_KV_HEADS
HEAD_DIM = 64
ATT_WIDTH = ATT_Q_HEADS * HEAD_DIM
ATT_KV_WIDTH = ATT_KV_HEADS * HEAD_DIM
WINDOW = 128
ROPE_THETA = 500000.0
ROPE_DIM = HEAD_DIM // 4
FFN_HIDDEN = 2816
IN_COLS = 4 * HG_WIDTH + ATT_WIDTH + 2 * ATT_KV_WIDTH + 2 * D_MODEL
EPS = 1e-6
MIN_F = 1e-30
COL_HQ, COL_HF, COL_HI, COL_HG = 0, 8, 16, 24
COL_GA, COL_GB, COL_AQ, COL_AK, COL_AV = 32, 40, 48, 56, 58


def _to_kernel_cols(w):
    return jnp.concatenate([w[..., :4096], w[..., 5632:7680], w[..., 4096:5120], w[..., 5120:5632]], axis=-1)


def _from_kernel_cols(w):
    return jnp.concatenate([w[..., :4096], w[..., 6144:7168], w[..., 7168:7680], w[..., 4096:6144]], axis=-1)

ADAM_LR = 0.001
ADAM_B1 = 0.9
ADAM_B2 = 0.999
ADAM_EPS = 1e-08
ADAM_WD = 0.01
ADAM_STEP = 10

VMEM_LIMIT_V7X = 56 * 1024 * 1024
MESH = pl.DeviceIdType.MESH


def _cp(sem, vmem=VMEM_LIMIT_V7X):
    return pltpu.CompilerParams(dimension_semantics=sem, vmem_limit_bytes=vmem)


def _sigmoid(x):
    return 1.0 / (1.0 + jnp.exp(-x))


def _dot(a, b):
    return jnp.dot(a, b, preferred_element_type=F32)


def _dot_nt(a, b):
    return lax.dot_general(a, b, (((1,), (1,)), ((), ())), preferred_element_type=F32)


def _dot_tn(a, b):
    return lax.dot_general(a, b, (((0,), (0,)), ((), ())), preferred_element_type=F32)


def _rowsum8(v):
    r, n = v.shape
    return jnp.sum(v.reshape(r // 8, 8, n), axis=0)


_ANY = pl.BlockSpec(memory_space=pl.ANY)
N_CHIPS = 4
PEER_CHIPS = N_CHIPS - 1


def _place():
    x, y, c = lax.axis_index("x"), lax.axis_index("y"), lax.axis_index("c")
    return x, y, c, [(1 - x, y), (x, 1 - y), (1 - x, 1 - y)]


def _exchange_scratch(n):
    return [pltpu.SemaphoreType.DMA((PEER_CHIPS * n,)), pltpu.SemaphoreType.DMA((PEER_CHIPS * n,)), pltpu.SemaphoreType.DMA((n,))]


def _exchange(ins, outs, sems, scatter):
    send_sems, recv_sems, local_sems = sems
    x, y, c, chips = _place()
    me = 2 * x + y
    n = len(ins)

    def remote(a, j, arriving):
        px, py = chips[j]
        them = 2 * px + py
        src = ins[a].at[them] if scatter else ins[a]
        return pltpu.make_async_remote_copy(
            src_ref=src, dst_ref=outs[a].at[them if arriving else me], send_sem=send_sems.at[a * PEER_CHIPS + j],
            recv_sem=recv_sems.at[a * PEER_CHIPS + j], device_id=(px, py, c), device_id_type=MESH)

    def local(a):
        return pltpu.make_async_copy(ins[a].at[me] if scatter else ins[a], outs[a].at[me], local_sems.at[a])

    def start():
        for a in range(n):
            local(a).start()
            for j in range(PEER_CHIPS):
                remote(a, j, False).start()

    def wait():
        for a in range(n):
            for j in range(PEER_CHIPS):
                remote(a, j, True).wait()
            local(a).wait()

    return start, wait


def _exchange_out_shapes(arrays, scatter):
    return [jax.ShapeDtypeStruct(a.shape if scatter else (N_CHIPS,) + a.shape, a.dtype) for a in arrays]


def exchange_between_chips(arrays, *, scatter, name):
    n = len(arrays)

    def body(*refs):
        start, wait = _exchange(refs[:n], refs[n:2 * n], refs[2 * n:], scatter)
        start()
        wait()

    return pl.pallas_call(
        body, name=name, out_shape=_exchange_out_shapes(arrays, scatter), in_specs=[_ANY] * n, out_specs=[_ANY] * n,
        scratch_shapes=_exchange_scratch(n),
    )(*arrays)


def ln_matmul(x, g, w, *, tm, tn, name, swiglu=False, gather=()):
    T, Dm = x.shape
    N = w.shape[1]
    if swiglu:
        assert tn == N
    half = N // 2
    ng = len(gather)
    n_out = 3 if swiglu else 2
    ni, nj = T // tm, N // tn

    def body(x_ref, g_ref, w_ref, *rest):
        gather_in, rest = rest[:ng], rest[ng:]
        h_ref, o_ref = rest[0], rest[1]
        i, j = pl.program_id(0), pl.program_id(1)
        if ng:
            start, wait = _exchange(gather_in, rest[n_out:n_out + ng], rest[n_out + ng:], False)
            pl.when((i == 0) & (j == 0))(start)

        @pl.when(j == 0)
        def _():
            xf = x_ref[...]
            r = lax.rsqrt(jnp.mean(xf * xf, axis=-1, keepdims=True) + EPS)
            h_ref[...] = (xf * r * g_ref[...]).astype(BF16)

        acc = _dot(h_ref[...], w_ref[...])
        o_ref[...] = acc.astype(BF16)
        if swiglu:
            gt, up = acc[:, :half], acc[:, half:]
            rest[2][...] = (gt * _sigmoid(gt) * up).astype(BF16)
        if ng:
            pl.when((i == ni - 1) & (j == nj - 1))(wait)

    out_shape = [jax.ShapeDtypeStruct((T, Dm), BF16), jax.ShapeDtypeStruct((T, N), BF16)]
    out_specs = [pl.BlockSpec((tm, Dm), lambda i, j: (i, 0)), pl.BlockSpec((tm, tn), lambda i, j: (i, j))]
    if swiglu:
        out_shape.append(jax.ShapeDtypeStruct((T, half), BF16))
        out_specs.append(pl.BlockSpec((tm, half), lambda i, j: (i, 0)))
    return pl.pallas_call(
        body, name=name, grid=(ni, nj),
        in_specs=[pl.BlockSpec((tm, Dm), lambda i, j: (i, 0)), pl.BlockSpec((1, Dm), lambda i, j: (0, 0)),
                  pl.BlockSpec((Dm, tn), lambda i, j: (0, j))] + [_ANY] * ng,
        out_specs=out_specs + [_ANY] * ng, out_shape=out_shape + _exchange_out_shapes(gather, False),
        scratch_shapes=_exchange_scratch(ng) if ng else [],
        compiler_params=_cp(("arbitrary", "arbitrary") if ng else ("parallel", "arbitrary")),
    )(x, g, w, *gather)


def matmul_residual(x, a, w, *, tm, name):
    T, N = x.shape
    K = a.shape[1]

    def body(x_ref, a_ref, w_ref, o_ref):
        o_ref[...] = x_ref[...] + _dot(a_ref[...], w_ref[...])

    return pl.pallas_call(
        body, name=name, grid=(T // tm,),
        in_specs=[pl.BlockSpec((tm, N), lambda i: (i, 0)), pl.BlockSpec((tm, K), lambda i: (i, 0)),
                  pl.BlockSpec((K, N), lambda i: (0, 0))],
        out_specs=pl.BlockSpec((tm, N), lambda i: (i, 0)), out_shape=jax.ShapeDtypeStruct((T, N), F32),
        compiler_params=_cp(("parallel",)),
    )(x, a, w)


def matmul_nt(a, w, *, tm, name):
    T, K = a.shape
    N = w.shape[0]

    def body(a_ref, w_ref, o_ref):
        o_ref[...] = _dot_nt(a_ref[...], w_ref[...]).astype(BF16)

    return pl.pallas_call(
        body, name=name, grid=(T // tm,),
        in_specs=[pl.BlockSpec((tm, K), lambda i: (i, 0)), pl.BlockSpec((N, K), lambda i: (0, 0))],
        out_specs=pl.BlockSpec((tm, N), lambda i: (i, 0)), out_shape=jax.ShapeDtypeStruct((T, N), BF16),
        compiler_params=_cp(("parallel",)),
    )(a, w)


def matmul_tn(a, g, *, tka, tn, tk, name, scatter=()):
    T, Ka = a.shape
    N = g.shape[1]
    ni, nj, nk = Ka // tka, N // tn, T // tk
    ns = len(scatter)

    def body(a_ref, g_ref, *rest):
        scatter_in, rest = rest[:ns], rest[ns:]
        o_ref, acc_ref = rest[0], rest[1 + ns]
        i, j, k = pl.program_id(0), pl.program_id(1), pl.program_id(2)
        if ns:
            start, wait = _exchange(scatter_in, rest[1:1 + ns], rest[2 + ns:], True)
            pl.when((i == 0) & (j == 0) & (k == 0))(start)

        @pl.when(k == 0)
        def _():
            acc_ref[...] = jnp.zeros_like(acc_ref)

        acc_ref[...] += _dot_tn(a_ref[...].astype(BF16), g_ref[...].astype(BF16))

        @pl.when(k == nk - 1)
        def _():
            o_ref[...] = acc_ref[...].astype(BF16)

        if ns:
            pl.when((i == ni - 1) & (j == nj - 1) & (k == nk - 1))(wait)

    res = pl.pallas_call(
        body, name=name, grid=(ni, nj, nk),
        in_specs=[pl.BlockSpec((tk, tka), lambda i, j, k: (k, i)), pl.BlockSpec((tk, tn), lambda i, j, k: (k, j))] + [_ANY] * ns,
        out_specs=[pl.BlockSpec((tka, tn), lambda i, j, k: (i, j))] + [_ANY] * ns,
        out_shape=[jax.ShapeDtypeStruct((Ka, N), BF16)] + _exchange_out_shapes(scatter, True),
        scratch_shapes=[pltpu.VMEM((tka, tn), F32)] + (_exchange_scratch(ns) if ns else []),
        compiler_params=_cp(("arbitrary",) * 3 if ns else ("parallel", "parallel", "arbitrary")),
    )(a, g, *scatter)
    return res if ns else res[0]


def matmul_nt_rmsbwd(a, w, x, g, dres, *, tm, tk, name, scatter=()):
    T, K = a.shape
    Dm = w.shape[0]
    nk = K // tk
    ni = T // tm
    ns = len(scatter)

    def body(a_ref, w_ref, x_ref, g_ref, dres_ref, *rest):
        scatter_in, rest = rest[:ns], rest[ns:]
        dx_ref, dg_ref = rest[0], rest[1]
        acc_ref = rest[2 + ns]
        i, k = pl.program_id(0), pl.program_id(1)
        if ns:
            start, wait = _exchange(scatter_in, rest[2:2 + ns], rest[3 + ns:], True)
            pl.when((i == 0) & (k == 0))(start)

        @pl.when(k == 0)
        def _():
            acc_ref[...] = jnp.zeros_like(acc_ref)

        @pl.when((i == 0) & (k == 0))
        def _():
            dg_ref[...] = jnp.zeros_like(dg_ref)

        acc_ref[...] += _dot_nt(a_ref[...], w_ref[...])

        @pl.when(k == nk - 1)
        def _():
            dh = acc_ref[...]
            xf = x_ref[...]
            r = lax.rsqrt(jnp.mean(xf * xf, axis=-1, keepdims=True) + EPS)
            xhat = xf * r
            dg_ref[...] += _rowsum8(dh * xhat)
            dxhat = dh * g_ref[...]
            dx_ref[...] = dres_ref[...] + r * (dxhat - xhat * jnp.mean(dxhat * xhat, axis=-1, keepdims=True))

        if ns:
            pl.when((i == ni - 1) & (k == nk - 1))(wait)

    return pl.pallas_call(
        body, name=name, grid=(ni, nk),
        in_specs=[pl.BlockSpec((tm, tk), lambda i, k: (i, k)), pl.BlockSpec((Dm, tk), lambda i, k: (0, k)),
                  pl.BlockSpec((tm, Dm), lambda i, k: (i, 0)), pl.BlockSpec((1, Dm), lambda i, k: (0, 0)),
                  pl.BlockSpec((tm, Dm), lambda i, k: (i, 0))] + [_ANY] * ns,
        out_specs=[pl.BlockSpec((tm, Dm), lambda i, k: (i, 0)), pl.BlockSpec((8, Dm), lambda i, k: (0, 0))] + [_ANY] * ns,
        out_shape=[jax.ShapeDtypeStruct((T, Dm), F32), jax.ShapeDtypeStruct((8, Dm), F32)] + _exchange_out_shapes(scatter, True),
        scratch_shapes=[pltpu.VMEM((tm, Dm), F32)] + (_exchange_scratch(ns) if ns else []),
        compiler_params=_cp(("arbitrary", "arbitrary")),
    )(a, w, x, g, dres, *scatter)


def ffn_down_bwd(dx, w_down, gu, *, tm, name):
    T, Dm = dx.shape
    H = w_down.shape[0]

    def body(dx_ref, w_ref, gu_ref, o_ref):
        dact = _dot_nt(dx_ref[...].astype(BF16), w_ref[...])
        gt = gu_ref[:, :H].astype(F32)
        up = gu_ref[:, H:].astype(F32)
        s = _sigmoid(gt)
        o_ref[:, :H] = (dact * up * (s * (1.0 + gt * (1.0 - s)))).astype(BF16)
        o_ref[:, H:] = (dact * gt * s).astype(BF16)

    return pl.pallas_call(
        body, name=name, grid=(T // tm,),
        in_specs=[pl.BlockSpec((tm, Dm), lambda i: (i, 0)), pl.BlockSpec((H, Dm), lambda i: (0, 0)),
                  pl.BlockSpec((tm, 2 * H), lambda i: (i, 0))],
        out_specs=pl.BlockSpec((tm, 2 * H), lambda i: (i, 0)), out_shape=jax.ShapeDtypeStruct((T, 2 * H), BF16),
        compiler_params=_cp(("parallel",)),
    )(dx, w_down, gu)


def _gate_specs(tm):
    return [pl.BlockSpec((tm, D_MODEL), lambda i, c=c: (i, c)) for c in (COL_GA // 8, COL_GB // 8)]


def merge_fwd(og, att, proj, w_pa, w_pb, *, tm, name):
    T, Dm = og.shape

    def body(og_ref, att_ref, ga_ref, gb_ref, wa_ref, wb_ref, ya_ref, yb_ref, mix_ref):
        ya = _dot(og_ref[...], wa_ref[...])
        yb = _dot(att_ref[...], wb_ref[...])
        ya_ref[...] = ya.astype(BF16)
        yb_ref[...] = yb.astype(BF16)
        mix_ref[...] = (_sigmoid(ga_ref[...].astype(F32)) * ya + _sigmoid(gb_ref[...].astype(F32)) * yb).astype(BF16)

    row = pl.BlockSpec((tm, Dm), lambda i: (i, 0))
    wsp = pl.BlockSpec((Dm, Dm), lambda i: (0, 0))
    return pl.pallas_call(
        body, name=name, grid=(T // tm,),
        in_specs=[row, row] + _gate_specs(tm) + [wsp, wsp],
        out_specs=[row, row, row], out_shape=[jax.ShapeDtypeStruct((T, Dm), BF16)] * 3,
        compiler_params=_cp(("parallel",)),
    )(og, att, proj, proj, w_pa, w_pb)


def merge_bwd(dx, w_o, proj, ya, yb, *, tm, name):
    T, Dm = dx.shape

    def body(dx_ref, w_ref, ga_ref, gb_ref, ya_ref, yb_ref, dya_ref, dyb_ref, dgate_ref):
        dmix = _dot_nt(dx_ref[...].astype(BF16), w_ref[...])
        sa = _sigmoid(ga_ref[...].astype(F32))
        sb = _sigmoid(gb_ref[...].astype(F32))
        dya_ref[...] = (dmix * sa).astype(BF16)
        dyb_ref[...] = (dmix * sb).astype(BF16)
        dgate_ref[:, :Dm] = (dmix * ya_ref[...].astype(F32) * sa * (1.0 - sa)).astype(BF16)
        dgate_ref[:, Dm:] = (dmix * yb_ref[...].astype(F32) * sb * (1.0 - sb)).astype(BF16)

    row = pl.BlockSpec((tm, Dm), lambda i: (i, 0))
    return pl.pallas_call(
        body, name=name, grid=(T // tm,),
        in_specs=[row, pl.BlockSpec((Dm, Dm), lambda i: (0, 0))] + _gate_specs(tm) + [row, row],
        out_specs=[row, row, pl.BlockSpec((tm, 2 * Dm), lambda i: (i, COL_GA // 16))],
        out_shape=[jax.ShapeDtypeStruct((T, Dm), BF16)] * 2 + [jax.ShapeDtypeStruct((T, IN_COLS), BF16)],
        compiler_params=_cp(("parallel",)),
    )(dx, w_o, proj, proj, ya, yb)


def loss_head(x, g, target, *, tm, name):
    T, Dm = x.shape

    def body(x_ref, g_ref, t_ref, dx_ref, l_ref, dg_ref):
        @pl.when(pl.program_id(0) == 0)
        def _():
            l_ref[...] = jnp.zeros_like(l_ref)
            dg_ref[...] = jnp.zeros_like(dg_ref)

        xf = x_ref[...]
        gv = g_ref[...]
        r = lax.rsqrt(jnp.mean(xf * xf, axis=-1, keepdims=True) + EPS)
        xhat = xf * r
        err = xhat * gv - t_ref[...]
        l_ref[...] += _rowsum8(err * err) * (0.5 / Dm)
        dy = err * (1.0 / Dm)
        dg_ref[...] += _rowsum8(dy * xhat)
        dxhat = dy * gv
        dx_ref[...] = r * (dxhat - xhat * jnp.mean(dxhat * xhat, axis=-1, keepdims=True))

    row = pl.BlockSpec((tm, Dm), lambda i: (i, 0))
    acc = pl.BlockSpec((8, Dm), lambda i: (0, 0))
    return pl.pallas_call(
        body, name=name, grid=(T // tm,),
        in_specs=[row, pl.BlockSpec((1, Dm), lambda i: (0, 0)), row],
        out_specs=[row, acc, acc],
        out_shape=[jax.ShapeDtypeStruct((T, Dm), F32), jax.ShapeDtypeStruct((8, Dm), F32), jax.ShapeDtypeStruct((8, Dm), F32)],
        compiler_params=_cp(("arbitrary",)),
    )(x, g, target)


_LEVELS = (32, 16, 8)
_DIAG = 8
_SUBLANES = 8
_HI = lax.Precision.HIGHEST


def _chunk_consts():
    t = lax.broadcasted_iota(jnp.int32, (CHUNK, CHUNK), 0)
    s = lax.broadcasted_iota(jnp.int32, (CHUNK, CHUNK), 1)
    tri = (s <= t).astype(F32)
    tri_t = (s >= t).astype(F32)
    masks = []
    for m in _LEVELS:
        sh = int(np.log2(2 * m))
        masks.append(((t >> sh) == (s >> sh)) & ((t & (2 * m - 1)) >= m) & ((s & (2 * m - 1)) < m))
    return tri, tri_t, masks


def _exact_dot(sel, x):
    hi = x.astype(BF16)
    r1 = x - hi.astype(F32)
    mid = r1.astype(BF16)
    lo = (r1 - mid.astype(F32)).astype(BF16)
    return _dot(jnp.concatenate([sel, sel, sel], axis=1).astype(BF16), jnp.concatenate([hi, mid, lo], axis=0))


def _level_ref(b, m):
    if 2 * m >= _SUBLANES:
        pieces = [jnp.broadcast_to(b[p * 2 * m + m - 1:p * 2 * m + m, :], (2 * m, b.shape[1])) for p in range(CHUNK // (2 * m))]
        return pieces[0] if len(pieces) == 1 else jnp.concatenate(pieces, axis=0)
    groups = CHUNK // _SUBLANES
    b3 = b.reshape(groups, _SUBLANES, b.shape[1])
    row = lax.broadcasted_iota(jnp.int32, b3.shape, 1)
    ref = None
    for p in reversed(range(_SUBLANES // (2 * m))):
        src = jnp.broadcast_to(b3[:, p * 2 * m + m - 1:p * 2 * m + m, :], b3.shape)
        ref = src if ref is None else jnp.where(row < (p + 1) * 2 * m, src, ref)
    return ref.reshape(b.shape)


def _intra_off(q, kk, b, masks):
    a = jnp.zeros((CHUNK, CHUNK), F32)
    keep = []
    for m, mask in zip(_LEVELS, masks):
        e = jnp.exp2(-jnp.abs(b - _level_ref(b, m)))
        qt = (q * e).astype(BF16)
        kt = (kk * e).astype(BF16)
        a = a + jnp.where(mask, _dot_nt(qt, kt), 0.0)
        keep.append((mask, e, qt, kt))
    return a, keep


def _roll_rows(x, d):
    g = x.shape[0] // _DIAG
    return pltpu.roll(x.reshape(g, _DIAG, x.shape[1]), d % _DIAG, 1).reshape(x.shape)


def _diag_fwd(q, kk, v, b):
    lower = lax.broadcasted_iota(jnp.int32, q.shape, 0) & (_DIAG - 1)
    out = jnp.sum(q * kk, axis=-1, keepdims=True) * v
    for d in range(1, _DIAG):
        dec = jnp.exp2(jnp.where(lower >= d, b - _roll_rows(b, d), _NEG))
        out = out + jnp.sum(q * dec * _roll_rows(kk, d), axis=-1, keepdims=True) * _roll_rows(v, d)
    return out


def _diag_bwd(q, kk, v, b, do):
    lower = lax.broadcasted_iota(jnp.int32, q.shape, 0) & (_DIAG - 1)
    a = jnp.sum(q * kk, axis=-1, keepdims=True)
    da = jnp.sum(do * v, axis=-1, keepdims=True)
    dq, dk, dv = da * kk, da * q, a * do
    for d in range(1, _DIAG):
        kr = _roll_rows(kk, d)
        dec = jnp.exp2(jnp.where(lower >= d, b - _roll_rows(b, d), _NEG))
        qd = q * dec
        a = jnp.sum(qd * kr, axis=-1, keepdims=True)
        da = jnp.sum(do * _roll_rows(v, d), axis=-1, keepdims=True)
        dq = dq + da * dec * kr
        dk = dk + _roll_rows(da * qd, -d)
        dv = dv + _roll_rows(a * do, -d)
    return dq, dk, dv


def _hgrn_in_specs(ct, hp, order):
    def spec(col):
        return pl.BlockSpec((ct, hp * HG_DK), lambda h, i, col=col: (order(i), col // hp + h))
    vec = pl.BlockSpec((1, hp * HG_DK), lambda h, i: (0, h))
    return [spec(COL_HQ), spec(COL_HF), spec(COL_HI), spec(COL_HG), vec, vec]


def hgrn_fwd(proj, lb, gn, *, ct, hp, name):
    T = proj.shape[0]
    nc = ct // CHUNK

    def body(hq_ref, hf_ref, hi_ref, hg_ref, lb_ref, gn_ref, og_ref, o_ref, st_ref, s_ref):
        @pl.when(pl.program_id(1) == 0)
        def _():
            s_ref[...] = jnp.zeros_like(s_ref)

        tri, _, masks = _chunk_consts()

        def chunk(c, carry):
            sl = pl.ds(pl.multiple_of(c * CHUNK, CHUNK), CHUNK)
            lb_all = lb_ref[...]
            f_all = lb_all + (1.0 - lb_all) * _sigmoid(hf_ref[sl, :].astype(F32))
            b_all = _exact_dot(tri, jnp.log2(jnp.maximum(f_all, MIN_F)))
            for p in range(hp):
                cols = slice(p * HG_DK, (p + 1) * HG_DK)
                hq = hq_ref[sl, cols].astype(F32)
                v = hi_ref[sl, cols].astype(F32)
                gate = hg_ref[sl, cols].astype(F32)
                q = hq * _sigmoid(hq)
                f = f_all[:, cols]
                kk = 1.0 - f
                b = b_all[:, cols]
                st = s_ref[p]
                st_ref[p, c] = st
                a, _ = _intra_off(q, kk, b, masks)
                vb = v.astype(BF16)
                o = (_dot_nt((q * jnp.exp2(b)).astype(BF16), st.astype(BF16)) + _dot(a.astype(BF16), vb)
                     + _diag_fwd(q, kk, v, b))
                bl = b[CHUNK - 1:CHUNK, :]
                s_ref[p] = st * jnp.exp2(bl) + _dot_tn(vb, (kk * jnp.exp2(bl - b)).astype(BF16))
                o_ref[sl, cols] = o
                r = lax.rsqrt(jnp.mean(o * o, axis=-1, keepdims=True) + EPS)
                og_ref[sl, cols] = (o * r * gn_ref[:, cols] * (gate * _sigmoid(gate))).astype(BF16)
            return carry

        lax.fori_loop(0, nc, chunk, 0)

    blk = pl.BlockSpec((ct, hp * HG_DK), lambda h, i: (i, h))
    return pl.pallas_call(
        body, name=name, grid=(HG_HEADS // hp, T // ct),
        in_specs=_hgrn_in_specs(ct, hp, lambda i: i),
        out_specs=[blk, blk, pl.BlockSpec((hp, nc, HG_DK, HG_DK), lambda h, i: (h, i, 0, 0))],
        out_shape=[jax.ShapeDtypeStruct((T, HG_WIDTH), BF16), jax.ShapeDtypeStruct((T, HG_WIDTH), F32),
                   jax.ShapeDtypeStruct((HG_HEADS, T // CHUNK, HG_DK, HG_DK), F32)],
        scratch_shapes=[pltpu.VMEM((hp, HG_DK, HG_DK), F32)],
        compiler_params=_cp(("parallel", "arbitrary")),
    )(proj, proj, proj, proj, lb, gn)


def hgrn_bwd(proj, lb, gn, o_raw, states, dog, dproj, *, ct, name):
    T = proj.shape[0]
    nc = ct // CHUNK
    nblk = T // ct
    hp = HG_HEADS

    def body(hq_ref, hf_ref, hi_ref, hg_ref, lb_ref, gn_ref, o_ref, st_ref, dog_ref, _dproj_in,
             dp_ref, dlb_ref, dgn_ref, ds_ref):
        dhq_ref, dhf_ref, dhi_ref, dhg_ref = (dp_ref.at[:, s * HG_WIDTH:(s + 1) * HG_WIDTH] for s in range(4))
        @pl.when(pl.program_id(1) == 0)
        def _():
            ds_ref[...] = jnp.zeros_like(ds_ref)
            dlb_ref[...] = jnp.zeros_like(dlb_ref)
            dgn_ref[...] = jnp.zeros_like(dgn_ref)

        tri, tri_t, masks = _chunk_consts()
        last_row = lax.broadcasted_iota(jnp.int32, (CHUNK, HG_DK), 0) == CHUNK - 1

        def chunk(ci, carry):
            c = nc - 1 - ci
            sl = pl.ds(pl.multiple_of(c * CHUNK, CHUNK), CHUNK)
            lb_all = lb_ref[...]
            sig_all = _sigmoid(hf_ref[sl, :].astype(F32))
            f_all = lb_all + (1.0 - lb_all) * sig_all
            b_all = _exact_dot(tri, jnp.log2(jnp.maximum(f_all, MIN_F)))
            dbs, dks = [], []
            for p in range(hp):
                cols = slice(p * HG_DK, (p + 1) * HG_DK)
                gn_v = gn_ref[:, cols]
                hq = hq_ref[sl, cols].astype(F32)
                v = hi_ref[sl, cols].astype(F32)
                gate = hg_ref[sl, cols].astype(F32)
                sq = _sigmoid(hq)
                q = hq * sq
                f = f_all[:, cols]
                kk = 1.0 - f
                b = b_all[:, cols]
                st = st_ref[p, c]
                dst = ds_ref[p]
                o = o_ref[sl, cols]
                r = lax.rsqrt(jnp.mean(o * o, axis=-1, keepdims=True) + EPS)
                xhat = o * r
                sg = _sigmoid(gate)
                dog_v = dog_ref[sl, cols].astype(F32)
                dy = dog_v * (gate * sg)
                dhg_ref[sl, cols] = (dog_v * xhat * gn_v * (sg * (1.0 + gate * (1.0 - sg)))).astype(BF16)
                dgn_ref[:, cols] += _rowsum8(dy * xhat)
                dxh = dy * gn_v
                do = r * (dxh - xhat * jnp.mean(dxh * xhat, axis=-1, keepdims=True))
                dob = do.astype(BF16)
                vb = v.astype(BF16)
                stb = st.astype(BF16)
                dstb = dst.astype(BF16)
                eb = jnp.exp2(b)
                bl = b[CHUNK - 1:CHUNK, :]
                ebl = jnp.exp2(bl)
                edec = jnp.exp2(bl - b)
                qe = q * eb
                kdec = kk * edec
                a, keep = _intra_off(q, kk, b, masks)
                da = _dot_nt(dob, vb)
                dkdec = _dot(vb, dstb)
                dq = _dot(dob, stb) * eb
                dk = dkdec * edec
                dv = _dot_tn(a.astype(BF16), dob) + _dot_nt(kdec.astype(BF16), dstb)
                for mask, e, qt, kt in keep:
                    dsm = jnp.where(mask, da, 0.0).astype(BF16)
                    dq = dq + _dot(dsm, kt) * e
                    dk = dk + _dot_tn(dsm, qt) * e
                ddq, ddk, ddv = _diag_bwd(q, kk, v, b, do)
                dq = dq + ddq
                dk = dk + ddk
                dhi_ref[sl, cols] = (dv + ddv).astype(BF16)
                dbl = jnp.sum(dkdec * kdec, axis=0, keepdims=True) + ebl * jnp.sum(st * dst, axis=0, keepdims=True)
                db = q * dq - kk * dk
                dbs.append(jnp.where(last_row, db + dbl, db))
                dks.append(dk)
                ds_ref[p] = _dot_tn(dob, qe.astype(BF16)) + dst * ebl
                dhq_ref[sl, cols] = (dq * (sq * (1.0 + hq * (1.0 - sq)))).astype(BF16)
            dlogf = _exact_dot(tri_t, jnp.concatenate(dbs, axis=1))
            df = jnp.where(f_all > MIN_F, dlogf / f_all, 0.0) - jnp.concatenate(dks, axis=1)
            dhf_ref[sl, :] = (df * (1.0 - lb_all) * sig_all * (1.0 - sig_all)).astype(BF16)
            dlb_ref[...] += _rowsum8(df * (1.0 - sig_all))
            return carry

        lax.fori_loop(0, nc, chunk, 0)

    rev = lambda i: nblk - 1 - i
    blk = pl.BlockSpec((ct, hp * HG_DK), lambda h, i: (rev(i), h))
    acc = pl.BlockSpec((8, hp * HG_DK), lambda h, i: (0, h))
    return pl.pallas_call(
        body, name=name, grid=(HG_HEADS // hp, nblk),
        in_specs=_hgrn_in_specs(ct, hp, rev) + [blk, pl.BlockSpec((hp, nc, HG_DK, HG_DK), lambda h, i: (h, rev(i), 0, 0)), blk, _ANY],
        out_specs=[pl.BlockSpec((ct, 4 * HG_WIDTH), lambda h, i: (rev(i), 0)), acc, acc],
        out_shape=[jax.ShapeDtypeStruct(dproj.shape, dproj.dtype)] + [jax.ShapeDtypeStruct((8, HG_WIDTH), F32)] * 2,
        scratch_shapes=[pltpu.VMEM((hp, HG_DK, HG_DK), F32)],
        input_output_aliases={9: 0},
        compiler_params=_cp(("parallel", "arbitrary")),
    )(proj, proj, proj, proj, lb, gn, o_raw, states, dog, dproj)


_GW = ATT_GROUP * HEAD_DIM
_NEG = -1e30


def rope_tables(T):
    half = ROPE_DIM // 2
    inv = ROPE_THETA ** (-jnp.arange(half, dtype=F32) * 2.0 / ROPE_DIM)
    ang = jnp.arange(T, dtype=F32)[:, None] * inv[None, :]
    pad1 = jnp.ones((T, HEAD_DIM - ROPE_DIM), F32)
    cos = jnp.concatenate([jnp.cos(ang), jnp.cos(ang), pad1], axis=1)
    sin = jnp.concatenate([jnp.sin(ang), jnp.sin(ang), 0.0 * pad1], axis=1)
    p = np.zeros((_GW, _GW), np.float32)
    for base in range(0, _GW, HEAD_DIM):
        for i in range(half):
            p[base + i + half, base + i] = -1.0
            p[base + i, base + i + half] = 1.0
    reps = _GW // HEAD_DIM
    return jnp.tile(cos, (1, reps)), jnp.tile(sin, (1, reps)), jnp.asarray(p, BF16)


def _rope(x, cos, sin, pm):
    return x.astype(F32) * cos + _dot(x, pm) * sin


def _unrope(dx, cos, sin, pm):
    return dx * cos + _dot_nt((dx * sin).astype(BF16), pm)


_GQ = ATT_GROUP * WINDOW


def _swa_mask(n, queries_on_rows):
    shape = (_GQ, 2 * WINDOW) if queries_on_rows else (2 * WINDOW, _GQ)
    r = lax.broadcasted_iota(jnp.int32, shape, 0 if queries_on_rows else 1) & (WINDOW - 1)
    c = lax.broadcasted_iota(jnp.int32, shape, 1 if queries_on_rows else 0)
    delta = r + WINDOW - c
    return (delta >= 0) & (delta < WINDOW) & ((c >= WINDOW) | (n > 0))


def _sink_vector(sink_ref, g, queries_on_rows):
    shape = (_GQ, 1) if queries_on_rows else (1, _GQ)
    i = lax.broadcasted_iota(jnp.int32, shape, 0 if queries_on_rows else 1)
    out = jnp.full(shape, sink_ref[g * ATT_GROUP + ATT_GROUP - 1], F32)
    for hh in range(ATT_GROUP - 2, -1, -1):
        out = jnp.where(i < (hh + 1) * WINDOW, sink_ref[g * ATT_GROUP + hh], out)
    return out


def _head(x, h):
    return x[:, h * HEAD_DIM:(h + 1) * HEAD_DIM]


def _stack_heads(x):
    return jnp.concatenate([_head(x, hh) for hh in range(ATT_GROUP)], axis=0)


def _unstack_heads(x):
    return jnp.concatenate([x[hh * WINDOW:(hh + 1) * WINDOW] for hh in range(ATT_GROUP)], axis=1)


def _swa_specs(cur):
    prev = lambda n: jnp.maximum(n - 1, 0)
    kv = lambda col, f: pl.BlockSpec((WINDOW, _GW), lambda n: (f(n), col))
    tab = lambda f: pl.BlockSpec((WINDOW, _GW), lambda n: (f(n), 0))
    return [pl.BlockSpec((WINDOW, ATT_WIDTH), lambda n: (cur(n), COL_AQ // 8)),
            kv(COL_AK // 2, prev), kv(COL_AK // 2, cur), kv(COL_AV // 2, prev), kv(COL_AV // 2, cur),
            tab(prev), tab(cur), tab(prev), tab(cur),
            pl.BlockSpec((_GW, _GW), lambda n: (0, 0)), pl.BlockSpec(memory_space=pltpu.SMEM)]


def swa_fwd(proj, cos, sin, pm, sinks, *, name):
    T = proj.shape[0]
    nb = T // WINDOW
    scale = HEAD_DIM ** -0.5

    def body(q_ref, kp_ref, kc_ref, vp_ref, vc_ref, cp_ref, cc_ref, sp_ref, sc_ref, pm_ref, sink_ref, o_ref):
        n = pl.program_id(0)
        pm_v = pm_ref[...]
        mask = _swa_mask(n, True)
        k_cat = jnp.concatenate([_rope(kp_ref[...], cp_ref[...], sp_ref[...], pm_v),
                                 _rope(kc_ref[...], cc_ref[...], sc_ref[...], pm_v)], axis=0).astype(BF16)
        v_cat = jnp.concatenate([vp_ref[...], vc_ref[...]], axis=0)
        for g in range(ATT_KV_HEADS):
            qs = _stack_heads(_rope(q_ref[:, g * _GW:(g + 1) * _GW], cc_ref[...], sc_ref[...], pm_v).astype(BF16))
            sk = _sink_vector(sink_ref, g, True)
            s = jnp.where(mask, _dot_nt(qs, _head(k_cat, g)) * scale, _NEG)
            m = jnp.maximum(jnp.max(s, axis=-1, keepdims=True), sk)
            p = jnp.exp(s - m)
            l = jnp.sum(p, axis=-1, keepdims=True) + jnp.exp(sk - m)
            o = _dot(p.astype(BF16), _head(v_cat, g)) / l
            o_ref[:, g * _GW:(g + 1) * _GW] = _unstack_heads(o).astype(BF16)

    return pl.pallas_call(
        body, name=name, grid=(nb,), in_specs=_swa_specs(lambda n: n),
        out_specs=pl.BlockSpec((WINDOW, ATT_WIDTH), lambda n: (n, 0)),
        out_shape=jax.ShapeDtypeStruct((T, ATT_WIDTH), BF16), compiler_params=_cp(("parallel",)),
    )(proj, proj, proj, proj, proj, cos, cos, sin, sin, pm, sinks)


def swa_bwd(proj, cos, sin, pm, sinks, att, datt, dproj, *, name):
    T = proj.shape[0]
    nb = T // WINDOW
    scale = HEAD_DIM ** -0.5
    clamp = lambda n: jnp.minimum(n, nb - 1)

    def body(q_ref, kp_ref, kc_ref, vp_ref, vc_ref, cp_ref, cc_ref, sp_ref, sc_ref, pm_ref, sink_ref, att_ref, datt_ref,
             _dproj_in, dp_ref, dsink_ref, cq_ref, ck_ref, cv_ref):
        n = pl.program_id(0)
        pm_v = pm_ref[...]

        @pl.when(n == 0)
        def _():
            dsink_ref[...] = jnp.zeros_like(dsink_ref)
            cq_ref[...] = jnp.zeros_like(cq_ref)
            ck_ref[...] = jnp.zeros_like(ck_ref)
            cv_ref[...] = jnp.zeros_like(cv_ref)

        def write_prev(dk_prev, dv_prev):
            dp_ref[:, :ATT_WIDTH] = cq_ref[...]
            dk = _unrope(ck_ref[...] + dk_prev, cp_ref[...], sp_ref[...], pm_v)
            dp_ref[:, ATT_WIDTH:ATT_WIDTH + _GW] = dk.astype(BF16)
            dp_ref[:, ATT_WIDTH + _GW:] = (cv_ref[...] + dv_prev).astype(BF16)

        @pl.when(n < nb)
        def _():
            mask = _swa_mask(n, False)
            ones = jnp.ones((8, HEAD_DIM), F32)
            k_cat = jnp.concatenate([_rope(kp_ref[...], cp_ref[...], sp_ref[...], pm_v),
                                     _rope(kc_ref[...], cc_ref[...], sc_ref[...], pm_v)], axis=0).astype(BF16)
            v_cat = jnp.concatenate([vp_ref[...], vc_ref[...]], axis=0)
            dqs, dks, dvs = [], [], []
            for g in range(ATT_KV_HEADS):
                gc = slice(g * _GW, (g + 1) * _GW)
                qs = _stack_heads(_rope(q_ref[:, gc], cc_ref[...], sc_ref[...], pm_v).astype(BF16))
                dos = _stack_heads(datt_ref[:, gc])
                os_ = _stack_heads(att_ref[:, gc])
                kh, vh = _head(k_cat, g), _head(v_cat, g)
                sk = _sink_vector(sink_ref, g, False)
                s = jnp.where(mask, _dot_nt(kh, qs) * scale, _NEG)
                m = jnp.maximum(jnp.max(s, axis=0, keepdims=True), sk)
                e = jnp.exp(s - m)
                es = jnp.exp(sk - m)
                inv_l = 1.0 / (jnp.sum(e, axis=0, keepdims=True) + es)
                p = e * inv_l
                dsum = lax.dot_general(ones, dos.astype(F32) * os_.astype(F32), (((1,), (1,)), ((), ())),
                                       precision=_HI, preferred_element_type=F32)[0:1]
                dsink_ref[g, 0:1, :] += -(es * inv_l) * dsum
                ds = (p * (_dot_nt(vh, dos) - dsum) * scale).astype(BF16)
                dqs.append(_unrope(_unstack_heads(_dot_tn(ds, kh)), cc_ref[...], sc_ref[...], pm_v))
                dks.append(_dot(ds, qs))
                dvs.append(_dot(p.astype(BF16), dos))
            dk_all = jnp.concatenate(dks, axis=1)
            dv_all = jnp.concatenate(dvs, axis=1)

            @pl.when(n > 0)
            def _():
                write_prev(dk_all[:WINDOW], dv_all[:WINDOW])

            cq_ref[...] = jnp.concatenate(dqs, axis=1).astype(BF16)
            ck_ref[...] = dk_all[WINDOW:]
            cv_ref[...] = dv_all[WINDOW:]

        @pl.when(n == nb)
        def _():
            write_prev(jnp.zeros((WINDOW, _GW), F32), jnp.zeros((WINDOW, _GW), F32))

    row = pl.BlockSpec((WINDOW, ATT_WIDTH), lambda n: (clamp(n), 0))
    slab = ATT_WIDTH + 2 * _GW
    return pl.pallas_call(
        body, name=name, grid=(nb + 1,), in_specs=_swa_specs(clamp) + [row, row, _ANY],
        out_specs=[pl.BlockSpec((WINDOW, slab), lambda n: (jnp.maximum(n - 1, 0), COL_AQ * HG_DK // slab)),
                   pl.BlockSpec((ATT_KV_HEADS, 8, _GQ), lambda n: (0, 0, 0))],
        out_shape=[jax.ShapeDtypeStruct(dproj.shape, dproj.dtype), jax.ShapeDtypeStruct((ATT_KV_HEADS, 8, _GQ), F32)],
        scratch_shapes=[pltpu.VMEM((WINDOW, ATT_WIDTH), BF16), pltpu.VMEM((WINDOW, _GW), F32), pltpu.VMEM((WINDOW, _GW), F32)],
        input_output_aliases={13: 0},
        compiler_params=_cp(("arbitrary",)),
    )(proj, proj, proj, proj, proj, cos, cos, sin, sin, pm, sinks, att, datt, dproj)


def swap_with_sibling(arrays, *, name):
    n = len(arrays)

    def body(*refs):
        ins, outs = refs[:n], refs[n:2 * n]
        send_sems, recv_sems = refs[2 * n:]
        x, y, c, _ = _place()
        copies = [pltpu.make_async_remote_copy(src_ref=ins[a], dst_ref=outs[a], send_sem=send_sems.at[a],
                                               recv_sem=recv_sems.at[a], device_id=(x, y, 1 - c), device_id_type=MESH)
                  for a in range(n)]
        for cp in copies:
            cp.start()
        for cp in copies:
            cp.wait()

    return pl.pallas_call(
        body, name=name, out_shape=[jax.ShapeDtypeStruct(a.shape, a.dtype) for a in arrays],
        in_specs=[_ANY] * n, out_specs=[_ANY] * n,
        scratch_shapes=[pltpu.SemaphoreType.DMA((n,)), pltpu.SemaphoreType.DMA((n,))],
    )(*arrays)


N_DEV = 8
SMALL_ROWS = 24


def allreduce_small(v, *, name):
    def body(v_ref, o_ref, recv_ref, send_sems, recv_sems):
        x, y, c, _ = _place()
        me = 4 * x + 2 * y + c
        recv_ref[me] = v_ref[...]

        def copy(k, slot):
            peer = (x ^ (k >> 2), y ^ ((k >> 1) & 1), c ^ (k & 1))
            return pltpu.make_async_remote_copy(src_ref=v_ref, dst_ref=recv_ref.at[slot], send_sem=send_sems.at[k - 1],
                                                recv_sem=recv_sems.at[k - 1], device_id=peer, device_id_type=MESH)

        for k in range(1, N_DEV):
            copy(k, me).start()
        for k in range(1, N_DEV):
            copy(k, me ^ k).wait()
        acc = recv_ref[0]
        for d in range(1, N_DEV):
            acc = acc + recv_ref[d]
        o_ref[...] = acc

    vm = pl.BlockSpec(memory_space=pltpu.VMEM)
    return pl.pallas_call(
        body, name=name, out_shape=jax.ShapeDtypeStruct(v.shape, v.dtype), in_specs=[vm], out_specs=vm,
        scratch_shapes=[pltpu.VMEM((N_DEV,) + v.shape, v.dtype), pltpu.SemaphoreType.DMA((N_DEV - 1,)),
                        pltpu.SemaphoreType.DMA((N_DEV - 1,))],
    )(v)


def sum_parts(parts, *, tr, name):
    _, R, C = parts.shape

    def body(p_ref, o_ref):
        acc = p_ref[0].astype(F32)
        for q in range(1, N_CHIPS):
            acc = acc + p_ref[q].astype(F32)
        o_ref[...] = acc

    return pl.pallas_call(
        body, name=name, grid=(R // tr,), in_specs=[pl.BlockSpec((N_CHIPS, tr, C), lambda i: (0, i, 0))],
        out_specs=pl.BlockSpec((tr, C), lambda i: (i, 0)), out_shape=jax.ShapeDtypeStruct((R, C), F32),
        compiler_params=_cp(("parallel",)),
    )(parts)


def adamw(g_a, g_b, w, m, v, *, tr, name):
    R, C = w.shape
    c1 = 1.0 - ADAM_B1 ** ADAM_STEP
    c2 = 1.0 - ADAM_B2 ** ADAM_STEP
    ng = 1 if g_b is None else 2

    def body(*refs):
        w_ref, m_ref, v_ref, g_ref, d_ref, nm_ref, nv_ref = refs[ng:]
        g = refs[0][...] if ng == 1 else refs[0][...] + refs[1][...]
        nm = ADAM_B1 * m_ref[...] + (1.0 - ADAM_B1) * g
        nv = ADAM_B2 * v_ref[...] + (1.0 - ADAM_B2) * (g * g)
        g_ref[...] = g
        nm_ref[...] = nm
        nv_ref[...] = nv
        d_ref[...] = -ADAM_LR * ((nm / c1) / (jnp.sqrt(nv / c2) + ADAM_EPS) + ADAM_WD * w_ref[...])

    blk = pl.BlockSpec((tr, C), lambda i: (i, 0))
    gs = [g_a] if g_b is None else [g_a, g_b]
    return pl.pallas_call(
        body, name=name, grid=(R // tr,), in_specs=[blk] * (ng + 3), out_specs=[blk] * 4,
        out_shape=[jax.ShapeDtypeStruct((R, C), F32)] * 4, compiler_params=_cp(("parallel",)),
    )(*gs, w, m, v)


def _lb_bounds(lb_logits):
    p = jax.nn.softmax(lb_logits.astype(F32), axis=0)
    return jnp.cumsum(p, axis=0) - p[0:1]


def _pack_small(n1, lb, hgn, n2, fin, sinks, extra):
    row = jnp.concatenate([sinks.reshape(-1), jnp.reshape(extra, (1,)),
                           jnp.zeros((D_MODEL - sinks.size - 1,), F32)])
    return jnp.concatenate([n1, lb, hgn, n2, fin[None], row[None], jnp.zeros((SMALL_ROWS - 18, D_MODEL), F32)], axis=0)


def _unpack_small(buf):
    return (buf[0:4], buf[4:8], buf[8:12], buf[12:16], buf[16], buf[17, :DEPTH * ATT_Q_HEADS].reshape(DEPTH, ATT_Q_HEADS),
            buf[17, DEPTH * ATT_Q_HEADS])


def _cols_full(g):
    q, r, c = g.shape
    return jnp.transpose(g, (1, 0, 2)).reshape(r, q * c)


def _cols_split(w):
    r, qc = w.shape
    return jnp.transpose(w.reshape(r, N_CHIPS, qc // N_CHIPS), (1, 0, 2))


def kernel(x, norm1, w_in, lb_logits, hg_norm, attn_sinks, w_pa, w_pb, w_o, norm2, w_gate, w_up, w_down, final_norm, loss_target, m_norm1, m_w_in, m_lb_logits, m_hg_norm, m_attn_sinks, m_w_pa, m_w_pb, m_w_o, m_norm2, m_w_gate, m_w_up, m_w_down, m_final_norm, v_norm1, v_w_in, v_lb_logits, v_hg_norm, v_attn_sinks, v_w_pa, v_w_pb, v_w_o, v_norm2, v_w_gate, v_w_up, v_w_down, v_final_norm):
    T = x.shape[1]
    H = FFN_HIDDEN
    tm = min(512, T)
    tmx = min(1024, T)
    tkx = min(2048, T)
    big = dict(w_in=w_in, w_pa=w_pa, w_pb=w_pb, w_o=w_o, w_gate=w_gate, w_up=w_up, w_down=w_down)
    col_sharded = ("w_in", "w_gate", "w_up")

    shards = {k: w.astype(BF16) for k, w in big.items()}
    ffn_pair = ("w_gate", "w_up")
    with_w_in = ("w_in", "w_pa", "w_pb", "w_o", "w_down")
    others = [k for k in big if k != "w_in"]

    def whole(k, g):
        return _cols_full(g) if k in col_sharded else g.reshape(-1, g.shape[-1])

    def layer_weights(blocks_l):
        full = {k: whole(k, blocks_l[k]) for k in big}
        full["w_in"] = _to_kernel_cols(full["w_in"])
        full["w_gu"] = jnp.concatenate([full.pop("w_gate"), full.pop("w_up")], axis=-1)
        return full

    def carried_by(kind, l):
        if l + 1 >= DEPTH:
            return []
        if l == 0:
            return [(0, k) for k in others] + [(1, "w_in")] if kind == "in_proj" else [(1, k) for k in others]
        return [(l + 1, k) for k in (with_w_in if kind == "in_proj" else ffn_pair)]

    lb_all, lb_vjp = jax.vjp(_lb_bounds, lb_logits)
    cos, sin, pm = rope_tables(T)

    xs = x[0]
    saved, weights = [], []
    blocks = [dict() for _ in range(DEPTH)]
    (blocks[0]["w_in"],) = exchange_between_chips([shards["w_in"][0]], scatter=False, name="gather_w_in_0")
    for l in range(DEPTH):
        tasks = carried_by("in_proj", l)
        h, proj, *got = ln_matmul(xs, norm1[l:l + 1], _to_kernel_cols(whole("w_in", blocks[l]["w_in"])), tm=tmx, tn=1536,
                                  name=f"in_proj_{l}", gather=[shards[k][j] for j, k in tasks])
        for (j, k), g in zip(tasks, got):
            blocks[j][k] = g
        full = layer_weights(blocks[l])
        weights.append(full)
        og, o_raw, states = hgrn_fwd(proj, lb_all[l:l + 1], hg_norm[l:l + 1], ct=min(256, T), hp=HG_HEADS, name=f"hgrn_fwd_{l}")
        att = swa_fwd(proj, cos, sin, pm, attn_sinks[l], name=f"swa_fwd_{l}")
        ya, yb, mix = merge_fwd(og, att, proj, full["w_pa"], full["w_pb"], tm=tm, name=f"merge_fwd_{l}")
        x1 = matmul_residual(xs, mix, full["w_o"], tm=tmx, name=f"out_proj_{l}")
        tasks = carried_by("ffn_up", l)
        h2, gu, act, *got = ln_matmul(x1, norm2[l:l + 1], full["w_gu"], tm=min(256, T), tn=2 * H, name=f"ffn_up_{l}",
                                      swiglu=True, gather=[shards[k][j] for j, k in tasks])
        for (j, k), g in zip(tasks, got):
            blocks[j][k] = g
        x2 = matmul_residual(x1, act, full["w_down"], tm=tmx, name=f"ffn_down_{l}")
        saved.append((xs, h, proj, og, o_raw, states, att, ya, yb, mix, x1, h2, gu, act))
        xs = x2

    dx, loss_parts, dfin_parts = loss_head(xs, final_norm[None], loss_target[0], tm=tm, name="loss_head")

    arrived = [None] * DEPTH
    dn1, dn2, dlb, dgn, dsk = ([None] * DEPTH for _ in range(5))
    for l in reversed(range(DEPTH)):
        xs, h, proj, og, o_raw, states, att, ya, yb, mix, x1, h2, gu, act = saved[l]
        full = weights[l]
        dw = {}
        dgu = ffn_down_bwd(dx, full["w_down"], gu, tm=min(256, T), name=f"ffn_down_bwd_{l}")
        dw["w_down"] = matmul_tn(act, dx, tka=H // 2, tn=D_MODEL, tk=tkx, name=f"dw_down_{l}")
        dwgu = matmul_tn(h2, dgu, tka=D_MODEL, tn=H // 2, tk=tkx, name=f"dw_gu_{l}")
        dw["w_gate"], dw["w_up"] = dwgu[:, :H], dwgu[:, H:]
        dx1, dn2_parts = matmul_nt_rmsbwd(dgu, full["w_gu"], x1, norm2[l:l + 1], dx, tm=tmx, tk=H // 2, name=f"ffn_up_bwd_{l}")
        dya, dyb, dproj = merge_bwd(dx1, full["w_o"], proj, ya, yb, tm=tm, name=f"merge_bwd_{l}")
        dw["w_o"] = matmul_tn(mix, dx1, tka=D_MODEL, tn=D_MODEL, tk=tkx, name=f"dw_o_{l}")
        dog = matmul_nt(dya, full["w_pa"], tm=tmx, name=f"pa_bwd_{l}")
        datt = matmul_nt(dyb, full["w_pb"], tm=tmx, name=f"pb_bwd_{l}")
        dw["w_pa"] = matmul_tn(og, dya, tka=D_MODEL, tn=D_MODEL, tk=tkx, name=f"dw_pa_{l}")
        dw["w_pb"] = matmul_tn(att, dyb, tka=D_MODEL, tn=D_MODEL, tk=tkx, name=f"dw_pb_{l}")
        dproj, dsk_parts = swa_bwd(proj, cos, sin, pm, attn_sinks[l], att, datt, dproj, name=f"swa_bwd_{l}")
        dproj, dlb_parts, dgn_parts = hgrn_bwd(proj, lb_all[l:l + 1], hg_norm[l:l + 1], o_raw, states, dog, dproj,
                                               ct=min(256, T), name=f"hgrn_bwd_{l}")
        split = lambda k, g: _cols_split(g) if k in col_sharded else g.reshape(N_CHIPS, -1, g.shape[-1])
        dw["w_in"], *got_pair = matmul_tn(h, dproj, tka=D_MODEL, tn=1536, tk=tkx, name=f"dw_in_{l}",
                                          scatter=[split(k, dw[k]) for k in ffn_pair])
        dw["w_in"] = _from_kernel_cols(dw["w_in"])
        dx, dn1_parts, *got = matmul_nt_rmsbwd(dproj, full["w_in"], xs, norm1[l:l + 1], dx1, tm=tmx, tk=1536,
                                               name=f"in_proj_bwd_{l}", scatter=[split(k, dw[k]) for k in with_w_in])
        arrived[l] = {**dict(zip(with_w_in, got)), **dict(zip(ffn_pair, got_pair))}
        dn1[l], dn2[l], dlb[l], dgn[l] = dn1_parts.sum(0), dn2_parts.sum(0), dlb_parts.sum(0), dgn_parts.sum(0)
        dsk[l] = dsk_parts[:, 0, :].reshape(ATT_Q_HEADS, WINDOW).sum(-1)
    grad_x = dx[None]

    mine = _pack_small(jnp.stack(dn1), jnp.stack(dlb), jnp.stack(dgn), jnp.stack(dn2), dfin_parts.sum(0), jnp.stack(dsk),
                       loss_parts.sum())
    total = allreduce_small(mine, name="allreduce_small")
    g_n1, g_lb_all, g_gn, g_n2, g_fin, g_sk, loss = _unpack_small(total)
    (g_lb,) = lb_vjp(g_lb_all)
    zero = jnp.zeros((), F32)
    small = adamw(_pack_small(g_n1, g_lb, g_gn, g_n2, g_fin, g_sk, zero), None,
                  _pack_small(norm1, lb_logits, hg_norm, norm2, final_norm, attn_sinks, zero),
                  _pack_small(m_norm1, m_lb_logits, m_hg_norm, m_norm2, m_final_norm, m_attn_sinks, zero),
                  _pack_small(v_norm1, v_lb_logits, v_hg_norm, v_norm2, v_final_norm, v_attn_sinks, zero),
                  tr=SMALL_ROWS, name="adamw_small")
    small = [_unpack_small(s)[:6] for s in small]

    sums = []
    for k in big:
        a = jnp.stack([arrived[l][k] for l in range(DEPTH)], axis=1)
        q, nl, r, c = a.shape
        sums.append(sum_parts(a.reshape(q, nl * r, c), tr=128, name=f"sum_{k}"))
    theirs = swap_with_sibling(sums, name="swap_sums")
    mom = dict(w_in=(m_w_in, v_w_in), w_pa=(m_w_pa, v_w_pa), w_pb=(m_w_pb, v_w_pb), w_o=(m_w_o, v_w_o),
               w_gate=(m_w_gate, v_w_gate), w_up=(m_w_up, v_w_up), w_down=(m_w_down, v_w_down))
    upd = {}
    for k, mine_k, theirs_k in zip(big, sums, theirs):
        shp = big[k].shape
        flat = lambda t: t.reshape(shp[0] * shp[1], shp[2])
        res = adamw(mine_k, theirs_k, flat(big[k]), flat(mom[k][0]), flat(mom[k][1]), tr=128, name=f"adamw_{k}")
        upd[k] = [t.reshape(shp) for t in res]

    order = ("norm1", "w_in", "lb_logits", "hg_norm", "attn_sinks", "w_pa", "w_pb", "w_o", "norm2", "w_gate", "w_up",
             "w_down", "final_norm")
    small_pos = dict(norm1=0, lb_logits=1, hg_norm=2, norm2=3, final_norm=4, attn_sinks=5)
    outs = [loss, grad_x]
    for kind in range(4):
        for name in order:
            outs.append(upd[name][kind] if name in upd else small[kind][small_pos[name]])
    return tuple(outs)
```

```python
import functools

import jax
import jax.numpy as jnp
import numpy as np
from jax import lax
from jax.experimental import pallas as pl
from jax.experimental.pallas import tpu as pltpu

F32 = jnp.float32
BF16 = jnp.bfloat16

D_MODEL = 1024
DEPTH = 4
HG_HEADS = 8
HG_DK = 128
HG_WIDTH = HG_HEADS * HG_DK
CHUNK = 64
ATT_Q_HEADS = 16
ATT_KV_HEADS = 4
ATT_GROUP = ATT_Q_HEADS // ATT_KV_HEADS
HEAD_DIM = 64
ATT_WIDTH = ATT_Q_HEADS * HEAD_DIM
ATT_KV_WIDTH = ATT_KV_HEADS * HEAD_DIM
WINDOW = 128
ROPE_THETA = 500000.0
ROPE_DIM = HEAD_DIM // 4
FFN_HIDDEN = 2816
IN_COLS = 4 * HG_WIDTH + ATT_WIDTH + 2 * ATT_KV_WIDTH + 2 * D_MODEL
EPS = 1e-6
MIN_F = 1e-30
COL_HQ, COL_HF, COL_HI, COL_HG = 0, 8, 16, 24
COL_GA, COL_GB, COL_AQ, COL_AK, COL_AV = 32, 40, 48, 56, 58


def _to_kernel_cols(w):
    return jnp.concatenate([w[..., :4096], w[..., 5632:7680], w[..., 4096:5120], w[..., 5120:5632]], axis=-1)


def _from_kernel_cols(w):
    return jnp.concatenate([w[..., :4096], w[..., 6144:7168], w[..., 7168:7680], w[..., 4096:6144]], axis=-1)

ADAM_LR = 0.001
ADAM_B1 = 0.9
ADAM_B2 = 0.999
ADAM_EPS = 1e-08
ADAM_WD = 0.01
ADAM_STEP = 10

VMEM_LIMIT_V7X = 56 * 1024 * 1024
MESH = pl.DeviceIdType.MESH


def _cp(sem, vmem=VMEM_LIMIT_V7X):
    return pltpu.CompilerParams(dimension_semantics=sem, vmem_limit_bytes=vmem)


def _sigmoid(x):
    return 1.0 / (1.0 + jnp.exp(-x))


def _dot(a, b):
    return jnp.dot(a, b, preferred_element_type=F32)


def _dot_nt(a, b):
    return lax.dot_general(a, b, (((1,), (1,)), ((), ())), preferred_element_type=F32)


def _dot_tn(a, b):
    return lax.dot_general(a, b, (((0,), (0,)), ((), ())), preferred_element_type=F32)


def _rowsum8(v):
    r, n = v.shape
    return jnp.sum(v.reshape(r // 8, 8, n), axis=0)


_ANY = pl.BlockSpec(memory_space=pl.ANY)
N_CHIPS = 4
PEER_CHIPS = N_CHIPS - 1


def _place():
    x, y, c = lax.axis_index("x"), lax.axis_index("y"), lax.axis_index("c")
    return x, y, c, [(1 - x, y), (x, 1 - y), (1 - x, 1 - y)]


def _exchange_scratch(n):
    return [pltpu.SemaphoreType.DMA((PEER_CHIPS * n,)), pltpu.SemaphoreType.DMA((PEER_CHIPS * n,)), pltpu.SemaphoreType.DMA((n,))]


def _exchange(ins, outs, sems, scatter):
    send_sems, recv_sems, local_sems = sems
    x, y, c, chips = _place()
    me = 2 * x + y
    n = len(ins)

    def remote(a, j, arriving):
        px, py = chips[j]
        them = 2 * px + py
        src = ins[a].at[them] if scatter else ins[a]
        return pltpu.make_async_remote_copy(
            src_ref=src, dst_ref=outs[a].at[them if arriving else me], send_sem=send_sems.at[a * PEER_CHIPS + j],
            recv_sem=recv_sems.at[a * PEER_CHIPS + j], device_id=(px, py, c), device_id_type=MESH)

    def local(a):
        return pltpu.make_async_copy(ins[a].at[me] if scatter else ins[a], outs[a].at[me], local_sems.at[a])

    def start():
        for a in range(n):
            local(a).start()
            for j in range(PEER_CHIPS):
                remote(a, j, False).start()

    def wait():
        for a in range(n):
            for j in range(PEER_CHIPS):
                remote(a, j, True).wait()
            local(a).wait()

    return start, wait


def _exchange_out_shapes(arrays, scatter):
    return [jax.ShapeDtypeStruct(a.shape if scatter else (N_CHIPS,) + a.shape, a.dtype) for a in arrays]


def exchange_between_chips(arrays, *, scatter, name):
    n = len(arrays)

    def body(*refs):
        start, wait = _exchange(refs[:n], refs[n:2 * n], refs[2 * n:], scatter)
        start()
        wait()

    return pl.pallas_call(
        body, name=name, out_shape=_exchange_out_shapes(arrays, scatter), in_specs=[_ANY] * n, out_specs=[_ANY] * n,
        scratch_shapes=_exchange_scratch(n),
    )(*arrays)


def ln_matmul(x, g, w, *, tm, tn, name, swiglu=False, gather=()):
    T, Dm = x.shape
    N = w.shape[1]
    if swiglu:
        assert tn == N
    half = N // 2
    ng = len(gather)
    n_out = 3 if swiglu else 2
    ni, nj = T // tm, N // tn

    def body(x_ref, g_ref, w_ref, *rest):
        gather_in, rest = rest[:ng], rest[ng:]
        h_ref, o_ref = rest[0], rest[1]
        i, j = pl.program_id(0), pl.program_id(1)
        if ng:
            start, wait = _exchange(gather_in, rest[n_out:n_out + ng], rest[n_out + ng:], False)
            pl.when((i == 0) & (j == 0))(start)

        @pl.when(j == 0)
        def _():
            xf = x_ref[...]
            r = lax.rsqrt(jnp.mean(xf * xf, axis=-1, keepdims=True) + EPS)
            h_ref[...] = (xf * r * g_ref[...]).astype(BF16)

        acc = _dot(h_ref[...], w_ref[...])
        o_ref[...] = acc.astype(BF16)
        if swiglu:
            gt, up = acc[:, :half], acc[:, half:]
            rest[2][...] = (gt * _sigmoid(gt) * up).astype(BF16)
        if ng:
            pl.when((i == ni - 1) & (j == nj - 1))(wait)

    out_shape = [jax.ShapeDtypeStruct((T, Dm), BF16), jax.ShapeDtypeStruct((T, N), BF16)]
    out_specs = [pl.BlockSpec((tm, Dm), lambda i, j: (i, 0)), pl.BlockSpec((tm, tn), lambda i, j: (i, j))]
    if swiglu:
        out_shape.append(jax.ShapeDtypeStruct((T, half), BF16))
        out_specs.append(pl.BlockSpec((tm, half), lambda i, j: (i, 0)))
    return pl.pallas_call(
        body, name=name, grid=(ni, nj),
        in_specs=[pl.BlockSpec((tm, Dm), lambda i, j: (i, 0)), pl.BlockSpec((1, Dm), lambda i, j: (0, 0)),
                  pl.BlockSpec((Dm, tn), lambda i, j: (0, j))] + [_ANY] * ng,
        out_specs=out_specs + [_ANY] * ng, out_shape=out_shape + _exchange_out_shapes(gather, False),
        scratch_shapes=_exchange_scratch(ng) if ng else [],
        compiler_params=_cp(("arbitrary", "arbitrary") if ng else ("parallel", "arbitrary")),
    )(x, g, w, *gather)


def matmul_residual(x, a, w, *, tm, name):
    T, N = x.shape
    K = a.shape[1]

    def body(x_ref, a_ref, w_ref, o_ref):
        o_ref[...] = x_ref[...] + _dot(a_ref[...], w_ref[...])

    return pl.pallas_call(
        body, name=name, grid=(T // tm,),
        in_specs=[pl.BlockSpec((tm, N), lambda i: (i, 0)), pl.BlockSpec((tm, K), lambda i: (i, 0)),
                  pl.BlockSpec((K, N), lambda i: (0, 0))],
        out_specs=pl.BlockSpec((tm, N), lambda i: (i, 0)), out_shape=jax.ShapeDtypeStruct((T, N), F32),
        compiler_params=_cp(("parallel",)),
    )(x, a, w)


def matmul_nt(a, w, *, tm, name):
    T, K = a.shape
    N = w.shape[0]

    def body(a_ref, w_ref, o_ref):
        o_ref[...] = _dot_nt(a_ref[...], w_ref[...]).astype(BF16)

    return pl.pallas_call(
        body, name=name, grid=(T // tm,),
        in_specs=[pl.BlockSpec((tm, K), lambda i: (i, 0)), pl.BlockSpec((N, K), lambda i: (0, 0))],
        out_specs=pl.BlockSpec((tm, N), lambda i: (i, 0)), out_shape=jax.ShapeDtypeStruct((T, N), BF16),
        compiler_params=_cp(("parallel",)),
    )(a, w)


def matmul_tn(a, g, *, tka, tn, tk, name, scatter=()):
    T, Ka = a.shape
    N = g.shape[1]
    ni, nj, nk = Ka // tka, N // tn, T // tk
    ns = len(scatter)

    def body(a_ref, g_ref, *rest):
        scatter_in, rest = rest[:ns], rest[ns:]
        o_ref, acc_ref = rest[0], rest[1 + ns]
        i, j, k = pl.program_id(0), pl.program_id(1), pl.program_id(2)
        if ns:
            start, wait = _exchange(scatter_in, rest[1:1 + ns], rest[2 + ns:], True)
            pl.when((i == 0) & (j == 0) & (k == 0))(start)

        @pl.when(k == 0)
        def _():
            acc_ref[...] = jnp.zeros_like(acc_ref)

        acc_ref[...] += _dot_tn(a_ref[...].astype(BF16), g_ref[...].astype(BF16))

        @pl.when(k == nk - 1)
        def _():
            o_ref[...] = acc_ref[...].astype(BF16)

        if ns:
            pl.when((i == ni - 1) & (j == nj - 1) & (k == nk - 1))(wait)

    res = pl.pallas_call(
        body, name=name, grid=(ni, nj, nk),
        in_specs=[pl.BlockSpec((tk, tka), lambda i, j, k: (k, i)), pl.BlockSpec((tk, tn), lambda i, j, k: (k, j))] + [_ANY] * ns,
        out_specs=[pl.BlockSpec((tka, tn), lambda i, j, k: (i, j))] + [_ANY] * ns,
        out_shape=[jax.ShapeDtypeStruct((Ka, N), BF16)] + _exchange_out_shapes(scatter, True),
        scratch_shapes=[pltpu.VMEM((tka, tn), F32)] + (_exchange_scratch(ns) if ns else []),
        compiler_params=_cp(("arbitrary",) * 3 if ns else ("parallel", "parallel", "arbitrary")),
    )(a, g, *scatter)
    return res if ns else res[0]


def matmul_nt_rmsbwd(a, w, x, g, dres, *, tm, tk, name, scatter=()):
    T, K = a.shape
    Dm = w.shape[0]
    nk = K // tk
    ni = T // tm
    ns = len(scatter)

    def body(a_ref, w_ref, x_ref, g_ref, dres_ref, *rest):
        scatter_in, rest = rest[:ns], rest[ns:]
        dx_ref, dg_ref = rest[0], rest[1]
        acc_ref = rest[2 + ns]
        i, k = pl.program_id(0), pl.program_id(1)
        if ns:
            start, wait = _exchange(scatter_in, rest[2:2 + ns], rest[3 + ns:], True)
            pl.when((i == 0) & (k == 0))(start)

        @pl.when(k == 0)
        def _():
            acc_ref[...] = jnp.zeros_like(acc_ref)

        @pl.when((i == 0) & (k == 0))
        def _():
            dg_ref[...] = jnp.zeros_like(dg_ref)

        acc_ref[...] += _dot_nt(a_ref[...], w_ref[...])

        @pl.when(k == nk - 1)
        def _():
            dh = acc_ref[...]
            xf = x_ref[...]
            r = lax.rsqrt(jnp.mean(xf * xf, axis=-1, keepdims=True) + EPS)
            xhat = xf * r
            dg_ref[...] += _rowsum8(dh * xhat)
            dxhat = dh * g_ref[...]
            dx_ref[...] = dres_ref[...] + r * (dxhat - xhat * jnp.mean(dxhat * xhat, axis=-1, keepdims=True))

        if ns:
            pl.when((i == ni - 1) & (k == nk - 1))(wait)

    return pl.pallas_call(
        body, name=name, grid=(ni, nk),
        in_specs=[pl.BlockSpec((tm, tk), lambda i, k: (i, k)), pl.BlockSpec((Dm, tk), lambda i, k: (0, k)),
                  pl.BlockSpec((tm, Dm), lambda i, k: (i, 0)), pl.BlockSpec((1, Dm), lambda i, k: (0, 0)),
                  pl.BlockSpec((tm, Dm), lambda i, k: (i, 0))] + [_ANY] * ns,
        out_specs=[pl.BlockSpec((tm, Dm), lambda i, k: (i, 0)), pl.BlockSpec((8, Dm), lambda i, k: (0, 0))] + [_ANY] * ns,
        out_shape=[jax.ShapeDtypeStruct((T, Dm), F32), jax.ShapeDtypeStruct((8, Dm), F32)] + _exchange_out_shapes(scatter, True),
        scratch_shapes=[pltpu.VMEM((tm, Dm), F32)] + (_exchange_scratch(ns) if ns else []),
        compiler_params=_cp(("arbitrary", "arbitrary")),
    )(a, w, x, g, dres, *scatter)


def ffn_down_bwd(dx, w_down, gu, *, tm, name):
    T, Dm = dx.shape
    H = w_down.shape[0]

    def body(dx_ref, w_ref, gu_ref, o_ref):
        dact = _dot_nt(dx_ref[...].astype(BF16), w_ref[...])
        gt = gu_ref[:, :H].astype(F32)
        up = gu_ref[:, H:].astype(F32)
        s = _sigmoid(gt)
        o_ref[:, :H] = (dact * up * (s * (1.0 + gt * (1.0 - s)))).astype(BF16)
        o_ref[:, H:] = (dact * gt * s).astype(BF16)

    return pl.pallas_call(
        body, name=name, grid=(T // tm,),
        in_specs=[pl.BlockSpec((tm, Dm), lambda i: (i, 0)), pl.BlockSpec((H, Dm), lambda i: (0, 0)),
                  pl.BlockSpec((tm, 2 * H), lambda i: (i, 0))],
        out_specs=pl.BlockSpec((tm, 2 * H), lambda i: (i, 0)), out_shape=jax.ShapeDtypeStruct((T, 2 * H), BF16),
        compiler_params=_cp(("parallel",)),
    )(dx, w_down, gu)


def _gate_specs(tm):
    return [pl.BlockSpec((tm, D_MODEL), lambda i, c=c: (i, c)) for c in (COL_GA // 8, COL_GB // 8)]


def merge_fwd(og, att, proj, w_pa, w_pb, *, tm, name):
    T, Dm = og.shape

    def body(og_ref, att_ref, ga_ref, gb_ref, wa_ref, wb_ref, ya_ref, yb_ref, mix_ref):
        ya = _dot(og_ref[...], wa_ref[...])
        yb = _dot(att_ref[...], wb_ref[...])
        ya_ref[...] = ya.astype(BF16)
        yb_ref[...] = yb.astype(BF16)
        mix_ref[...] = (_sigmoid(ga_ref[...].astype(F32)) * ya + _sigmoid(gb_ref[...].astype(F32)) * yb).astype(BF16)

    row = pl.BlockSpec((tm, Dm), lambda i: (i, 0))
    wsp = pl.BlockSpec((Dm, Dm), lambda i: (0, 0))
    return pl.pallas_call(
        body, name=name, grid=(T // tm,),
        in_specs=[row, row] + _gate_specs(tm) + [wsp, wsp],
        out_specs=[row, row, row], out_shape=[jax.ShapeDtypeStruct((T, Dm), BF16)] * 3,
        compiler_params=_cp(("parallel",)),
    )(og, att, proj, proj, w_pa, w_pb)


def merge_bwd(dx, w_o, proj, ya, yb, *, tm, name):
    T, Dm = dx.shape

    def body(dx_ref, w_ref, ga_ref, gb_ref, ya_ref, yb_ref, dya_ref, dyb_ref, dgate_ref):
        dmix = _dot_nt(dx_ref[...].astype(BF16), w_ref[...])
        sa = _sigmoid(ga_ref[...].astype(F32))
        sb = _sigmoid(gb_ref[...].astype(F32))
        dya_ref[...] = (dmix * sa).astype(BF16)
        dyb_ref[...] = (dmix * sb).astype(BF16)
        dgate_ref[:, :Dm] = (dmix * ya_ref[...].astype(F32) * sa * (1.0 - sa)).astype(BF16)
        dgate_ref[:, Dm:] = (dmix * yb_ref[...].astype(F32) * sb * (1.0 - sb)).astype(BF16)

    row = pl.BlockSpec((tm, Dm), lambda i: (i, 0))
    return pl.pallas_call(
        body, name=name, grid=(T // tm,),
        in_specs=[row, pl.BlockSpec((Dm, Dm), lambda i: (0, 0))] + _gate_specs(tm) + [row, row],
        out_specs=[row, row, pl.BlockSpec((tm, 2 * Dm), lambda i: (i, COL_GA // 16))],
        out_shape=[jax.ShapeDtypeStruct((T, Dm), BF16)] * 2 + [jax.ShapeDtypeStruct((T, IN_COLS), BF16)],
        compiler_params=_cp(("parallel",)),
    )(dx, w_o, proj, proj, ya, yb)


def loss_head(x, g, target, *, tm, name):
    T, Dm = x.shape

    def body(x_ref, g_ref, t_ref, dx_ref, l_ref, dg_ref):
        @pl.when(pl.program_id(0) == 0)
        def _():
            l_ref[...] = jnp.zeros_like(l_ref)
            dg_ref[...] = jnp.zeros_like(dg_ref)

        xf = x_ref[...]
        gv = g_ref[...]
        r = lax.rsqrt(jnp.mean(xf * xf, axis=-1, keepdims=True) + EPS)
        xhat = xf * r
        err = xhat * gv - t_ref[...]
        l_ref[...] += _rowsum8(err * err) * (0.5 / Dm)
        dy = err * (1.0 / Dm)
        dg_ref[...] += _rowsum8(dy * xhat)
        dxhat = dy * gv
        dx_ref[...] = r * (dxhat - xhat * jnp.mean(dxhat * xhat, axis=-1, keepdims=True))

    row = pl.BlockSpec((tm, Dm), lambda i: (i, 0))
    acc = pl.BlockSpec((8, Dm), lambda i: (0, 0))
    return pl.pallas_call(
        body, name=name, grid=(T // tm,),
        in_specs=[row, pl.BlockSpec((1, Dm), lambda i: (0, 0)), row],
        out_specs=[row, acc, acc],
        out_shape=[jax.ShapeDtypeStruct((T, Dm), F32), jax.ShapeDtypeStruct((8, Dm), F32), jax.ShapeDtypeStruct((8, Dm), F32)],
        compiler_params=_cp(("arbitrary",)),
    )(x, g, target)


_LEVELS = (32, 16, 8)
_DIAG = 8
_SUBLANES = 8
_HI = lax.Precision.HIGHEST


def _chunk_consts():
    t = lax.broadcasted_iota(jnp.int32, (CHUNK, CHUNK), 0)
    s = lax.broadcasted_iota(jnp.int32, (CHUNK, CHUNK), 1)
    tri = (s <= t).astype(F32)
    tri_t = (s >= t).astype(F32)
    masks = []
    for m in _LEVELS:
        sh = int(np.log2(2 * m))
        masks.append(((t >> sh) == (s >> sh)) & ((t & (2 * m - 1)) >= m) & ((s & (2 * m - 1)) < m))
    return tri, tri_t, masks


def _exact_dot(sel, x):
    hi = x.astype(BF16)
    r1 = x - hi.astype(F32)
    mid = r1.astype(BF16)
    lo = (r1 - mid.astype(F32)).astype(BF16)
    return _dot(jnp.concatenate([sel, sel, sel], axis=1).astype(BF16), jnp.concatenate([hi, mid, lo], axis=0))


def _level_ref(b, m):
    if 2 * m >= _SUBLANES:
        pieces = [jnp.broadcast_to(b[p * 2 * m + m - 1:p * 2 * m + m, :], (2 * m, b.shape[1])) for p in range(CHUNK // (2 * m))]
        return pieces[0] if len(pieces) == 1 else jnp.concatenate(pieces, axis=0)
    groups = CHUNK // _SUBLANES
    b3 = b.reshape(groups, _SUBLANES, b.shape[1])
    row = lax.broadcasted_iota(jnp.int32, b3.shape, 1)
    ref = None
    for p in reversed(range(_SUBLANES // (2 * m))):
        src = jnp.broadcast_to(b3[:, p * 2 * m + m - 1:p * 2 * m + m, :], b3.shape)
        ref = src if ref is None else jnp.where(row < (p + 1) * 2 * m, src, ref)
    return ref.reshape(b.shape)


def _intra_off(q, kk, b, masks):
    a = jnp.zeros((CHUNK, CHUNK), F32)
    keep = []
    for m, mask in zip(_LEVELS, masks):
        e = jnp.exp2(-jnp.abs(b - _level_ref(b, m)))
        qt = (q * e).astype(BF16)
        kt = (kk * e).astype(BF16)
        a = a + jnp.where(mask, _dot_nt(qt, kt), 0.0)
        keep.append((mask, e, qt, kt))
    return a, keep


def _roll_rows(x, d):
    g = x.shape[0] // _DIAG
    return pltpu.roll(x.reshape(g, _DIAG, x.shape[1]), d % _DIAG, 1).reshape(x.shape)


def _diag_fwd(q, kk, v, b):
    lower = lax.broadcasted_iota(jnp.int32, q.shape, 0) & (_DIAG - 1)
    out = jnp.sum(q * kk, axis=-1, keepdims=True) * v
    for d in range(1, _DIAG):
        dec = jnp.exp2(jnp.where(lower >= d, b - _roll_rows(b, d), _NEG))
        out = out + jnp.sum(q * dec * _roll_rows(kk, d), axis=-1, keepdims=True) * _roll_rows(v, d)
    return out


def _diag_bwd(q, kk, v, b, do):
    lower = lax.broadcasted_iota(jnp.int32, q.shape, 0) & (_DIAG - 1)
    a = jnp.sum(q * kk, axis=-1, keepdims=True)
    da = jnp.sum(do * v, axis=-1, keepdims=True)
    dq, dk, dv = da * kk, da * q, a * do
    for d in range(1, _DIAG):
        kr = _roll_rows(kk, d)
        dec = jnp.exp2(jnp.where(lower >= d, b - _roll_rows(b, d), _NEG))
        qd = q * dec
        a = jnp.sum(qd * kr, axis=-1, keepdims=True)
        da = jnp.sum(do * _roll_rows(v, d), axis=-1, keepdims=True)
        dq = dq + da * dec * kr
        dk = dk + _roll_rows(da * qd, -d)
        dv = dv + _roll_rows(a * do, -d)
    return dq, dk, dv


def _hgrn_in_specs(ct, hp, order):
    def spec(col):
        return pl.BlockSpec((ct, hp * HG_DK), lambda h, i, col=col: (order(i), col // hp + h))
    vec = pl.BlockSpec((1, hp * HG_DK), lambda h, i: (0, h))
    return [spec(COL_HQ), spec(COL_HF), spec(COL_HI), spec(COL_HG), vec, vec]


def hgrn_fwd(proj, lb, gn, *, ct, hp, name):
    T = proj.shape[0]
    nc = ct // CHUNK

    def body(hq_ref, hf_ref, hi_ref, hg_ref, lb_ref, gn_ref, og_ref, o_ref, st_ref, s_ref):
        @pl.when(pl.program_id(1) == 0)
        def _():
            s_ref[...] = jnp.zeros_like(s_ref)

        tri, _, masks = _chunk_consts()

        def chunk(c, carry):
            sl = pl.ds(pl.multiple_of(c * CHUNK, CHUNK), CHUNK)
            lb_all = lb_ref[...]
            f_all = lb_all + (1.0 - lb_all) * _sigmoid(hf_ref[sl, :].astype(F32))
            b_all = _exact_dot(tri, jnp.log2(jnp.maximum(f_all, MIN_F)))
            for p in range(hp):
                cols = slice(p * HG_DK, (p + 1) * HG_DK)
                hq = hq_ref[sl, cols].astype(F32)
                v = hi_ref[sl, cols].astype(F32)
                gate = hg_ref[sl, cols].astype(F32)
                q = hq * _sigmoid(hq)
                f = f_all[:, cols]
                kk = 1.0 - f
                b = b_all[:, cols]
                st = s_ref[p]
                st_ref[p, c] = st
                a, _ = _intra_off(q, kk, b, masks)
                vb = v.astype(BF16)
                o = (_dot_nt((q * jnp.exp2(b)).astype(BF16), st.astype(BF16)) + _dot(a.astype(BF16), vb)
                     + _diag_fwd(q, kk, v, b))
                bl = b[CHUNK - 1:CHUNK, :]
                s_ref[p] = st * jnp.exp2(bl) + _dot_tn(vb, (kk * jnp.exp2(bl - b)).astype(BF16))
                o_ref[sl, cols] = o
                r = lax.rsqrt(jnp.mean(o * o, axis=-1, keepdims=True) + EPS)
                og_ref[sl, cols] = (o * r * gn_ref[:, cols] * (gate * _sigmoid(gate))).astype(BF16)
            return carry

        lax.fori_loop(0, nc, chunk, 0)

    blk = pl.BlockSpec((ct, hp * HG_DK), lambda h, i: (i, h))
    return pl.pallas_call(
        body, name=name, grid=(HG_HEADS // hp, T // ct),
        in_specs=_hgrn_in_specs(ct, hp, lambda i: i),
        out_specs=[blk, blk, pl.BlockSpec((hp, nc, HG_DK, HG_DK), lambda h, i: (h, i, 0, 0))],
        out_shape=[jax.ShapeDtypeStruct((T, HG_WIDTH), BF16), jax.ShapeDtypeStruct((T, HG_WIDTH), F32),
                   jax.ShapeDtypeStruct((HG_HEADS, T // CHUNK, HG_DK, HG_DK), F32)],
        scratch_shapes=[pltpu.VMEM((hp, HG_DK, HG_DK), F32)],
        compiler_params=_cp(("parallel", "arbitrary")),
    )(proj, proj, proj, proj, lb, gn)


def hgrn_bwd(proj, lb, gn, o_raw, states, dog, dproj, *, ct, name):
    T = proj.shape[0]
    nc = ct // CHUNK
    nblk = T // ct
    hp = HG_HEADS

    def body(hq_ref, hf_ref, hi_ref, hg_ref, lb_ref, gn_ref, o_ref, st_ref, dog_ref, _dproj_in,
             dp_ref, dlb_ref, dgn_ref, ds_ref):
        dhq_ref, dhf_ref, dhi_ref, dhg_ref = (dp_ref.at[:, s * HG_WIDTH:(s + 1) * HG_WIDTH] for s in range(4))
        @pl.when(pl.program_id(1) == 0)
        def _():
            ds_ref[...] = jnp.zeros_like(ds_ref)
            dlb_ref[...] = jnp.zeros_like(dlb_ref)
            dgn_ref[...] = jnp.zeros_like(dgn_ref)

        tri, tri_t, masks = _chunk_consts()
        last_row = lax.broadcasted_iota(jnp.int32, (CHUNK, HG_DK), 0) == CHUNK - 1

        def chunk(ci, carry):
            c = nc - 1 - ci
            sl = pl.ds(pl.multiple_of(c * CHUNK, CHUNK), CHUNK)
            lb_all = lb_ref[...]
            sig_all = _sigmoid(hf_ref[sl, :].astype(F32))
            f_all = lb_all + (1.0 - lb_all) * sig_all
            b_all = _exact_dot(tri, jnp.log2(jnp.maximum(f_all, MIN_F)))
            dbs, dks = [], []
            for p in range(hp):
                cols = slice(p * HG_DK, (p + 1) * HG_DK)
                gn_v = gn_ref[:, cols]
                hq = hq_ref[sl, cols].astype(F32)
                v = hi_ref[sl, cols].astype(F32)
                gate = hg_ref[sl, cols].astype(F32)
                sq = _sigmoid(hq)
                q = hq * sq
                f = f_all[:, cols]
                kk = 1.0 - f
                b = b_all[:, cols]
                st = st_ref[p, c]
                dst = ds_ref[p]
                o = o_ref[sl, cols]
                r = lax.rsqrt(jnp.mean(o * o, axis=-1, keepdims=True) + EPS)
                xhat = o * r
                sg = _sigmoid(gate)
                dog_v = dog_ref[sl, cols].astype(F32)
                dy = dog_v * (gate * sg)
                dhg_ref[sl, cols] = (dog_v * xhat * gn_v * (sg * (1.0 + gate * (1.0 - sg)))).astype(BF16)
                dgn_ref[:, cols] += _rowsum8(dy * xhat)
                dxh = dy * gn_v
                do = r * (dxh - xhat * jnp.mean(dxh * xhat, axis=-1, keepdims=True))
                dob = do.astype(BF16)
                vb = v.astype(BF16)
                stb = st.astype(BF16)
                dstb = dst.astype(BF16)
                eb = jnp.exp2(b)
                bl = b[CHUNK - 1:CHUNK, :]
                ebl = jnp.exp2(bl)
                edec = jnp.exp2(bl - b)
                qe = q * eb
                kdec = kk * edec
                a, keep = _intra_off(q, kk, b, masks)
                da = _dot_nt(dob, vb)
                dkdec = _dot(vb, dstb)
                dq = _dot(dob, stb) * eb
                dk = dkdec * edec
                dv = _dot_tn(a.astype(BF16), dob) + _dot_nt(kdec.astype(BF16), dstb)
                for mask, e, qt, kt in keep:
                    dsm = jnp.where(mask, da, 0.0).astype(BF16)
                    dq = dq + _dot(dsm, kt) * e
                    dk = dk + _dot_tn(dsm, qt) * e
                ddq, ddk, ddv = _diag_bwd(q, kk, v, b, do)
                dq = dq + ddq
                dk = dk + ddk
                dhi_ref[sl, cols] = (dv + ddv).astype(BF16)
                dbl = jnp.sum(dkdec * kdec, axis=0, keepdims=True) + ebl * jnp.sum(st * dst, axis=0, keepdims=True)
                db = q * dq - kk * dk
                dbs.append(jnp.where(last_row, db + dbl, db))
                dks.append(dk)
                ds_ref[p] = _dot_tn(dob, qe.astype(BF16)) + dst * ebl
                dhq_ref[sl, cols] = (dq * (sq * (1.0 + hq * (1.0 - sq)))).astype(BF16)
            dlogf = _exact_dot(tri_t, jnp.concatenate(dbs, axis=1))
            df = jnp.where(f_all > MIN_F, dlogf / f_all, 0.0) - jnp.concatenate(dks, axis=1)
            dhf_ref[sl, :] = (df * (1.0 - lb_all) * sig_all * (1.0 - sig_all)).astype(BF16)
            dlb_ref[...] += _rowsum8(df * (1.0 - sig_all))
            return carry

        lax.fori_loop(0, nc, chunk, 0)

    rev = lambda i: nblk - 1 - i
    blk = pl.BlockSpec((ct, hp * HG_DK), lambda h, i: (rev(i), h))
    acc = pl.BlockSpec((8, hp * HG_DK), lambda h, i: (0, h))
    return pl.pallas_call(
        body, name=name, grid=(HG_HEADS // hp, nblk),
        in_specs=_hgrn_in_specs(ct, hp, rev) + [blk, pl.BlockSpec((hp, nc, HG_DK, HG_DK), lambda h, i: (h, rev(i), 0, 0)), blk, _ANY],
        out_specs=[pl.BlockSpec((ct, 4 * HG_WIDTH), lambda h, i: (rev(i), 0)), acc, acc],
        out_shape=[jax.ShapeDtypeStruct(dproj.shape, dproj.dtype)] + [jax.ShapeDtypeStruct((8, HG_WIDTH), F32)] * 2,
        scratch_shapes=[pltpu.VMEM((hp, HG_DK, HG_DK), F32)],
        input_output_aliases={9: 0},
        compiler_params=_cp(("parallel", "arbitrary")),
    )(proj, proj, proj, proj, lb, gn, o_raw, states, dog, dproj)


_GW = ATT_GROUP * HEAD_DIM
_NEG = -1e30


def rope_tables(T):
    half = ROPE_DIM // 2
    inv = ROPE_THETA ** (-jnp.arange(half, dtype=F32) * 2.0 / ROPE_DIM)
    ang = jnp.arange(T, dtype=F32)[:, None] * inv[None, :]
    pad1 = jnp.ones((T, HEAD_DIM - ROPE_DIM), F32)
    cos = jnp.concatenate([jnp.cos(ang), jnp.cos(ang), pad1], axis=1)
    sin = jnp.concatenate([jnp.sin(ang), jnp.sin(ang), 0.0 * pad1], axis=1)
    p = np.zeros((_GW, _GW), np.float32)
    for base in range(0, _GW, HEAD_DIM):
        for i in range(half):
            p[base + i + half, base + i] = -1.0
            p[base + i, base + i + half] = 1.0
    reps = _GW // HEAD_DIM
    return jnp.tile(cos, (1, reps)), jnp.tile(sin, (1, reps)), jnp.asarray(p, BF16)


def _rope(x, cos, sin, pm):
    return x.astype(F32) * cos + _dot(x, pm) * sin


def _unrope(dx, cos, sin, pm):
    return dx * cos + _dot_nt((dx * sin).astype(BF16), pm)


_GQ = ATT_GROUP * WINDOW


def _swa_mask(n, queries_on_rows):
    shape = (_GQ, 2 * WINDOW) if queries_on_rows else (2 * WINDOW, _GQ)
    r = lax.broadcasted_iota(jnp.int32, shape, 0 if queries_on_rows else 1) & (WINDOW - 1)
    c = lax.broadcasted_iota(jnp.int32, shape, 1 if queries_on_rows else 0)
    delta = r + WINDOW - c
    return (delta >= 0) & (delta < WINDOW) & ((c >= WINDOW) | (n > 0))


def _sink_vector(sink_ref, g, queries_on_rows):
    shape = (_GQ, 1) if queries_on_rows else (1, _GQ)
    i = lax.broadcasted_iota(jnp.int32, shape, 0 if queries_on_rows else 1)
    out = jnp.full(shape, sink_ref[g * ATT_GROUP + ATT_GROUP - 1], F32)
    for hh in range(ATT_GROUP - 2, -1, -1):
        out = jnp.where(i < (hh + 1) * WINDOW, sink_ref[g * ATT_GROUP + hh], out)
    return out


def _head(x, h):
    return x[:, h * HEAD_DIM:(h + 1) * HEAD_DIM]


def _stack_heads(x):
    return jnp.concatenate([_head(x, hh) for hh in range(ATT_GROUP)], axis=0)


def _unstack_heads(x):
    return jnp.concatenate([x[hh * WINDOW:(hh + 1) * WINDOW] for hh in range(ATT_GROUP)], axis=1)


def _swa_specs(cur):
    prev = lambda n: jnp.maximum(n - 1, 0)
    kv = lambda col, f: pl.BlockSpec((WINDOW, _GW), lambda n: (f(n), col))
    tab = lambda f: pl.BlockSpec((WINDOW, _GW), lambda n: (f(n), 0))
    return [pl.BlockSpec((WINDOW, ATT_WIDTH), lambda n: (cur(n), COL_AQ // 8)),
            kv(COL_AK // 2, prev), kv(COL_AK // 2, cur), kv(COL_AV // 2, prev), kv(COL_AV // 2, cur),
            tab(prev), tab(cur), tab(prev), tab(cur),
            pl.BlockSpec((_GW, _GW), lambda n: (0, 0)), pl.BlockSpec(memory_space=pltpu.SMEM)]


def swa_fwd(proj, cos, sin, pm, sinks, *, name):
    T = proj.shape[0]
    nb = T // WINDOW
    scale = HEAD_DIM ** -0.5

    def body(q_ref, kp_ref, kc_ref, vp_ref, vc_ref, cp_ref, cc_ref, sp_ref, sc_ref, pm_ref, sink_ref, o_ref):
        n = pl.program_id(0)
        pm_v = pm_ref[...]
        mask = _swa_mask(n, True)
        k_cat = jnp.concatenate([_rope(kp_ref[...], cp_ref[...], sp_ref[...], pm_v),
                                 _rope(kc_ref[...], cc_ref[...], sc_ref[...], pm_v)], axis=0).astype(BF16)
        v_cat = jnp.concatenate([vp_ref[...], vc_ref[...]], axis=0)
        for g in range(ATT_KV_HEADS):
            qs = _stack_heads((_rope(q_ref[:, g * _GW:(g + 1) * _GW], cc_ref[...], sc_ref[...], pm_v) * scale).astype(BF16))
            sk = _sink_vector(sink_ref, g, True)
            s = jnp.where(mask, _dot_nt(qs, _head(k_cat, g)), _NEG)
            m = jnp.maximum(jnp.max(s, axis=-1, keepdims=True), sk)
            p = jnp.exp(s - m)
            l = jnp.sum(p, axis=-1, keepdims=True) + jnp.exp(sk - m)
            o = _dot(p.astype(BF16), _head(v_cat, g)) / l
            o_ref[:, g * _GW:(g + 1) * _GW] = _unstack_heads(o).astype(BF16)

    return pl.pallas_call(
        body, name=name, grid=(nb,), in_specs=_swa_specs(lambda n: n),
        out_specs=pl.BlockSpec((WINDOW, ATT_WIDTH), lambda n: (n, 0)),
        out_shape=jax.ShapeDtypeStruct((T, ATT_WIDTH), BF16), compiler_params=_cp(("parallel",)),
    )(proj, proj, proj, proj, proj, cos, cos, sin, sin, pm, sinks)


def swa_bwd(proj, cos, sin, pm, sinks, att, datt, dproj, *, name):
    T = proj.shape[0]
    nb = T // WINDOW
    scale = HEAD_DIM ** -0.5
    clamp = lambda n: jnp.minimum(n, nb - 1)

    def body(q_ref, kp_ref, kc_ref, vp_ref, vc_ref, cp_ref, cc_ref, sp_ref, sc_ref, pm_ref, sink_ref, att_ref, datt_ref,
             _dproj_in, dp_ref, dsink_ref, cq_ref, ck_ref, cv_ref):
        n = pl.program_id(0)
        pm_v = pm_ref[...]

        @pl.when(n == 0)
        def _():
            dsink_ref[...] = jnp.zeros_like(dsink_ref)
            cq_ref[...] = jnp.zeros_like(cq_ref)
            ck_ref[...] = jnp.zeros_like(ck_ref)
            cv_ref[...] = jnp.zeros_like(cv_ref)

        def write_prev(dk_prev, dv_prev):
            dp_ref[:, :ATT_WIDTH] = cq_ref[...]
            dk = _unrope(ck_ref[...] + dk_prev, cp_ref[...], sp_ref[...], pm_v)
            dp_ref[:, ATT_WIDTH:ATT_WIDTH + _GW] = dk.astype(BF16)
            dp_ref[:, ATT_WIDTH + _GW:] = (cv_ref[...] + dv_prev).astype(BF16)

        @pl.when(n < nb)
        def _():
            mask = _swa_mask(n, False)
            ones = jnp.ones((8, 2 * HEAD_DIM), BF16)
            k_cat = jnp.concatenate([_rope(kp_ref[...], cp_ref[...], sp_ref[...], pm_v),
                                     _rope(kc_ref[...], cc_ref[...], sc_ref[...], pm_v)], axis=0).astype(BF16)
            v_cat = jnp.concatenate([vp_ref[...], vc_ref[...]], axis=0)
            dqs, dks, dvs = [], [], []
            for g in range(ATT_KV_HEADS):
                gc = slice(g * _GW, (g + 1) * _GW)
                qs = _stack_heads((_rope(q_ref[:, gc], cc_ref[...], sc_ref[...], pm_v) * scale).astype(BF16))
                dos = _stack_heads(datt_ref[:, gc])
                os_ = _stack_heads(att_ref[:, gc])
                kh, vh = _head(k_cat, g), _head(v_cat, g)
                sk = _sink_vector(sink_ref, g, False)
                s = jnp.where(mask, _dot_nt(kh, qs), _NEG)
                m = jnp.maximum(jnp.max(s, axis=0, keepdims=True), sk)
                e = jnp.exp(s - m)
                es = jnp.exp(sk - m)
                inv_l = 1.0 / (jnp.sum(e, axis=0, keepdims=True) + es)
                p = e * inv_l
                prod = dos.astype(F32) * os_.astype(F32)
                prod_hi = prod.astype(BF16)
                prod_lo = (prod - prod_hi.astype(F32)).astype(BF16)
                dsum = _dot_nt(ones, jnp.concatenate([prod_hi, prod_lo], axis=1))[0:1]
                dsink_ref[g, 0:1, :] += -(es * inv_l) * dsum
                ds = (p * (_dot_nt(vh, dos) - dsum)).astype(BF16)
                dqs.append(_unrope(_unstack_heads(_dot_tn(ds, kh) * scale), cc_ref[...], sc_ref[...], pm_v))
                dks.append(_dot(ds, qs))
                dvs.append(_dot(p.astype(BF16), dos))
            dk_all = jnp.concatenate(dks, axis=1)
            dv_all = jnp.concatenate(dvs, axis=1)

            @pl.when(n > 0)
            def _():
                write_prev(dk_all[:WINDOW], dv_all[:WINDOW])

            cq_ref[...] = jnp.concatenate(dqs, axis=1).astype(BF16)
            ck_ref[...] = dk_all[WINDOW:]
            cv_ref[...] = dv_all[WINDOW:]

        @pl.when(n == nb)
        def _():
            write_prev(jnp.zeros((WINDOW, _GW), F32), jnp.zeros((WINDOW, _GW), F32))

    row = pl.BlockSpec((WINDOW, ATT_WIDTH), lambda n: (clamp(n), 0))
    slab = ATT_WIDTH + 2 * _GW
    return pl.pallas_call(
        body, name=name, grid=(nb + 1,), in_specs=_swa_specs(clamp) + [row, row, _ANY],
        out_specs=[pl.BlockSpec((WINDOW, slab), lambda n: (jnp.maximum(n - 1, 0), COL_AQ * HG_DK // slab)),
                   pl.BlockSpec((ATT_KV_HEADS, 8, _GQ), lambda n: (0, 0, 0))],
        out_shape=[jax.ShapeDtypeStruct(dproj.shape, dproj.dtype), jax.ShapeDtypeStruct((ATT_KV_HEADS, 8, _GQ), F32)],
        scratch_shapes=[pltpu.VMEM((WINDOW, ATT_WIDTH), BF16), pltpu.VMEM((WINDOW, _GW), F32), pltpu.VMEM((WINDOW, _GW), F32)],
        input_output_aliases={13: 0},
        compiler_params=_cp(("arbitrary",)),
    )(proj, proj, proj, proj, proj, cos, cos, sin, sin, pm, sinks, att, datt, dproj)


def swap_with_sibling(arrays, *, name):
    n = len(arrays)

    def body(*refs):
        ins, outs = refs[:n], refs[n:2 * n]
        send_sems, recv_sems = refs[2 * n:]
        x, y, c, _ = _place()
        copies = [pltpu.make_async_remote_copy(src_ref=ins[a], dst_ref=outs[a], send_sem=send_sems.at[a],
                                               recv_sem=recv_sems.at[a], device_id=(x, y, 1 - c), device_id_type=MESH)
                  for a in range(n)]
        for cp in copies:
            cp.start()
        for cp in copies:
            cp.wait()

    return pl.pallas_call(
        body, name=name, out_shape=[jax.ShapeDtypeStruct(a.shape, a.dtype) for a in arrays],
        in_specs=[_ANY] * n, out_specs=[_ANY] * n,
        scratch_shapes=[pltpu.SemaphoreType.DMA((n,)), pltpu.SemaphoreType.DMA((n,))],
    )(*arrays)


N_DEV = 8
SMALL_ROWS = 24


def allreduce_small(v, *, name):
    def body(v_ref, o_ref, recv_ref, send_sems, recv_sems):
        x, y, c, _ = _place()
        me = 4 * x + 2 * y + c
        recv_ref[me] = v_ref[...]

        def copy(k, slot):
            peer = (x ^ (k >> 2), y ^ ((k >> 1) & 1), c ^ (k & 1))
            return pltpu.make_async_remote_copy(src_ref=v_ref, dst_ref=recv_ref.at[slot], send_sem=send_sems.at[k - 1],
                                                recv_sem=recv_sems.at[k - 1], device_id=peer, device_id_type=MESH)

        for k in range(1, N_DEV):
            copy(k, me).start()
        for k in range(1, N_DEV):
            copy(k, me ^ k).wait()
        acc = recv_ref[0]
        for d in range(1, N_DEV):
            acc = acc + recv_ref[d]
        o_ref[...] = acc

    vm = pl.BlockSpec(memory_space=pltpu.VMEM)
    return pl.pallas_call(
        body, name=name, out_shape=jax.ShapeDtypeStruct(v.shape, v.dtype), in_specs=[vm], out_specs=vm,
        scratch_shapes=[pltpu.VMEM((N_DEV,) + v.shape, v.dtype), pltpu.SemaphoreType.DMA((N_DEV - 1,)),
                        pltpu.SemaphoreType.DMA((N_DEV - 1,))],
    )(v)


def sum_parts(layers, *, tr, name):
    nl = len(layers)
    _, R, C = layers[0].shape
    nr = R // tr

    def body(*refs):
        o_ref = refs[nl]
        for j in range(nl):
            @pl.when(pl.program_id(0) == j)
            def _(p_ref=refs[j]):
                acc = p_ref[0].astype(F32)
                for q in range(1, N_CHIPS):
                    acc = acc + p_ref[q].astype(F32)
                o_ref[...] = acc

    return pl.pallas_call(
        body, name=name, grid=(nl, nr),
        in_specs=[pl.BlockSpec((N_CHIPS, tr, C), lambda l, i, j=j: (0, jnp.where(l == j, i, 0), 0)) for j in range(nl)],
        out_specs=pl.BlockSpec((tr, C), lambda l, i: (l * nr + i, 0)), out_shape=jax.ShapeDtypeStruct((nl * R, C), F32),
        compiler_params=_cp(("arbitrary", "arbitrary")),
    )(*layers)


def adamw(g_a, g_b, w, m, v, *, tr, name):
    R, C = w.shape
    c1 = 1.0 - ADAM_B1 ** ADAM_STEP
    c2 = 1.0 - ADAM_B2 ** ADAM_STEP
    ng = 1 if g_b is None else 2

    def body(*refs):
        w_ref, m_ref, v_ref, g_ref, d_ref, nm_ref, nv_ref = refs[ng:]
        g = refs[0][...] if ng == 1 else refs[0][...] + refs[1][...]
        nm = ADAM_B1 * m_ref[...] + (1.0 - ADAM_B1) * g
        nv = ADAM_B2 * v_ref[...] + (1.0 - ADAM_B2) * (g * g)
        g_ref[...] = g
        nm_ref[...] = nm
        nv_ref[...] = nv
        d_ref[...] = -ADAM_LR * ((nm / c1) / (jnp.sqrt(nv / c2) + ADAM_EPS) + ADAM_WD * w_ref[...])

    blk = pl.BlockSpec((tr, C), lambda i: (i, 0))
    gs = [g_a] if g_b is None else [g_a, g_b]
    return pl.pallas_call(
        body, name=name, grid=(R // tr,), in_specs=[blk] * (ng + 3), out_specs=[blk] * 4,
        out_shape=[jax.ShapeDtypeStruct((R, C), F32)] * 4, compiler_params=_cp(("parallel",)),
    )(*gs, w, m, v)


def _lb_bounds(lb_logits):
    p = jax.nn.softmax(lb_logits.astype(F32), axis=0)
    return jnp.cumsum(p, axis=0) - p[0:1]


def _pack_small(n1, lb, hgn, n2, fin, sinks, extra):
    row = jnp.concatenate([sinks.reshape(-1), jnp.reshape(extra, (1,)),
                           jnp.zeros((D_MODEL - sinks.size - 1,), F32)])
    return jnp.concatenate([n1, lb, hgn, n2, fin[None], row[None], jnp.zeros((SMALL_ROWS - 18, D_MODEL), F32)], axis=0)


def _unpack_small(buf):
    return (buf[0:4], buf[4:8], buf[8:12], buf[12:16], buf[16], buf[17, :DEPTH * ATT_Q_HEADS].reshape(DEPTH, ATT_Q_HEADS),
            buf[17, DEPTH * ATT_Q_HEADS])


def _cols_full(g):
    q, r, c = g.shape
    return jnp.transpose(g, (1, 0, 2)).reshape(r, q * c)


def _cols_split(w):
    r, qc = w.shape
    return jnp.transpose(w.reshape(r, N_CHIPS, qc // N_CHIPS), (1, 0, 2))


def kernel(x, norm1, w_in, lb_logits, hg_norm, attn_sinks, w_pa, w_pb, w_o, norm2, w_gate, w_up, w_down, final_norm, loss_target, m_norm1, m_w_in, m_lb_logits, m_hg_norm, m_attn_sinks, m_w_pa, m_w_pb, m_w_o, m_norm2, m_w_gate, m_w_up, m_w_down, m_final_norm, v_norm1, v_w_in, v_lb_logits, v_hg_norm, v_attn_sinks, v_w_pa, v_w_pb, v_w_o, v_norm2, v_w_gate, v_w_up, v_w_down, v_final_norm):
    T = x.shape[1]
    H = FFN_HIDDEN
    tm = min(512, T)
    tmx = min(1024, T)
    tkx = min(2048, T)
    big = dict(w_in=w_in, w_pa=w_pa, w_pb=w_pb, w_o=w_o, w_gate=w_gate, w_up=w_up, w_down=w_down)
    col_sharded = ("w_in", "w_gate", "w_up")

    shards = {k: w.astype(BF16) for k, w in big.items()}
    ffn_pair = ("w_gate", "w_up")
    with_w_in = ("w_in", "w_pa", "w_pb", "w_o", "w_down")
    others = [k for k in big if k != "w_in"]

    def whole(k, g):
        return _cols_full(g) if k in col_sharded else g.reshape(-1, g.shape[-1])

    def layer_weights(blocks_l):
        full = {k: whole(k, blocks_l[k]) for k in big}
        full["w_in"] = _to_kernel_cols(full["w_in"])
        full["w_gu"] = jnp.concatenate([full.pop("w_gate"), full.pop("w_up")], axis=-1)
        return full

    def carried_by(kind, l):
        if l + 1 >= DEPTH:
            return []
        if l == 0:
            return [(0, k) for k in others] + [(1, "w_in")] if kind == "in_proj" else [(1, k) for k in others]
        return [(l + 1, k) for k in (with_w_in if kind == "in_proj" else ffn_pair)]

    lb_all, lb_vjp = jax.vjp(_lb_bounds, lb_logits)
    cos, sin, pm = rope_tables(T)

    xs = x[0]
    saved, weights = [], []
    blocks = [dict() for _ in range(DEPTH)]
    (blocks[0]["w_in"],) = exchange_between_chips([shards["w_in"][0]], scatter=False, name="gather_w_in_0")
    for l in range(DEPTH):
        tasks = carried_by("in_proj", l)
        h, proj, *got = ln_matmul(xs, norm1[l:l + 1], _to_kernel_cols(whole("w_in", blocks[l]["w_in"])), tm=tmx, tn=1536,
                                  name=f"in_proj_{l}", gather=[shards[k][j] for j, k in tasks])
        for (j, k), g in zip(tasks, got):
            blocks[j][k] = g
        full = layer_weights(blocks[l])
        weights.append(full)
        og, o_raw, states = hgrn_fwd(proj, lb_all[l:l + 1], hg_norm[l:l + 1], ct=min(256, T), hp=HG_HEADS, name=f"hgrn_fwd_{l}")
        att = swa_fwd(proj, cos, sin, pm, attn_sinks[l], name=f"swa_fwd_{l}")
        ya, yb, mix = merge_fwd(og, att, proj, full["w_pa"], full["w_pb"], tm=tm, name=f"merge_fwd_{l}")
        x1 = matmul_residual(xs, mix, full["w_o"], tm=tmx, name=f"out_proj_{l}")
        tasks = carried_by("ffn_up", l)
        h2, gu, act, *got = ln_matmul(x1, norm2[l:l + 1], full["w_gu"], tm=min(256, T), tn=2 * H, name=f"ffn_up_{l}",
                                      swiglu=True, gather=[shards[k][j] for j, k in tasks])
        for (j, k), g in zip(tasks, got):
            blocks[j][k] = g
        x2 = matmul_residual(x1, act, full["w_down"], tm=tmx, name=f"ffn_down_{l}")
        saved.append((xs, h, proj, og, o_raw, states, att, ya, yb, mix, x1, h2, gu, act))
        xs = x2

    dx, loss_parts, dfin_parts = loss_head(xs, final_norm[None], loss_target[0], tm=tm, name="loss_head")

    arrived = [None] * DEPTH
    dn1, dn2, dlb, dgn, dsk = ([None] * DEPTH for _ in range(5))
    for l in reversed(range(DEPTH)):
        xs, h, proj, og, o_raw, states, att, ya, yb, mix, x1, h2, gu, act = saved[l]
        full = weights[l]
        dw = {}
        dgu = ffn_down_bwd(dx, full["w_down"], gu, tm=min(256, T), name=f"ffn_down_bwd_{l}")
        dw["w_down"] = matmul_tn(act, dx, tka=H // 2, tn=D_MODEL, tk=tkx, name=f"dw_down_{l}")
        dwgu = matmul_tn(h2, dgu, tka=D_MODEL, tn=H // 2, tk=tkx, name=f"dw_gu_{l}")
        dw["w_gate"], dw["w_up"] = dwgu[:, :H], dwgu[:, H:]
        dx1, dn2_parts = matmul_nt_rmsbwd(dgu, full["w_gu"], x1, norm2[l:l + 1], dx, tm=tmx, tk=H // 2, name=f"ffn_up_bwd_{l}")
        dya, dyb, dproj = merge_bwd(dx1, full["w_o"], proj, ya, yb, tm=tm, name=f"merge_bwd_{l}")
        dw["w_o"] = matmul_tn(mix, dx1, tka=D_MODEL, tn=D_MODEL, tk=tkx, name=f"dw_o_{l}")
        dog = matmul_nt(dya, full["w_pa"], tm=tmx, name=f"pa_bwd_{l}")
        datt = matmul_nt(dyb, full["w_pb"], tm=tmx, name=f"pb_bwd_{l}")
        dw["w_pa"] = matmul_tn(og, dya, tka=D_MODEL, tn=D_MODEL, tk=tkx, name=f"dw_pa_{l}")
        dw["w_pb"] = matmul_tn(att, dyb, tka=D_MODEL, tn=D_MODEL, tk=tkx, name=f"dw_pb_{l}")
        dproj, dsk_parts = swa_bwd(proj, cos, sin, pm, attn_sinks[l], att, datt, dproj, name=f"swa_bwd_{l}")
        dproj, dlb_parts, dgn_parts = hgrn_bwd(proj, lb_all[l:l + 1], hg_norm[l:l + 1], o_raw, states, dog, dproj,
                                               ct=min(256, T), name=f"hgrn_bwd_{l}")
        split = lambda k, g: _cols_split(g) if k in col_sharded else g.reshape(N_CHIPS, -1, g.shape[-1])
        dw["w_in"], *got_pair = matmul_tn(h, dproj, tka=D_MODEL, tn=1536, tk=tkx, name=f"dw_in_{l}",
                                          scatter=[split(k, dw[k]) for k in ffn_pair])
        dw["w_in"] = _from_kernel_cols(dw["w_in"])
        dx, dn1_parts, *got = matmul_nt_rmsbwd(dproj, full["w_in"], xs, norm1[l:l + 1], dx1, tm=tmx, tk=1536,
                                               name=f"in_proj_bwd_{l}", scatter=[split(k, dw[k]) for k in with_w_in])
        arrived[l] = {**dict(zip(with_w_in, got)), **dict(zip(ffn_pair, got_pair))}
        dn1[l], dn2[l], dlb[l], dgn[l] = dn1_parts.sum(0), dn2_parts.sum(0), dlb_parts.sum(0), dgn_parts.sum(0)
        dsk[l] = dsk_parts[:, 0, :].reshape(ATT_Q_HEADS, WINDOW).sum(-1)
    grad_x = dx[None]

    mine = _pack_small(jnp.stack(dn1), jnp.stack(dlb), jnp.stack(dgn), jnp.stack(dn2), dfin_parts.sum(0), jnp.stack(dsk),
                       loss_parts.sum())
    total = allreduce_small(mine, name="allreduce_small")
    g_n1, g_lb_all, g_gn, g_n2, g_fin, g_sk, loss = _unpack_small(total)
    (g_lb,) = lb_vjp(g_lb_all)
    zero = jnp.zeros((), F32)
    small = adamw(_pack_small(g_n1, g_lb, g_gn, g_n2, g_fin, g_sk, zero), None,
                  _pack_small(norm1, lb_logits, hg_norm, norm2, final_norm, attn_sinks, zero),
                  _pack_small(m_norm1, m_lb_logits, m_hg_norm, m_norm2, m_final_norm, m_attn_sinks, zero),
                  _pack_small(v_norm1, v_lb_logits, v_hg_norm, v_norm2, v_final_norm, v_attn_sinks, zero),
                  tr=SMALL_ROWS, name="adamw_small")
    small = [_unpack_small(s)[:6] for s in small]

    sums = []
    for k in big:
        rows = arrived[0][k].shape[1]
        sums.append(sum_parts([arrived[l][k] for l in range(DEPTH)], tr=128 if rows % 128 == 0 else 64, name=f"sum_{k}"))
    theirs = swap_with_sibling(sums, name="swap_sums")
    mom = dict(w_in=(m_w_in, v_w_in), w_pa=(m_w_pa, v_w_pa), w_pb=(m_w_pb, v_w_pb), w_o=(m_w_o, v_w_o),
               w_gate=(m_w_gate, v_w_gate), w_up=(m_w_up, v_w_up), w_down=(m_w_down, v_w_down))
    upd = {}
    for k, mine_k, theirs_k in zip(big, sums, theirs):
        shp = big[k].shape
        flat = lambda t: t.reshape(shp[0] * shp[1], shp[2])
        res = adamw(mine_k, theirs_k, flat(big[k]), flat(mom[k][0]), flat(mom[k][1]), tr=128, name=f"adamw_{k}")
        upd[k] = [t.reshape(shp) for t in res]

    order = ("norm1", "w_in", "lb_logits", "hg_norm", "attn_sinks", "w_pa", "w_pb", "w_o", "norm2", "w_gate", "w_up",
             "w_down", "final_norm")
    small_pos = dict(norm1=0, lb_logits=1, hg_norm=2, norm2=3, final_norm=4, attn_sinks=5)
    outs = [loss, grad_x]
    for kind in range(4):
        for name in order:
            outs.append(upd[name][kind] if name in upd else small[kind][small_pos[name]])
    return tuple(outs)
```

```python
import functools

import jax
import jax.numpy as jnp
import numpy as np
from jax import lax
from jax.experimental import pallas as pl
from jax.experimental.pallas import tpu as pltpu

F32 = jnp.float32
BF16 = jnp.bfloat16

D_MODEL = 1024
DEPTH = 4
HG_HEADS = 8
HG_DK = 128
HG_WIDTH = HG_HEADS * HG_DK
CHUNK = 64
ATT_Q_HEADS = 16
ATT_KV_HEADS = 4
ATT_GROUP = ATT_Q_HEADS // ATT_KV_HEADS
HEAD_DIM = 64
ATT_WIDTH = ATT_Q_HEADS * HEAD_DIM
ATT_KV_WIDTH = ATT_KV_HEADS * HEAD_DIM
WINDOW = 128
ROPE_THETA = 500000.0
ROPE_DIM = HEAD_DIM // 4
FFN_HIDDEN = 2816
IN_COLS = 4 * HG_WIDTH + ATT_WIDTH + 2 * ATT_KV_WIDTH + 2 * D_MODEL
EPS = 1e-6
MIN_F = 1e-30
COL_HQ, COL_HF, COL_HI, COL_HG = 0, 8, 16, 24
COL_GA, COL_GB, COL_AQ, COL_AK, COL_AV = 32, 40, 48, 56, 58


def _to_kernel_cols(w):
    return jnp.concatenate([w[..., :4096], w[..., 5632:7680], w[..., 4096:5120], w[..., 5120:5632]], axis=-1)


def _from_kernel_cols(w):
    return jnp.concatenate([w[..., :4096], w[..., 6144:7168], w[..., 7168:7680], w[..., 4096:6144]], axis=-1)

ADAM_LR = 0.001
ADAM_B1 = 0.9
ADAM_B2 = 0.999
ADAM_EPS = 1e-08
ADAM_WD = 0.01
ADAM_STEP = 10

VMEM_LIMIT_V7X = 56 * 1024 * 1024
MESH = pl.DeviceIdType.MESH


def _cp(sem, vmem=VMEM_LIMIT_V7X):
    return pltpu.CompilerParams(dimension_semantics=sem, vmem_limit_bytes=vmem)


def _sigmoid(x):
    return 1.0 / (1.0 + jnp.exp(-x))


def _dot(a, b):
    return jnp.dot(a, b, preferred_element_type=F32)


def _dot_nt(a, b):
    return lax.dot_general(a, b, (((1,), (1,)), ((), ())), preferred_element_type=F32)


def _dot_tn(a, b):
    return lax.dot_general(a, b, (((0,), (0,)), ((), ())), preferred_element_type=F32)


def _rowsum8(v):
    r, n = v.shape
    return jnp.sum(v.reshape(r // 8, 8, n), axis=0)


_ANY = pl.BlockSpec(memory_space=pl.ANY)
N_CHIPS = 4
PEER_CHIPS = N_CHIPS - 1


def _place():
    x, y, c = lax.axis_index("x"), lax.axis_index("y"), lax.axis_index("c")
    return x, y, c, [(1 - x, y), (x, 1 - y), (1 - x, 1 - y)]


def _exchange_scratch(n):
    return [pltpu.SemaphoreType.DMA((PEER_CHIPS * n,)), pltpu.SemaphoreType.DMA((PEER_CHIPS * n,)), pltpu.SemaphoreType.DMA((n,))]


def _exchange(ins, outs, sems, scatter):
    send_sems, recv_sems, local_sems = sems
    x, y, c, chips = _place()
    me = 2 * x + y
    n = len(ins)

    def remote(a, j, arriving):
        px, py = chips[j]
        them = 2 * px + py
        src = ins[a].at[them] if scatter else ins[a]
        return pltpu.make_async_remote_copy(
            src_ref=src, dst_ref=outs[a].at[them if arriving else me], send_sem=send_sems.at[a * PEER_CHIPS + j],
            recv_sem=recv_sems.at[a * PEER_CHIPS + j], device_id=(px, py, c), device_id_type=MESH)

    def local(a):
        return pltpu.make_async_copy(ins[a].at[me] if scatter else ins[a], outs[a].at[me], local_sems.at[a])

    def start():
        for a in range(n):
            local(a).start()
            for j in range(PEER_CHIPS):
                remote(a, j, False).start()

    def wait():
        for a in range(n):
            for j in range(PEER_CHIPS):
                remote(a, j, True).wait()
            local(a).wait()

    return start, wait


def _exchange_out_shapes(arrays, scatter):
    return [jax.ShapeDtypeStruct(a.shape if scatter else (N_CHIPS,) + a.shape, a.dtype) for a in arrays]


def exchange_between_chips(arrays, *, scatter, name):
    n = len(arrays)

    def body(*refs):
        start, wait = _exchange(refs[:n], refs[n:2 * n], refs[2 * n:], scatter)
        start()
        wait()

    return pl.pallas_call(
        body, name=name, out_shape=_exchange_out_shapes(arrays, scatter), in_specs=[_ANY] * n, out_specs=[_ANY] * n,
        scratch_shapes=_exchange_scratch(n),
    )(*arrays)


def ln_matmul(x, g, w, *, tm, tn, name, swiglu=False, gather=()):
    T, Dm = x.shape
    N = w.shape[1]
    if swiglu:
        assert tn == N
    half = N // 2
    ng = len(gather)
    n_out = 3 if swiglu else 2
    ni, nj = T // tm, N // tn

    def body(x_ref, g_ref, w_ref, *rest):
        gather_in, rest = rest[:ng], rest[ng:]
        h_ref, o_ref = rest[0], rest[1]
        i, j = pl.program_id(0), pl.program_id(1)
        if ng:
            start, wait = _exchange(gather_in, rest[n_out:n_out + ng], rest[n_out + ng:], False)
            pl.when((i == 0) & (j == 0))(start)

        @pl.when(j == 0)
        def _():
            xf = x_ref[...]
            r = lax.rsqrt(jnp.mean(xf * xf, axis=-1, keepdims=True) + EPS)
            h_ref[...] = (xf * r * g_ref[...]).astype(BF16)

        acc = _dot(h_ref[...], w_ref[...])
        o_ref[...] = acc.astype(BF16)
        if swiglu:
            gt, up = acc[:, :half], acc[:, half:]
            rest[2][...] = (gt * _sigmoid(gt) * up).astype(BF16)
        if ng:
            pl.when((i == ni - 1) & (j == nj - 1))(wait)

    out_shape = [jax.ShapeDtypeStruct((T, Dm), BF16), jax.ShapeDtypeStruct((T, N), BF16)]
    out_specs = [pl.BlockSpec((tm, Dm), lambda i, j: (i, 0)), pl.BlockSpec((tm, tn), lambda i, j: (i, j))]
    if swiglu:
        out_shape.append(jax.ShapeDtypeStruct((T, half), BF16))
        out_specs.append(pl.BlockSpec((tm, half), lambda i, j: (i, 0)))
    return pl.pallas_call(
        body, name=name, grid=(ni, nj),
        in_specs=[pl.BlockSpec((tm, Dm), lambda i, j: (i, 0)), pl.BlockSpec((1, Dm), lambda i, j: (0, 0)),
                  pl.BlockSpec((Dm, tn), lambda i, j: (0, j))] + [_ANY] * ng,
        out_specs=out_specs + [_ANY] * ng, out_shape=out_shape + _exchange_out_shapes(gather, False),
        scratch_shapes=_exchange_scratch(ng) if ng else [],
        compiler_params=_cp(("arbitrary", "arbitrary") if ng else ("parallel", "arbitrary")),
    )(x, g, w, *gather)


def matmul_residual(x, a, w, *, tm, name):
    T, N = x.shape
    K = a.shape[1]

    def body(x_ref, a_ref, w_ref, o_ref):
        o_ref[...] = x_ref[...] + _dot(a_ref[...], w_ref[...])

    return pl.pallas_call(
        body, name=name, grid=(T // tm,),
        in_specs=[pl.BlockSpec((tm, N), lambda i: (i, 0)), pl.BlockSpec((tm, K), lambda i: (i, 0)),
                  pl.BlockSpec((K, N), lambda i: (0, 0))],
        out_specs=pl.BlockSpec((tm, N), lambda i: (i, 0)), out_shape=jax.ShapeDtypeStruct((T, N), F32),
        compiler_params=_cp(("parallel",)),
    )(x, a, w)


def matmul_tn(a, g, *, tka, tn, tk, name, scatter=()):
    T, Ka = a.shape
    N = g.shape[1]
    ni, nj, nk = Ka // tka, N // tn, T // tk
    ns = len(scatter)

    def body(a_ref, g_ref, *rest):
        scatter_in, rest = rest[:ns], rest[ns:]
        o_ref, acc_ref = rest[0], rest[1 + ns]
        i, j, k = pl.program_id(0), pl.program_id(1), pl.program_id(2)
        if ns:
            start, wait = _exchange(scatter_in, rest[1:1 + ns], rest[2 + ns:], True)
            pl.when((i == 0) & (j == 0) & (k == 0))(start)

        @pl.when(k == 0)
        def _():
            acc_ref[...] = jnp.zeros_like(acc_ref)

        acc_ref[...] += _dot_tn(a_ref[...].astype(BF16), g_ref[...].astype(BF16))

        @pl.when(k == nk - 1)
        def _():
            o_ref[...] = acc_ref[...].astype(BF16)

        if ns:
            pl.when((i == ni - 1) & (j == nj - 1) & (k == nk - 1))(wait)

    res = pl.pallas_call(
        body, name=name, grid=(ni, nj, nk),
        in_specs=[pl.BlockSpec((tk, tka), lambda i, j, k: (k, i)), pl.BlockSpec((tk, tn), lambda i, j, k: (k, j))] + [_ANY] * ns,
        out_specs=[pl.BlockSpec((tka, tn), lambda i, j, k: (i, j))] + [_ANY] * ns,
        out_shape=[jax.ShapeDtypeStruct((Ka, N), BF16)] + _exchange_out_shapes(scatter, True),
        scratch_shapes=[pltpu.VMEM((tka, tn), F32)] + (_exchange_scratch(ns) if ns else []),
        compiler_params=_cp(("arbitrary",) * 3 if ns else ("parallel", "parallel", "arbitrary")),
    )(a, g, *scatter)
    return res if ns else res[0]


def matmul_nt_rmsbwd(a, w, x, g, dres, *, tm, tk, name, scatter=()):
    T, K = a.shape
    Dm = w.shape[0]
    nk = K // tk
    ni = T // tm
    ns = len(scatter)

    def body(a_ref, w_ref, x_ref, g_ref, dres_ref, *rest):
        scatter_in, rest = rest[:ns], rest[ns:]
        dx_ref, dg_ref = rest[0], rest[1]
        acc_ref = rest[2 + ns]
        i, k = pl.program_id(0), pl.program_id(1)
        if ns:
            start, wait = _exchange(scatter_in, rest[2:2 + ns], rest[3 + ns:], True)
            pl.when((i == 0) & (k == 0))(start)

        @pl.when(k == 0)
        def _():
            acc_ref[...] = jnp.zeros_like(acc_ref)

        @pl.when((i == 0) & (k == 0))
        def _():
            dg_ref[...] = jnp.zeros_like(dg_ref)

        acc_ref[...] += _dot_nt(a_ref[...], w_ref[...])

        @pl.when(k == nk - 1)
        def _():
            dh = acc_ref[...]
            xf = x_ref[...]
            r = lax.rsqrt(jnp.mean(xf * xf, axis=-1, keepdims=True) + EPS)
            xhat = xf * r
            dg_ref[...] += _rowsum8(dh * xhat)
            dxhat = dh * g_ref[...]
            dx_ref[...] = dres_ref[...] + r * (dxhat - xhat * jnp.mean(dxhat * xhat, axis=-1, keepdims=True))

        if ns:
            pl.when((i == ni - 1) & (k == nk - 1))(wait)

    return pl.pallas_call(
        body, name=name, grid=(ni, nk),
        in_specs=[pl.BlockSpec((tm, tk), lambda i, k: (i, k)), pl.BlockSpec((Dm, tk), lambda i, k: (0, k)),
                  pl.BlockSpec((tm, Dm), lambda i, k: (i, 0)), pl.BlockSpec((1, Dm), lambda i, k: (0, 0)),
                  pl.BlockSpec((tm, Dm), lambda i, k: (i, 0))] + [_ANY] * ns,
        out_specs=[pl.BlockSpec((tm, Dm), lambda i, k: (i, 0)), pl.BlockSpec((8, Dm), lambda i, k: (0, 0))] + [_ANY] * ns,
        out_shape=[jax.ShapeDtypeStruct((T, Dm), F32), jax.ShapeDtypeStruct((8, Dm), F32)] + _exchange_out_shapes(scatter, True),
        scratch_shapes=[pltpu.VMEM((tm, Dm), F32)] + (_exchange_scratch(ns) if ns else []),
        compiler_params=_cp(("arbitrary", "arbitrary")),
    )(a, w, x, g, dres, *scatter)


def ffn_down_bwd(dx, w_down, gu, *, tm, name):
    T, Dm = dx.shape
    H = w_down.shape[0]

    def body(dx_ref, w_ref, gu_ref, o_ref):
        dact = _dot_nt(dx_ref[...].astype(BF16), w_ref[...])
        gt = gu_ref[:, :H].astype(F32)
        up = gu_ref[:, H:].astype(F32)
        s = _sigmoid(gt)
        o_ref[:, :H] = (dact * up * (s * (1.0 + gt * (1.0 - s)))).astype(BF16)
        o_ref[:, H:] = (dact * gt * s).astype(BF16)

    return pl.pallas_call(
        body, name=name, grid=(T // tm,),
        in_specs=[pl.BlockSpec((tm, Dm), lambda i: (i, 0)), pl.BlockSpec((H, Dm), lambda i: (0, 0)),
                  pl.BlockSpec((tm, 2 * H), lambda i: (i, 0))],
        out_specs=pl.BlockSpec((tm, 2 * H), lambda i: (i, 0)), out_shape=jax.ShapeDtypeStruct((T, 2 * H), BF16),
        compiler_params=_cp(("parallel",)),
    )(dx, w_down, gu)


def _gate_specs(tm):
    return [pl.BlockSpec((tm, D_MODEL), lambda i, c=c: (i, c)) for c in (COL_GA // 8, COL_GB // 8)]


def merge_fwd(x, og, att, proj, w_pa, w_pb, w_o, *, tm, name):
    T, Dm = og.shape

    def body(x_ref, og_ref, att_ref, ga_ref, gb_ref, wa_ref, wb_ref, wo_ref, ya_ref, yb_ref, mix_ref, x1_ref):
        ya = _dot(og_ref[...], wa_ref[...])
        yb = _dot(att_ref[...], wb_ref[...])
        ya_ref[...] = ya.astype(BF16)
        yb_ref[...] = yb.astype(BF16)
        mix = (_sigmoid(ga_ref[...].astype(F32)) * ya + _sigmoid(gb_ref[...].astype(F32)) * yb).astype(BF16)
        mix_ref[...] = mix
        x1_ref[...] = x_ref[...] + _dot(mix, wo_ref[...])

    row = pl.BlockSpec((tm, Dm), lambda i: (i, 0))
    wsp = pl.BlockSpec((Dm, Dm), lambda i: (0, 0))
    return pl.pallas_call(
        body, name=name, grid=(T // tm,),
        in_specs=[row, row, row] + _gate_specs(tm) + [wsp, wsp, wsp],
        out_specs=[row, row, row, row], out_shape=[jax.ShapeDtypeStruct((T, Dm), BF16)] * 3 + [jax.ShapeDtypeStruct((T, Dm), F32)],
        compiler_params=_cp(("parallel",)),
    )(x, og, att, proj, proj, w_pa, w_pb, w_o)


def merge_bwd(dx, w_o, proj, ya, yb, w_pa, w_pb, *, tm, name):
    T, Dm = dx.shape

    def body(dx_ref, w_ref, ga_ref, gb_ref, ya_ref, yb_ref, wa_ref, wb_ref, dya_ref, dyb_ref, dgate_ref, dog_ref, datt_ref):
        dmix = _dot_nt(dx_ref[...].astype(BF16), w_ref[...])
        sa = _sigmoid(ga_ref[...].astype(F32))
        sb = _sigmoid(gb_ref[...].astype(F32))
        dya = (dmix * sa).astype(BF16)
        dyb = (dmix * sb).astype(BF16)
        dya_ref[...] = dya
        dyb_ref[...] = dyb
        dgate_ref[:, :Dm] = (dmix * ya_ref[...].astype(F32) * sa * (1.0 - sa)).astype(BF16)
        dgate_ref[:, Dm:] = (dmix * yb_ref[...].astype(F32) * sb * (1.0 - sb)).astype(BF16)
        dog_ref[...] = _dot_nt(dya, wa_ref[...]).astype(BF16)
        datt_ref[...] = _dot_nt(dyb, wb_ref[...]).astype(BF16)

    row = pl.BlockSpec((tm, Dm), lambda i: (i, 0))
    wsp = pl.BlockSpec((Dm, Dm), lambda i: (0, 0))
    return pl.pallas_call(
        body, name=name, grid=(T // tm,),
        in_specs=[row, wsp] + _gate_specs(tm) + [row, row, wsp, wsp],
        out_specs=[row, row, pl.BlockSpec((tm, 2 * Dm), lambda i: (i, COL_GA // 16)), row, row],
        out_shape=[jax.ShapeDtypeStruct((T, Dm), BF16)] * 2 + [jax.ShapeDtypeStruct((T, IN_COLS), BF16)]
        + [jax.ShapeDtypeStruct((T, Dm), BF16)] * 2,
        compiler_params=_cp(("parallel",)),
    )(dx, w_o, proj, proj, ya, yb, w_pa, w_pb)


def loss_head(x, g, target, *, tm, name):
    T, Dm = x.shape

    def body(x_ref, g_ref, t_ref, dx_ref, l_ref, dg_ref):
        @pl.when(pl.program_id(0) == 0)
        def _():
            l_ref[...] = jnp.zeros_like(l_ref)
            dg_ref[...] = jnp.zeros_like(dg_ref)

        xf = x_ref[...]
        gv = g_ref[...]
        r = lax.rsqrt(jnp.mean(xf * xf, axis=-1, keepdims=True) + EPS)
        xhat = xf * r
        err = xhat * gv - t_ref[...]
        l_ref[...] += _rowsum8(err * err) * (0.5 / Dm)
        dy = err * (1.0 / Dm)
        dg_ref[...] += _rowsum8(dy * xhat)
        dxhat = dy * gv
        dx_ref[...] = r * (dxhat - xhat * jnp.mean(dxhat * xhat, axis=-1, keepdims=True))

    row = pl.BlockSpec((tm, Dm), lambda i: (i, 0))
    acc = pl.BlockSpec((8, Dm), lambda i: (0, 0))
    return pl.pallas_call(
        body, name=name, grid=(T // tm,),
        in_specs=[row, pl.BlockSpec((1, Dm), lambda i: (0, 0)), row],
        out_specs=[row, acc, acc],
        out_shape=[jax.ShapeDtypeStruct((T, Dm), F32), jax.ShapeDtypeStruct((8, Dm), F32), jax.ShapeDtypeStruct((8, Dm), F32)],
        compiler_params=_cp(("arbitrary",)),
    )(x, g, target)


_LEVELS = (32, 16, 8)
_DIAG = 8
_SUBLANES = 8


def _chunk_consts():
    t = lax.broadcasted_iota(jnp.int32, (CHUNK, CHUNK), 0)
    s = lax.broadcasted_iota(jnp.int32, (CHUNK, CHUNK), 1)
    tri = (s <= t).astype(F32)
    tri_t = (s >= t).astype(F32)
    masks = []
    for m in _LEVELS:
        sh = int(np.log2(2 * m))
        masks.append(((t >> sh) == (s >> sh)) & ((t & (2 * m - 1)) >= m) & ((s & (2 * m - 1)) < m))
    return tri, tri_t, masks


def _exact_dot(sel, x):
    hi = x.astype(BF16)
    r1 = x - hi.astype(F32)
    mid = r1.astype(BF16)
    lo = (r1 - mid.astype(F32)).astype(BF16)
    return _dot(jnp.concatenate([sel, sel, sel], axis=1).astype(BF16), jnp.concatenate([hi, mid, lo], axis=0))


def _level_ref(b, m):
    if 2 * m >= _SUBLANES:
        pieces = [jnp.broadcast_to(b[p * 2 * m + m - 1:p * 2 * m + m, :], (2 * m, b.shape[1])) for p in range(CHUNK // (2 * m))]
        return pieces[0] if len(pieces) == 1 else jnp.concatenate(pieces, axis=0)
    groups = CHUNK // _SUBLANES
    b3 = b.reshape(groups, _SUBLANES, b.shape[1])
    row = lax.broadcasted_iota(jnp.int32, b3.shape, 1)
    ref = None
    for p in reversed(range(_SUBLANES // (2 * m))):
        src = jnp.broadcast_to(b3[:, p * 2 * m + m - 1:p * 2 * m + m, :], b3.shape)
        ref = src if ref is None else jnp.where(row < (p + 1) * 2 * m, src, ref)
    return ref.reshape(b.shape)


def _intra_off(q, kk, b, masks):
    a = jnp.zeros((CHUNK, CHUNK), F32)
    keep = []
    for m, mask in zip(_LEVELS, masks):
        e = jnp.exp2(-jnp.abs(b - _level_ref(b, m)))
        qt = (q * e).astype(BF16)
        kt = (kk * e).astype(BF16)
        a = a + jnp.where(mask, _dot_nt(qt, kt), 0.0)
        keep.append((mask, e, qt, kt))
    return a, keep


def _roll_rows(x, d):
    g = x.shape[0] // _DIAG
    return pltpu.roll(x.reshape(g, _DIAG, x.shape[1]), d % _DIAG, 1).reshape(x.shape)


def _diag_fwd(q, kk, v, b):
    lower = lax.broadcasted_iota(jnp.int32, q.shape, 0) & (_DIAG - 1)
    out = jnp.sum(q * kk, axis=-1, keepdims=True) * v
    for d in range(1, _DIAG):
        dec = jnp.exp2(jnp.where(lower >= d, b - _roll_rows(b, d), _NEG))
        out = out + jnp.sum(q * dec * _roll_rows(kk, d), axis=-1, keepdims=True) * _roll_rows(v, d)
    return out


def _diag_bwd(q, kk, v, b, do):
    lower = lax.broadcasted_iota(jnp.int32, q.shape, 0) & (_DIAG - 1)
    a = jnp.sum(q * kk, axis=-1, keepdims=True)
    da = jnp.sum(do * v, axis=-1, keepdims=True)
    dq, dk, dv = da * kk, da * q, a * do
    for d in range(1, _DIAG):
        kr = _roll_rows(kk, d)
        dec = jnp.exp2(jnp.where(lower >= d, b - _roll_rows(b, d), _NEG))
        qd = q * dec
        a = jnp.sum(qd * kr, axis=-1, keepdims=True)
        da = jnp.sum(do * _roll_rows(v, d), axis=-1, keepdims=True)
        dq = dq + da * dec * kr
        dk = dk + _roll_rows(da * qd, -d)
        dv = dv + _roll_rows(a * do, -d)
    return dq, dk, dv


def _hgrn_in_specs(ct, hp, order):
    def spec(col):
        return pl.BlockSpec((ct, hp * HG_DK), lambda h, i, col=col: (order(i), col // hp + h))
    vec = pl.BlockSpec((1, hp * HG_DK), lambda h, i: (0, h))
    return [spec(COL_HQ), spec(COL_HF), spec(COL_HI), spec(COL_HG), vec, vec]


def hgrn_fwd(proj, lb, gn, *, ct, hp, name):
    T = proj.shape[0]
    nc = ct // CHUNK

    def body(hq_ref, hf_ref, hi_ref, hg_ref, lb_ref, gn_ref, og_ref, o_ref, st_ref, s_ref):
        @pl.when(pl.program_id(1) == 0)
        def _():
            s_ref[...] = jnp.zeros_like(s_ref)

        tri, _, masks = _chunk_consts()

        def chunk(c, carry):
            sl = pl.ds(pl.multiple_of(c * CHUNK, CHUNK), CHUNK)
            lb_all = lb_ref[...]
            f_all = lb_all + (1.0 - lb_all) * _sigmoid(hf_ref[sl, :].astype(F32))
            b_all = _exact_dot(tri, jnp.log2(jnp.maximum(f_all, MIN_F)))
            for p in range(hp):
                cols = slice(p * HG_DK, (p + 1) * HG_DK)
                hq = hq_ref[sl, cols].astype(F32)
                v = hi_ref[sl, cols].astype(F32)
                gate = hg_ref[sl, cols].astype(F32)
                q = hq * _sigmoid(hq)
                f = f_all[:, cols]
                kk = 1.0 - f
                b = b_all[:, cols]
                st = s_ref[p]
                st_ref[p, c] = st
                a, _ = _intra_off(q, kk, b, masks)
                vb = v.astype(BF16)
                o = (_dot_nt((q * jnp.exp2(b)).astype(BF16), st.astype(BF16)) + _dot(a.astype(BF16), vb)
                     + _diag_fwd(q, kk, v, b))
                bl = b[CHUNK - 1:CHUNK, :]
                s_ref[p] = st * jnp.exp2(bl) + _dot_tn(vb, (kk * jnp.exp2(bl - b)).astype(BF16))
                o_ref[sl, cols] = o
                r = lax.rsqrt(jnp.mean(o * o, axis=-1, keepdims=True) + EPS)
                og_ref[sl, cols] = (o * r * gn_ref[:, cols] * (gate * _sigmoid(gate))).astype(BF16)
            return carry

        lax.fori_loop(0, nc, chunk, 0)

    blk = pl.BlockSpec((ct, hp * HG_DK), lambda h, i: (i, h))
    return pl.pallas_call(
        body, name=name, grid=(HG_HEADS // hp, T // ct),
        in_specs=_hgrn_in_specs(ct, hp, lambda i: i),
        out_specs=[blk, blk, pl.BlockSpec((hp, nc, HG_DK, HG_DK), lambda h, i: (h, i, 0, 0))],
        out_shape=[jax.ShapeDtypeStruct((T, HG_WIDTH), BF16), jax.ShapeDtypeStruct((T, HG_WIDTH), F32),
                   jax.ShapeDtypeStruct((HG_HEADS, T // CHUNK, HG_DK, HG_DK), F32)],
        scratch_shapes=[pltpu.VMEM((hp, HG_DK, HG_DK), F32)],
        compiler_params=_cp(("parallel", "arbitrary")),
    )(proj, proj, proj, proj, lb, gn)


def hgrn_bwd(proj, lb, gn, o_raw, states, dog, dproj, *, ct, name):
    T = proj.shape[0]
    nc = ct // CHUNK
    nblk = T // ct
    hp = HG_HEADS

    def body(hq_ref, hf_ref, hi_ref, hg_ref, lb_ref, gn_ref, o_ref, st_ref, dog_ref, _dproj_in,
             dp_ref, dlb_ref, dgn_ref, ds_ref):
        dhq_ref, dhf_ref, dhi_ref, dhg_ref = (dp_ref.at[:, s * HG_WIDTH:(s + 1) * HG_WIDTH] for s in range(4))
        @pl.when(pl.program_id(1) == 0)
        def _():
            ds_ref[...] = jnp.zeros_like(ds_ref)
            dlb_ref[...] = jnp.zeros_like(dlb_ref)
            dgn_ref[...] = jnp.zeros_like(dgn_ref)

        tri, tri_t, masks = _chunk_consts()
        last_row = lax.broadcasted_iota(jnp.int32, (CHUNK, HG_DK), 0) == CHUNK - 1

        def chunk(ci, carry):
            c = nc - 1 - ci
            sl = pl.ds(pl.multiple_of(c * CHUNK, CHUNK), CHUNK)
            lb_all = lb_ref[...]
            sig_all = _sigmoid(hf_ref[sl, :].astype(F32))
            f_all = lb_all + (1.0 - lb_all) * sig_all
            b_all = _exact_dot(tri, jnp.log2(jnp.maximum(f_all, MIN_F)))
            dbs, dks = [], []
            for p in range(hp):
                cols = slice(p * HG_DK, (p + 1) * HG_DK)
                gn_v = gn_ref[:, cols]
                hq = hq_ref[sl, cols].astype(F32)
                v = hi_ref[sl, cols].astype(F32)
                gate = hg_ref[sl, cols].astype(F32)
                sq = _sigmoid(hq)
                q = hq * sq
                f = f_all[:, cols]
                kk = 1.0 - f
                b = b_all[:, cols]
                st = st_ref[p, c]
                dst = ds_ref[p]
                o = o_ref[sl, cols]
                r = lax.rsqrt(jnp.mean(o * o, axis=-1, keepdims=True) + EPS)
                xhat = o * r
                sg = _sigmoid(gate)
                dog_v = dog_ref[sl, cols].astype(F32)
                dy = dog_v * (gate * sg)
                dhg_ref[sl, cols] = (dog_v * xhat * gn_v * (sg * (1.0 + gate * (1.0 - sg)))).astype(BF16)
                dgn_ref[:, cols] += _rowsum8(dy * xhat)
                dxh = dy * gn_v
                do = r * (dxh - xhat * jnp.mean(dxh * xhat, axis=-1, keepdims=True))
                dob = do.astype(BF16)
                vb = v.astype(BF16)
                stb = st.astype(BF16)
                dstb = dst.astype(BF16)
                eb = jnp.exp2(b)
                bl = b[CHUNK - 1:CHUNK, :]
                ebl = jnp.exp2(bl)
                edec = jnp.exp2(bl - b)
                qe = q * eb
                kdec = kk * edec
                a, keep = _intra_off(q, kk, b, masks)
                da = _dot_nt(dob, vb)
                dkdec = _dot(vb, dstb)
                dq = _dot(dob, stb) * eb
                dk = dkdec * edec
                dv = _dot_tn(a.astype(BF16), dob) + _dot_nt(kdec.astype(BF16), dstb)
                for mask, e, qt, kt in keep:
                    dsm = jnp.where(mask, da, 0.0).astype(BF16)
                    dq = dq + _dot(dsm, kt) * e
                    dk = dk + _dot_tn(dsm, qt) * e
                ddq, ddk, ddv = _diag_bwd(q, kk, v, b, do)
                dq = dq + ddq
                dk = dk + ddk
                dhi_ref[sl, cols] = (dv + ddv).astype(BF16)
                dbl = jnp.sum(dkdec * kdec, axis=0, keepdims=True) + ebl * jnp.sum(st * dst, axis=0, keepdims=True)
                db = q * dq - kk * dk
                dbs.append(jnp.where(last_row, db + dbl, db))
                dks.append(dk)
                ds_ref[p] = _dot_tn(dob, qe.astype(BF16)) + dst * ebl
                dhq_ref[sl, cols] = (dq * (sq * (1.0 + hq * (1.0 - sq)))).astype(BF16)
            dlogf = _exact_dot(tri_t, jnp.concatenate(dbs, axis=1))
            df = jnp.where(f_all > MIN_F, dlogf / f_all, 0.0) - jnp.concatenate(dks, axis=1)
            dhf_ref[sl, :] = (df * (1.0 - lb_all) * sig_all * (1.0 - sig_all)).astype(BF16)
            dlb_ref[...] += _rowsum8(df * (1.0 - sig_all))
            return carry

        lax.fori_loop(0, nc, chunk, 0)

    rev = lambda i: nblk - 1 - i
    blk = pl.BlockSpec((ct, hp * HG_DK), lambda h, i: (rev(i), h))
    acc = pl.BlockSpec((8, hp * HG_DK), lambda h, i: (0, h))
    return pl.pallas_call(
        body, name=name, grid=(HG_HEADS // hp, nblk),
        in_specs=_hgrn_in_specs(ct, hp, rev) + [blk, pl.BlockSpec((hp, nc, HG_DK, HG_DK), lambda h, i: (h, rev(i), 0, 0)), blk, _ANY],
        out_specs=[pl.BlockSpec((ct, 4 * HG_WIDTH), lambda h, i: (rev(i), 0)), acc, acc],
        out_shape=[jax.ShapeDtypeStruct(dproj.shape, dproj.dtype)] + [jax.ShapeDtypeStruct((8, HG_WIDTH), F32)] * 2,
        scratch_shapes=[pltpu.VMEM((hp, HG_DK, HG_DK), F32)],
        input_output_aliases={9: 0},
        compiler_params=_cp(("parallel", "arbitrary")),
    )(proj, proj, proj, proj, lb, gn, o_raw, states, dog, dproj)


_GW = ATT_GROUP * HEAD_DIM
_NEG = -1e30


def rope_tables(T):
    half = ROPE_DIM // 2
    inv = ROPE_THETA ** (-jnp.arange(half, dtype=F32) * 2.0 / ROPE_DIM)
    ang = jnp.arange(T, dtype=F32)[:, None] * inv[None, :]
    pad1 = jnp.ones((T, HEAD_DIM - ROPE_DIM), F32)
    cos = jnp.concatenate([jnp.cos(ang), jnp.cos(ang), pad1], axis=1)
    sin = jnp.concatenate([jnp.sin(ang), jnp.sin(ang), 0.0 * pad1], axis=1)
    p = np.zeros((_GW, _GW), np.float32)
    for base in range(0, _GW, HEAD_DIM):
        for i in range(half):
            p[base + i + half, base + i] = -1.0
            p[base + i, base + i + half] = 1.0
    reps = _GW // HEAD_DIM
    return jnp.tile(cos, (1, reps)), jnp.tile(sin, (1, reps)), jnp.asarray(p, BF16)


def _rope(x, cos, sin, pm):
    return x.astype(F32) * cos + _dot(x, pm) * sin


def _unrope(dx, cos, sin, pm):
    return dx * cos + _dot_nt((dx * sin).astype(BF16), pm)


_GQ = ATT_GROUP * WINDOW


def _swa_mask(n, queries_on_rows):
    shape = (_GQ, 2 * WINDOW) if queries_on_rows else (2 * WINDOW, _GQ)
    r = lax.broadcasted_iota(jnp.int32, shape, 0 if queries_on_rows else 1) & (WINDOW - 1)
    c = lax.broadcasted_iota(jnp.int32, shape, 1 if queries_on_rows else 0)
    delta = r + WINDOW - c
    return (delta >= 0) & (delta < WINDOW) & ((c >= WINDOW) | (n > 0))


def _sink_vector(sink_ref, g, queries_on_rows):
    shape = (_GQ, 1) if queries_on_rows else (1, _GQ)
    i = lax.broadcasted_iota(jnp.int32, shape, 0 if queries_on_rows else 1)
    out = jnp.full(shape, sink_ref[g * ATT_GROUP + ATT_GROUP - 1], F32)
    for hh in range(ATT_GROUP - 2, -1, -1):
        out = jnp.where(i < (hh + 1) * WINDOW, sink_ref[g * ATT_GROUP + hh], out)
    return out


def _head(x, h):
    return x[:, h * HEAD_DIM:(h + 1) * HEAD_DIM]


def _stack_heads(x):
    return jnp.concatenate([_head(x, hh) for hh in range(ATT_GROUP)], axis=0)


def _unstack_heads(x):
    return jnp.concatenate([x[hh * WINDOW:(hh + 1) * WINDOW] for hh in range(ATT_GROUP)], axis=1)


def _swa_specs(cur):
    prev = lambda n: jnp.maximum(n - 1, 0)
    kv = lambda col, f: pl.BlockSpec((WINDOW, _GW), lambda n: (f(n), col))
    tab = lambda f: pl.BlockSpec((WINDOW, _GW), lambda n: (f(n), 0))
    return [pl.BlockSpec((WINDOW, ATT_WIDTH), lambda n: (cur(n), COL_AQ // 8)),
            kv(COL_AK // 2, prev), kv(COL_AK // 2, cur), kv(COL_AV // 2, prev), kv(COL_AV // 2, cur),
            tab(prev), tab(cur), tab(prev), tab(cur),
            pl.BlockSpec((_GW, _GW), lambda n: (0, 0)), pl.BlockSpec(memory_space=pltpu.SMEM)]


def swa_fwd(proj, cos, sin, pm, sinks, *, name):
    T = proj.shape[0]
    nb = T // WINDOW
    scale = HEAD_DIM ** -0.5

    def body(q_ref, kp_ref, kc_ref, vp_ref, vc_ref, cp_ref, cc_ref, sp_ref, sc_ref, pm_ref, sink_ref, o_ref):
        n = pl.program_id(0)
        pm_v = pm_ref[...]
        mask = _swa_mask(n, True)
        k_cat = jnp.concatenate([_rope(kp_ref[...], cp_ref[...], sp_ref[...], pm_v),
                                 _rope(kc_ref[...], cc_ref[...], sc_ref[...], pm_v)], axis=0).astype(BF16)
        v_cat = jnp.concatenate([vp_ref[...], vc_ref[...]], axis=0)
        for g in range(ATT_KV_HEADS):
            qs = _stack_heads((_rope(q_ref[:, g * _GW:(g + 1) * _GW], cc_ref[...], sc_ref[...], pm_v) * scale).astype(BF16))
            sk = _sink_vector(sink_ref, g, True)
            s = jnp.where(mask, _dot_nt(qs, _head(k_cat, g)), _NEG)
            m = jnp.maximum(jnp.max(s, axis=-1, keepdims=True), sk)
            p = jnp.exp(s - m)
            l = jnp.sum(p, axis=-1, keepdims=True) + jnp.exp(sk - m)
            o = _dot(p.astype(BF16), _head(v_cat, g)) / l
            o_ref[:, g * _GW:(g + 1) * _GW] = _unstack_heads(o).astype(BF16)

    return pl.pallas_call(
        body, name=name, grid=(nb,), in_specs=_swa_specs(lambda n: n),
        out_specs=pl.BlockSpec((WINDOW, ATT_WIDTH), lambda n: (n, 0)),
        out_shape=jax.ShapeDtypeStruct((T, ATT_WIDTH), BF16), compiler_params=_cp(("parallel",)),
    )(proj, proj, proj, proj, proj, cos, cos, sin, sin, pm, sinks)


def swa_bwd(proj, cos, sin, pm, sinks, att, datt, dproj, *, name):
    T = proj.shape[0]
    nb = T // WINDOW
    scale = HEAD_DIM ** -0.5
    clamp = lambda n: jnp.minimum(n, nb - 1)

    def body(q_ref, kp_ref, kc_ref, vp_ref, vc_ref, cp_ref, cc_ref, sp_ref, sc_ref, pm_ref, sink_ref, att_ref, datt_ref,
             _dproj_in, dp_ref, dsink_ref, cq_ref, ck_ref, cv_ref):
        n = pl.program_id(0)
        pm_v = pm_ref[...]

        @pl.when(n == 0)
        def _():
            dsink_ref[...] = jnp.zeros_like(dsink_ref)
            cq_ref[...] = jnp.zeros_like(cq_ref)
            ck_ref[...] = jnp.zeros_like(ck_ref)
            cv_ref[...] = jnp.zeros_like(cv_ref)

        def write_prev(dk_prev, dv_prev):
            dp_ref[:, :ATT_WIDTH] = cq_ref[...]
            dk = _unrope(ck_ref[...] + dk_prev, cp_ref[...], sp_ref[...], pm_v)
            dp_ref[:, ATT_WIDTH:ATT_WIDTH + _GW] = dk.astype(BF16)
            dp_ref[:, ATT_WIDTH + _GW:] = (cv_ref[...] + dv_prev).astype(BF16)

        @pl.when(n < nb)
        def _():
            mask = _swa_mask(n, False)
            ones = jnp.ones((8, 2 * HEAD_DIM), BF16)
            k_cat = jnp.concatenate([_rope(kp_ref[...], cp_ref[...], sp_ref[...], pm_v),
                                     _rope(kc_ref[...], cc_ref[...], sc_ref[...], pm_v)], axis=0).astype(BF16)
            v_cat = jnp.concatenate([vp_ref[...], vc_ref[...]], axis=0)
            dqs, dks, dvs = [], [], []
            for g in range(ATT_KV_HEADS):
                gc = slice(g * _GW, (g + 1) * _GW)
                qs = _stack_heads((_rope(q_ref[:, gc], cc_ref[...], sc_ref[...], pm_v) * scale).astype(BF16))
                dos = _stack_heads(datt_ref[:, gc])
                os_ = _stack_heads(att_ref[:, gc])
                kh, vh = _head(k_cat, g), _head(v_cat, g)
                sk = _sink_vector(sink_ref, g, False)
                s = jnp.where(mask, _dot_nt(kh, qs), _NEG)
                m = jnp.maximum(jnp.max(s, axis=0, keepdims=True), sk)
                e = jnp.exp(s - m)
                es = jnp.exp(sk - m)
                inv_l = 1.0 / (jnp.sum(e, axis=0, keepdims=True) + es)
                p = e * inv_l
                prod = dos.astype(F32) * os_.astype(F32)
                prod_hi = prod.astype(BF16)
                prod_lo = (prod - prod_hi.astype(F32)).astype(BF16)
                dsum = _dot_nt(ones, jnp.concatenate([prod_hi, prod_lo], axis=1))[0:1]
                dsink_ref[g, 0:1, :] += -(es * inv_l) * dsum
                ds = (p * (_dot_nt(vh, dos) - dsum)).astype(BF16)
                dqs.append(_unrope(_unstack_heads(_dot_tn(ds, kh) * scale), cc_ref[...], sc_ref[...], pm_v))
                dks.append(_dot(ds, qs))
                dvs.append(_dot(p.astype(BF16), dos))
            dk_all = jnp.concatenate(dks, axis=1)
            dv_all = jnp.concatenate(dvs, axis=1)

            @pl.when(n > 0)
            def _():
                write_prev(dk_all[:WINDOW], dv_all[:WINDOW])

            cq_ref[...] = jnp.concatenate(dqs, axis=1).astype(BF16)
            ck_ref[...] = dk_all[WINDOW:]
            cv_ref[...] = dv_all[WINDOW:]

        @pl.when(n == nb)
        def _():
            write_prev(jnp.zeros((WINDOW, _GW), F32), jnp.zeros((WINDOW, _GW), F32))

    row = pl.BlockSpec((WINDOW, ATT_WIDTH), lambda n: (clamp(n), 0))
    slab = ATT_WIDTH + 2 * _GW
    return pl.pallas_call(
        body, name=name, grid=(nb + 1,), in_specs=_swa_specs(clamp) + [row, row, _ANY],
        out_specs=[pl.BlockSpec((WINDOW, slab), lambda n: (jnp.maximum(n - 1, 0), COL_AQ * HG_DK // slab)),
                   pl.BlockSpec((ATT_KV_HEADS, 8, _GQ), lambda n: (0, 0, 0))],
        out_shape=[jax.ShapeDtypeStruct(dproj.shape, dproj.dtype), jax.ShapeDtypeStruct((ATT_KV_HEADS, 8, _GQ), F32)],
        scratch_shapes=[pltpu.VMEM((WINDOW, ATT_WIDTH), BF16), pltpu.VMEM((WINDOW, _GW), F32), pltpu.VMEM((WINDOW, _GW), F32)],
        input_output_aliases={13: 0},
        compiler_params=_cp(("arbitrary",)),
    )(proj, proj, proj, proj, proj, cos, cos, sin, sin, pm, sinks, att, datt, dproj)


def swap_with_sibling(arrays, *, name):
    n = len(arrays)

    def body(*refs):
        ins, outs = refs[:n], refs[n:2 * n]
        send_sems, recv_sems = refs[2 * n:]
        x, y, c, _ = _place()
        copies = [pltpu.make_async_remote_copy(src_ref=ins[a], dst_ref=outs[a], send_sem=send_sems.at[a],
                                               recv_sem=recv_sems.at[a], device_id=(x, y, 1 - c), device_id_type=MESH)
                  for a in range(n)]
        for cp in copies:
            cp.start()
        for cp in copies:
            cp.wait()

    return pl.pallas_call(
        body, name=name, out_shape=[jax.ShapeDtypeStruct(a.shape, a.dtype) for a in arrays],
        in_specs=[_ANY] * n, out_specs=[_ANY] * n,
        scratch_shapes=[pltpu.SemaphoreType.DMA((n,)), pltpu.SemaphoreType.DMA((n,))],
    )(*arrays)


N_DEV = 8
SMALL_ROWS = 24


def allreduce_small(v, *, name):
    def body(v_ref, o_ref, recv_ref, send_sems, recv_sems):
        x, y, c, _ = _place()
        me = 4 * x + 2 * y + c
        recv_ref[me] = v_ref[...]

        def copy(k, slot):
            peer = (x ^ (k >> 2), y ^ ((k >> 1) & 1), c ^ (k & 1))
            return pltpu.make_async_remote_copy(src_ref=v_ref, dst_ref=recv_ref.at[slot], send_sem=send_sems.at[k - 1],
                                                recv_sem=recv_sems.at[k - 1], device_id=peer, device_id_type=MESH)

        for k in range(1, N_DEV):
            copy(k, me).start()
        for k in range(1, N_DEV):
            copy(k, me ^ k).wait()
        acc = recv_ref[0]
        for d in range(1, N_DEV):
            acc = acc + recv_ref[d]
        o_ref[...] = acc

    vm = pl.BlockSpec(memory_space=pltpu.VMEM)
    return pl.pallas_call(
        body, name=name, out_shape=jax.ShapeDtypeStruct(v.shape, v.dtype), in_specs=[vm], out_specs=vm,
        scratch_shapes=[pltpu.VMEM((N_DEV,) + v.shape, v.dtype), pltpu.SemaphoreType.DMA((N_DEV - 1,)),
                        pltpu.SemaphoreType.DMA((N_DEV - 1,))],
    )(v)


def sum_parts(layers, *, tr, name):
    nl = len(layers)
    _, R, C = layers[0].shape
    nr = R // tr

    def body(*refs):
        o_ref = refs[nl]
        for j in range(nl):
            @pl.when(pl.program_id(0) == j)
            def _(p_ref=refs[j]):
                acc = p_ref[0].astype(F32)
                for q in range(1, N_CHIPS):
                    acc = acc + p_ref[q].astype(F32)
                o_ref[...] = acc

    return pl.pallas_call(
        body, name=name, grid=(nl, nr),
        in_specs=[pl.BlockSpec((N_CHIPS, tr, C), lambda l, i, j=j: (0, jnp.where(l == j, i, 0), 0)) for j in range(nl)],
        out_specs=pl.BlockSpec((tr, C), lambda l, i: (l * nr + i, 0)), out_shape=jax.ShapeDtypeStruct((nl * R, C), F32),
        compiler_params=_cp(("arbitrary", "arbitrary")),
    )(*layers)


def adamw(g_a, g_b, w, m, v, *, tr, name):
    R, C = w.shape
    c1 = 1.0 - ADAM_B1 ** ADAM_STEP
    c2 = 1.0 - ADAM_B2 ** ADAM_STEP
    ng = 1 if g_b is None else 2

    def body(*refs):
        w_ref, m_ref, v_ref, g_ref, d_ref, nm_ref, nv_ref = refs[ng:]
        g = refs[0][...] if ng == 1 else refs[0][...] + refs[1][...]
        nm = ADAM_B1 * m_ref[...] + (1.0 - ADAM_B1) * g
        nv = ADAM_B2 * v_ref[...] + (1.0 - ADAM_B2) * (g * g)
        g_ref[...] = g
        nm_ref[...] = nm
        nv_ref[...] = nv
        d_ref[...] = -ADAM_LR * ((nm / c1) / (jnp.sqrt(nv / c2) + ADAM_EPS) + ADAM_WD * w_ref[...])

    blk = pl.BlockSpec((tr, C), lambda i: (i, 0))
    gs = [g_a] if g_b is None else [g_a, g_b]
    return pl.pallas_call(
        body, name=name, grid=(R // tr,), in_specs=[blk] * (ng + 3), out_specs=[blk] * 4,
        out_shape=[jax.ShapeDtypeStruct((R, C), F32)] * 4, compiler_params=_cp(("parallel",)),
    )(*gs, w, m, v)


def _lb_bounds(lb_logits):
    p = jax.nn.softmax(lb_logits.astype(F32), axis=0)
    return jnp.cumsum(p, axis=0) - p[0:1]


def _pack_small(n1, lb, hgn, n2, fin, sinks, extra):
    row = jnp.concatenate([sinks.reshape(-1), jnp.reshape(extra, (1,)),
                           jnp.zeros((D_MODEL - sinks.size - 1,), F32)])
    return jnp.concatenate([n1, lb, hgn, n2, fin[None], row[None], jnp.zeros((SMALL_ROWS - 18, D_MODEL), F32)], axis=0)


def _unpack_small(buf):
    return (buf[0:4], buf[4:8], buf[8:12], buf[12:16], buf[16], buf[17, :DEPTH * ATT_Q_HEADS].reshape(DEPTH, ATT_Q_HEADS),
            buf[17, DEPTH * ATT_Q_HEADS])


def _cols_full(g):
    q, r, c = g.shape
    return jnp.transpose(g, (1, 0, 2)).reshape(r, q * c)


def _cols_split(w):
    r, qc = w.shape
    return jnp.transpose(w.reshape(r, N_CHIPS, qc // N_CHIPS), (1, 0, 2))


def kernel(x, norm1, w_in, lb_logits, hg_norm, attn_sinks, w_pa, w_pb, w_o, norm2, w_gate, w_up, w_down, final_norm, loss_target, m_norm1, m_w_in, m_lb_logits, m_hg_norm, m_attn_sinks, m_w_pa, m_w_pb, m_w_o, m_norm2, m_w_gate, m_w_up, m_w_down, m_final_norm, v_norm1, v_w_in, v_lb_logits, v_hg_norm, v_attn_sinks, v_w_pa, v_w_pb, v_w_o, v_norm2, v_w_gate, v_w_up, v_w_down, v_final_norm):
    T = x.shape[1]
    H = FFN_HIDDEN
    tm = min(512, T)
    tmx = min(1024, T)
    tkx = min(2048, T)
    big = dict(w_in=w_in, w_pa=w_pa, w_pb=w_pb, w_o=w_o, w_gate=w_gate, w_up=w_up, w_down=w_down)
    col_sharded = ("w_in", "w_gate", "w_up")

    shards = {k: w.astype(BF16) for k, w in big.items()}
    ffn_pair = ("w_gate", "w_up")
    with_w_in = ("w_in", "w_pa", "w_pb", "w_o", "w_down")
    others = [k for k in big if k != "w_in"]

    def whole(k, g):
        return _cols_full(g) if k in col_sharded else g.reshape(-1, g.shape[-1])

    def layer_weights(blocks_l):
        full = {k: whole(k, blocks_l[k]) for k in big}
        full["w_in"] = _to_kernel_cols(full["w_in"])
        full["w_gu"] = jnp.concatenate([full.pop("w_gate"), full.pop("w_up")], axis=-1)
        return full

    def carried_by(kind, l):
        if l + 1 >= DEPTH:
            return []
        if l == 0:
            return [(0, k) for k in others] + [(1, "w_in")] if kind == "in_proj" else [(1, k) for k in others]
        return [(l + 1, k) for k in (with_w_in if kind == "in_proj" else ffn_pair)]

    lb_all, lb_vjp = jax.vjp(_lb_bounds, lb_logits)
    cos, sin, pm = rope_tables(T)

    xs = x[0]
    saved, weights = [], []
    blocks = [dict() for _ in range(DEPTH)]
    (blocks[0]["w_in"],) = exchange_between_chips([shards["w_in"][0]], scatter=False, name="gather_w_in_0")
    for l in range(DEPTH):
        tasks = carried_by("in_proj", l)
        h, proj, *got = ln_matmul(xs, norm1[l:l + 1], _to_kernel_cols(whole("w_in", blocks[l]["w_in"])), tm=tmx, tn=2560,
                                  name=f"in_proj_{l}", gather=[shards[k][j] for j, k in tasks])
        for (j, k), g in zip(tasks, got):
            blocks[j][k] = g
        full = layer_weights(blocks[l])
        weights.append(full)
        og, o_raw, states = hgrn_fwd(proj, lb_all[l:l + 1], hg_norm[l:l + 1], ct=min(512, T), hp=HG_HEADS, name=f"hgrn_fwd_{l}")
        att = swa_fwd(proj, cos, sin, pm, attn_sinks[l], name=f"swa_fwd_{l}")
        ya, yb, mix, x1 = merge_fwd(xs, og, att, proj, full["w_pa"], full["w_pb"], full["w_o"], tm=tm, name=f"merge_fwd_{l}")
        tasks = carried_by("ffn_up", l)
        h2, gu, act, *got = ln_matmul(x1, norm2[l:l + 1], full["w_gu"], tm=min(256, T), tn=2 * H, name=f"ffn_up_{l}",
                                      swiglu=True, gather=[shards[k][j] for j, k in tasks])
        for (j, k), g in zip(tasks, got):
            blocks[j][k] = g
        x2 = matmul_residual(x1, act, full["w_down"], tm=tmx, name=f"ffn_down_{l}")
        saved.append((xs, h, proj, og, o_raw, states, att, ya, yb, mix, x1, h2, gu, act))
        xs = x2

    dx, loss_parts, dfin_parts = loss_head(xs, final_norm[None], loss_target[0], tm=tm, name="loss_head")

    arrived = [None] * DEPTH
    dn1, dn2, dlb, dgn, dsk = ([None] * DEPTH for _ in range(5))
    for l in reversed(range(DEPTH)):
        xs, h, proj, og, o_raw, states, att, ya, yb, mix, x1, h2, gu, act = saved[l]
        full = weights[l]
        dw = {}
        dgu = ffn_down_bwd(dx, full["w_down"], gu, tm=min(256, T), name=f"ffn_down_bwd_{l}")
        dw["w_down"] = matmul_tn(act, dx, tka=H // 2, tn=D_MODEL, tk=tkx, name=f"dw_down_{l}")
        dwgu = matmul_tn(h2, dgu, tka=D_MODEL, tn=H // 2, tk=tkx, name=f"dw_gu_{l}")
        dw["w_gate"], dw["w_up"] = dwgu[:, :H], dwgu[:, H:]
        dx1, dn2_parts = matmul_nt_rmsbwd(dgu, full["w_gu"], x1, norm2[l:l + 1], dx, tm=tmx, tk=H // 2, name=f"ffn_up_bwd_{l}")
        dya, dyb, dproj, dog, datt = merge_bwd(dx1, full["w_o"], proj, ya, yb, full["w_pa"], full["w_pb"], tm=tm,
                                               name=f"merge_bwd_{l}")
        dw["w_o"] = matmul_tn(mix, dx1, tka=D_MODEL, tn=D_MODEL, tk=tkx, name=f"dw_o_{l}")
        dw["w_pa"] = matmul_tn(og, dya, tka=D_MODEL, tn=D_MODEL, tk=tkx, name=f"dw_pa_{l}")
        dw["w_pb"] = matmul_tn(att, dyb, tka=D_MODEL, tn=D_MODEL, tk=tkx, name=f"dw_pb_{l}")
        dproj, dsk_parts = swa_bwd(proj, cos, sin, pm, attn_sinks[l], att, datt, dproj, name=f"swa_bwd_{l}")
        dproj, dlb_parts, dgn_parts = hgrn_bwd(proj, lb_all[l:l + 1], hg_norm[l:l + 1], o_raw, states, dog, dproj,
                                               ct=min(512, T), name=f"hgrn_bwd_{l}")
        split = lambda k, g: _cols_split(g) if k in col_sharded else g.reshape(N_CHIPS, -1, g.shape[-1])
        dw["w_in"], *got_pair = matmul_tn(h, dproj, tka=D_MODEL, tn=1536, tk=tkx, name=f"dw_in_{l}",
                                          scatter=[split(k, dw[k]) for k in ffn_pair])
        dw["w_in"] = _from_kernel_cols(dw["w_in"])
        dx, dn1_parts, *got = matmul_nt_rmsbwd(dproj, full["w_in"], xs, norm1[l:l + 1], dx1, tm=tmx, tk=1536,
                                               name=f"in_proj_bwd_{l}", scatter=[split(k, dw[k]) for k in with_w_in])
        arrived[l] = {**dict(zip(with_w_in, got)), **dict(zip(ffn_pair, got_pair))}
        dn1[l], dn2[l], dlb[l], dgn[l] = dn1_parts.sum(0), dn2_parts.sum(0), dlb_parts.sum(0), dgn_parts.sum(0)
        dsk[l] = dsk_parts[:, 0, :].reshape(ATT_Q_HEADS, WINDOW).sum(-1)
    grad_x = dx[None]

    mine = _pack_small(jnp.stack(dn1), jnp.stack(dlb), jnp.stack(dgn), jnp.stack(dn2), dfin_parts.sum(0), jnp.stack(dsk),
                       loss_parts.sum())
    total = allreduce_small(mine, name="allreduce_small")
    g_n1, g_lb_all, g_gn, g_n2, g_fin, g_sk, loss = _unpack_small(total)
    (g_lb,) = lb_vjp(g_lb_all)
    zero = jnp.zeros((), F32)
    small = adamw(_pack_small(g_n1, g_lb, g_gn, g_n2, g_fin, g_sk, zero), None,
                  _pack_small(norm1, lb_logits, hg_norm, norm2, final_norm, attn_sinks, zero),
                  _pack_small(m_norm1, m_lb_logits, m_hg_norm, m_norm2, m_final_norm, m_attn_sinks, zero),
                  _pack_small(v_norm1, v_lb_logits, v_hg_norm, v_norm2, v_final_norm, v_attn_sinks, zero),
                  tr=SMALL_ROWS, name="adamw_small")
    small = [_unpack_small(s)[:6] for s in small]

    sums = []
    for k in big:
        rows = arrived[0][k].shape[1]
        sums.append(sum_parts([arrived[l][k] for l in range(DEPTH)], tr=128 if rows % 128 == 0 else 64, name=f"sum_{k}"))
    theirs = swap_with_sibling(sums, name="swap_sums")
    mom = dict(w_in=(m_w_in, v_w_in), w_pa=(m_w_pa, v_w_pa), w_pb=(m_w_pb, v_w_pb), w_o=(m_w_o, v_w_o),
               w_gate=(m_w_gate, v_w_gate), w_up=(m_w_up, v_w_up), w_down=(m_w_down, v_w_down))
    upd = {}
    for k, mine_k, theirs_k in zip(big, sums, theirs):
        shp = big[k].shape
        flat = lambda t: t.reshape(shp[0] * shp[1], shp[2])
        res = adamw(mine_k, theirs_k, flat(big[k]), flat(mom[k][0]), flat(mom[k][1]), tr=128, name=f"adamw_{k}")
        upd[k] = [t.reshape(shp) for t in res]

    order = ("norm1", "w_in", "lb_logits", "hg_norm", "attn_sinks", "w_pa", "w_pb", "w_o", "norm2", "w_gate", "w_up",
             "w_down", "final_norm")
    small_pos = dict(norm1=0, lb_logits=1, hg_norm=2, norm2=3, final_norm=4, attn_sinks=5)
    outs = [loss, grad_x]
    for kind in range(4):
        for name in order:
            outs.append(upd[name][kind] if name in upd else small[kind][small_pos[name]])
    return tuple(outs)
```

```python
import functools

import jax
import jax.numpy as jnp
import numpy as np
from jax import lax
from jax.experimental import pallas as pl
from jax.experimental.pallas import tpu as pltpu

F32 = jnp.float32
BF16 = jnp.bfloat16

D_MODEL = 1024
DEPTH = 4
HG_HEADS = 8
HG_DK = 128
HG_WIDTH = HG_HEADS * HG_DK
CHUNK = 64
ATT_Q_HEADS = 16
ATT_KV_HEADS = 4
ATT_GROUP = ATT_Q_HEADS // ATT_KV_HEADS
HEAD_DIM = 64
ATT_WIDTH = ATT_Q_HEADS * HEAD_DIM
ATT_KV_WIDTH = ATT_KV_HEADS * HEAD_DIM
WINDOW = 128
ROPE_THETA = 500000.0
ROPE_DIM = HEAD_DIM // 4
FFN_HIDDEN = 2816
IN_COLS = 4 * HG_WIDTH + ATT_WIDTH + 2 * ATT_KV_WIDTH + 2 * D_MODEL
EPS = 1e-6
MIN_F = 1e-30
COL_HQ, COL_HF, COL_HI, COL_HG = 0, 8, 16, 24
COL_GA, COL_GB, COL_AQ, COL_AK, COL_AV = 32, 40, 48, 56, 58


def _to_kernel_cols(w):
    return jnp.concatenate([w[..., :4096], w[..., 5632:7680], w[..., 4096:5120], w[..., 5120:5632]], axis=-1)


def _from_kernel_cols(w):
    return jnp.concatenate([w[..., :4096], w[..., 6144:7168], w[..., 7168:7680], w[..., 4096:6144]], axis=-1)

ADAM_LR = 0.001
ADAM_B1 = 0.9
ADAM_B2 = 0.999
ADAM_EPS = 1e-08
ADAM_WD = 0.01
ADAM_STEP = 10

VMEM_LIMIT_V7X = 56 * 1024 * 1024
MESH = pl.DeviceIdType.MESH


def _cp(sem, vmem=VMEM_LIMIT_V7X):
    return pltpu.CompilerParams(dimension_semantics=sem, vmem_limit_bytes=vmem)


def _sigmoid(x):
    return 1.0 / (1.0 + jnp.exp(-x))


def _dot(a, b):
    return jnp.dot(a, b, preferred_element_type=F32)


def _dot_nt(a, b):
    return lax.dot_general(a, b, (((1,), (1,)), ((), ())), preferred_element_type=F32)


def _dot_tn(a, b):
    return lax.dot_general(a, b, (((0,), (0,)), ((), ())), preferred_element_type=F32)


def _rowsum8(v):
    r, n = v.shape
    return jnp.sum(v.reshape(r // 8, 8, n), axis=0)


_ANY = pl.BlockSpec(memory_space=pl.ANY)
N_CHIPS = 4
PEER_CHIPS = N_CHIPS - 1


def _place():
    x, y, c = lax.axis_index("x"), lax.axis_index("y"), lax.axis_index("c")
    return x, y, c, [(1 - x, y), (x, 1 - y), (1 - x, 1 - y)]


def _exchange_scratch(n):
    return [pltpu.SemaphoreType.DMA((PEER_CHIPS * n,)), pltpu.SemaphoreType.DMA((PEER_CHIPS * n,)), pltpu.SemaphoreType.DMA((n,))]


def _exchange(ins, outs, sems, scatter):
    send_sems, recv_sems, local_sems = sems
    x, y, c, chips = _place()
    me = 2 * x + y
    n = len(ins)

    def remote(a, j, arriving):
        px, py = chips[j]
        them = 2 * px + py
        src = ins[a].at[them] if scatter else ins[a]
        return pltpu.make_async_remote_copy(
            src_ref=src, dst_ref=outs[a].at[them if arriving else me], send_sem=send_sems.at[a * PEER_CHIPS + j],
            recv_sem=recv_sems.at[a * PEER_CHIPS + j], device_id=(px, py, c), device_id_type=MESH)

    def local(a):
        return pltpu.make_async_copy(ins[a].at[me] if scatter else ins[a], outs[a].at[me], local_sems.at[a])

    def start():
        for a in range(n):
            local(a).start()
            for j in range(PEER_CHIPS):
                remote(a, j, False).start()

    def wait():
        for a in range(n):
            for j in range(PEER_CHIPS):
                remote(a, j, True).wait()
            local(a).wait()

    return start, wait


def _exchange_out_shapes(arrays, scatter):
    return [jax.ShapeDtypeStruct(a.shape if scatter else (N_CHIPS,) + a.shape, a.dtype) for a in arrays]


def exchange_between_chips(arrays, *, scatter, name):
    n = len(arrays)

    def body(*refs):
        start, wait = _exchange(refs[:n], refs[n:2 * n], refs[2 * n:], scatter)
        start()
        wait()

    return pl.pallas_call(
        body, name=name, out_shape=_exchange_out_shapes(arrays, scatter), in_specs=[_ANY] * n, out_specs=[_ANY] * n,
        scratch_shapes=_exchange_scratch(n),
    )(*arrays)


def ln_matmul(x, g, w, *, tm, tn, name, swiglu=False, gather=()):
    T, Dm = x.shape
    N = w.shape[1]
    if swiglu:
        assert tn == N
    half = N // 2
    ng = len(gather)
    n_out = 3 if swiglu else 2
    ni, nj = T // tm, N // tn

    def body(x_ref, g_ref, w_ref, *rest):
        gather_in, rest = rest[:ng], rest[ng:]
        h_ref, o_ref = rest[0], rest[1]
        i, j = pl.program_id(0), pl.program_id(1)
        if ng:
            start, wait = _exchange(gather_in, rest[n_out:n_out + ng], rest[n_out + ng:], False)
            pl.when((i == 0) & (j == 0))(start)

        @pl.when(j == 0)
        def _():
            xf = x_ref[...]
            r = lax.rsqrt(jnp.mean(xf * xf, axis=-1, keepdims=True) + EPS)
            h_ref[...] = (xf * r * g_ref[...]).astype(BF16)

        acc = _dot(h_ref[...], w_ref[...])
        o_ref[...] = acc.astype(BF16)
        if swiglu:
            gt, up = acc[:, :half], acc[:, half:]
            rest[2][...] = (gt * _sigmoid(gt) * up).astype(BF16)
        if ng:
            pl.when((i == ni - 1) & (j == nj - 1))(wait)

    out_shape = [jax.ShapeDtypeStruct((T, Dm), BF16), jax.ShapeDtypeStruct((T, N), BF16)]
    out_specs = [pl.BlockSpec((tm, Dm), lambda i, j: (i, 0)), pl.BlockSpec((tm, tn), lambda i, j: (i, j))]
    if swiglu:
        out_shape.append(jax.ShapeDtypeStruct((T, half), BF16))
        out_specs.append(pl.BlockSpec((tm, half), lambda i, j: (i, 0)))
    return pl.pallas_call(
        body, name=name, grid=(ni, nj),
        in_specs=[pl.BlockSpec((tm, Dm), lambda i, j: (i, 0)), pl.BlockSpec((1, Dm), lambda i, j: (0, 0)),
                  pl.BlockSpec((Dm, tn), lambda i, j: (0, j))] + [_ANY] * ng,
        out_specs=out_specs + [_ANY] * ng, out_shape=out_shape + _exchange_out_shapes(gather, False),
        scratch_shapes=_exchange_scratch(ng) if ng else [],
        compiler_params=_cp(("arbitrary", "arbitrary") if ng else ("parallel", "arbitrary")),
    )(x, g, w, *gather)


def matmul_residual(x, a, w, *, tm, name):
    T, N = x.shape
    K = a.shape[1]

    def body(x_ref, a_ref, w_ref, o_ref):
        o_ref[...] = x_ref[...] + _dot(a_ref[...], w_ref[...])

    return pl.pallas_call(
        body, name=name, grid=(T // tm,),
        in_specs=[pl.BlockSpec((tm, N), lambda i: (i, 0)), pl.BlockSpec((tm, K), lambda i: (i, 0)),
                  pl.BlockSpec((K, N), lambda i: (0, 0))],
        out_specs=pl.BlockSpec((tm, N), lambda i: (i, 0)), out_shape=jax.ShapeDtypeStruct((T, N), F32),
        compiler_params=_cp(("parallel",)),
    )(x, a, w)


def matmul_tn(a, g, *, tka, tn, tk, name, scatter=()):
    T, Ka = a.shape
    N = g.shape[1]
    ni, nj, nk = Ka // tka, N // tn, T // tk
    ns = len(scatter)

    def body(a_ref, g_ref, *rest):
        scatter_in, rest = rest[:ns], rest[ns:]
        o_ref, acc_ref = rest[0], rest[1 + ns]
        i, j, k = pl.program_id(0), pl.program_id(1), pl.program_id(2)
        if ns:
            start, wait = _exchange(scatter_in, rest[1:1 + ns], rest[2 + ns:], True)
            pl.when((i == 0) & (j == 0) & (k == 0))(start)

        @pl.when(k == 0)
        def _():
            acc_ref[...] = jnp.zeros_like(acc_ref)

        acc_ref[...] += _dot_tn(a_ref[...].astype(BF16), g_ref[...].astype(BF16))

        @pl.when(k == nk - 1)
        def _():
            o_ref[...] = acc_ref[...].astype(BF16)

        if ns:
            pl.when((i == ni - 1) & (j == nj - 1) & (k == nk - 1))(wait)

    res = pl.pallas_call(
        body, name=name, grid=(ni, nj, nk),
        in_specs=[pl.BlockSpec((tk, tka), lambda i, j, k: (k, i)), pl.BlockSpec((tk, tn), lambda i, j, k: (k, j))] + [_ANY] * ns,
        out_specs=[pl.BlockSpec((tka, tn), lambda i, j, k: (i, j))] + [_ANY] * ns,
        out_shape=[jax.ShapeDtypeStruct((Ka, N), BF16)] + _exchange_out_shapes(scatter, True),
        scratch_shapes=[pltpu.VMEM((tka, tn), F32)] + (_exchange_scratch(ns) if ns else []),
        compiler_params=_cp(("arbitrary",) * 3 if ns else ("parallel", "parallel", "arbitrary")),
    )(a, g, *scatter)
    return res if ns else res[0]


def matmul_nt_rmsbwd(a, w, x, g, dres, *, tm, tk, name, scatter=()):
    T, K = a.shape
    Dm = w.shape[0]
    nk = K // tk
    ni = T // tm
    ns = len(scatter)

    def body(a_ref, w_ref, x_ref, g_ref, dres_ref, *rest):
        scatter_in, rest = rest[:ns], rest[ns:]
        dx_ref, dg_ref = rest[0], rest[1]
        acc_ref = rest[2 + ns]
        i, k = pl.program_id(0), pl.program_id(1)
        if ns:
            start, wait = _exchange(scatter_in, rest[2:2 + ns], rest[3 + ns:], True)
            pl.when((i == 0) & (k == 0))(start)

        @pl.when(k == 0)
        def _():
            acc_ref[...] = jnp.zeros_like(acc_ref)

        @pl.when((i == 0) & (k == 0))
        def _():
            dg_ref[...] = jnp.zeros_like(dg_ref)

        acc_ref[...] += _dot_nt(a_ref[...], w_ref[...])

        @pl.when(k == nk - 1)
        def _():
            dh = acc_ref[...]
            xf = x_ref[...]
            r = lax.rsqrt(jnp.mean(xf * xf, axis=-1, keepdims=True) + EPS)
            xhat = xf * r
            dg_ref[...] += _rowsum8(dh * xhat)
            dxhat = dh * g_ref[...]
            dx_ref[...] = dres_ref[...] + r * (dxhat - xhat * jnp.mean(dxhat * xhat, axis=-1, keepdims=True))

        if ns:
            pl.when((i == ni - 1) & (k == nk - 1))(wait)

    return pl.pallas_call(
        body, name=name, grid=(ni, nk),
        in_specs=[pl.BlockSpec((tm, tk), lambda i, k: (i, k)), pl.BlockSpec((Dm, tk), lambda i, k: (0, k)),
                  pl.BlockSpec((tm, Dm), lambda i, k: (i, 0)), pl.BlockSpec((1, Dm), lambda i, k: (0, 0)),
                  pl.BlockSpec((tm, Dm), lambda i, k: (i, 0))] + [_ANY] * ns,
        out_specs=[pl.BlockSpec((tm, Dm), lambda i, k: (i, 0)), pl.BlockSpec((8, Dm), lambda i, k: (0, 0))] + [_ANY] * ns,
        out_shape=[jax.ShapeDtypeStruct((T, Dm), F32), jax.ShapeDtypeStruct((8, Dm), F32)] + _exchange_out_shapes(scatter, True),
        scratch_shapes=[pltpu.VMEM((tm, Dm), F32)] + (_exchange_scratch(ns) if ns else []),
        compiler_params=_cp(("arbitrary", "arbitrary")),
    )(a, w, x, g, dres, *scatter)


def ffn_down_bwd(dx, w_down, gu, *, tm, name):
    T, Dm = dx.shape
    H = w_down.shape[0]

    def body(dx_ref, w_ref, gu_ref, o_ref):
        dact = _dot_nt(dx_ref[...].astype(BF16), w_ref[...])
        gt = gu_ref[:, :H].astype(F32)
        up = gu_ref[:, H:].astype(F32)
        s = _sigmoid(gt)
        o_ref[:, :H] = (dact * up * (s * (1.0 + gt * (1.0 - s)))).astype(BF16)
        o_ref[:, H:] = (dact * gt * s).astype(BF16)

    return pl.pallas_call(
        body, name=name, grid=(T // tm,),
        in_specs=[pl.BlockSpec((tm, Dm), lambda i: (i, 0)), pl.BlockSpec((H, Dm), lambda i: (0, 0)),
                  pl.BlockSpec((tm, 2 * H), lambda i: (i, 0))],
        out_specs=pl.BlockSpec((tm, 2 * H), lambda i: (i, 0)), out_shape=jax.ShapeDtypeStruct((T, 2 * H), BF16),
        compiler_params=_cp(("parallel",)),
    )(dx, w_down, gu)


def _gate_specs(tm):
    return [pl.BlockSpec((tm, D_MODEL), lambda i, c=c: (i, c)) for c in (COL_GA // 8, COL_GB // 8)]


def merge_fwd(x, og, att, proj, w_pa, w_pb, w_o, *, tm, name):
    T, Dm = og.shape

    def body(x_ref, og_ref, att_ref, ga_ref, gb_ref, wa_ref, wb_ref, wo_ref, ya_ref, yb_ref, mix_ref, x1_ref):
        ya = _dot(og_ref[...], wa_ref[...])
        yb = _dot(att_ref[...], wb_ref[...])
        ya_ref[...] = ya.astype(BF16)
        yb_ref[...] = yb.astype(BF16)
        mix = (_sigmoid(ga_ref[...].astype(F32)) * ya + _sigmoid(gb_ref[...].astype(F32)) * yb).astype(BF16)
        mix_ref[...] = mix
        x1_ref[...] = x_ref[...] + _dot(mix, wo_ref[...])

    row = pl.BlockSpec((tm, Dm), lambda i: (i, 0))
    wsp = pl.BlockSpec((Dm, Dm), lambda i: (0, 0))
    return pl.pallas_call(
        body, name=name, grid=(T // tm,),
        in_specs=[row, row, row] + _gate_specs(tm) + [wsp, wsp, wsp],
        out_specs=[row, row, row, row], out_shape=[jax.ShapeDtypeStruct((T, Dm), BF16)] * 3 + [jax.ShapeDtypeStruct((T, Dm), F32)],
        compiler_params=_cp(("parallel",)),
    )(x, og, att, proj, proj, w_pa, w_pb, w_o)


def merge_bwd(dx, w_o, proj, ya, yb, w_pa, w_pb, *, tm, name):
    T, Dm = dx.shape

    def body(dx_ref, w_ref, ga_ref, gb_ref, ya_ref, yb_ref, wa_ref, wb_ref, dya_ref, dyb_ref, dgate_ref, dog_ref, datt_ref):
        dmix = _dot_nt(dx_ref[...].astype(BF16), w_ref[...])
        sa = _sigmoid(ga_ref[...].astype(F32))
        sb = _sigmoid(gb_ref[...].astype(F32))
        dya = (dmix * sa).astype(BF16)
        dyb = (dmix * sb).astype(BF16)
        dya_ref[...] = dya
        dyb_ref[...] = dyb
        dgate_ref[:, :Dm] = (dmix * ya_ref[...].astype(F32) * sa * (1.0 - sa)).astype(BF16)
        dgate_ref[:, Dm:] = (dmix * yb_ref[...].astype(F32) * sb * (1.0 - sb)).astype(BF16)
        dog_ref[...] = _dot_nt(dya, wa_ref[...]).astype(BF16)
        datt_ref[...] = _dot_nt(dyb, wb_ref[...]).astype(BF16)

    row = pl.BlockSpec((tm, Dm), lambda i: (i, 0))
    wsp = pl.BlockSpec((Dm, Dm), lambda i: (0, 0))
    return pl.pallas_call(
        body, name=name, grid=(T // tm,),
        in_specs=[row, wsp] + _gate_specs(tm) + [row, row, wsp, wsp],
        out_specs=[row, row, pl.BlockSpec((tm, 2 * Dm), lambda i: (i, COL_GA // 16)), row, row],
        out_shape=[jax.ShapeDtypeStruct((T, Dm), BF16)] * 2 + [jax.ShapeDtypeStruct((T, IN_COLS), BF16)]
        + [jax.ShapeDtypeStruct((T, Dm), BF16)] * 2,
        compiler_params=_cp(("parallel",)),
    )(dx, w_o, proj, proj, ya, yb, w_pa, w_pb)


def loss_head(x, g, target, *, tm, name):
    T, Dm = x.shape

    def body(x_ref, g_ref, t_ref, dx_ref, l_ref, dg_ref):
        @pl.when(pl.program_id(0) == 0)
        def _():
            l_ref[...] = jnp.zeros_like(l_ref)
            dg_ref[...] = jnp.zeros_like(dg_ref)

        xf = x_ref[...]
        gv = g_ref[...]
        r = lax.rsqrt(jnp.mean(xf * xf, axis=-1, keepdims=True) + EPS)
        xhat = xf * r
        err = xhat * gv - t_ref[...]
        l_ref[...] += _rowsum8(err * err) * (0.5 / Dm)
        dy = err * (1.0 / Dm)
        dg_ref[...] += _rowsum8(dy * xhat)
        dxhat = dy * gv
        dx_ref[...] = r * (dxhat - xhat * jnp.mean(dxhat * xhat, axis=-1, keepdims=True))

    row = pl.BlockSpec((tm, Dm), lambda i: (i, 0))
    acc = pl.BlockSpec((8, Dm), lambda i: (0, 0))
    return pl.pallas_call(
        body, name=name, grid=(T // tm,),
        in_specs=[row, pl.BlockSpec((1, Dm), lambda i: (0, 0)), row],
        out_specs=[row, acc, acc],
        out_shape=[jax.ShapeDtypeStruct((T, Dm), F32), jax.ShapeDtypeStruct((8, Dm), F32), jax.ShapeDtypeStruct((8, Dm), F32)],
        compiler_params=_cp(("arbitrary",)),
    )(x, g, target)


_LEVELS = (32, 16, 8)
_DIAG = 8
_SUBLANES = 8


def _chunk_consts():
    t = lax.broadcasted_iota(jnp.int32, (CHUNK, CHUNK), 0)
    s = lax.broadcasted_iota(jnp.int32, (CHUNK, CHUNK), 1)
    tri = (s <= t).astype(F32)
    tri_t = (s >= t).astype(F32)
    masks = []
    for m in _LEVELS:
        sh = int(np.log2(2 * m))
        masks.append(((t >> sh) == (s >> sh)) & ((t & (2 * m - 1)) >= m) & ((s & (2 * m - 1)) < m))
    return tri, tri_t, masks


def _exact_dot(sel, x):
    hi = x.astype(BF16)
    r1 = x - hi.astype(F32)
    mid = r1.astype(BF16)
    lo = (r1 - mid.astype(F32)).astype(BF16)
    return _dot(jnp.concatenate([sel, sel, sel], axis=1).astype(BF16), jnp.concatenate([hi, mid, lo], axis=0))


def _level_ref(b, m):
    if 2 * m >= _SUBLANES:
        pieces = [jnp.broadcast_to(b[p * 2 * m + m - 1:p * 2 * m + m, :], (2 * m, b.shape[1])) for p in range(CHUNK // (2 * m))]
        return pieces[0] if len(pieces) == 1 else jnp.concatenate(pieces, axis=0)
    groups = CHUNK // _SUBLANES
    b3 = b.reshape(groups, _SUBLANES, b.shape[1])
    row = lax.broadcasted_iota(jnp.int32, b3.shape, 1)
    ref = None
    for p in reversed(range(_SUBLANES // (2 * m))):
        src = jnp.broadcast_to(b3[:, p * 2 * m + m - 1:p * 2 * m + m, :], b3.shape)
        ref = src if ref is None else jnp.where(row < (p + 1) * 2 * m, src, ref)
    return ref.reshape(b.shape)


def _intra_off(q, kk, b, masks):
    a = jnp.zeros((CHUNK, CHUNK), F32)
    keep = []
    for m, mask in zip(_LEVELS, masks):
        e = jnp.exp2(-jnp.abs(b - _level_ref(b, m)))
        qt = (q * e).astype(BF16)
        kt = (kk * e).astype(BF16)
        a = a + jnp.where(mask, _dot_nt(qt, kt), 0.0)
        keep.append((mask, e, qt, kt))
    return a, keep


def _roll_rows(x, d):
    g = x.shape[0] // _DIAG
    return pltpu.roll(x.reshape(g, _DIAG, x.shape[1]), d % _DIAG, 1).reshape(x.shape)


def _diag_fwd(q, kk, v, b):
    lower = lax.broadcasted_iota(jnp.int32, q.shape, 0) & (_DIAG - 1)
    out = jnp.sum(q * kk, axis=-1, keepdims=True) * v
    for d in range(1, _DIAG):
        dec = jnp.exp2(jnp.where(lower >= d, b - _roll_rows(b, d), _NEG))
        out = out + jnp.sum(q * dec * _roll_rows(kk, d), axis=-1, keepdims=True) * _roll_rows(v, d)
    return out


def _diag_bwd(q, kk, v, b, do):
    lower = lax.broadcasted_iota(jnp.int32, q.shape, 0) & (_DIAG - 1)
    a = jnp.sum(q * kk, axis=-1, keepdims=True)
    da = jnp.sum(do * v, axis=-1, keepdims=True)
    dq, dk, dv = da * kk, da * q, a * do
    for d in range(1, _DIAG):
        kr = _roll_rows(kk, d)
        dec = jnp.exp2(jnp.where(lower >= d, b - _roll_rows(b, d), _NEG))
        qd = q * dec
        a = jnp.sum(qd * kr, axis=-1, keepdims=True)
        da = jnp.sum(do * _roll_rows(v, d), axis=-1, keepdims=True)
        dq = dq + da * dec * kr
        dk = dk + _roll_rows(da * qd, -d)
        dv = dv + _roll_rows(a * do, -d)
    return dq, dk, dv


def _hgrn_in_specs(ct, hp, order):
    def spec(col):
        return pl.BlockSpec((ct, hp * HG_DK), lambda h, i, col=col: (order(i), col // hp + h))
    vec = pl.BlockSpec((1, hp * HG_DK), lambda h, i: (0, h))
    return [spec(COL_HQ), spec(COL_HF), spec(COL_HI), spec(COL_HG), vec, vec]


def hgrn_fwd(proj, lb, gn, *, ct, hp, name):
    T = proj.shape[0]
    nc = ct // CHUNK

    def body(hq_ref, hf_ref, hi_ref, hg_ref, lb_ref, gn_ref, og_ref, o_ref, st_ref, s_ref):
        @pl.when(pl.program_id(1) == 0)
        def _():
            s_ref[...] = jnp.zeros_like(s_ref)

        tri, _, masks = _chunk_consts()

        def chunk(c, carry):
            sl = pl.ds(pl.multiple_of(c * CHUNK, CHUNK), CHUNK)
            lb_all = lb_ref[...]
            f_all = lb_all + (1.0 - lb_all) * _sigmoid(hf_ref[sl, :].astype(F32))
            b_all = _exact_dot(tri, jnp.log2(jnp.maximum(f_all, MIN_F)))
            for p in range(hp):
                cols = slice(p * HG_DK, (p + 1) * HG_DK)
                hq = hq_ref[sl, cols].astype(F32)
                v = hi_ref[sl, cols].astype(F32)
                gate = hg_ref[sl, cols].astype(F32)
                q = hq * _sigmoid(hq)
                f = f_all[:, cols]
                kk = 1.0 - f
                b = b_all[:, cols]
                st = s_ref[p]
                st_ref[p, c] = st
                a, _ = _intra_off(q, kk, b, masks)
                vb = v.astype(BF16)
                o = (_dot_nt((q * jnp.exp2(b)).astype(BF16), st.astype(BF16)) + _dot(a.astype(BF16), vb)
                     + _diag_fwd(q, kk, v, b))
                bl = b[CHUNK - 1:CHUNK, :]
                s_ref[p] = st * jnp.exp2(bl) + _dot_tn(vb, (kk * jnp.exp2(bl - b)).astype(BF16))
                o_ref[sl, cols] = o
                r = lax.rsqrt(jnp.mean(o * o, axis=-1, keepdims=True) + EPS)
                og_ref[sl, cols] = (o * r * gn_ref[:, cols] * (gate * _sigmoid(gate))).astype(BF16)
            return carry

        lax.fori_loop(0, nc, chunk, 0)

    blk = pl.BlockSpec((ct, hp * HG_DK), lambda h, i: (i, h))
    return pl.pallas_call(
        body, name=name, grid=(HG_HEADS // hp, T // ct),
        in_specs=_hgrn_in_specs(ct, hp, lambda i: i),
        out_specs=[blk, blk, pl.BlockSpec((hp, nc, HG_DK, HG_DK), lambda h, i: (h, i, 0, 0))],
        out_shape=[jax.ShapeDtypeStruct((T, HG_WIDTH), BF16), jax.ShapeDtypeStruct((T, HG_WIDTH), F32),
                   jax.ShapeDtypeStruct((HG_HEADS, T // CHUNK, HG_DK, HG_DK), F32)],
        scratch_shapes=[pltpu.VMEM((hp, HG_DK, HG_DK), F32)],
        compiler_params=_cp(("parallel", "arbitrary")),
    )(proj, proj, proj, proj, lb, gn)


def hgrn_bwd(proj, lb, gn, o_raw, states, dog, dproj, *, ct, name):
    T = proj.shape[0]
    nc = ct // CHUNK
    nblk = T // ct
    hp = HG_HEADS

    def body(hq_ref, hf_ref, hi_ref, hg_ref, lb_ref, gn_ref, o_ref, st_ref, dog_ref, _dproj_in,
             dp_ref, dlb_ref, dgn_ref, ds_ref):
        dhq_ref, dhf_ref, dhi_ref, dhg_ref = (dp_ref.at[:, s * HG_WIDTH:(s + 1) * HG_WIDTH] for s in range(4))
        @pl.when(pl.program_id(1) == 0)
        def _():
            ds_ref[...] = jnp.zeros_like(ds_ref)
            dlb_ref[...] = jnp.zeros_like(dlb_ref)
            dgn_ref[...] = jnp.zeros_like(dgn_ref)

        tri, tri_t, masks = _chunk_consts()
        last_row = lax.broadcasted_iota(jnp.int32, (CHUNK, HG_DK), 0) == CHUNK - 1

        def chunk(ci, carry):
            c = nc - 1 - ci
            sl = pl.ds(pl.multiple_of(c * CHUNK, CHUNK), CHUNK)
            lb_all = lb_ref[...]
            sig_all = _sigmoid(hf_ref[sl, :].astype(F32))
            f_all = lb_all + (1.0 - lb_all) * sig_all
            b_all = _exact_dot(tri, jnp.log2(jnp.maximum(f_all, MIN_F)))
            dbs, dks = [], []
            for p in range(hp):
                cols = slice(p * HG_DK, (p + 1) * HG_DK)
                gn_v = gn_ref[:, cols]
                hq = hq_ref[sl, cols].astype(F32)
                v = hi_ref[sl, cols].astype(F32)
                gate = hg_ref[sl, cols].astype(F32)
                sq = _sigmoid(hq)
                q = hq * sq
                f = f_all[:, cols]
                kk = 1.0 - f
                b = b_all[:, cols]
                st = st_ref[p, c]
                dst = ds_ref[p]
                o = o_ref[sl, cols]
                r = lax.rsqrt(jnp.mean(o * o, axis=-1, keepdims=True) + EPS)
                xhat = o * r
                sg = _sigmoid(gate)
                dog_v = dog_ref[sl, cols].astype(F32)
                dy = dog_v * (gate * sg)
                dhg_ref[sl, cols] = (dog_v * xhat * gn_v * (sg * (1.0 + gate * (1.0 - sg)))).astype(BF16)
                dgn_ref[:, cols] += _rowsum8(dy * xhat)
                dxh = dy * gn_v
                do = r * (dxh - xhat * jnp.mean(dxh * xhat, axis=-1, keepdims=True))
                dob = do.astype(BF16)
                vb = v.astype(BF16)
                stb = st.astype(BF16)
                dstb = dst.astype(BF16)
                eb = jnp.exp2(b)
                bl = b[CHUNK - 1:CHUNK, :]
                ebl = jnp.exp2(bl)
                edec = jnp.exp2(bl - b)
                qe = q * eb
                kdec = kk * edec
                a, keep = _intra_off(q, kk, b, masks)
                da = _dot_nt(dob, vb)
                dkdec = _dot(vb, dstb)
                dq = _dot(dob, stb) * eb
                dk = dkdec * edec
                dv = _dot_tn(a.astype(BF16), dob) + _dot_nt(kdec.astype(BF16), dstb)
                for mask, e, qt, kt in keep:
                    dsm = jnp.where(mask, da, 0.0).astype(BF16)
                    dq = dq + _dot(dsm, kt) * e
                    dk = dk + _dot_tn(dsm, qt) * e
                ddq, ddk, ddv = _diag_bwd(q, kk, v, b, do)
                dq = dq + ddq
                dk = dk + ddk
                dhi_ref[sl, cols] = (dv + ddv).astype(BF16)
                dbl = jnp.sum(dkdec * kdec, axis=0, keepdims=True) + ebl * jnp.sum(st * dst, axis=0, keepdims=True)
                db = q * dq - kk * dk
                dbs.append(jnp.where(last_row, db + dbl, db))
                dks.append(dk)
                ds_ref[p] = _dot_tn(dob, qe.astype(BF16)) + dst * ebl
                dhq_ref[sl, cols] = (dq * (sq * (1.0 + hq * (1.0 - sq)))).astype(BF16)
            dlogf = _exact_dot(tri_t, jnp.concatenate(dbs, axis=1))
            df = jnp.where(f_all > MIN_F, dlogf / f_all, 0.0) - jnp.concatenate(dks, axis=1)
            dhf_ref[sl, :] = (df * (1.0 - lb_all) * sig_all * (1.0 - sig_all)).astype(BF16)
            dlb_ref[...] += _rowsum8(df * (1.0 - sig_all))
            return carry

        lax.fori_loop(0, nc, chunk, 0)

    rev = lambda i: nblk - 1 - i
    blk = pl.BlockSpec((ct, hp * HG_DK), lambda h, i: (rev(i), h))
    acc = pl.BlockSpec((8, hp * HG_DK), lambda h, i: (0, h))
    return pl.pallas_call(
        body, name=name, grid=(HG_HEADS // hp, nblk),
        in_specs=_hgrn_in_specs(ct, hp, rev) + [blk, pl.BlockSpec((hp, nc, HG_DK, HG_DK), lambda h, i: (h, rev(i), 0, 0)), blk, _ANY],
        out_specs=[pl.BlockSpec((ct, 4 * HG_WIDTH), lambda h, i: (rev(i), 0)), acc, acc],
        out_shape=[jax.ShapeDtypeStruct(dproj.shape, dproj.dtype)] + [jax.ShapeDtypeStruct((8, HG_WIDTH), F32)] * 2,
        scratch_shapes=[pltpu.VMEM((hp, HG_DK, HG_DK), F32)],
        input_output_aliases={9: 0},
        compiler_params=_cp(("parallel", "arbitrary")),
    )(proj, proj, proj, proj, lb, gn, o_raw, states, dog, dproj)


_GW = ATT_GROUP * HEAD_DIM
_NEG = -1e30


def rope_tables(T):
    half = ROPE_DIM // 2
    inv = ROPE_THETA ** (-jnp.arange(half, dtype=F32) * 2.0 / ROPE_DIM)
    ang = jnp.arange(T, dtype=F32)[:, None] * inv[None, :]
    pad1 = jnp.ones((T, HEAD_DIM - ROPE_DIM), F32)
    cos = jnp.concatenate([jnp.cos(ang), jnp.cos(ang), pad1], axis=1)
    sin = jnp.concatenate([jnp.sin(ang), jnp.sin(ang), 0.0 * pad1], axis=1)
    p = np.zeros((_GW, _GW), np.float32)
    for base in range(0, _GW, HEAD_DIM):
        for i in range(half):
            p[base + i + half, base + i] = -1.0
            p[base + i, base + i + half] = 1.0
    reps = _GW // HEAD_DIM
    return jnp.tile(cos, (1, reps)), jnp.tile(sin, (1, reps)), jnp.asarray(p, BF16)


def _rope(x, cos, sin, pm):
    return x.astype(F32) * cos + _dot(x, pm) * sin


def _unrope(dx, cos, sin, pm):
    return dx * cos + _dot_nt((dx * sin).astype(BF16), pm)


def _groups_to_rows(x):
    return jnp.concatenate([x[:, g * _GW:(g + 1) * _GW] for g in range(ATT_KV_HEADS)], axis=0)


def _rows_to_groups(x):
    return jnp.concatenate([x[g * WINDOW:(g + 1) * WINDOW] for g in range(ATT_KV_HEADS)], axis=1)


def _tile_rows(t):
    return jnp.concatenate([t] * ATT_KV_HEADS, axis=0)


_GQ = ATT_GROUP * WINDOW


def _swa_mask(n, queries_on_rows):
    shape = (_GQ, 2 * WINDOW) if queries_on_rows else (2 * WINDOW, _GQ)
    r = lax.broadcasted_iota(jnp.int32, shape, 0 if queries_on_rows else 1) & (WINDOW - 1)
    c = lax.broadcasted_iota(jnp.int32, shape, 1 if queries_on_rows else 0)
    delta = r + WINDOW - c
    return (delta >= 0) & (delta < WINDOW) & ((c >= WINDOW) | (n > 0))


def _sink_vector(sink_ref, g, queries_on_rows):
    shape = (_GQ, 1) if queries_on_rows else (1, _GQ)
    i = lax.broadcasted_iota(jnp.int32, shape, 0 if queries_on_rows else 1)
    out = jnp.full(shape, sink_ref[g * ATT_GROUP + ATT_GROUP - 1], F32)
    for hh in range(ATT_GROUP - 2, -1, -1):
        out = jnp.where(i < (hh + 1) * WINDOW, sink_ref[g * ATT_GROUP + hh], out)
    return out


def _head(x, h):
    return x[:, h * HEAD_DIM:(h + 1) * HEAD_DIM]


def _stack_heads(x):
    return jnp.concatenate([_head(x, hh) for hh in range(ATT_GROUP)], axis=0)


def _unstack_heads(x):
    return jnp.concatenate([x[hh * WINDOW:(hh + 1) * WINDOW] for hh in range(ATT_GROUP)], axis=1)


def _swa_specs(cur):
    prev = lambda n: jnp.maximum(n - 1, 0)
    kv = lambda col, f: pl.BlockSpec((WINDOW, _GW), lambda n: (f(n), col))
    tab = lambda f: pl.BlockSpec((WINDOW, _GW), lambda n: (f(n), 0))
    return [pl.BlockSpec((WINDOW, ATT_WIDTH), lambda n: (cur(n), COL_AQ // 8)),
            kv(COL_AK // 2, prev), kv(COL_AK // 2, cur), kv(COL_AV // 2, prev), kv(COL_AV // 2, cur),
            tab(prev), tab(cur), tab(prev), tab(cur),
            pl.BlockSpec((_GW, _GW), lambda n: (0, 0)), pl.BlockSpec(memory_space=pltpu.SMEM)]


def swa_fwd(proj, cos, sin, pm, sinks, *, name):
    T = proj.shape[0]
    nb = T // WINDOW
    scale = HEAD_DIM ** -0.5

    def body(q_ref, kp_ref, kc_ref, vp_ref, vc_ref, cp_ref, cc_ref, sp_ref, sc_ref, pm_ref, sink_ref, o_ref):
        n = pl.program_id(0)
        pm_v = pm_ref[...]
        mask = _swa_mask(n, True)
        k_cat = jnp.concatenate([_rope(kp_ref[...], cp_ref[...], sp_ref[...], pm_v),
                                 _rope(kc_ref[...], cc_ref[...], sc_ref[...], pm_v)], axis=0).astype(BF16)
        v_cat = jnp.concatenate([vp_ref[...], vc_ref[...]], axis=0)
        q_rot = (_rope(_groups_to_rows(q_ref[...]), _tile_rows(cc_ref[...]), _tile_rows(sc_ref[...]), pm_v) * scale).astype(BF16)
        for g in range(ATT_KV_HEADS):
            qs = _stack_heads(q_rot[g * WINDOW:(g + 1) * WINDOW])
            sk = _sink_vector(sink_ref, g, True)
            s = jnp.where(mask, _dot_nt(qs, _head(k_cat, g)), _NEG)
            m = jnp.maximum(jnp.max(s, axis=-1, keepdims=True), sk)
            p = jnp.exp(s - m)
            l = jnp.sum(p, axis=-1, keepdims=True) + jnp.exp(sk - m)
            o = _dot(p.astype(BF16), _head(v_cat, g)) / l
            o_ref[:, g * _GW:(g + 1) * _GW] = _unstack_heads(o).astype(BF16)

    return pl.pallas_call(
        body, name=name, grid=(nb,), in_specs=_swa_specs(lambda n: n),
        out_specs=pl.BlockSpec((WINDOW, ATT_WIDTH), lambda n: (n, 0)),
        out_shape=jax.ShapeDtypeStruct((T, ATT_WIDTH), BF16), compiler_params=_cp(("parallel",)),
    )(proj, proj, proj, proj, proj, cos, cos, sin, sin, pm, sinks)


def swa_bwd(proj, cos, sin, pm, sinks, att, datt, dproj, *, name):
    T = proj.shape[0]
    nb = T // WINDOW
    scale = HEAD_DIM ** -0.5
    clamp = lambda n: jnp.minimum(n, nb - 1)

    def body(q_ref, kp_ref, kc_ref, vp_ref, vc_ref, cp_ref, cc_ref, sp_ref, sc_ref, pm_ref, sink_ref, att_ref, datt_ref,
             _dproj_in, dp_ref, dsink_ref, cq_ref, ck_ref, cv_ref):
        n = pl.program_id(0)
        pm_v = pm_ref[...]

        @pl.when(n == 0)
        def _():
            dsink_ref[...] = jnp.zeros_like(dsink_ref)
            cq_ref[...] = jnp.zeros_like(cq_ref)
            ck_ref[...] = jnp.zeros_like(ck_ref)
            cv_ref[...] = jnp.zeros_like(cv_ref)

        def write_prev(dk_prev, dv_prev):
            dp_ref[:, :ATT_WIDTH] = cq_ref[...]
            dk = _unrope(ck_ref[...] + dk_prev, cp_ref[...], sp_ref[...], pm_v)
            dp_ref[:, ATT_WIDTH:ATT_WIDTH + _GW] = dk.astype(BF16)
            dp_ref[:, ATT_WIDTH + _GW:] = (cv_ref[...] + dv_prev).astype(BF16)

        @pl.when(n < nb)
        def _():
            mask = _swa_mask(n, False)
            ones = jnp.ones((8, 2 * HEAD_DIM), BF16)
            k_cat = jnp.concatenate([_rope(kp_ref[...], cp_ref[...], sp_ref[...], pm_v),
                                     _rope(kc_ref[...], cc_ref[...], sc_ref[...], pm_v)], axis=0).astype(BF16)
            v_cat = jnp.concatenate([vp_ref[...], vc_ref[...]], axis=0)
            dqs, dks, dvs = [], [], []
            cos_q, sin_q = _tile_rows(cc_ref[...]), _tile_rows(sc_ref[...])
            q_rot = (_rope(_groups_to_rows(q_ref[...]), cos_q, sin_q, pm_v) * scale).astype(BF16)
            for g in range(ATT_KV_HEADS):
                gc = slice(g * _GW, (g + 1) * _GW)
                qs = _stack_heads(q_rot[g * WINDOW:(g + 1) * WINDOW])
                dos = _stack_heads(datt_ref[:, gc])
                os_ = _stack_heads(att_ref[:, gc])
                kh, vh = _head(k_cat, g), _head(v_cat, g)
                sk = _sink_vector(sink_ref, g, False)
                s = jnp.where(mask, _dot_nt(kh, qs), _NEG)
                m = jnp.maximum(jnp.max(s, axis=0, keepdims=True), sk)
                e = jnp.exp(s - m)
                es = jnp.exp(sk - m)
                inv_l = 1.0 / (jnp.sum(e, axis=0, keepdims=True) + es)
                p = e * inv_l
                prod = dos.astype(F32) * os_.astype(F32)
                prod_hi = prod.astype(BF16)
                prod_lo = (prod - prod_hi.astype(F32)).astype(BF16)
                dsum = _dot_nt(ones, jnp.concatenate([prod_hi, prod_lo], axis=1))[0:1]
                dsink_ref[g, 0:1, :] += -(es * inv_l) * dsum
                ds = (p * (_dot_nt(vh, dos) - dsum)).astype(BF16)
                dqs.append(_unstack_heads(_dot_tn(ds, kh) * scale))
                dks.append(_dot(ds, qs))
                dvs.append(_dot(p.astype(BF16), dos))
            dk_all = jnp.concatenate(dks, axis=1)
            dv_all = jnp.concatenate(dvs, axis=1)

            @pl.when(n > 0)
            def _():
                write_prev(dk_all[:WINDOW], dv_all[:WINDOW])

            cq_ref[...] = _rows_to_groups(_unrope(jnp.concatenate(dqs, axis=0), cos_q, sin_q, pm_v)).astype(BF16)
            ck_ref[...] = dk_all[WINDOW:]
            cv_ref[...] = dv_all[WINDOW:]

        @pl.when(n == nb)
        def _():
            write_prev(jnp.zeros((WINDOW, _GW), F32), jnp.zeros((WINDOW, _GW), F32))

    row = pl.BlockSpec((WINDOW, ATT_WIDTH), lambda n: (clamp(n), 0))
    slab = ATT_WIDTH + 2 * _GW
    return pl.pallas_call(
        body, name=name, grid=(nb + 1,), in_specs=_swa_specs(clamp) + [row, row, _ANY],
        out_specs=[pl.BlockSpec((WINDOW, slab), lambda n: (jnp.maximum(n - 1, 0), COL_AQ * HG_DK // slab)),
                   pl.BlockSpec((ATT_KV_HEADS, 8, _GQ), lambda n: (0, 0, 0))],
        out_shape=[jax.ShapeDtypeStruct(dproj.shape, dproj.dtype), jax.ShapeDtypeStruct((ATT_KV_HEADS, 8, _GQ), F32)],
        scratch_shapes=[pltpu.VMEM((WINDOW, ATT_WIDTH), BF16), pltpu.VMEM((WINDOW, _GW), F32), pltpu.VMEM((WINDOW, _GW), F32)],
        input_output_aliases={13: 0},
        compiler_params=_cp(("arbitrary",)),
    )(proj, proj, proj, proj, proj, cos, cos, sin, sin, pm, sinks, att, datt, dproj)


def swap_with_sibling(arrays, *, name):
    n = len(arrays)

    def body(*refs):
        ins, outs = refs[:n], refs[n:2 * n]
        send_sems, recv_sems = refs[2 * n:]
        x, y, c, _ = _place()
        copies = [pltpu.make_async_remote_copy(src_ref=ins[a], dst_ref=outs[a], send_sem=send_sems.at[a],
                                               recv_sem=recv_sems.at[a], device_id=(x, y, 1 - c), device_id_type=MESH)
                  for a in range(n)]
        for cp in copies:
            cp.start()
        for cp in copies:
            cp.wait()

    return pl.pallas_call(
        body, name=name, out_shape=[jax.ShapeDtypeStruct(a.shape, a.dtype) for a in arrays],
        in_specs=[_ANY] * n, out_specs=[_ANY] * n,
        scratch_shapes=[pltpu.SemaphoreType.DMA((n,)), pltpu.SemaphoreType.DMA((n,))],
    )(*arrays)


N_DEV = 8
SMALL_ROWS = 24


def allreduce_small(v, *, name):
    def body(v_ref, o_ref, recv_ref, send_sems, recv_sems):
        x, y, c, _ = _place()
        me = 4 * x + 2 * y + c
        recv_ref[me] = v_ref[...]

        def copy(k, slot):
            peer = (x ^ (k >> 2), y ^ ((k >> 1) & 1), c ^ (k & 1))
            return pltpu.make_async_remote_copy(src_ref=v_ref, dst_ref=recv_ref.at[slot], send_sem=send_sems.at[k - 1],
                                                recv_sem=recv_sems.at[k - 1], device_id=peer, device_id_type=MESH)

        for k in range(1, N_DEV):
            copy(k, me).start()
        for k in range(1, N_DEV):
            copy(k, me ^ k).wait()
        acc = recv_ref[0]
        for d in range(1, N_DEV):
            acc = acc + recv_ref[d]
        o_ref[...] = acc

    vm = pl.BlockSpec(memory_space=pltpu.VMEM)
    return pl.pallas_call(
        body, name=name, out_shape=jax.ShapeDtypeStruct(v.shape, v.dtype), in_specs=[vm], out_specs=vm,
        scratch_shapes=[pltpu.VMEM((N_DEV,) + v.shape, v.dtype), pltpu.SemaphoreType.DMA((N_DEV - 1,)),
                        pltpu.SemaphoreType.DMA((N_DEV - 1,))],
    )(v)


def sum_parts(layers, *, tr, name):
    nl = len(layers)
    _, R, C = layers[0].shape
    nr = R // tr

    def body(*refs):
        o_ref = refs[nl]
        for j in range(nl):
            @pl.when(pl.program_id(0) == j)
            def _(p_ref=refs[j]):
                acc = p_ref[0].astype(F32)
                for q in range(1, N_CHIPS):
                    acc = acc + p_ref[q].astype(F32)
                o_ref[...] = acc

    return pl.pallas_call(
        body, name=name, grid=(nl, nr),
        in_specs=[pl.BlockSpec((N_CHIPS, tr, C), lambda l, i, j=j: (0, jnp.where(l == j, i, 0), 0)) for j in range(nl)],
        out_specs=pl.BlockSpec((tr, C), lambda l, i: (l * nr + i, 0)), out_shape=jax.ShapeDtypeStruct((nl * R, C), F32),
        compiler_params=_cp(("arbitrary", "arbitrary")),
    )(*layers)


def adamw(g_a, g_b, w, m, v, *, tr, name):
    R, C = w.shape
    c1 = 1.0 - ADAM_B1 ** ADAM_STEP
    c2 = 1.0 - ADAM_B2 ** ADAM_STEP
    ng = 1 if g_b is None else 2

    def body(*refs):
        w_ref, m_ref, v_ref, g_ref, d_ref, nm_ref, nv_ref = refs[ng:]
        g = refs[0][...] if ng == 1 else refs[0][...] + refs[1][...]
        nm = ADAM_B1 * m_ref[...] + (1.0 - ADAM_B1) * g
        nv = ADAM_B2 * v_ref[...] + (1.0 - ADAM_B2) * (g * g)
        g_ref[...] = g
        nm_ref[...] = nm
        nv_ref[...] = nv
        d_ref[...] = -ADAM_LR * ((nm / c1) / (jnp.sqrt(nv / c2) + ADAM_EPS) + ADAM_WD * w_ref[...])

    blk = pl.BlockSpec((tr, C), lambda i: (i, 0))
    gs = [g_a] if g_b is None else [g_a, g_b]
    return pl.pallas_call(
        body, name=name, grid=(R // tr,), in_specs=[blk] * (ng + 3), out_specs=[blk] * 4,
        out_shape=[jax.ShapeDtypeStruct((R, C), F32)] * 4, compiler_params=_cp(("parallel",)),
    )(*gs, w, m, v)


def _lb_bounds(lb_logits):
    p = jax.nn.softmax(lb_logits.astype(F32), axis=0)
    return jnp.cumsum(p, axis=0) - p[0:1]


def _pack_small(n1, lb, hgn, n2, fin, sinks, extra):
    row = jnp.concatenate([sinks.reshape(-1), jnp.reshape(extra, (1,)),
                           jnp.zeros((D_MODEL - sinks.size - 1,), F32)])
    return jnp.concatenate([n1, lb, hgn, n2, fin[None], row[None], jnp.zeros((SMALL_ROWS - 18, D_MODEL), F32)], axis=0)


def _unpack_small(buf):
    return (buf[0:4], buf[4:8], buf[8:12], buf[12:16], buf[16], buf[17, :DEPTH * ATT_Q_HEADS].reshape(DEPTH, ATT_Q_HEADS),
            buf[17, DEPTH * ATT_Q_HEADS])


def _cols_full(g):
    q, r, c = g.shape
    return jnp.transpose(g, (1, 0, 2)).reshape(r, q * c)


def _cols_split(w):
    r, qc = w.shape
    return jnp.transpose(w.reshape(r, N_CHIPS, qc // N_CHIPS), (1, 0, 2))


def kernel(x, norm1, w_in, lb_logits, hg_norm, attn_sinks, w_pa, w_pb, w_o, norm2, w_gate, w_up, w_down, final_norm, loss_target, m_norm1, m_w_in, m_lb_logits, m_hg_norm, m_attn_sinks, m_w_pa, m_w_pb, m_w_o, m_norm2, m_w_gate, m_w_up, m_w_down, m_final_norm, v_norm1, v_w_in, v_lb_logits, v_hg_norm, v_attn_sinks, v_w_pa, v_w_pb, v_w_o, v_norm2, v_w_gate, v_w_up, v_w_down, v_final_norm):
    T = x.shape[1]
    H = FFN_HIDDEN
    tm = min(512, T)
    tmx = min(1024, T)
    tkx = min(2048, T)
    big = dict(w_in=w_in, w_pa=w_pa, w_pb=w_pb, w_o=w_o, w_gate=w_gate, w_up=w_up, w_down=w_down)
    col_sharded = ("w_in", "w_gate", "w_up")

    shards = {k: w.astype(BF16) for k, w in big.items()}
    ffn_pair = ("w_gate", "w_up")
    with_w_in = ("w_in", "w_pa", "w_pb", "w_o", "w_down")
    others = [k for k in big if k != "w_in"]

    def whole(k, g):
        return _cols_full(g) if k in col_sharded else g.reshape(-1, g.shape[-1])

    def layer_weights(blocks_l):
        full = {k: whole(k, blocks_l[k]) for k in big}
        full["w_in"] = _to_kernel_cols(full["w_in"])
        full["w_gu"] = jnp.concatenate([full.pop("w_gate"), full.pop("w_up")], axis=-1)
        return full

    def carried_by(kind, l):
        if l + 1 >= DEPTH:
            return []
        if l == 0:
            return [(0, k) for k in others] + [(1, "w_in")] if kind == "in_proj" else [(1, k) for k in others]
        return [(l + 1, k) for k in (with_w_in if kind == "in_proj" else ffn_pair)]

    lb_all, lb_vjp = jax.vjp(_lb_bounds, lb_logits)
    cos, sin, pm = rope_tables(T)

    xs = x[0]
    saved, weights = [], []
    blocks = [dict() for _ in range(DEPTH)]
    (blocks[0]["w_in"],) = exchange_between_chips([shards["w_in"][0]], scatter=False, name="gather_w_in_0")
    for l in range(DEPTH):
        tasks = carried_by("in_proj", l)
        h, proj, *got = ln_matmul(xs, norm1[l:l + 1], _to_kernel_cols(whole("w_in", blocks[l]["w_in"])), tm=tmx, tn=2560,
                                  name=f"in_proj_{l}", gather=[shards[k][j] for j, k in tasks])
        for (j, k), g in zip(tasks, got):
            blocks[j][k] = g
        full = layer_weights(blocks[l])
        weights.append(full)
        og, o_raw, states = hgrn_fwd(proj, lb_all[l:l + 1], hg_norm[l:l + 1], ct=min(512, T), hp=HG_HEADS, name=f"hgrn_fwd_{l}")
        att = swa_fwd(proj, cos, sin, pm, attn_sinks[l], name=f"swa_fwd_{l}")
        ya, yb, mix, x1 = merge_fwd(xs, og, att, proj, full["w_pa"], full["w_pb"], full["w_o"], tm=tm, name=f"merge_fwd_{l}")
        tasks = carried_by("ffn_up", l)
        h2, gu, act, *got = ln_matmul(x1, norm2[l:l + 1], full["w_gu"], tm=min(256, T), tn=2 * H, name=f"ffn_up_{l}",
                                      swiglu=True, gather=[shards[k][j] for j, k in tasks])
        for (j, k), g in zip(tasks, got):
            blocks[j][k] = g
        x2 = matmul_residual(x1, act, full["w_down"], tm=tmx, name=f"ffn_down_{l}")
        saved.append((xs, h, proj, og, o_raw, states, att, ya, yb, mix, x1, h2, gu, act))
        xs = x2

    dx, loss_parts, dfin_parts = loss_head(xs, final_norm[None], loss_target[0], tm=tm, name="loss_head")

    arrived = [None] * DEPTH
    dn1, dn2, dlb, dgn, dsk = ([None] * DEPTH for _ in range(5))
    for l in reversed(range(DEPTH)):
        xs, h, proj, og, o_raw, states, att, ya, yb, mix, x1, h2, gu, act = saved[l]
        full = weights[l]
        dw = {}
        dgu = ffn_down_bwd(dx, full["w_down"], gu, tm=min(256, T), name=f"ffn_down_bwd_{l}")
        dw["w_down"] = matmul_tn(act, dx, tka=H // 2, tn=D_MODEL, tk=tkx, name=f"dw_down_{l}")
        dwgu = matmul_tn(h2, dgu, tka=D_MODEL, tn=H // 2, tk=tkx, name=f"dw_gu_{l}")
        dw["w_gate"], dw["w_up"] = dwgu[:, :H], dwgu[:, H:]
        dx1, dn2_parts = matmul_nt_rmsbwd(dgu, full["w_gu"], x1, norm2[l:l + 1], dx, tm=tmx, tk=H // 2, name=f"ffn_up_bwd_{l}")
        dya, dyb, dproj, dog, datt = merge_bwd(dx1, full["w_o"], proj, ya, yb, full["w_pa"], full["w_pb"], tm=tm,
                                               name=f"merge_bwd_{l}")
        dw["w_o"] = matmul_tn(mix, dx1, tka=D_MODEL, tn=D_MODEL, tk=tkx, name=f"dw_o_{l}")
        dw["w_pa"] = matmul_tn(og, dya, tka=D_MODEL, tn=D_MODEL, tk=tkx, name=f"dw_pa_{l}")
        dw["w_pb"] = matmul_tn(att, dyb, tka=D_MODEL, tn=D_MODEL, tk=tkx, name=f"dw_pb_{l}")
        dproj, dsk_parts = swa_bwd(proj, cos, sin, pm, attn_sinks[l], att, datt, dproj, name=f"swa_bwd_{l}")
        dproj, dlb_parts, dgn_parts = hgrn_bwd(proj, lb_all[l:l + 1], hg_norm[l:l + 1], o_raw, states, dog, dproj,
                                               ct=min(512, T), name=f"hgrn_bwd_{l}")
        split = lambda k, g: _cols_split(g) if k in col_sharded else g.reshape(N_CHIPS, -1, g.shape[-1])
        dw["w_in"], *got_pair = matmul_tn(h, dproj, tka=D_MODEL, tn=1536, tk=tkx, name=f"dw_in_{l}",
                                          scatter=[split(k, dw[k]) for k in ffn_pair])
        dw["w_in"] = _from_kernel_cols(dw["w_in"])
        dx, dn1_parts, *got = matmul_nt_rmsbwd(dproj, full["w_in"], xs, norm1[l:l + 1], dx1, tm=tmx, tk=1536,
                                               name=f"in_proj_bwd_{l}", scatter=[split(k, dw[k]) for k in with_w_in])
        arrived[l] = {**dict(zip(with_w_in, got)), **dict(zip(ffn_pair, got_pair))}
        dn1[l], dn2[l], dlb[l], dgn[l] = dn1_parts.sum(0), dn2_parts.sum(0), dlb_parts.sum(0), dgn_parts.sum(0)
        dsk[l] = dsk_parts[:, 0, :].reshape(ATT_Q_HEADS, WINDOW).sum(-1)
    grad_x = dx[None]

    mine = _pack_small(jnp.stack(dn1), jnp.stack(dlb), jnp.stack(dgn), jnp.stack(dn2), dfin_parts.sum(0), jnp.stack(dsk),
                       loss_parts.sum())
    total = allreduce_small(mine, name="allreduce_small")
    g_n1, g_lb_all, g_gn, g_n2, g_fin, g_sk, loss = _unpack_small(total)
    (g_lb,) = lb_vjp(g_lb_all)
    zero = jnp.zeros((), F32)
    small = adamw(_pack_small(g_n1, g_lb, g_gn, g_n2, g_fin, g_sk, zero), None,
                  _pack_small(norm1, lb_logits, hg_norm, norm2, final_norm, attn_sinks, zero),
                  _pack_small(m_norm1, m_lb_logits, m_hg_norm, m_norm2, m_final_norm, m_attn_sinks, zero),
                  _pack_small(v_norm1, v_lb_logits, v_hg_norm, v_norm2, v_final_norm, v_attn_sinks, zero),
                  tr=SMALL_ROWS, name="adamw_small")
    small = [_unpack_small(s)[:6] for s in small]

    sums = []
    for k in big:
        rows = arrived[0][k].shape[1]
        sums.append(sum_parts([arrived[l][k] for l in range(DEPTH)], tr=128 if rows % 128 == 0 else 64, name=f"sum_{k}"))
    theirs = swap_with_sibling(sums, name="swap_sums")
    mom = dict(w_in=(m_w_in, v_w_in), w_pa=(m_w_pa, v_w_pa), w_pb=(m_w_pb, v_w_pb), w_o=(m_w_o, v_w_o),
               w_gate=(m_w_gate, v_w_gate), w_up=(m_w_up, v_w_up), w_down=(m_w_down, v_w_down))
    upd = {}
    for k, mine_k, theirs_k in zip(big, sums, theirs):
        shp = big[k].shape
        flat = lambda t: t.reshape(shp[0] * shp[1], shp[2])
        res = adamw(mine_k, theirs_k, flat(big[k]), flat(mom[k][0]), flat(mom[k][1]), tr=128, name=f"adamw_{k}")
        upd[k] = [t.reshape(shp) for t in res]

    order = ("norm1", "w_in", "lb_logits", "hg_norm", "attn_sinks", "w_pa", "w_pb", "w_o", "norm2", "w_gate", "w_up",
             "w_down", "final_norm")
    small_pos = dict(norm1=0, lb_logits=1, hg_norm=2, norm2=3, final_norm=4, attn_sinks=5)
    outs = [loss, grad_x]
    for kind in range(4):
        for name in order:
            outs.append(upd[name][kind] if name in upd else small[kind][small_pos[name]])
    return tuple(outs)
```

```python
import functools

import jax
import jax.numpy as jnp
import numpy as np
from jax import lax
from jax.experimental import pallas as pl
from jax.experimental.pallas import tpu as pltpu

F32 = jnp.float32
BF16 = jnp.bfloat16

D_MODEL = 1024
DEPTH = 4
HG_HEADS = 8
HG_DK = 128
HG_WIDTH = HG_HEADS * HG_DK
CHUNK = 64
ATT_Q_HEADS = 16
ATT_KV_HEADS = 4
ATT_GROUP = ATT_Q_HEADS // ATT_KV_HEADS
HEAD_DIM = 64
ATT_WIDTH = ATT_Q_HEADS * HEAD_DIM
ATT_KV_WIDTH = ATT_KV_HEADS * HEAD_DIM
WINDOW = 128
ROPE_THETA = 500000.0
ROPE_DIM = HEAD_DIM // 4
FFN_HIDDEN = 2816
IN_COLS = 4 * HG_WIDTH + ATT_WIDTH + 2 * ATT_KV_WIDTH + 2 * D_MODEL
EPS = 1e-6
MIN_F = 1e-30
COL_HQ, COL_HF, COL_HI, COL_HG = 0, 8, 16, 24
COL_GA, COL_GB, COL_AQ, COL_AK, COL_AV = 32, 40, 48, 56, 58


def _to_kernel_cols(w):
    return jnp.concatenate([w[..., :4096], w[..., 5632:7680], w[..., 4096:5120], w[..., 5120:5632]], axis=-1)


def _from_kernel_cols(w):
    return jnp.concatenate([w[..., :4096], w[..., 6144:7168], w[..., 7168:7680], w[..., 4096:6144]], axis=-1)

ADAM_LR = 0.001
ADAM_B1 = 0.9
ADAM_B2 = 0.999
ADAM_EPS = 1e-08
ADAM_WD = 0.01
ADAM_STEP = 10

VMEM_LIMIT_V7X = 56 * 1024 * 1024
MESH = pl.DeviceIdType.MESH


def _cp(sem, vmem=VMEM_LIMIT_V7X):
    return pltpu.CompilerParams(dimension_semantics=sem, vmem_limit_bytes=vmem)


def _sigmoid(x):
    return 1.0 / (1.0 + jnp.exp(-x))


def _dot(a, b):
    return jnp.dot(a, b, preferred_element_type=F32)


def _dot_nt(a, b):
    return lax.dot_general(a, b, (((1,), (1,)), ((), ())), preferred_element_type=F32)


def _dot_tn(a, b):
    return lax.dot_general(a, b, (((0,), (0,)), ((), ())), preferred_element_type=F32)


def _rowsum8(v):
    r, n = v.shape
    return jnp.sum(v.reshape(r // 8, 8, n), axis=0)


_ANY = pl.BlockSpec(memory_space=pl.ANY)
N_CHIPS = 4
PEER_CHIPS = N_CHIPS - 1


def _place():
    x, y, c = lax.axis_index("x"), lax.axis_index("y"), lax.axis_index("c")
    return x, y, c, [(1 - x, y), (x, 1 - y), (1 - x, 1 - y)]


def _exchange_scratch(n):
    return [pltpu.SemaphoreType.DMA((PEER_CHIPS * n,)), pltpu.SemaphoreType.DMA((PEER_CHIPS * n,)), pltpu.SemaphoreType.DMA((n,))]


def _exchange(ins, outs, sems, scatter):
    send_sems, recv_sems, local_sems = sems
    x, y, c, chips = _place()
    me = 2 * x + y
    n = len(ins)

    def remote(a, j, arriving):
        px, py = chips[j]
        them = 2 * px + py
        src = ins[a].at[them] if scatter else ins[a]
        return pltpu.make_async_remote_copy(
            src_ref=src, dst_ref=outs[a].at[them if arriving else me], send_sem=send_sems.at[a * PEER_CHIPS + j],
            recv_sem=recv_sems.at[a * PEER_CHIPS + j], device_id=(px, py, c), device_id_type=MESH)

    def local(a):
        return pltpu.make_async_copy(ins[a].at[me] if scatter else ins[a], outs[a].at[me], local_sems.at[a])

    def start():
        for a in range(n):
            local(a).start()
            for j in range(PEER_CHIPS):
                remote(a, j, False).start()

    def wait():
        for a in range(n):
            for j in range(PEER_CHIPS):
                remote(a, j, True).wait()
            local(a).wait()

    return start, wait


def _exchange_out_shapes(arrays, scatter):
    return [jax.ShapeDtypeStruct(a.shape if scatter else (N_CHIPS,) + a.shape, a.dtype) for a in arrays]


def exchange_between_chips(arrays, *, scatter, name):
    n = len(arrays)

    def body(*refs):
        start, wait = _exchange(refs[:n], refs[n:2 * n], refs[2 * n:], scatter)
        start()
        wait()

    return pl.pallas_call(
        body, name=name, out_shape=_exchange_out_shapes(arrays, scatter), in_specs=[_ANY] * n, out_specs=[_ANY] * n,
        scratch_shapes=_exchange_scratch(n),
    )(*arrays)


def ln_matmul(x, g, w, *, tm, tn, name, swiglu=False, gather=()):
    T, Dm = x.shape
    N = w.shape[1]
    if swiglu:
        assert tn == N
    half = N // 2
    ng = len(gather)
    n_out = 3 if swiglu else 2
    ni, nj = T // tm, N // tn

    def body(x_ref, g_ref, w_ref, *rest):
        gather_in, rest = rest[:ng], rest[ng:]
        h_ref, o_ref = rest[0], rest[1]
        i, j = pl.program_id(0), pl.program_id(1)
        if ng:
            start, wait = _exchange(gather_in, rest[n_out:n_out + ng], rest[n_out + ng:], False)
            pl.when((i == 0) & (j == 0))(start)

        @pl.when(j == 0)
        def _():
            xf = x_ref[...]
            r = lax.rsqrt(jnp.mean(xf * xf, axis=-1, keepdims=True) + EPS)
            h_ref[...] = (xf * r * g_ref[...]).astype(BF16)

        acc = _dot(h_ref[...], w_ref[...])
        o_ref[...] = acc.astype(BF16)
        if swiglu:
            gt, up = acc[:, :half], acc[:, half:]
            rest[2][...] = (gt * _sigmoid(gt) * up).astype(BF16)
        if ng:
            pl.when((i == ni - 1) & (j == nj - 1))(wait)

    out_shape = [jax.ShapeDtypeStruct((T, Dm), BF16), jax.ShapeDtypeStruct((T, N), BF16)]
    out_specs = [pl.BlockSpec((tm, Dm), lambda i, j: (i, 0)), pl.BlockSpec((tm, tn), lambda i, j: (i, j))]
    if swiglu:
        out_shape.append(jax.ShapeDtypeStruct((T, half), BF16))
        out_specs.append(pl.BlockSpec((tm, half), lambda i, j: (i, 0)))
    return pl.pallas_call(
        body, name=name, grid=(ni, nj),
        in_specs=[pl.BlockSpec((tm, Dm), lambda i, j: (i, 0)), pl.BlockSpec((1, Dm), lambda i, j: (0, 0)),
                  pl.BlockSpec((Dm, tn), lambda i, j: (0, j))] + [_ANY] * ng,
        out_specs=out_specs + [_ANY] * ng, out_shape=out_shape + _exchange_out_shapes(gather, False),
        scratch_shapes=_exchange_scratch(ng) if ng else [],
        compiler_params=_cp(("arbitrary", "arbitrary") if ng else ("parallel", "arbitrary")),
    )(x, g, w, *gather)


def matmul_residual(x, a, w, *, tm, name):
    T, N = x.shape
    K = a.shape[1]

    def body(x_ref, a_ref, w_ref, o_ref):
        o_ref[...] = x_ref[...] + _dot(a_ref[...], w_ref[...])

    return pl.pallas_call(
        body, name=name, grid=(T // tm,),
        in_specs=[pl.BlockSpec((tm, N), lambda i: (i, 0)), pl.BlockSpec((tm, K), lambda i: (i, 0)),
                  pl.BlockSpec((K, N), lambda i: (0, 0))],
        out_specs=pl.BlockSpec((tm, N), lambda i: (i, 0)), out_shape=jax.ShapeDtypeStruct((T, N), F32),
        compiler_params=_cp(("parallel",)),
    )(x, a, w)


def matmul_tn(a, g, *, tka, tn, tk, name, scatter=()):
    T, Ka = a.shape
    N = g.shape[1]
    ni, nj, nk = Ka // tka, N // tn, T // tk
    ns = len(scatter)

    def body(a_ref, g_ref, *rest):
        scatter_in, rest = rest[:ns], rest[ns:]
        o_ref, acc_ref = rest[0], rest[1 + ns]
        i, j, k = pl.program_id(0), pl.program_id(1), pl.program_id(2)
        if ns:
            start, wait = _exchange(scatter_in, rest[1:1 + ns], rest[2 + ns:], True)
            pl.when((i == 0) & (j == 0) & (k == 0))(start)

        @pl.when(k == 0)
        def _():
            acc_ref[...] = jnp.zeros_like(acc_ref)

        acc_ref[...] += _dot_tn(a_ref[...].astype(BF16), g_ref[...].astype(BF16))

        @pl.when(k == nk - 1)
        def _():
            o_ref[...] = acc_ref[...].astype(BF16)

        if ns:
            pl.when((i == ni - 1) & (j == nj - 1) & (k == nk - 1))(wait)

    res = pl.pallas_call(
        body, name=name, grid=(ni, nj, nk),
        in_specs=[pl.BlockSpec((tk, tka), lambda i, j, k: (k, i)), pl.BlockSpec((tk, tn), lambda i, j, k: (k, j))] + [_ANY] * ns,
        out_specs=[pl.BlockSpec((tka, tn), lambda i, j, k: (i, j))] + [_ANY] * ns,
        out_shape=[jax.ShapeDtypeStruct((Ka, N), BF16)] + _exchange_out_shapes(scatter, True),
        scratch_shapes=[pltpu.VMEM((tka, tn), F32)] + (_exchange_scratch(ns) if ns else []),
        compiler_params=_cp(("arbitrary",) * 3 if ns else ("parallel", "parallel", "arbitrary")),
    )(a, g, *scatter)
    return res if ns else res[0]


def matmul_nt_rmsbwd(a, w, x, g, dres, *, tm, tk, name, scatter=()):
    T, K = a.shape
    Dm = w.shape[0]
    nk = K // tk
    ni = T // tm
    ns = len(scatter)

    def body(a_ref, w_ref, x_ref, g_ref, dres_ref, *rest):
        scatter_in, rest = rest[:ns], rest[ns:]
        dx_ref, dg_ref = rest[0], rest[1]
        acc_ref = rest[2 + ns]
        i, k = pl.program_id(0), pl.program_id(1)
        if ns:
            start, wait = _exchange(scatter_in, rest[2:2 + ns], rest[3 + ns:], True)
            pl.when((i == 0) & (k == 0))(start)

        @pl.when(k == 0)
        def _():
            acc_ref[...] = jnp.zeros_like(acc_ref)

        @pl.when((i == 0) & (k == 0))
        def _():
            dg_ref[...] = jnp.zeros_like(dg_ref)

        acc_ref[...] += _dot_nt(a_ref[...], w_ref[...])

        @pl.when(k == nk - 1)
        def _():
            dh = acc_ref[...]
            xf = x_ref[...]
            r = lax.rsqrt(jnp.mean(xf * xf, axis=-1, keepdims=True) + EPS)
            xhat = xf * r
            dg_ref[...] += _rowsum8(dh * xhat)
            dxhat = dh * g_ref[...]
            dx_ref[...] = dres_ref[...] + r * (dxhat - xhat * jnp.mean(dxhat * xhat, axis=-1, keepdims=True))

        if ns:
            pl.when((i == ni - 1) & (k == nk - 1))(wait)

    return pl.pallas_call(
        body, name=name, grid=(ni, nk),
        in_specs=[pl.BlockSpec((tm, tk), lambda i, k: (i, k)), pl.BlockSpec((Dm, tk), lambda i, k: (0, k)),
                  pl.BlockSpec((tm, Dm), lambda i, k: (i, 0)), pl.BlockSpec((1, Dm), lambda i, k: (0, 0)),
                  pl.BlockSpec((tm, Dm), lambda i, k: (i, 0))] + [_ANY] * ns,
        out_specs=[pl.BlockSpec((tm, Dm), lambda i, k: (i, 0)), pl.BlockSpec((8, Dm), lambda i, k: (0, 0))] + [_ANY] * ns,
        out_shape=[jax.ShapeDtypeStruct((T, Dm), F32), jax.ShapeDtypeStruct((8, Dm), F32)] + _exchange_out_shapes(scatter, True),
        scratch_shapes=[pltpu.VMEM((tm, Dm), F32)] + (_exchange_scratch(ns) if ns else []),
        compiler_params=_cp(("arbitrary", "arbitrary")),
    )(a, w, x, g, dres, *scatter)


def ffn_down_bwd(dx, w_down, gu, *, tm, name):
    T, Dm = dx.shape
    H = w_down.shape[0]

    def body(dx_ref, w_ref, gu_ref, o_ref):
        dact = _dot_nt(dx_ref[...].astype(BF16), w_ref[...])
        gt = gu_ref[:, :H].astype(F32)
        up = gu_ref[:, H:].astype(F32)
        s = _sigmoid(gt)
        o_ref[:, :H] = (dact * up * (s * (1.0 + gt * (1.0 - s)))).astype(BF16)
        o_ref[:, H:] = (dact * gt * s).astype(BF16)

    return pl.pallas_call(
        body, name=name, grid=(T // tm,),
        in_specs=[pl.BlockSpec((tm, Dm), lambda i: (i, 0)), pl.BlockSpec((H, Dm), lambda i: (0, 0)),
                  pl.BlockSpec((tm, 2 * H), lambda i: (i, 0))],
        out_specs=pl.BlockSpec((tm, 2 * H), lambda i: (i, 0)), out_shape=jax.ShapeDtypeStruct((T, 2 * H), BF16),
        compiler_params=_cp(("parallel",)),
    )(dx, w_down, gu)


def _gate_specs(tm):
    return [pl.BlockSpec((tm, D_MODEL), lambda i, c=c: (i, c)) for c in (COL_GA // 8, COL_GB // 8)]


def merge_fwd(x, og, att, proj, w_pa, w_pb, w_o, *, tm, name):
    T, Dm = og.shape

    def body(x_ref, og_ref, att_ref, ga_ref, gb_ref, wa_ref, wb_ref, wo_ref, ya_ref, yb_ref, mix_ref, x1_ref):
        ya = _dot(og_ref[...], wa_ref[...])
        yb = _dot(att_ref[...], wb_ref[...])
        ya_ref[...] = ya.astype(BF16)
        yb_ref[...] = yb.astype(BF16)
        mix = (_sigmoid(ga_ref[...].astype(F32)) * ya + _sigmoid(gb_ref[...].astype(F32)) * yb).astype(BF16)
        mix_ref[...] = mix
        x1_ref[...] = x_ref[...] + _dot(mix, wo_ref[...])

    row = pl.BlockSpec((tm, Dm), lambda i: (i, 0))
    wsp = pl.BlockSpec((Dm, Dm), lambda i: (0, 0))
    return pl.pallas_call(
        body, name=name, grid=(T // tm,),
        in_specs=[row, row, row] + _gate_specs(tm) + [wsp, wsp, wsp],
        out_specs=[row, row, row, row], out_shape=[jax.ShapeDtypeStruct((T, Dm), BF16)] * 3 + [jax.ShapeDtypeStruct((T, Dm), F32)],
        compiler_params=_cp(("parallel",)),
    )(x, og, att, proj, proj, w_pa, w_pb, w_o)


def merge_bwd(dx, w_o, proj, ya, yb, w_pa, w_pb, *, tm, name):
    T, Dm = dx.shape

    def body(dx_ref, w_ref, ga_ref, gb_ref, ya_ref, yb_ref, wa_ref, wb_ref, dya_ref, dyb_ref, dgate_ref, dog_ref, datt_ref):
        dmix = _dot_nt(dx_ref[...].astype(BF16), w_ref[...])
        sa = _sigmoid(ga_ref[...].astype(F32))
        sb = _sigmoid(gb_ref[...].astype(F32))
        dya = (dmix * sa).astype(BF16)
        dyb = (dmix * sb).astype(BF16)
        dya_ref[...] = dya
        dyb_ref[...] = dyb
        dgate_ref[:, :Dm] = (dmix * ya_ref[...].astype(F32) * sa * (1.0 - sa)).astype(BF16)
        dgate_ref[:, Dm:] = (dmix * yb_ref[...].astype(F32) * sb * (1.0 - sb)).astype(BF16)
        dog_ref[...] = _dot_nt(dya, wa_ref[...]).astype(BF16)
        datt_ref[...] = _dot_nt(dyb, wb_ref[...]).astype(BF16)

    row = pl.BlockSpec((tm, Dm), lambda i: (i, 0))
    wsp = pl.BlockSpec((Dm, Dm), lambda i: (0, 0))
    return pl.pallas_call(
        body, name=name, grid=(T // tm,),
        in_specs=[row, wsp] + _gate_specs(tm) + [row, row, wsp, wsp],
        out_specs=[row, row, pl.BlockSpec((tm, 2 * Dm), lambda i: (i, COL_GA // 16)), row, row],
        out_shape=[jax.ShapeDtypeStruct((T, Dm), BF16)] * 2 + [jax.ShapeDtypeStruct((T, IN_COLS), BF16)]
        + [jax.ShapeDtypeStruct((T, Dm), BF16)] * 2,
        compiler_params=_cp(("parallel",)),
    )(dx, w_o, proj, proj, ya, yb, w_pa, w_pb)


def loss_head(x, g, target, *, tm, name):
    T, Dm = x.shape

    def body(x_ref, g_ref, t_ref, dx_ref, l_ref, dg_ref):
        @pl.when(pl.program_id(0) == 0)
        def _():
            l_ref[...] = jnp.zeros_like(l_ref)
            dg_ref[...] = jnp.zeros_like(dg_ref)

        xf = x_ref[...]
        gv = g_ref[...]
        r = lax.rsqrt(jnp.mean(xf * xf, axis=-1, keepdims=True) + EPS)
        xhat = xf * r
        err = xhat * gv - t_ref[...]
        l_ref[...] += _rowsum8(err * err) * (0.5 / Dm)
        dy = err * (1.0 / Dm)
        dg_ref[...] += _rowsum8(dy * xhat)
        dxhat = dy * gv
        dx_ref[...] = r * (dxhat - xhat * jnp.mean(dxhat * xhat, axis=-1, keepdims=True))

    row = pl.BlockSpec((tm, Dm), lambda i: (i, 0))
    acc = pl.BlockSpec((8, Dm), lambda i: (0, 0))
    return pl.pallas_call(
        body, name=name, grid=(T // tm,),
        in_specs=[row, pl.BlockSpec((1, Dm), lambda i: (0, 0)), row],
        out_specs=[row, acc, acc],
        out_shape=[jax.ShapeDtypeStruct((T, Dm), F32), jax.ShapeDtypeStruct((8, Dm), F32), jax.ShapeDtypeStruct((8, Dm), F32)],
        compiler_params=_cp(("arbitrary",)),
    )(x, g, target)


_LEVELS = (32, 16, 8)
_DIAG = 8
_SUBLANES = 8


def _chunk_consts():
    t = lax.broadcasted_iota(jnp.int32, (CHUNK, CHUNK), 0)
    s = lax.broadcasted_iota(jnp.int32, (CHUNK, CHUNK), 1)
    tri = (s <= t).astype(F32)
    tri_t = (s >= t).astype(F32)
    masks = []
    for m in _LEVELS:
        sh = int(np.log2(2 * m))
        masks.append(((t >> sh) == (s >> sh)) & ((t & (2 * m - 1)) >= m) & ((s & (2 * m - 1)) < m))
    return tri, tri_t, masks


def _exact_dot(sel, x):
    hi = x.astype(BF16)
    r1 = x - hi.astype(F32)
    mid = r1.astype(BF16)
    lo = (r1 - mid.astype(F32)).astype(BF16)
    return _dot(jnp.concatenate([sel, sel, sel], axis=1).astype(BF16), jnp.concatenate([hi, mid, lo], axis=0))


def _level_ref(b, m):
    if 2 * m >= _SUBLANES:
        pieces = [jnp.broadcast_to(b[p * 2 * m + m - 1:p * 2 * m + m, :], (2 * m, b.shape[1])) for p in range(CHUNK // (2 * m))]
        return pieces[0] if len(pieces) == 1 else jnp.concatenate(pieces, axis=0)
    groups = CHUNK // _SUBLANES
    b3 = b.reshape(groups, _SUBLANES, b.shape[1])
    row = lax.broadcasted_iota(jnp.int32, b3.shape, 1)
    ref = None
    for p in reversed(range(_SUBLANES // (2 * m))):
        src = jnp.broadcast_to(b3[:, p * 2 * m + m - 1:p * 2 * m + m, :], b3.shape)
        ref = src if ref is None else jnp.where(row < (p + 1) * 2 * m, src, ref)
    return ref.reshape(b.shape)


def _intra_off(q, kk, b, masks, scores=True):
    a = jnp.zeros((CHUNK, CHUNK), F32)
    keep = []
    for m, mask in zip(_LEVELS, masks):
        e = jnp.exp2(-jnp.abs(b - _level_ref(b, m)))
        qt = (q * e).astype(BF16)
        kt = (kk * e).astype(BF16)
        if scores:
            a = a + jnp.where(mask, _dot_nt(qt, kt), 0.0)
        keep.append((mask, e, qt, kt))
    return a, keep


def _roll_rows(x, d):
    g = x.shape[0] // _DIAG
    return pltpu.roll(x.reshape(g, _DIAG, x.shape[1]), d % _DIAG, 1).reshape(x.shape)


def _diag_fwd(q, kk, v, b):
    lower = lax.broadcasted_iota(jnp.int32, q.shape, 0) & (_DIAG - 1)
    out = jnp.sum(q * kk, axis=-1, keepdims=True) * v
    for d in range(1, _DIAG):
        dec = jnp.exp2(jnp.where(lower >= d, b - _roll_rows(b, d), _NEG))
        out = out + jnp.sum(q * dec * _roll_rows(kk, d), axis=-1, keepdims=True) * _roll_rows(v, d)
    return out


def _diag_bwd(q, kk, v, b, do):
    lower = lax.broadcasted_iota(jnp.int32, q.shape, 0) & (_DIAG - 1)
    a = jnp.sum(q * kk, axis=-1, keepdims=True)
    da = jnp.sum(do * v, axis=-1, keepdims=True)
    dq, dk, dv = da * kk, da * q, a * do
    for d in range(1, _DIAG):
        kr = _roll_rows(kk, d)
        dec = jnp.exp2(jnp.where(lower >= d, b - _roll_rows(b, d), _NEG))
        qd = q * dec
        a = jnp.sum(qd * kr, axis=-1, keepdims=True)
        da = jnp.sum(do * _roll_rows(v, d), axis=-1, keepdims=True)
        dq = dq + da * dec * kr
        dk = dk + _roll_rows(da * qd, -d)
        dv = dv + _roll_rows(a * do, -d)
    return dq, dk, dv


def _hgrn_in_specs(ct, hp, order):
    def spec(col):
        return pl.BlockSpec((ct, hp * HG_DK), lambda h, i, col=col: (order(i), col // hp + h))
    vec = pl.BlockSpec((1, hp * HG_DK), lambda h, i: (0, h))
    return [spec(COL_HQ), spec(COL_HF), spec(COL_HI), spec(COL_HG), vec, vec]


def hgrn_fwd(proj, lb, gn, *, ct, hp, name):
    T = proj.shape[0]
    nc = ct // CHUNK

    def body(hq_ref, hf_ref, hi_ref, hg_ref, lb_ref, gn_ref, og_ref, o_ref, st_ref, a_ref, s_ref):
        @pl.when(pl.program_id(1) == 0)
        def _():
            s_ref[...] = jnp.zeros_like(s_ref)

        tri, _, masks = _chunk_consts()

        def chunk(c, carry):
            sl = pl.ds(pl.multiple_of(c * CHUNK, CHUNK), CHUNK)
            lb_all = lb_ref[...]
            f_all = lb_all + (1.0 - lb_all) * _sigmoid(hf_ref[sl, :].astype(F32))
            b_all = _exact_dot(tri, jnp.log2(jnp.maximum(f_all, MIN_F)))
            for p in range(hp):
                cols = slice(p * HG_DK, (p + 1) * HG_DK)
                hq = hq_ref[sl, cols].astype(F32)
                v = hi_ref[sl, cols].astype(F32)
                gate = hg_ref[sl, cols].astype(F32)
                q = hq * _sigmoid(hq)
                f = f_all[:, cols]
                kk = 1.0 - f
                b = b_all[:, cols]
                st = s_ref[p]
                st_ref[p, c] = st
                a, _ = _intra_off(q, kk, b, masks)
                a_ref[p, c] = a
                vb = v.astype(BF16)
                o = (_dot_nt((q * jnp.exp2(b)).astype(BF16), st.astype(BF16)) + _dot(a.astype(BF16), vb)
                     + _diag_fwd(q, kk, v, b))
                bl = b[CHUNK - 1:CHUNK, :]
                s_ref[p] = st * jnp.exp2(bl) + _dot_tn(vb, (kk * jnp.exp2(bl - b)).astype(BF16))
                o_ref[sl, cols] = o
                r = lax.rsqrt(jnp.mean(o * o, axis=-1, keepdims=True) + EPS)
                og_ref[sl, cols] = (o * r * gn_ref[:, cols] * (gate * _sigmoid(gate))).astype(BF16)
            return carry

        lax.fori_loop(0, nc, chunk, 0)

    blk = pl.BlockSpec((ct, hp * HG_DK), lambda h, i: (i, h))
    og, o_raw, states, scores = pl.pallas_call(
        body, name=name, grid=(HG_HEADS // hp, T // ct),
        in_specs=_hgrn_in_specs(ct, hp, lambda i: i),
        out_specs=[blk, blk, pl.BlockSpec((hp, nc, HG_DK, HG_DK), lambda h, i: (h, i, 0, 0)),
                   pl.BlockSpec((hp, nc, CHUNK, CHUNK), lambda h, i: (h, i, 0, 0))],
        out_shape=[jax.ShapeDtypeStruct((T, HG_WIDTH), BF16), jax.ShapeDtypeStruct((T, HG_WIDTH), F32),
                   jax.ShapeDtypeStruct((HG_HEADS, T // CHUNK, HG_DK, HG_DK), F32),
                   jax.ShapeDtypeStruct((HG_HEADS, T // CHUNK, CHUNK, CHUNK), F32)],
        scratch_shapes=[pltpu.VMEM((hp, HG_DK, HG_DK), F32)],
        compiler_params=_cp(("parallel", "arbitrary")),
    )(proj, proj, proj, proj, lb, gn)
    return og, o_raw, (states, scores)


def hgrn_bwd(proj, lb, gn, o_raw, states, dog, dproj, *, ct, name):
    T = proj.shape[0]
    nc = ct // CHUNK
    nblk = T // ct
    hp = HG_HEADS

    states, scores = states

    def body(hq_ref, hf_ref, hi_ref, hg_ref, lb_ref, gn_ref, o_ref, st_ref, a_ref, dog_ref, _dproj_in,
             dp_ref, dlb_ref, dgn_ref, ds_ref):
        dhq_ref, dhf_ref, dhi_ref, dhg_ref = (dp_ref.at[:, s * HG_WIDTH:(s + 1) * HG_WIDTH] for s in range(4))
        @pl.when(pl.program_id(1) == 0)
        def _():
            ds_ref[...] = jnp.zeros_like(ds_ref)
            dlb_ref[...] = jnp.zeros_like(dlb_ref)
            dgn_ref[...] = jnp.zeros_like(dgn_ref)

        tri, tri_t, masks = _chunk_consts()
        last_row = lax.broadcasted_iota(jnp.int32, (CHUNK, HG_DK), 0) == CHUNK - 1

        def chunk(ci, carry):
            c = nc - 1 - ci
            sl = pl.ds(pl.multiple_of(c * CHUNK, CHUNK), CHUNK)
            lb_all = lb_ref[...]
            sig_all = _sigmoid(hf_ref[sl, :].astype(F32))
            f_all = lb_all + (1.0 - lb_all) * sig_all
            b_all = _exact_dot(tri, jnp.log2(jnp.maximum(f_all, MIN_F)))
            dbs, dks = [], []
            for p in range(hp):
                cols = slice(p * HG_DK, (p + 1) * HG_DK)
                gn_v = gn_ref[:, cols]
                hq = hq_ref[sl, cols].astype(F32)
                v = hi_ref[sl, cols].astype(F32)
                gate = hg_ref[sl, cols].astype(F32)
                sq = _sigmoid(hq)
                q = hq * sq
                f = f_all[:, cols]
                kk = 1.0 - f
                b = b_all[:, cols]
                st = st_ref[p, c]
                dst = ds_ref[p]
                o = o_ref[sl, cols]
                r = lax.rsqrt(jnp.mean(o * o, axis=-1, keepdims=True) + EPS)
                xhat = o * r
                sg = _sigmoid(gate)
                dog_v = dog_ref[sl, cols].astype(F32)
                dy = dog_v * (gate * sg)
                dhg_ref[sl, cols] = (dog_v * xhat * gn_v * (sg * (1.0 + gate * (1.0 - sg)))).astype(BF16)
                dgn_ref[:, cols] += _rowsum8(dy * xhat)
                dxh = dy * gn_v
                do = r * (dxh - xhat * jnp.mean(dxh * xhat, axis=-1, keepdims=True))
                dob = do.astype(BF16)
                vb = v.astype(BF16)
                stb = st.astype(BF16)
                dstb = dst.astype(BF16)
                eb = jnp.exp2(b)
                bl = b[CHUNK - 1:CHUNK, :]
                ebl = jnp.exp2(bl)
                edec = jnp.exp2(bl - b)
                qe = q * eb
                kdec = kk * edec
                _, keep = _intra_off(q, kk, b, masks, scores=False)
                a = a_ref[p, c]
                da = _dot_nt(dob, vb)
                dkdec = _dot(vb, dstb)
                dq = _dot(dob, stb) * eb
                dk = dkdec * edec
                dv = _dot_tn(a.astype(BF16), dob) + _dot_nt(kdec.astype(BF16), dstb)
                for mask, e, qt, kt in keep:
                    dsm = jnp.where(mask, da, 0.0).astype(BF16)
                    dq = dq + _dot(dsm, kt) * e
                    dk = dk + _dot_tn(dsm, qt) * e
                ddq, ddk, ddv = _diag_bwd(q, kk, v, b, do)
                dq = dq + ddq
                dk = dk + ddk
                dhi_ref[sl, cols] = (dv + ddv).astype(BF16)
                dbl = jnp.sum(dkdec * kdec, axis=0, keepdims=True) + ebl * jnp.sum(st * dst, axis=0, keepdims=True)
                db = q * dq - kk * dk
                dbs.append(jnp.where(last_row, db + dbl, db))
                dks.append(dk)
                ds_ref[p] = _dot_tn(dob, qe.astype(BF16)) + dst * ebl
                dhq_ref[sl, cols] = (dq * (sq * (1.0 + hq * (1.0 - sq)))).astype(BF16)
            dlogf = _exact_dot(tri_t, jnp.concatenate(dbs, axis=1))
            df = jnp.where(f_all > MIN_F, dlogf / f_all, 0.0) - jnp.concatenate(dks, axis=1)
            dhf_ref[sl, :] = (df * (1.0 - lb_all) * sig_all * (1.0 - sig_all)).astype(BF16)
            dlb_ref[...] += _rowsum8(df * (1.0 - sig_all))
            return carry

        lax.fori_loop(0, nc, chunk, 0)

    rev = lambda i: nblk - 1 - i
    blk = pl.BlockSpec((ct, hp * HG_DK), lambda h, i: (rev(i), h))
    acc = pl.BlockSpec((8, hp * HG_DK), lambda h, i: (0, h))
    return pl.pallas_call(
        body, name=name, grid=(HG_HEADS // hp, nblk),
        in_specs=_hgrn_in_specs(ct, hp, rev) + [blk, pl.BlockSpec((hp, nc, HG_DK, HG_DK), lambda h, i: (h, rev(i), 0, 0)),
                                                pl.BlockSpec((hp, nc, CHUNK, CHUNK), lambda h, i: (h, rev(i), 0, 0)), blk, _ANY],
        out_specs=[pl.BlockSpec((ct, 4 * HG_WIDTH), lambda h, i: (rev(i), 0)), acc, acc],
        out_shape=[jax.ShapeDtypeStruct(dproj.shape, dproj.dtype)] + [jax.ShapeDtypeStruct((8, HG_WIDTH), F32)] * 2,
        scratch_shapes=[pltpu.VMEM((hp, HG_DK, HG_DK), F32)],
        input_output_aliases={10: 0},
        compiler_params=_cp(("parallel", "arbitrary")),
    )(proj, proj, proj, proj, lb, gn, o_raw, states, scores, dog, dproj)


_GW = ATT_GROUP * HEAD_DIM
_NEG = -1e30


def rope_tables(T):
    half = ROPE_DIM // 2
    inv = ROPE_THETA ** (-jnp.arange(half, dtype=F32) * 2.0 / ROPE_DIM)
    ang = jnp.arange(T, dtype=F32)[:, None] * inv[None, :]
    pad1 = jnp.ones((T, HEAD_DIM - ROPE_DIM), F32)
    cos = jnp.concatenate([jnp.cos(ang), jnp.cos(ang), pad1], axis=1)
    sin = jnp.concatenate([jnp.sin(ang), jnp.sin(ang), 0.0 * pad1], axis=1)
    p = np.zeros((_GW, _GW), np.float32)
    for base in range(0, _GW, HEAD_DIM):
        for i in range(half):
            p[base + i + half, base + i] = -1.0
            p[base + i, base + i + half] = 1.0
    reps = _GW // HEAD_DIM
    return jnp.tile(cos, (1, reps)), jnp.tile(sin, (1, reps)), jnp.asarray(p, BF16)


def _rope(x, cos, sin, pm):
    return x.astype(F32) * cos + _dot(x, pm) * sin


def _unrope(dx, cos, sin, pm):
    return dx * cos + _dot_nt((dx * sin).astype(BF16), pm)


def _groups_to_rows(x):
    return jnp.concatenate([x[:, g * _GW:(g + 1) * _GW] for g in range(ATT_KV_HEADS)], axis=0)


def _rows_to_groups(x):
    return jnp.concatenate([x[g * WINDOW:(g + 1) * WINDOW] for g in range(ATT_KV_HEADS)], axis=1)


def _tile_rows(t):
    return jnp.concatenate([t] * ATT_KV_HEADS, axis=0)


_GQ = ATT_GROUP * WINDOW


def _swa_mask(n, queries_on_rows):
    shape = (_GQ, 2 * WINDOW) if queries_on_rows else (2 * WINDOW, _GQ)
    r = lax.broadcasted_iota(jnp.int32, shape, 0 if queries_on_rows else 1) & (WINDOW - 1)
    c = lax.broadcasted_iota(jnp.int32, shape, 1 if queries_on_rows else 0)
    delta = r + WINDOW - c
    return (delta >= 0) & (delta < WINDOW) & ((c >= WINDOW) | (n > 0))


def _sink_vector(sink_ref, g, queries_on_rows):
    shape = (_GQ, 1) if queries_on_rows else (1, _GQ)
    i = lax.broadcasted_iota(jnp.int32, shape, 0 if queries_on_rows else 1)
    out = jnp.full(shape, sink_ref[g * ATT_GROUP + ATT_GROUP - 1], F32)
    for hh in range(ATT_GROUP - 2, -1, -1):
        out = jnp.where(i < (hh + 1) * WINDOW, sink_ref[g * ATT_GROUP + hh], out)
    return out


def _head(x, h):
    return x[:, h * HEAD_DIM:(h + 1) * HEAD_DIM]


def _stack_heads(x):
    return jnp.concatenate([_head(x, hh) for hh in range(ATT_GROUP)], axis=0)


def _unstack_heads(x):
    return jnp.concatenate([x[hh * WINDOW:(hh + 1) * WINDOW] for hh in range(ATT_GROUP)], axis=1)


def _swa_specs(cur):
    prev = lambda n: jnp.maximum(n - 1, 0)
    kv = lambda col, f: pl.BlockSpec((WINDOW, _GW), lambda n: (f(n), col))
    tab = lambda f: pl.BlockSpec((WINDOW, _GW), lambda n: (f(n), 0))
    return [pl.BlockSpec((WINDOW, ATT_WIDTH), lambda n: (cur(n), COL_AQ // 8)),
            kv(COL_AK // 2, prev), kv(COL_AK // 2, cur), kv(COL_AV // 2, prev), kv(COL_AV // 2, cur),
            tab(prev), tab(cur), tab(prev), tab(cur),
            pl.BlockSpec((_GW, _GW), lambda n: (0, 0)), pl.BlockSpec(memory_space=pltpu.SMEM)]


def swa_fwd(proj, cos, sin, pm, sinks, *, name):
    T = proj.shape[0]
    nb = T // WINDOW
    scale = HEAD_DIM ** -0.5

    def body(q_ref, kp_ref, kc_ref, vp_ref, vc_ref, cp_ref, cc_ref, sp_ref, sc_ref, pm_ref, sink_ref, o_ref):
        n = pl.program_id(0)
        pm_v = pm_ref[...]
        mask = _swa_mask(n, True)
        k_cat = jnp.concatenate([_rope(kp_ref[...], cp_ref[...], sp_ref[...], pm_v),
                                 _rope(kc_ref[...], cc_ref[...], sc_ref[...], pm_v)], axis=0).astype(BF16)
        v_cat = jnp.concatenate([vp_ref[...], vc_ref[...]], axis=0)
        q_rot = (_rope(_groups_to_rows(q_ref[...]), _tile_rows(cc_ref[...]), _tile_rows(sc_ref[...]), pm_v) * scale).astype(BF16)
        for g in range(ATT_KV_HEADS):
            qs = _stack_heads(q_rot[g * WINDOW:(g + 1) * WINDOW])
            sk = _sink_vector(sink_ref, g, True)
            s = jnp.where(mask, _dot_nt(qs, _head(k_cat, g)), _NEG)
            m = jnp.maximum(jnp.max(s, axis=-1, keepdims=True), sk)
            p = jnp.exp(s - m)
            l = jnp.sum(p, axis=-1, keepdims=True) + jnp.exp(sk - m)
            o = _dot(p.astype(BF16), _head(v_cat, g)) / l
            o_ref[:, g * _GW:(g + 1) * _GW] = _unstack_heads(o).astype(BF16)

    return pl.pallas_call(
        body, name=name, grid=(nb,), in_specs=_swa_specs(lambda n: n),
        out_specs=pl.BlockSpec((WINDOW, ATT_WIDTH), lambda n: (n, 0)),
        out_shape=jax.ShapeDtypeStruct((T, ATT_WIDTH), BF16), compiler_params=_cp(("parallel",)),
    )(proj, proj, proj, proj, proj, cos, cos, sin, sin, pm, sinks)


def swa_bwd(proj, cos, sin, pm, sinks, att, datt, dproj, *, name):
    T = proj.shape[0]
    nb = T // WINDOW
    scale = HEAD_DIM ** -0.5
    clamp = lambda n: jnp.minimum(n, nb - 1)

    def body(q_ref, kp_ref, kc_ref, vp_ref, vc_ref, cp_ref, cc_ref, sp_ref, sc_ref, pm_ref, sink_ref, att_ref, datt_ref,
             _dproj_in, dp_ref, dsink_ref, cq_ref, ck_ref, cv_ref):
        n = pl.program_id(0)
        pm_v = pm_ref[...]

        @pl.when(n == 0)
        def _():
            dsink_ref[...] = jnp.zeros_like(dsink_ref)
            cq_ref[...] = jnp.zeros_like(cq_ref)
            ck_ref[...] = jnp.zeros_like(ck_ref)
            cv_ref[...] = jnp.zeros_like(cv_ref)

        def write_prev(dk_prev, dv_prev):
            dp_ref[:, :ATT_WIDTH] = cq_ref[...]
            dk = _unrope(ck_ref[...] + dk_prev, cp_ref[...], sp_ref[...], pm_v)
            dp_ref[:, ATT_WIDTH:ATT_WIDTH + _GW] = dk.astype(BF16)
            dp_ref[:, ATT_WIDTH + _GW:] = (cv_ref[...] + dv_prev).astype(BF16)

        @pl.when(n < nb)
        def _():
            mask = _swa_mask(n, False)
            ones = jnp.ones((8, 2 * HEAD_DIM), BF16)
            k_cat = jnp.concatenate([_rope(kp_ref[...], cp_ref[...], sp_ref[...], pm_v),
                                     _rope(kc_ref[...], cc_ref[...], sc_ref[...], pm_v)], axis=0).astype(BF16)
            v_cat = jnp.concatenate([vp_ref[...], vc_ref[...]], axis=0)
            dqs, dks, dvs = [], [], []
            cos_q, sin_q = _tile_rows(cc_ref[...]), _tile_rows(sc_ref[...])
            q_rot = (_rope(_groups_to_rows(q_ref[...]), cos_q, sin_q, pm_v) * scale).astype(BF16)
            for g in range(ATT_KV_HEADS):
                gc = slice(g * _GW, (g + 1) * _GW)
                qs = _stack_heads(q_rot[g * WINDOW:(g + 1) * WINDOW])
                dos = _stack_heads(datt_ref[:, gc])
                os_ = _stack_heads(att_ref[:, gc])
                kh, vh = _head(k_cat, g), _head(v_cat, g)
                sk = _sink_vector(sink_ref, g, False)
                s = jnp.where(mask, _dot_nt(kh, qs), _NEG)
                m = jnp.maximum(jnp.max(s, axis=0, keepdims=True), sk)
                e = jnp.exp(s - m)
                es = jnp.exp(sk - m)
                inv_l = 1.0 / (jnp.sum(e, axis=0, keepdims=True) + es)
                p = e * inv_l
                prod = dos.astype(F32) * os_.astype(F32)
                prod_hi = prod.astype(BF16)
                prod_lo = (prod - prod_hi.astype(F32)).astype(BF16)
                dsum = _dot_nt(ones, jnp.concatenate([prod_hi, prod_lo], axis=1))[0:1]
                dsink_ref[g, 0:1, :] += -(es * inv_l) * dsum
                ds = (p * (_dot_nt(vh, dos) - dsum)).astype(BF16)
                dqs.append(_unstack_heads(_dot_tn(ds, kh) * scale))
                dks.append(_dot(ds, qs))
                dvs.append(_dot(p.astype(BF16), dos))
            dk_all = jnp.concatenate(dks, axis=1)
            dv_all = jnp.concatenate(dvs, axis=1)

            @pl.when(n > 0)
            def _():
                write_prev(dk_all[:WINDOW], dv_all[:WINDOW])

            cq_ref[...] = _rows_to_groups(_unrope(jnp.concatenate(dqs, axis=0), cos_q, sin_q, pm_v)).astype(BF16)
            ck_ref[...] = dk_all[WINDOW:]
            cv_ref[...] = dv_all[WINDOW:]

        @pl.when(n == nb)
        def _():
            write_prev(jnp.zeros((WINDOW, _GW), F32), jnp.zeros((WINDOW, _GW), F32))

    row = pl.BlockSpec((WINDOW, ATT_WIDTH), lambda n: (clamp(n), 0))
    slab = ATT_WIDTH + 2 * _GW
    return pl.pallas_call(
        body, name=name, grid=(nb + 1,), in_specs=_swa_specs(clamp) + [row, row, _ANY],
        out_specs=[pl.BlockSpec((WINDOW, slab), lambda n: (jnp.maximum(n - 1, 0), COL_AQ * HG_DK // slab)),
                   pl.BlockSpec((ATT_KV_HEADS, 8, _GQ), lambda n: (0, 0, 0))],
        out_shape=[jax.ShapeDtypeStruct(dproj.shape, dproj.dtype), jax.ShapeDtypeStruct((ATT_KV_HEADS, 8, _GQ), F32)],
        scratch_shapes=[pltpu.VMEM((WINDOW, ATT_WIDTH), BF16), pltpu.VMEM((WINDOW, _GW), F32), pltpu.VMEM((WINDOW, _GW), F32)],
        input_output_aliases={13: 0},
        compiler_params=_cp(("arbitrary",)),
    )(proj, proj, proj, proj, proj, cos, cos, sin, sin, pm, sinks, att, datt, dproj)


def swap_with_sibling(arrays, *, name):
    n = len(arrays)

    def body(*refs):
        ins, outs = refs[:n], refs[n:2 * n]
        send_sems, recv_sems = refs[2 * n:]
        x, y, c, _ = _place()
        copies = [pltpu.make_async_remote_copy(src_ref=ins[a], dst_ref=outs[a], send_sem=send_sems.at[a],
                                               recv_sem=recv_sems.at[a], device_id=(x, y, 1 - c), device_id_type=MESH)
                  for a in range(n)]
        for cp in copies:
            cp.start()
        for cp in copies:
            cp.wait()

    return pl.pallas_call(
        body, name=name, out_shape=[jax.ShapeDtypeStruct(a.shape, a.dtype) for a in arrays],
        in_specs=[_ANY] * n, out_specs=[_ANY] * n,
        scratch_shapes=[pltpu.SemaphoreType.DMA((n,)), pltpu.SemaphoreType.DMA((n,))],
    )(*arrays)


N_DEV = 8
SMALL_ROWS = 24


def allreduce_small(v, *, name):
    def body(v_ref, o_ref, recv_ref, send_sems, recv_sems):
        x, y, c, _ = _place()
        me = 4 * x + 2 * y + c
        recv_ref[me] = v_ref[...]

        def copy(k, slot):
            peer = (x ^ (k >> 2), y ^ ((k >> 1) & 1), c ^ (k & 1))
            return pltpu.make_async_remote_copy(src_ref=v_ref, dst_ref=recv_ref.at[slot], send_sem=send_sems.at[k - 1],
                                                recv_sem=recv_sems.at[k - 1], device_id=peer, device_id_type=MESH)

        for k in range(1, N_DEV):
            copy(k, me).start()
        for k in range(1, N_DEV):
            copy(k, me ^ k).wait()
        acc = recv_ref[0]
        for d in range(1, N_DEV):
            acc = acc + recv_ref[d]
        o_ref[...] = acc

    vm = pl.BlockSpec(memory_space=pltpu.VMEM)
    return pl.pallas_call(
        body, name=name, out_shape=jax.ShapeDtypeStruct(v.shape, v.dtype), in_specs=[vm], out_specs=vm,
        scratch_shapes=[pltpu.VMEM((N_DEV,) + v.shape, v.dtype), pltpu.SemaphoreType.DMA((N_DEV - 1,)),
                        pltpu.SemaphoreType.DMA((N_DEV - 1,))],
    )(v)


def sum_parts(layers, *, tr, name):
    nl = len(layers)
    _, R, C = layers[0].shape
    nr = R // tr

    def body(*refs):
        o_ref = refs[nl]
        for j in range(nl):
            @pl.when(pl.program_id(0) == j)
            def _(p_ref=refs[j]):
                acc = p_ref[0].astype(F32)
                for q in range(1, N_CHIPS):
                    acc = acc + p_ref[q].astype(F32)
                o_ref[...] = acc

    return pl.pallas_call(
        body, name=name, grid=(nl, nr),
        in_specs=[pl.BlockSpec((N_CHIPS, tr, C), lambda l, i, j=j: (0, jnp.where(l == j, i, 0), 0)) for j in range(nl)],
        out_specs=pl.BlockSpec((tr, C), lambda l, i: (l * nr + i, 0)), out_shape=jax.ShapeDtypeStruct((nl * R, C), F32),
        compiler_params=_cp(("arbitrary", "arbitrary")),
    )(*layers)


def adamw(g_a, g_b, w, m, v, *, tr, name):
    R, C = w.shape
    c1 = 1.0 - ADAM_B1 ** ADAM_STEP
    c2 = 1.0 - ADAM_B2 ** ADAM_STEP
    ng = 1 if g_b is None else 2

    def body(*refs):
        w_ref, m_ref, v_ref, g_ref, d_ref, nm_ref, nv_ref = refs[ng:]
        g = refs[0][...] if ng == 1 else refs[0][...] + refs[1][...]
        nm = ADAM_B1 * m_ref[...] + (1.0 - ADAM_B1) * g
        nv = ADAM_B2 * v_ref[...] + (1.0 - ADAM_B2) * (g * g)
        g_ref[...] = g
        nm_ref[...] = nm
        nv_ref[...] = nv
        d_ref[...] = -ADAM_LR * ((nm / c1) / (jnp.sqrt(nv / c2) + ADAM_EPS) + ADAM_WD * w_ref[...])

    blk = pl.BlockSpec((tr, C), lambda i: (i, 0))
    gs = [g_a] if g_b is None else [g_a, g_b]
    return pl.pallas_call(
        body, name=name, grid=(R // tr,), in_specs=[blk] * (ng + 3), out_specs=[blk] * 4,
        out_shape=[jax.ShapeDtypeStruct((R, C), F32)] * 4, compiler_params=_cp(("parallel",)),
    )(*gs, w, m, v)


def _lb_bounds(lb_logits):
    p = jax.nn.softmax(lb_logits.astype(F32), axis=0)
    return jnp.cumsum(p, axis=0) - p[0:1]


def _pack_small(n1, lb, hgn, n2, fin, sinks, extra):
    row = jnp.concatenate([sinks.reshape(-1), jnp.reshape(extra, (1,)),
                           jnp.zeros((D_MODEL - sinks.size - 1,), F32)])
    return jnp.concatenate([n1, lb, hgn, n2, fin[None], row[None], jnp.zeros((SMALL_ROWS - 18, D_MODEL), F32)], axis=0)


def _unpack_small(buf):
    return (buf[0:4], buf[4:8], buf[8:12], buf[12:16], buf[16], buf[17, :DEPTH * ATT_Q_HEADS].reshape(DEPTH, ATT_Q_HEADS),
            buf[17, DEPTH * ATT_Q_HEADS])


def _cols_full(g):
    q, r, c = g.shape
    return jnp.transpose(g, (1, 0, 2)).reshape(r, q * c)


def _cols_split(w):
    r, qc = w.shape
    return jnp.transpose(w.reshape(r, N_CHIPS, qc // N_CHIPS), (1, 0, 2))


def kernel(x, norm1, w_in, lb_logits, hg_norm, attn_sinks, w_pa, w_pb, w_o, norm2, w_gate, w_up, w_down, final_norm, loss_target, m_norm1, m_w_in, m_lb_logits, m_hg_norm, m_attn_sinks, m_w_pa, m_w_pb, m_w_o, m_norm2, m_w_gate, m_w_up, m_w_down, m_final_norm, v_norm1, v_w_in, v_lb_logits, v_hg_norm, v_attn_sinks, v_w_pa, v_w_pb, v_w_o, v_norm2, v_w_gate, v_w_up, v_w_down, v_final_norm):
    T = x.shape[1]
    H = FFN_HIDDEN
    tm = min(512, T)
    tmx = min(1024, T)
    tkx = min(2048, T)
    big = dict(w_in=w_in, w_pa=w_pa, w_pb=w_pb, w_o=w_o, w_gate=w_gate, w_up=w_up, w_down=w_down)
    col_sharded = ("w_in", "w_gate", "w_up")

    shards = {k: w.astype(BF16) for k, w in big.items()}
    ffn_pair = ("w_gate", "w_up")
    with_w_in = ("w_in", "w_pa", "w_pb", "w_o", "w_down")
    others = [k for k in big if k != "w_in"]

    def whole(k, g):
        return _cols_full(g) if k in col_sharded else g.reshape(-1, g.shape[-1])

    def layer_weights(blocks_l):
        full = {k: whole(k, blocks_l[k]) for k in big}
        full["w_in"] = _to_kernel_cols(full["w_in"])
        full["w_gu"] = jnp.concatenate([full.pop("w_gate"), full.pop("w_up")], axis=-1)
        return full

    def carried_by(kind, l):
        if l + 1 >= DEPTH:
            return []
        if l == 0:
            return [(0, k) for k in others] + [(1, "w_in")] if kind == "in_proj" else [(1, k) for k in others]
        return [(l + 1, k) for k in (with_w_in if kind == "in_proj" else ffn_pair)]

    lb_all, lb_vjp = jax.vjp(_lb_bounds, lb_logits)
    cos, sin, pm = rope_tables(T)

    xs = x[0]
    saved, weights = [], []
    blocks = [dict() for _ in range(DEPTH)]
    (blocks[0]["w_in"],) = exchange_between_chips([shards["w_in"][0]], scatter=False, name="gather_w_in_0")
    for l in range(DEPTH):
        tasks = carried_by("in_proj", l)
        h, proj, *got = ln_matmul(xs, norm1[l:l + 1], _to_kernel_cols(whole("w_in", blocks[l]["w_in"])), tm=tmx, tn=2560,
                                  name=f"in_proj_{l}", gather=[shards[k][j] for j, k in tasks])
        for (j, k), g in zip(tasks, got):
            blocks[j][k] = g
        full = layer_weights(blocks[l])
        weights.append(full)
        og, o_raw, states = hgrn_fwd(proj, lb_all[l:l + 1], hg_norm[l:l + 1], ct=min(512, T), hp=HG_HEADS, name=f"hgrn_fwd_{l}")
        att = swa_fwd(proj, cos, sin, pm, attn_sinks[l], name=f"swa_fwd_{l}")
        ya, yb, mix, x1 = merge_fwd(xs, og, att, proj, full["w_pa"], full["w_pb"], full["w_o"], tm=tm, name=f"merge_fwd_{l}")
        tasks = carried_by("ffn_up", l)
        h2, gu, act, *got = ln_matmul(x1, norm2[l:l + 1], full["w_gu"], tm=min(256, T), tn=2 * H, name=f"ffn_up_{l}",
                                      swiglu=True, gather=[shards[k][j] for j, k in tasks])
        for (j, k), g in zip(tasks, got):
            blocks[j][k] = g
        x2 = matmul_residual(x1, act, full["w_down"], tm=tmx, name=f"ffn_down_{l}")
        saved.append((xs, h, proj, og, o_raw, states, att, ya, yb, mix, x1, h2, gu, act))
        xs = x2

    dx, loss_parts, dfin_parts = loss_head(xs, final_norm[None], loss_target[0], tm=tm, name="loss_head")

    arrived = [None] * DEPTH
    dn1, dn2, dlb, dgn, dsk = ([None] * DEPTH for _ in range(5))
    for l in reversed(range(DEPTH)):
        xs, h, proj, og, o_raw, states, att, ya, yb, mix, x1, h2, gu, act = saved[l]
        full = weights[l]
        dw = {}
        dgu = ffn_down_bwd(dx, full["w_down"], gu, tm=min(256, T), name=f"ffn_down_bwd_{l}")
        dw["w_down"] = matmul_tn(act, dx, tka=H // 2, tn=D_MODEL, tk=tkx, name=f"dw_down_{l}")
        dwgu = matmul_tn(h2, dgu, tka=D_MODEL, tn=H // 2, tk=tkx, name=f"dw_gu_{l}")
        dw["w_gate"], dw["w_up"] = dwgu[:, :H], dwgu[:, H:]
        dx1, dn2_parts = matmul_nt_rmsbwd(dgu, full["w_gu"], x1, norm2[l:l + 1], dx, tm=tmx, tk=H // 2, name=f"ffn_up_bwd_{l}")
        dya, dyb, dproj, dog, datt = merge_bwd(dx1, full["w_o"], proj, ya, yb, full["w_pa"], full["w_pb"], tm=tm,
                                               name=f"merge_bwd_{l}")
        dw["w_o"] = matmul_tn(mix, dx1, tka=D_MODEL, tn=D_MODEL, tk=tkx, name=f"dw_o_{l}")
        dw["w_pa"] = matmul_tn(og, dya, tka=D_MODEL, tn=D_MODEL, tk=tkx, name=f"dw_pa_{l}")
        dw["w_pb"] = matmul_tn(att, dyb, tka=D_MODEL, tn=D_MODEL, tk=tkx, name=f"dw_pb_{l}")
        dproj, dsk_parts = swa_bwd(proj, cos, sin, pm, attn_sinks[l], att, datt, dproj, name=f"swa_bwd_{l}")
        dproj, dlb_parts, dgn_parts = hgrn_bwd(proj, lb_all[l:l + 1], hg_norm[l:l + 1], o_raw, states, dog, dproj,
                                               ct=min(512, T), name=f"hgrn_bwd_{l}")
        split = lambda k, g: _cols_split(g) if k in col_sharded else g.reshape(N_CHIPS, -1, g.shape[-1])
        dw["w_in"], *got_pair = matmul_tn(h, dproj, tka=D_MODEL, tn=1536, tk=tkx, name=f"dw_in_{l}",
                                          scatter=[split(k, dw[k]) for k in ffn_pair])
        dw["w_in"] = _from_kernel_cols(dw["w_in"])
        dx, dn1_parts, *got = matmul_nt_rmsbwd(dproj, full["w_in"], xs, norm1[l:l + 1], dx1, tm=tmx, tk=1536,
                                               name=f"in_proj_bwd_{l}", scatter=[split(k, dw[k]) for k in with_w_in])
        arrived[l] = {**dict(zip(with_w_in, got)), **dict(zip(ffn_pair, got_pair))}
        dn1[l], dn2[l], dlb[l], dgn[l] = dn1_parts.sum(0), dn2_parts.sum(0), dlb_parts.sum(0), dgn_parts.sum(0)
        dsk[l] = dsk_parts[:, 0, :].reshape(ATT_Q_HEADS, WINDOW).sum(-1)
    grad_x = dx[None]

    mine = _pack_small(jnp.stack(dn1), jnp.stack(dlb), jnp.stack(dgn), jnp.stack(dn2), dfin_parts.sum(0), jnp.stack(dsk),
                       loss_parts.sum())
    total = allreduce_small(mine, name="allreduce_small")
    g_n1, g_lb_all, g_gn, g_n2, g_fin, g_sk, loss = _unpack_small(total)
    (g_lb,) = lb_vjp(g_lb_all)
    zero = jnp.zeros((), F32)
    small = adamw(_pack_small(g_n1, g_lb, g_gn, g_n2, g_fin, g_sk, zero), None,
                  _pack_small(norm1, lb_logits, hg_norm, norm2, final_norm, attn_sinks, zero),
                  _pack_small(m_norm1, m_lb_logits, m_hg_norm, m_norm2, m_final_norm, m_attn_sinks, zero),
                  _pack_small(v_norm1, v_lb_logits, v_hg_norm, v_norm2, v_final_norm, v_attn_sinks, zero),
                  tr=SMALL_ROWS, name="adamw_small")
    small = [_unpack_small(s)[:6] for s in small]

    sums = []
    for k in big:
        rows = arrived[0][k].shape[1]
        sums.append(sum_parts([arrived[l][k] for l in range(DEPTH)], tr=128 if rows % 128 == 0 else 64, name=f"sum_{k}"))
    theirs = swap_with_sibling(sums, name="swap_sums")
    mom = dict(w_in=(m_w_in, v_w_in), w_pa=(m_w_pa, v_w_pa), w_pb=(m_w_pb, v_w_pb), w_o=(m_w_o, v_w_o),
               w_gate=(m_w_gate, v_w_gate), w_up=(m_w_up, v_w_up), w_down=(m_w_down, v_w_down))
    upd = {}
    for k, mine_k, theirs_k in zip(big, sums, theirs):
        shp = big[k].shape
        flat = lambda t: t.reshape(shp[0] * shp[1], shp[2])
        res = adamw(mine_k, theirs_k, flat(big[k]), flat(mom[k][0]), flat(mom[k][1]), tr=128, name=f"adamw_{k}")
        upd[k] = [t.reshape(shp) for t in res]

    order = ("norm1", "w_in", "lb_logits", "hg_norm", "attn_sinks", "w_pa", "w_pb", "w_o", "norm2", "w_gate", "w_up",
             "w_down", "final_norm")
    small_pos = dict(norm1=0, lb_logits=1, hg_norm=2, norm2=3, final_norm=4, attn_sinks=5)
    outs = [loss, grad_x]
    for kind in range(4):
        for name in order:
            outs.append(upd[name][kind] if name in upd else small[kind][small_pos[name]])
    return tuple(outs)
```
